```python
import jax, jax.numpy as jnp
from jax import lax
import numpy as np

D_MODEL = 1024
BATCH = 16
SEQ = 256
DEPTH = 2
DEC_BATCH = 4
DEC_SEQ = 4096
PAST_LEN = 512

GRID_W = 64
ROPE_THETA = 10000.0
Q_BLOCK = 128
EPS = 1e-6
HEAD_DIM = 64
CHUNK = 128
A_GROUPS = 8
A_GROUP_DIM = 64
A_WIDTH = A_GROUPS * A_GROUP_DIM
B_HEADS = 8
B_KV_HEADS = 2
B_GROUP = B_HEADS // B_KV_HEADS
B_WIDTH = B_HEADS * HEAD_DIM
EV_IN = 2 * A_WIDTH + B_WIDTH + 2 * B_KV_HEADS * HEAD_DIM
EV_SPLITS = (A_WIDTH, 2 * A_WIDTH, 2 * A_WIDTH + B_WIDTH, 2 * A_WIDTH + B_WIDTH + B_KV_HEADS * HEAD_DIM)
EV_OUT = A_WIDTH + B_WIDTH
C_HEADS = 8
C_NOPE = 64
C_ROPE = 32
C_V = 64
C_Q_LORA = 256
C_KV_LORA = 128
C_WIDTH = C_HEADS * C_V
D_WIDTH = 512
D_CONV = 3
OD_IN = C_Q_LORA + C_KV_LORA + C_ROPE + 3 * D_WIDTH
OD_SPLITS = (C_Q_LORA, C_Q_LORA + C_KV_LORA, C_Q_LORA + C_KV_LORA + C_ROPE,
             C_Q_LORA + C_KV_LORA + C_ROPE + D_WIDTH, C_Q_LORA + C_KV_LORA + C_ROPE + 2 * D_WIDTH)
OD_OUT = C_WIDTH + D_WIDTH
N_EXPERTS = 16
EXPERT_FF = 512
EC_CAPACITY = 2
N_EVEN = (DEPTH + 1) // 2
N_ODD = DEPTH // 2

kernel_name = 'hybrid_diffusion_prefix_trunk_step'


def rms_norm(x, g):
    xf = x.astype(jnp.float32)
    y = xf * lax.rsqrt(jnp.mean(xf * xf, axis=-1, keepdims=True) + EPS)
    return (y * g.astype(jnp.float32)).astype(x.dtype)


def grid_angles(n_tokens, rot_dim):
    n_rows = n_tokens // GRID_W
    rows = jnp.repeat(jnp.arange(n_rows, dtype=jnp.float32), GRID_W)
    cols = jnp.tile(jnp.arange(GRID_W, dtype=jnp.float32), n_rows)
    axis_dim = rot_dim // 2
    inv_freq = ROPE_THETA ** (-jnp.arange(0, axis_dim, 2, dtype=jnp.float32) / axis_dim)
    return rows[:, None] * inv_freq, cols[:, None] * inv_freq


def rope_1d(x, ang):
    x1, x2 = jnp.split(x, 2, axis=-1)
    cos = jnp.cos(ang).astype(x.dtype)
    sin = jnp.sin(ang).astype(x.dtype)
    return jnp.concatenate([x1 * cos - x2 * sin, x2 * cos + x1 * sin], axis=-1)


def axial_rope(x, ang_row, ang_col):
    xr, xc = jnp.split(x, 2, axis=-1)
    return jnp.concatenate([rope_1d(xr, ang_row), rope_1d(xc, ang_col)], axis=-1)


def block_attention(q, k, v):
    b, hk, g, t, d = q.shape
    nb = t // Q_BLOCK
    qb = jnp.moveaxis(q.reshape(b, hk, g, nb, Q_BLOCK, d), 3, 0)
    scale = d ** -0.5

    def one_block(qi):
        s = jnp.einsum('bkgqd,bksd->bkgqs', qi, k, preferred_element_type=jnp.float32) * scale
        p = jax.nn.softmax(s, axis=-1).astype(v.dtype)
        return jnp.einsum('bkgqs,bksd->bkgqd', p, v)

    o = lax.map(one_block, qb)
    return jnp.moveaxis(o, 0, 3).reshape(b, hk, g, t, v.shape[-1])


def merge_heads(o):
    b, hk, g, t, d = o.shape
    return o.transpose(0, 3, 1, 2, 4).reshape(b, t, hk * g * d)


def modulation(cvec, w_mod, b_mod):
    m = jax.nn.silu(cvec) @ w_mod + b_mod
    return [p[:, None, :] for p in jnp.split(m, 6, axis=-1)]


def chunk_gating(u, va, w_s, b_s):
    b, t, _ = u.shape
    vc = va.reshape(b, t // CHUNK, CHUNK, A_GROUPS, A_GROUP_DIM)
    s = jnp.einsum('gpq,bnqgc->bnpgc', w_s, vc) + b_s.T[None, None, :, :, None]
    return u * s.reshape(b, t, A_WIDTH)


def even_project(h, w_in, q_gain, k_gain):
    b, t, _ = h.shape
    u, va, q, k, v = jnp.split(h @ w_in, list(EV_SPLITS), axis=-1)
    q = rms_norm(q.reshape(b, t, B_KV_HEADS, B_GROUP, HEAD_DIM), q_gain).transpose(0, 2, 3, 1, 4)
    k = rms_norm(k.reshape(b, t, B_KV_HEADS, HEAD_DIM), k_gain).transpose(0, 2, 1, 3)
    v = v.reshape(b, t, B_KV_HEADS, HEAD_DIM).transpose(0, 2, 1, 3)
    return u, va, q, k, v


def even_context(h, w_in, q_gain, k_gain, w_s, b_s, w_out):
    u, va, q, k, v = even_project(h, w_in, q_gain, k_gain)
    o = block_attention(q, k, v)
    out = jnp.concatenate([chunk_gating(u, va, w_s, b_s), merge_heads(o)], axis=-1) @ w_out
    return out, k, v


def even_latent(h, k_ctx, v_ctx, w_in, q_gain, k_gain, w_s, b_s, w_out):
    t = h.shape[1]
    u, va, q, k, v = even_project(h, w_in, q_gain, k_gain)
    ang_r, ang_c = grid_angles(t, HEAD_DIM)
    q = axial_rope(q, ang_r, ang_c)
    k = axial_rope(k, ang_r, ang_c)
    o = block_attention(q, jnp.concatenate([k_ctx, k], axis=2), jnp.concatenate([v_ctx, v], axis=2))
    return jnp.concatenate([chunk_gating(u, va, w_s, b_s), merge_heads(o)], axis=-1) @ w_out


def odd_project(h, w_in, q_a_norm, kv_a_norm, w_uq):
    b, t, _ = h.shape
    cq, ckv, krope, gb, gc, hv = jnp.split(h @ w_in, list(OD_SPLITS), axis=-1)
    q = (rms_norm(cq, q_a_norm) @ w_uq).reshape(b, t, C_HEADS, C_NOPE + C_ROPE)
    q_nope, q_rope = jnp.split(q, [C_NOPE], axis=-1)
    return q_nope, q_rope, rms_norm(ckv, kv_a_norm), krope, gb, gc, hv


def mla_attention(q_nope, q_rope, ckv, krope, w_ukv):
    b, s, _ = ckv.shape
    k_nope, v = jnp.split((ckv @ w_ukv).reshape(b, s, C_HEADS, C_NOPE + C_V), [C_NOPE], axis=-1)
    k = jnp.concatenate([k_nope, jnp.broadcast_to(krope[:, :, None, :], (b, s, C_HEADS, C_ROPE))], axis=-1)
    q = jnp.concatenate([q_nope, q_rope], axis=-1)
    o = block_attention(q.transpose(0, 2, 1, 3)[:, :, None], k.transpose(0, 2, 1, 3), v.transpose(0, 2, 1, 3))
    return merge_heads(o)


def short_conv(gb, gc, hv, w):
    z = gc * hv
    zp = jnp.pad(z, ((0, 0), (1, 1), (0, 0)))
    y = zp[:, :-2] * w[0] + zp[:, 1:-1] * w[1] + zp[:, 2:] * w[2]
    return gb * y


def odd_context(h, w_in, q_a_norm, kv_a_norm, w_uq, w_ukv, conv_w, w_out):
    q_nope, q_rope, ckv, krope, gb, gc, hv = odd_project(h, w_in, q_a_norm, kv_a_norm, w_uq)
    a = mla_attention(q_nope, q_rope, ckv, krope, w_ukv)
    out = jnp.concatenate([a, short_conv(gb, gc, hv, conv_w)], axis=-1) @ w_out
    return out, ckv, krope


def odd_latent(h, ckv_ctx, krope_ctx, w_in, q_a_norm, kv_a_norm, w_uq, w_ukv, conv_w, w_out):
    t = h.shape[1]
    q_nope, q_rope, ckv, krope, gb, gc, hv = odd_project(h, w_in, q_a_norm, kv_a_norm, w_uq)
    ang_r, ang_c = grid_angles(t, C_ROPE)
    q_rope = axial_rope(q_rope, ang_r[:, None], ang_c[:, None])
    krope = axial_rope(krope, ang_r, ang_c)
    a = mla_attention(q_nope, q_rope, jnp.concatenate([ckv_ctx, ckv], axis=1),
                      jnp.concatenate([krope_ctx, krope], axis=1), w_ukv)
    return jnp.concatenate([a, short_conv(gb, gc, hv, conv_w)], axis=-1) @ w_out


def ec_moe(x, w_router, w_gate, w_up, w_down):
    b, t, d = x.shape
    cap = EC_CAPACITY * t // N_EXPERTS
    logits = jnp.einsum('btd,de->bet', x, w_router, preferred_element_type=jnp.float32)
    aff = jax.nn.softmax(logits, axis=1)
    gate, idx = lax.top_k(aff, cap)
    xs = jax.vmap(lambda xb, ib: xb[ib])(x, idx)
    hid = jax.nn.silu(jnp.einsum('becd,edf->becf', xs, w_gate)) * jnp.einsum('becd,edf->becf', xs, w_up)
    y = jnp.einsum('becf,efd->becd', hid, w_down) * gate[..., None].astype(x.dtype)
    return jax.vmap(lambda yb, ib: jnp.zeros((t, d), yb.dtype).at[ib].add(yb))(y, idx)


def setup_inputs(seed: int = 0) -> dict:
    key = jax.random.key(seed)
    ks = iter(jax.random.split(key, 40))

    def nrm(shape, scale=1.0):
        return jax.random.normal(next(ks), shape, jnp.float32) * scale

    def gain(shape):
        return 1.0 + 0.01 * nrm(shape)

    D = D_MODEL
    return {
        'x_prompt': nrm((BATCH, SEQ, D)),
        'x_sample': nrm((DEC_BATCH, DEC_SEQ, D)),
        'c': nrm((DEC_BATCH, D)),
        'cache_attn_k': nrm((DEC_BATCH, N_EVEN, B_KV_HEADS, PAST_LEN, HEAD_DIM)),
        'cache_attn_v': nrm((DEC_BATCH, N_EVEN, B_KV_HEADS, PAST_LEN, HEAD_DIM)),
        'cache_mla_ckv': nrm((DEC_BATCH, N_ODD, PAST_LEN, C_KV_LORA)),
        'cache_mla_krope': nrm((DEC_BATCH, N_ODD, PAST_LEN, C_ROPE)),
        'c_ctx': nrm((D,)),
        'w_mod': nrm((DEPTH, D, 6 * D), 0.5 * D ** -0.5),
        'b_mod': nrm((DEPTH, 6 * D), 0.01),
        'norm1': gain((DEPTH, D)),
        'norm2': gain((DEPTH, D)),
        'ev_w_in': nrm((N_EVEN, D, EV_IN), D ** -0.5),
        'ev_q_norm': gain((N_EVEN, HEAD_DIM)),
        'ev_k_norm': gain((N_EVEN, HEAD_DIM)),
        'ev_w_s': nrm((N_EVEN, A_GROUPS, CHUNK, CHUNK), CHUNK ** -0.5),
        'ev_b_s': nrm((N_EVEN, A_GROUPS, CHUNK), 0.01),
        'ev_w_out': nrm((N_EVEN, EV_OUT, D), EV_OUT ** -0.5),
        'od_w_in': nrm((N_ODD, D, OD_IN), D ** -0.5),
        'od_q_a_norm': gain((N_ODD, C_Q_LORA)),
        'od_kv_a_norm': gain((N_ODD, C_KV_LORA)),
        'od_w_uq': nrm((N_ODD, C_Q_LORA, C_HEADS * (C_NOPE + C_ROPE)), C_Q_LORA ** -0.5),
        'od_w_ukv': nrm((N_ODD, C_KV_LORA, C_HEADS * (C_NOPE + C_V)), C_KV_LORA ** -0.5),
        'od_conv_w': nrm((N_ODD, D_CONV, D_WIDTH), D_CONV ** -0.5),
        'od_w_out': nrm((N_ODD, OD_OUT, D), OD_OUT ** -0.5),
        'moe_router': nrm((DEPTH, D, N_EXPERTS), D ** -0.5),
        'moe_w_gate': nrm((DEPTH, N_EXPERTS, D, EXPERT_FF), D ** -0.5),
        'moe_w_up': nrm((DEPTH, N_EXPERTS, D, EXPERT_FF), D ** -0.5),
        'moe_w_down': nrm((DEPTH, N_EXPERTS, EXPERT_FF, D), EXPERT_FF ** -0.5),
        'final_norm': gain((D,)),
    }


def reference(x_prompt, x_sample, c, cache_attn_k, cache_attn_v, cache_mla_ckv, cache_mla_krope, c_ctx,
              w_mod, b_mod, norm1, norm2, ev_w_in, ev_q_norm, ev_k_norm, ev_w_s, ev_b_s, ev_w_out,
              od_w_in, od_q_a_norm, od_kv_a_norm, od_w_uq, od_w_ukv, od_conv_w, od_w_out,
              moe_router, moe_w_gate, moe_w_up, moe_w_down, final_norm):
    xc, xl = x_prompt, x_sample
    new_k, new_v, new_ckv, new_kr = [], [], [], []
    for layer in range(DEPTH):
        sh1c, sc1c, g1c, sh2c, sc2c, g2c = modulation(c_ctx[None], w_mod[layer], b_mod[layer])
        sh1l, sc1l, g1l, sh2l, sc2l, g2l = modulation(c, w_mod[layer], b_mod[layer])
        hc = rms_norm(xc, norm1[layer]) * (1 + sc1c) + sh1c
        hl = rms_norm(xl, norm1[layer]) * (1 + sc1l) + sh1l
        i = layer // 2
        if layer % 2 == 0:
            oc, kc, vc = even_context(hc, ev_w_in[i], ev_q_norm[i], ev_k_norm[i], ev_w_s[i], ev_b_s[i], ev_w_out[i])
            ol = even_latent(hl, cache_attn_k[:, i], cache_attn_v[:, i], ev_w_in[i], ev_q_norm[i], ev_k_norm[i],
                             ev_w_s[i], ev_b_s[i], ev_w_out[i])
            new_k.append(kc)
            new_v.append(vc)
        else:
            oc, ckv_c, kr_c = odd_context(hc, od_w_in[i], od_q_a_norm[i], od_kv_a_norm[i], od_w_uq[i],
                                          od_w_ukv[i], od_conv_w[i], od_w_out[i])
            ol = odd_latent(hl, cache_mla_ckv[:, i], cache_mla_krope[:, i], od_w_in[i], od_q_a_norm[i],
                            od_kv_a_norm[i], od_w_uq[i], od_w_ukv[i], od_conv_w[i], od_w_out[i])
            new_ckv.append(ckv_c)
            new_kr.append(kr_c)
        xc = xc + g1c * oc
        xl = xl + g1l * ol
        hc = rms_norm(xc, norm2[layer]) * (1 + sc2c) + sh2c
        hl = rms_norm(xl, norm2[layer]) * (1 + sc2l) + sh2l
        xc = xc + g2c * ec_moe(hc, moe_router[layer], moe_w_gate[layer], moe_w_up[layer], moe_w_down[layer])
        xl = xl + g2l * ec_moe(hl, moe_router[layer], moe_w_gate[layer], moe_w_up[layer], moe_w_down[layer])
    y_prompt = rms_norm(xc, final_norm)
    y_sample = rms_norm(xl, final_norm)
    return (y_prompt, y_sample, jnp.stack(new_k, axis=1), jnp.stack(new_v, axis=1),
            jnp.stack(new_ckv, axis=1), jnp.stack(new_kr, axis=1))
```

```python
import functools
import math

import jax
import jax.numpy as jnp
from jax import lax
from jax.experimental import pallas as pl
from jax.experimental.pallas import tpu as pltpu

F32 = jnp.float32
BF16 = jnp.bfloat16

D_MODEL = 1024
BATCH = 16
SEQ = 256
DEC_BATCH = 4
DEC_SEQ = 4096
PAST_LEN = 512
GRID_W = 64
ROPE_THETA = 10000.0
EPS = 1e-6
HEAD_DIM = 64
CHUNK = 128
A_GROUPS = 8
A_WIDTH = 512
B_HEADS = 8
B_KV_HEADS = 2
B_WIDTH = 512
C_HEADS = 8
C_NOPE = 64
C_ROPE = 32
C_V = 64
C_Q_LORA = 256
C_KV_LORA = 128
D_WIDTH = 512
N_EXPERTS = 16
EXPERT_FF = 512
EC_CAPACITY = 2

LOG2E = math.log2(math.e)
LANES = 128
BF16_ROWS = 16
ATT_TK = 256
TILE = 256
NT_CTX = BATCH * SEQ // TILE
NT_SEQ = DEC_SEQ // TILE
NT_LAT = DEC_BATCH * NT_SEQ
NT = NT_CTX + NT_LAT
N_TOK = NT * TILE
GROUP = 4096
N_GROUPS = N_TOK // GROUP
GROUP_TILES = GROUP // TILE
GROUP_CAP = EC_CAPACITY * GROUP // N_EXPERTS
LIST_BLK = 128
VMEM_LIMIT = 56 * 1024 * 1024


def _cparams(sem):
    return pltpu.CompilerParams(dimension_semantics=sem, vmem_limit_bytes=VMEM_LIMIT)


def _dot(a, b):
    return jnp.dot(a, b, preferred_element_type=F32)


def _silu(x):
    return x / (1.0 + jnp.exp(-x))


def _mod_row(i):
    return jnp.where(i < NT_CTX, 0, 1 + (i - NT_CTX) // NT_SEQ)


def _tab_blk(i):
    return jnp.where(i < NT_CTX, 0, 1 + (i - NT_CTX) % NT_SEQ)


def _mod_kernel(c_ref, w_ref, b_ref, o_ref):
    s = _silu(c_ref[...])
    o_ref[0] = _dot(s.astype(BF16), w_ref[0].astype(BF16)) + b_ref[0]


def _modulation(cvec, w_mod, b_mod):
    depth = w_mod.shape[0]
    nchunk = 6
    return pl.pallas_call(
        _mod_kernel,
        grid=(depth, nchunk),
        in_specs=[
            pl.BlockSpec((8, D_MODEL), lambda l, k: (0, 0)),
            pl.BlockSpec((1, D_MODEL, D_MODEL), lambda l, k: (l, 0, k)),
            pl.BlockSpec((1, 1, D_MODEL), lambda l, k: (l, 0, k)),
        ],
        out_specs=pl.BlockSpec((1, 8, D_MODEL), lambda l, k: (l, 0, k)),
        out_shape=jax.ShapeDtypeStruct((depth, 8, 6 * D_MODEL), F32),
        compiler_params=_cparams(("arbitrary", "arbitrary")),
        name="modulation",
    )(cvec, w_mod, b_mod.reshape(depth, 1, 6 * D_MODEL))


def _norm_mod(x, gain, scale, shift):
    ms = jnp.mean(x * x, axis=-1, keepdims=True)
    return (x * lax.rsqrt(ms + EPS) * gain) * (1.0 + scale) + shift


def _seg_mean_sq(z, bd, width):
    zz = z * z
    hi = zz.astype(BF16)
    lo = (zz - hi.astype(F32)).astype(BF16)
    return (_dot(hi, bd) + _dot(lo, bd)) * (1.0 / width)


def _rope(z, cos, sin_signed, half):
    w = z.shape[1]
    lane = lax.broadcasted_iota(jnp.int32, z.shape, 1)
    first = (lane % (2 * half)) < half
    partner = jnp.where(first, pltpu.roll(z, w - half, 1), pltpu.roll(z, half, 1))
    return z * cos + partner * sin_signed


def _tile_lanes(t, n):
    return jnp.concatenate([t] * n, axis=1) if n > 1 else t


def _even_in_kernel(xc_ref, xl_ref, mod_ref, n1_ref, w_ref, qg_ref, kg_ref, cos_ref, sin_ref, ws_ref, bs_ref, bd_ref,
                    gated_ref, qt_ref, k_ref, kb_ref, v_ref, vt_ref):
    x = _group_tile(pl.program_id(0), xc_ref, xl_ref)
    h = _norm_mod(x, n1_ref[...], mod_ref[0, 0, 1:2, :], mod_ref[0, 0, 0:1, :])
    p = _dot(h.astype(BF16), w_ref[...])
    u = p[:, 0:512]
    va = p[:, 512:1024].astype(BF16)
    q = p[:, 1024:1536]
    k = p[:, 1536:1664]
    v = p[:, 1664:1792]
    bd = bd_ref[...]
    cos = cos_ref[0]
    sin = sin_ref[0]
    qn = q * lax.rsqrt(_seg_mean_sq(q, bd, HEAD_DIM) + EPS) * qg_ref[...]
    kn = k * lax.rsqrt(_seg_mean_sq(k, bd[0:LANES, 0:LANES], HEAD_DIM) + EPS) * kg_ref[...]
    qr = _rope(qn, _tile_lanes(cos, 4), _tile_lanes(sin, 4), HEAD_DIM // 4)
    kr = _rope(kn, cos, sin, HEAD_DIM // 4)
    qt_ref[...] = (qr * (HEAD_DIM ** -0.5 * LOG2E)).T.astype(BF16)
    k_ref[...] = kr
    kb_ref[...] = kr.astype(BF16)
    v_ref[...] = v
    vt_ref[...] = v.T.astype(BF16)
    lane = lax.broadcasted_iota(jnp.int32, (CHUNK, LANES), 1)
    for ch in range(TILE // CHUNK):
        rows = slice(ch * CHUNK, (ch + 1) * CHUNK)
        cols = []
        for pair in range(A_GROUPS // 2):
            vp = va[rows, pair * LANES:(pair + 1) * LANES]
            r0 = _dot(ws_ref[2 * pair], vp)
            r1 = _dot(ws_ref[2 * pair + 1], vp)
            cols.append(jnp.where(lane < LANES // 2, r0, r1))
        s = jnp.concatenate(cols, axis=1) + bs_ref[...]
        gated_ref[rows, :] = (u[rows, :] * s).astype(BF16)


def _even_in(xc, xl, mod, n1, w_in, qg, kg, cos, sin, ws, bs_tab, bd):
    full = lambda shape: pl.BlockSpec(shape, lambda i: (0,) * len(shape))
    return pl.pallas_call(
        _even_in_kernel,
        grid=(NT,),
        in_specs=[
            pl.BlockSpec((TILE, D_MODEL), _CTX_ROW),
            pl.BlockSpec((TILE, D_MODEL), _LAT_ROW),
            pl.BlockSpec((1, 1, 6, D_MODEL), lambda i: (0, _mod_row(i), 0, 0)),
            full((1, D_MODEL)),
            full(w_in.shape),
            full((1, 512)),
            full((1, LANES)),
            pl.BlockSpec((1, TILE, LANES), lambda i: (_tab_blk(i), 0, 0)),
            pl.BlockSpec((1, TILE, LANES), lambda i: (_tab_blk(i), 0, 0)),
            full(ws.shape),
            full(bs_tab.shape),
            full(bd.shape),
        ],
        out_specs=[
            pl.BlockSpec((TILE, 512), lambda i: (i, 0)),
            pl.BlockSpec((512, TILE), lambda i: (0, i)),
            pl.BlockSpec((TILE, LANES), lambda i: (i, 0)),
            pl.BlockSpec((TILE, LANES), lambda i: (i, 0)),
            pl.BlockSpec((TILE, LANES), lambda i: (i, 0)),
            pl.BlockSpec((LANES, TILE), lambda i: (0, i)),
        ],
        out_shape=[
            jax.ShapeDtypeStruct((N_TOK, 512), BF16),
            jax.ShapeDtypeStruct((512, N_TOK), BF16),
            jax.ShapeDtypeStruct((N_TOK, LANES), F32),
            jax.ShapeDtypeStruct((N_TOK, LANES), BF16),
            jax.ShapeDtypeStruct((N_TOK, LANES), F32),
            jax.ShapeDtypeStruct((LANES, N_TOK), BF16),
        ],
        compiler_params=_cparams(("parallel",)),
        name="even_in",
    )(xc, xl, mod, n1, w_in, qg, kg, cos, sin, ws, bs_tab, bd)


def _odd_in_kernel(x_ref, xb_ref, modp_ref, mod_ref, n1_ref, w_ref, qa_ref, kva_ref, wuq_ref, cos_ref, sin_ref,
                   x2_ref, qnt_ref, qrt_ref, ckv_ref, kr_ref, krb_ref, gb_ref, z_ref):
    x = x_ref[...] + modp_ref[0, 0, 5:6, :] * xb_ref[...]
    x2_ref[...] = x
    h = _norm_mod(x, n1_ref[...], mod_ref[0, 0, 1:2, :], mod_ref[0, 0, 0:1, :])
    p = _dot(h.astype(BF16), w_ref[...])
    cq = p[:, 0:256]
    ckv = p[:, 256:384]
    kr = p[:, 384:512]
    gb_ref[...] = p[:, 512:1024]
    z_ref[...] = p[:, 1024:1536] * p[:, 1536:2048]
    cqn = cq * lax.rsqrt(jnp.mean(cq * cq, axis=-1, keepdims=True) + EPS) * qa_ref[...]
    q = _dot(cqn.astype(BF16), wuq_ref[...])
    scale = (C_NOPE + C_ROPE) ** -0.5 * LOG2E
    cos = cos_ref[0]
    sin = sin_ref[0]
    qnt_ref[...] = (q[:, 0:512] * scale).T.astype(BF16)
    qr = _rope(q[:, 512:768], _tile_lanes(cos, 2), _tile_lanes(sin, 2), C_ROPE // 4)
    qrt_ref[...] = (qr * scale).T.astype(BF16)
    ckv_ref[...] = ckv * lax.rsqrt(jnp.mean(ckv * ckv, axis=-1, keepdims=True) + EPS) * kva_ref[...]
    krr = _rope(kr, cos, sin, C_ROPE // 4)
    kr_ref[...] = krr
    krb_ref[...] = krr.astype(BF16)


def _odd_in(x, xb, modp, mod, n1, w_in, qa, kva, wuq, cos, sin):
    full = lambda shape: pl.BlockSpec(shape, lambda i: (0,) * len(shape))
    tile = lambda w: pl.BlockSpec((TILE, w), lambda i: (i, 0))
    return pl.pallas_call(
        _odd_in_kernel,
        grid=(NT,),
        in_specs=[
            tile(D_MODEL), tile(D_MODEL),
            pl.BlockSpec((1, 1, 6, D_MODEL), lambda i: (0, _mod_row(i), 0, 0)),
            pl.BlockSpec((1, 1, 6, D_MODEL), lambda i: (1, _mod_row(i), 0, 0)),
            full((1, D_MODEL)),
            full(w_in.shape),
            full((1, C_Q_LORA)),
            full((1, C_KV_LORA)),
            full(wuq.shape),
            pl.BlockSpec((1, TILE, LANES), lambda i: (_tab_blk(i), 0, 0)),
            pl.BlockSpec((1, TILE, LANES), lambda i: (_tab_blk(i), 0, 0)),
        ],
        out_specs=[tile(D_MODEL),
                   pl.BlockSpec((512, TILE), lambda i: (0, i)),
                   pl.BlockSpec((256, TILE), lambda i: (0, i)),
                   tile(LANES), tile(LANES), tile(LANES), tile(512), tile(512)],
        out_shape=[
            jax.ShapeDtypeStruct((N_TOK, D_MODEL), F32),
            jax.ShapeDtypeStruct((512, N_TOK), BF16),
            jax.ShapeDtypeStruct((256, N_TOK), BF16),
            jax.ShapeDtypeStruct((N_TOK, LANES), F32),
            jax.ShapeDtypeStruct((N_TOK, LANES), F32),
            jax.ShapeDtypeStruct((N_TOK, LANES), BF16),
            jax.ShapeDtypeStruct((N_TOK, 512), F32),
            jax.ShapeDtypeStruct((N_TOK, 512), F32),
        ],
        compiler_params=_cparams(("parallel",)),
        name="odd_in",
    )(x, xb, modp, mod, n1, w_in, qa, kva, wuq, cos, sin)


def _kvup_kernel(c_ref, w_ref, kn_ref, vt_ref):
    p = _dot(c_ref[...].astype(BF16), w_ref[...])
    kn_ref[...] = p[:, 0:512].astype(BF16)
    vt_ref[...] = p[:, 512:1024].T.astype(BF16)


def _kvup(ckv_all, w_ukv):
    n = ckv_all.shape[0]
    return pl.pallas_call(
        _kvup_kernel,
        grid=(n // TILE,),
        in_specs=[pl.BlockSpec((TILE, C_KV_LORA), lambda i: (i, 0)),
                  pl.BlockSpec(w_ukv.shape, lambda i: (0, 0))],
        out_specs=[pl.BlockSpec((TILE, 512), lambda i: (i, 0)), pl.BlockSpec((512, TILE), lambda i: (0, i))],
        out_shape=[jax.ShapeDtypeStruct((n, 512), BF16), jax.ShapeDtypeStruct((512, n), BF16)],
        compiler_params=_cparams(("parallel",)),
        name="mla_kv_up",
    )(ckv_all, w_ukv)


def _softmax_pv(qc, key_chunk, vt_chunks, nk):
    w = qc.shape[1]
    m = jnp.full((1, w), -jnp.inf, F32)
    acc = jnp.zeros((HEAD_DIM + BF16_ROWS, w), F32)
    ones = jnp.ones((BF16_ROWS, ATT_TK), BF16)
    s_next = _dot(key_chunk(0), qc)
    for c in range(nk):
        s = s_next
        if c + 1 < nk:
            s_next = _dot(key_chunk(c + 1), qc)
        mn = jnp.maximum(m, jnp.max(s, axis=0, keepdims=True))
        p = jnp.exp2(s - mn).astype(BF16)
        vts = vt_chunks(c)
        if len(vts) == 1:
            pv = _dot(jnp.concatenate([vts[0], ones], axis=0), p)
        else:
            tq = w // len(vts)
            pv = jnp.concatenate([_dot(jnp.concatenate([vt, ones], axis=0), p[:, i * tq:(i + 1) * tq])
                                  for i, vt in enumerate(vts)], axis=1)
        acc = jnp.exp2(m - mn) * acc + pv
        m = mn
    return acc[0:HEAD_DIM, :] / acc[HEAD_DIM:HEAD_DIM + 1, :]


def _gqa_kernel(q_ref, k_ref, vt_ref, o_ref):
    kvh = pl.program_id(1)
    nk = k_ref.shape[0] // ATT_TK
    tq = q_ref.shape[1]
    n = B_HEADS // B_KV_HEADS
    cols = []
    for hh in range(n):
        q = q_ref[hh * HEAD_DIM:(hh + 1) * HEAD_DIM, :]
        z = jnp.zeros_like(q)
        cols.append(jnp.where(kvh == 0, jnp.concatenate([q, z], axis=0), jnp.concatenate([z, q], axis=0)))
    o = _softmax_pv(jnp.concatenate(cols, axis=1), lambda c: k_ref[c * ATT_TK:(c + 1) * ATT_TK, :],
                    lambda c: [vt_ref[:, c * ATT_TK:(c + 1) * ATT_TK]], nk)
    for hh in range(n):
        o_ref[hh * HEAD_DIM:(hh + 1) * HEAD_DIM, :] = o[:, hh * tq:(hh + 1) * tq].astype(BF16)


def _mla_kernel(qn_ref, qr_ref, kn_ref, kr_ref, vt_ref, o_ref):
    nk = kn_ref.shape[0] // ATT_TK
    tq = qn_ref.shape[1]
    z = jnp.zeros((C_NOPE, tq), BF16)
    zr = jnp.zeros((LANES - C_ROPE, tq), BF16)
    cols = []
    for hh in range(2):
        qn = qn_ref[hh * C_NOPE:(hh + 1) * C_NOPE, :]
        qr = qr_ref[hh * C_ROPE:(hh + 1) * C_ROPE, :]
        cols.append(jnp.concatenate(([qn, z] if hh == 0 else [z, qn]) + [qr, zr], axis=0))
    o = _softmax_pv(
        jnp.concatenate(cols, axis=1),
        lambda c: jnp.concatenate([kn_ref[c * ATT_TK:(c + 1) * ATT_TK, :],
                                   kr_ref[c * ATT_TK:(c + 1) * ATT_TK, :]], axis=1),
        lambda c: [vt_ref[hh * C_V:(hh + 1) * C_V, c * ATT_TK:(c + 1) * ATT_TK] for hh in range(2)], nk)
    for hh in range(2):
        o_ref[hh * C_V:(hh + 1) * C_V, :] = o[:, hh * tq:(hh + 1) * tq].astype(BF16)


def _attn_call(body, grid, in_specs, out_spec, args, n_tiles):
    return pl.pallas_call(
        body,
        grid=grid,
        in_specs=in_specs,
        out_specs=out_spec,
        out_shape=jax.ShapeDtypeStruct((512, n_tiles * TILE), BF16),
        compiler_params=_cparams(("parallel",) * len(grid)),
        name="attention",
    )(*args)


def _gqa_attention(qt, kb, vt, k_lat, vt_lat):
    s = k_lat.shape[1]
    rows = (B_HEADS // B_KV_HEADS) * HEAD_DIM
    o_ctx = _attn_call(
        _gqa_kernel, (BATCH, B_KV_HEADS),
        [pl.BlockSpec((rows, TILE), lambda b, h: (h, b)),
         pl.BlockSpec((None, TILE, LANES), lambda b, h: (b, 0, 0)),
         pl.BlockSpec((HEAD_DIM, TILE), lambda b, h: (h, b))],
        pl.BlockSpec((rows, TILE), lambda b, h: (h, b)),
        [qt, kb.reshape(NT, TILE, LANES), vt], NT_CTX)
    o_lat = _attn_call(
        _gqa_kernel, (DEC_BATCH, B_KV_HEADS, NT_SEQ),
        [pl.BlockSpec((rows, TILE), lambda b, h, j: (h, NT_CTX + b * NT_SEQ + j)),
         pl.BlockSpec((None, s, LANES), lambda b, h, j: (b, 0, 0)),
         pl.BlockSpec((None, HEAD_DIM, s), lambda b, h, j: (b, h, 0))],
        pl.BlockSpec((rows, TILE), lambda b, h, j: (h, b * NT_SEQ + j)),
        [qt, k_lat, vt_lat], NT_LAT)
    return o_ctx, o_lat


def _mla_attention(qnt, qrt, kn_c, krb, vt_c, kn_l, kr_l, vt_l):
    s = kr_l.shape[1]
    pairs = C_HEADS // 2
    o_ctx = _attn_call(
        _mla_kernel, (BATCH, pairs),
        [pl.BlockSpec((2 * C_NOPE, TILE), lambda b, p: (p, b)),
         pl.BlockSpec((2 * C_ROPE, TILE), lambda b, p: (p, b)),
         pl.BlockSpec((TILE, LANES), lambda b, p: (b, p)),
         pl.BlockSpec((TILE, LANES), lambda b, p: (b, 0)),
         pl.BlockSpec((2 * C_V, TILE), lambda b, p: (p, b))],
        pl.BlockSpec((2 * C_V, TILE), lambda b, p: (p, b)),
        [qnt, qrt, kn_c, krb, vt_c], NT_CTX)
    o_lat = _attn_call(
        _mla_kernel, (DEC_BATCH, pairs, NT_SEQ),
        [pl.BlockSpec((2 * C_NOPE, TILE), lambda b, p, j: (p, NT_CTX + b * NT_SEQ + j)),
         pl.BlockSpec((2 * C_ROPE, TILE), lambda b, p, j: (p, NT_CTX + b * NT_SEQ + j)),
         pl.BlockSpec((s, LANES), lambda b, p, j: (b, p)),
         pl.BlockSpec((None, s, LANES), lambda b, p, j: (b, 0, 0)),
         pl.BlockSpec((2 * C_V, s), lambda b, p, j: (p, b))],
        pl.BlockSpec((2 * C_V, TILE), lambda b, p, j: (p, b * NT_SEQ + j)),
        [qnt, qrt, kn_l, kr_l, vt_l], NT_LAT)
    return o_ctx, o_lat


def _group_tile(i, ctx_ref, lat_ref):
    return jnp.where(i < NT_CTX, ctx_ref[...], lat_ref[...])


_CTX_COL = lambda i: (0, jnp.minimum(i, NT_CTX - 1))
_LAT_COL = lambda i: (0, jnp.maximum(i - NT_CTX, 0))
_CTX_ROW = lambda i: (jnp.minimum(i, NT_CTX - 1), 0)
_LAT_ROW = lambda i: (jnp.maximum(i - NT_CTX, 0), 0)


def _finish_out(o, x_in, mod_ref, n2_ref, wrh_ref, wrl_ref, xo_ref, h2_ref, lg_ref):
    x = x_in + mod_ref[0, 0, 2:3, :] * o
    xo_ref[...] = x
    h2 = _norm_mod(x, n2_ref[...], mod_ref[0, 0, 4:5, :], mod_ref[0, 0, 3:4, :])
    hi = h2.astype(BF16)
    lo = (h2 - hi.astype(F32)).astype(BF16)
    h2_ref[...] = hi
    lg_ref[...] = _dot(hi, wrh_ref[...]) + _dot(lo, wrh_ref[...]) + _dot(hi, wrl_ref[...])


def _dot_t(at, b):
    return lax.dot_general(at, b, (((0,), (0,)), ((), ())), preferred_element_type=F32)


def _even_out_kernel(a_ref, btc_ref, btl_ref, w_ref, xc_ref, xl_ref, mod_ref, n2_ref, wrh_ref, wrl_ref,
                     xo_ref, h2_ref, lg_ref):
    i = pl.program_id(0)
    bt = _group_tile(i, btc_ref, btl_ref)
    o = _dot(a_ref[...], w_ref[0:512, :]) + _dot_t(bt, w_ref[512:1024, :])
    _finish_out(o, _group_tile(i, xc_ref, xl_ref), mod_ref, n2_ref, wrh_ref, wrl_ref, xo_ref, h2_ref, lg_ref)


def _odd_out_kernel(atc_ref, atl_ref, z_ref, zp_ref, zn_ref, gb_ref, cw_ref, w_ref, x_ref, mod_ref, n2_ref,
                    wrh_ref, wrl_ref, xo_ref, h2_ref, lg_ref):
    i = pl.program_id(0)
    at = _group_tile(i, atc_ref, atl_ref)
    j = (i - NT_CTX) % NT_SEQ
    first = jnp.logical_or(i < NT_CTX, j == 0)
    last = jnp.logical_or(i < NT_CTX, j == NT_SEQ - 1)
    z = z_ref[...]
    row = lax.broadcasted_iota(jnp.int32, z.shape, 0)
    halo_p = jnp.where(first, 0.0, zp_ref[7:8, :])
    halo_n = jnp.where(last, 0.0, zn_ref[0:1, :])
    zprev = jnp.where(row == 0, halo_p, pltpu.roll(z, 1, 0))
    znext = jnp.where(row == TILE - 1, halo_n, pltpu.roll(z, TILE - 1, 0))
    y = zprev * cw_ref[0:1, :] + z * cw_ref[1:2, :] + znext * cw_ref[2:3, :]
    d = (gb_ref[...] * y).astype(BF16)
    o = _dot_t(at, w_ref[0:512, :]) + _dot(d, w_ref[512:1024, :])
    _finish_out(o, x_ref[...], mod_ref, n2_ref, wrh_ref, wrl_ref, xo_ref, h2_ref, lg_ref)


_OUT_SHAPES = [
    jax.ShapeDtypeStruct((N_TOK, D_MODEL), F32),
    jax.ShapeDtypeStruct((N_TOK, D_MODEL), BF16),
    jax.ShapeDtypeStruct((N_TOK, LANES), F32),
]


def _even_out(layer, a, b, w_out, xc, xl, mod, n2, wrh, wrl):
    full = lambda shape: pl.BlockSpec(shape, lambda i: (0,) * len(shape))
    tile = lambda w: pl.BlockSpec((TILE, w), lambda i: (i, 0))
    return pl.pallas_call(
        _even_out_kernel,
        grid=(NT,),
        in_specs=[tile(512), pl.BlockSpec((512, TILE), _CTX_COL), pl.BlockSpec((512, TILE), _LAT_COL),
                  full(w_out.shape), pl.BlockSpec((TILE, D_MODEL), _CTX_ROW), pl.BlockSpec((TILE, D_MODEL), _LAT_ROW),
                  pl.BlockSpec((1, 1, 6, D_MODEL), lambda i: (layer, _mod_row(i), 0, 0)),
                  full((1, D_MODEL)), full(wrh.shape), full(wrl.shape)],
        out_specs=[tile(D_MODEL), tile(D_MODEL), tile(LANES)],
        out_shape=_OUT_SHAPES,
        compiler_params=_cparams(("parallel",)),
        name="even_out",
    )(a, b[0], b[1], w_out, xc, xl, mod, n2, wrh, wrl)


def _odd_out(layer, a, z, gb, cw, w_out, x, mod, n2, wrh, wrl):
    full = lambda shape: pl.BlockSpec(shape, lambda i: (0,) * len(shape))
    tile = lambda w: pl.BlockSpec((TILE, w), lambda i: (i, 0))
    rb = TILE // 8
    return pl.pallas_call(
        _odd_out_kernel,
        grid=(NT,),
        in_specs=[pl.BlockSpec((512, TILE), _CTX_COL), pl.BlockSpec((512, TILE), _LAT_COL), tile(512),
                  pl.BlockSpec((8, 512), lambda i: (jnp.maximum(i * rb - 1, 0), 0)),
                  pl.BlockSpec((8, 512), lambda i: (jnp.minimum(i * rb + rb, NT * rb - 1), 0)),
                  tile(512), full(cw.shape), full(w_out.shape), tile(D_MODEL),
                  pl.BlockSpec((1, 1, 6, D_MODEL), lambda i: (layer, _mod_row(i), 0, 0)),
                  full((1, D_MODEL)), full(wrh.shape), full(wrl.shape)],
        out_specs=[tile(D_MODEL), tile(D_MODEL), tile(LANES)],
        out_shape=_OUT_SHAPES,
        compiler_params=_cparams(("parallel",)),
        name="odd_out",
    )(a[0], a[1], z, z, z, gb, cw, w_out, x, mod, n2, wrh, wrl)


def _excl_prefix(mask_f, tri):
    t = mask_f.shape[0]
    carry = jnp.zeros((1, LANES), F32)
    outs = []
    for blk in range(t // TILE):
        m = mask_f[blk * TILE:(blk + 1) * TILE]
        outs.append(_dot(tri, m.astype(BF16)) + carry)
        carry = carry + jnp.sum(m, axis=0, keepdims=True)
    return jnp.concatenate(outs, axis=0) if len(outs) > 1 else outs[0]


def _route_kernel(lg_ref, pos_ref, gate_ref, *, cap, per_domain_base):
    lg = lg_ref[...]
    t = lg.shape[0]
    lane = lax.broadcasted_iota(jnp.int32, lg.shape, 1)
    valid = lane < N_EXPERTS
    mx = jnp.max(jnp.where(valid, lg, -jnp.inf), axis=1, keepdims=True)
    ex = jnp.where(valid, jnp.exp(lg - mx), 0.0)
    aff = ex / jnp.sum(ex, axis=1, keepdims=True)
    thr_bits = jnp.zeros((1, LANES), jnp.int32)
    for bit in range(30, -1, -1):
        cand = thr_bits | (1 << bit)
        cnt = jnp.sum((aff >= lax.bitcast_convert_type(cand, F32)).astype(F32), axis=0, keepdims=True)
        thr_bits = jnp.where(cnt >= cap, cand, thr_bits)
    thr = lax.bitcast_convert_type(thr_bits, F32)
    gt = aff > thr
    eq = aff == thr
    need = cap - jnp.sum(gt.astype(F32), axis=0, keepdims=True)
    r = lax.broadcasted_iota(jnp.int32, (TILE, TILE), 0)
    c = lax.broadcasted_iota(jnp.int32, (TILE, TILE), 1)
    tri = (c < r).astype(BF16)
    eq_rank = _excl_prefix(eq.astype(F32), tri)
    sel = jnp.logical_and(valid, jnp.logical_or(gt, jnp.logical_and(eq, eq_rank < need)))
    pos = _excl_prefix(sel.astype(F32), tri)
    if per_domain_base:
        pos = pos + (pl.program_id(0) * cap).astype(F32)
    pos_ref[...] = jnp.where(sel, pos, -1.0)
    gate_ref[...] = jnp.where(sel, aff, 0.0)


def _route(logits, t_dom, cap, per_domain_base):
    n = logits.shape[0]
    return pl.pallas_call(
        functools.partial(_route_kernel, cap=cap, per_domain_base=per_domain_base),
        grid=(n // t_dom,),
        in_specs=[pl.BlockSpec((t_dom, LANES), lambda i: (i, 0))],
        out_specs=[pl.BlockSpec((t_dom, LANES), lambda i: (i, 0)), pl.BlockSpec((t_dom, LANES), lambda i: (i, 0))],
        out_shape=[jax.ShapeDtypeStruct((n, LANES), F32), jax.ShapeDtypeStruct((n, LANES), F32)],
        compiler_params=_cparams(("parallel",)),
        name="route",
    )(logits)


def _moe_kernel(cum_ref, x_ref, pos_ref, gate_ref, wg_ref, wu_ref, wd_ref, out_ref,
                acc_ref, xs_ref, gl_ref, yh_ref, yl_ref, sem):
    g = pl.program_id(0)
    e = pl.program_id(1)
    base = (g * N_EXPERTS + e) * (GROUP_TILES + 1)

    @pl.when(e == 0)
    def _():
        acc_ref[...] = jnp.zeros_like(acc_ref)

    xs_ref[...] = jnp.zeros_like(xs_ref)
    gl_ref[...] = jnp.zeros_like(gl_ref)
    row = lax.broadcasted_iota(jnp.int32, (LIST_BLK, TILE), 0).astype(F32)

    def one_hot(j, c):
        posrow = pos_ref[0, j, pl.ds(e, 1), :]
        return (posrow == row + (c * LIST_BLK).astype(F32)).astype(F32)

    def list_blocks(j):
        lo = cum_ref[base + j]
        hi = cum_ref[base + j + 1]
        return lo, hi, lo // LIST_BLK, (hi + LIST_BLK - 1) // LIST_BLK

    def gather_tile(j, _):
        lo, hi, c_lo, c_hi = list_blocks(j)

        @pl.when(hi > lo)
        def _():
            tok = pl.multiple_of(j * TILE, TILE)
            xt = x_ref[pl.ds(tok, TILE), :]
            gaterow = gate_ref[0, j, pl.ds(e, 1), :]

            def blk(c, _):
                s = one_hot(j, c)
                r0 = pl.multiple_of(c * LIST_BLK, LIST_BLK)
                xs_ref[pl.ds(r0, LIST_BLK), :] += _dot(s.astype(BF16), xt)
                gl_ref[pl.ds(r0, LIST_BLK), :] += jnp.sum(s * gaterow, axis=1, keepdims=True)
                return 0

            lax.fori_loop(c_lo, c_hi, blk, 0)

        return 0

    lax.fori_loop(0, GROUP_TILES, gather_tile, 0)

    xs = xs_ref[...].astype(BF16)
    hid = _silu(_dot(xs, wg_ref[0])) * _dot(xs, wu_ref[0])
    y = _dot(hid.astype(BF16), wd_ref[0]) * gl_ref[...]
    yh = y.astype(BF16)
    yh_ref[...] = yh
    yl_ref[...] = (y - yh.astype(F32)).astype(BF16)

    def scatter_tile(j, _):
        lo, hi, c_lo, c_hi = list_blocks(j)

        @pl.when(hi > lo)
        def _():
            tok = pl.multiple_of(j * TILE, TILE)

            def blk(c, _):
                st = one_hot(j, c).T.astype(BF16)
                r0 = pl.multiple_of(c * LIST_BLK, LIST_BLK)
                acc_ref[pl.ds(tok, TILE), :] += (_dot(st, yh_ref[pl.ds(r0, LIST_BLK), :])
                                                 + _dot(st, yl_ref[pl.ds(r0, LIST_BLK), :]))
                return 0

            lax.fori_loop(c_lo, c_hi, blk, 0)

        return 0

    lax.fori_loop(0, GROUP_TILES, scatter_tile, 0)

    @pl.when(e == N_EXPERTS - 1)
    def _():
        cp = pltpu.make_async_copy(acc_ref, out_ref.at[pl.ds(pl.multiple_of(g * GROUP, GROUP), GROUP), :], sem)
        cp.start()
        cp.wait()


def _moe(cum, h2, pos_t, gate_t, wg, wu, wd):
    grid_spec = pltpu.PrefetchScalarGridSpec(
        num_scalar_prefetch=1,
        grid=(N_GROUPS, N_EXPERTS),
        in_specs=[
            pl.BlockSpec((GROUP, D_MODEL), lambda g, e, cum: (g, 0)),
            pl.BlockSpec((1, GROUP_TILES, N_EXPERTS, TILE), lambda g, e, cum: (g, 0, 0, 0)),
            pl.BlockSpec((1, GROUP_TILES, N_EXPERTS, TILE), lambda g, e, cum: (g, 0, 0, 0)),
            pl.BlockSpec((1, D_MODEL, EXPERT_FF), lambda g, e, cum: (e, 0, 0)),
            pl.BlockSpec((1, D_MODEL, EXPERT_FF), lambda g, e, cum: (e, 0, 0)),
            pl.BlockSpec((1, EXPERT_FF, D_MODEL), lambda g, e, cum: (e, 0, 0)),
        ],
        out_specs=pl.BlockSpec(memory_space=pl.ANY),
        scratch_shapes=[
            pltpu.VMEM((GROUP, D_MODEL), F32),
            pltpu.VMEM((GROUP_CAP, D_MODEL), F32),
            pltpu.VMEM((GROUP_CAP, 1), F32),
            pltpu.VMEM((GROUP_CAP, D_MODEL), BF16),
            pltpu.VMEM((GROUP_CAP, D_MODEL), BF16),
            pltpu.SemaphoreType.DMA,
        ],
    )
    return pl.pallas_call(
        _moe_kernel,
        grid_spec=grid_spec,
        out_shape=jax.ShapeDtypeStruct((N_TOK, D_MODEL), F32),
        compiler_params=_cparams(("arbitrary", "arbitrary")),
        name="moe",
    )(cum, h2, pos_t, gate_t, wg, wu, wd)


def _final_kernel(x_ref, xb_ref, modp_ref, fn_ref, o_ref):
    x = x_ref[...] + modp_ref[0, 0, 5:6, :] * xb_ref[...]
    o_ref[...] = x * lax.rsqrt(jnp.mean(x * x, axis=-1, keepdims=True) + EPS) * fn_ref[...]


def _final(layer, x, xb, mod, fn):
    tile = pl.BlockSpec((TILE, D_MODEL), lambda i: (i, 0))
    return pl.pallas_call(
        _final_kernel,
        grid=(NT,),
        in_specs=[tile, tile,
                  pl.BlockSpec((1, 1, 6, D_MODEL), lambda i: (layer, _mod_row(i), 0, 0)),
                  pl.BlockSpec((1, D_MODEL), lambda i: (0, 0))],
        out_specs=tile,
        out_shape=jax.ShapeDtypeStruct((N_TOK, D_MODEL), F32),
        compiler_params=_cparams(("parallel",)),
        name="final_norm",
    )(x, xb, mod, fn)


def _rope_tables(rot_dim):
    axis_dim = rot_dim // 2
    tok = jnp.arange(DEC_SEQ)
    rows = (tok // GRID_W).astype(F32)
    cols = (tok % GRID_W).astype(F32)
    inv_freq = ROPE_THETA ** (-jnp.arange(0, axis_dim, 2, dtype=F32) / axis_dim)
    ar = rows[:, None] * inv_freq
    ac = cols[:, None] * inv_freq
    cos = jnp.concatenate([jnp.cos(ar), jnp.cos(ar), jnp.cos(ac), jnp.cos(ac)], axis=1)
    sin = jnp.concatenate([-jnp.sin(ar), jnp.sin(ar), -jnp.sin(ac), jnp.sin(ac)], axis=1)
    rep = LANES // rot_dim
    cos = jnp.tile(cos, (1, rep)).reshape(NT_SEQ, TILE, LANES)
    sin = jnp.tile(sin, (1, rep)).reshape(NT_SEQ, TILE, LANES)
    cos = jnp.concatenate([jnp.ones((1, TILE, LANES), F32), cos], axis=0)
    sin = jnp.concatenate([jnp.zeros((1, TILE, LANES), F32), sin], axis=0)
    return cos, sin


def _route_and_moe(h2, logits, wg, wu, wd):
    pos_c, gate_c = _route(logits[:NT_CTX * TILE], SEQ, EC_CAPACITY * SEQ // N_EXPERTS, True)
    pos_l, gate_l = _route(logits[NT_CTX * TILE:], DEC_SEQ, EC_CAPACITY * DEC_SEQ // N_EXPERTS, False)
    pos = jnp.concatenate([pos_c, pos_l], axis=0)[:, :N_EXPERTS]
    gate = jnp.concatenate([gate_c, gate_l], axis=0)[:, :N_EXPERTS]
    pos_t = pos.reshape(N_GROUPS, GROUP_TILES, TILE, N_EXPERTS).transpose(0, 1, 3, 2)
    gate_t = gate.reshape(N_GROUPS, GROUP_TILES, TILE, N_EXPERTS).transpose(0, 1, 3, 2)
    per_tile = jnp.sum(pos_t >= 0, axis=-1, dtype=jnp.int32).transpose(0, 2, 1)
    cum = jnp.concatenate([jnp.zeros((N_GROUPS, N_EXPERTS, 1), jnp.int32), jnp.cumsum(per_tile, axis=-1)], axis=-1)
    return _moe(cum.reshape(-1), h2, pos_t, gate_t, wg, wu, wd)


def _split_router(w):
    wp = jnp.zeros((D_MODEL, LANES), F32).at[:, :N_EXPERTS].set(w)
    hi = wp.astype(BF16)
    lo = (wp - hi.astype(F32)).astype(BF16)
    return hi, lo


def kernel(x_prompt, x_sample, c, cache_attn_k, cache_attn_v, cache_mla_ckv, cache_mla_krope, c_ctx, w_mod, b_mod,
           norm1, norm2, ev_w_in, ev_q_norm, ev_k_norm, ev_w_s, ev_b_s, ev_w_out, od_w_in, od_q_a_norm,
           od_kv_a_norm, od_w_uq, od_w_ukv, od_conv_w, od_w_out, moe_router, moe_w_gate, moe_w_up, moe_w_down,
           final_norm):
    nctx = NT_CTX * TILE
    xc = x_prompt.reshape(nctx, D_MODEL)
    xl = x_sample.reshape(NT_LAT * TILE, D_MODEL)
    cvec = jnp.concatenate([c_ctx[None], c, jnp.zeros((3, D_MODEL), F32)], axis=0)
    mod = _modulation(cvec, w_mod, b_mod).reshape(2, 8, 6, D_MODEL)
    wg = moe_w_gate.astype(BF16)
    wu = moe_w_up.astype(BF16)
    wd = moe_w_down.astype(BF16)

    cos64, sin64 = _rope_tables(HEAD_DIM)
    seg = jnp.arange(512) // HEAD_DIM
    bd = (seg[:, None] == seg[None, :]).astype(BF16)
    bs_tab = jnp.repeat(ev_b_s[0].T, 64, axis=1)
    gated, qt, k, kb, v, vt = _even_in(
        xc, xl, mod, norm1[0:1], ev_w_in[0].astype(BF16), jnp.tile(ev_q_norm[0], 8)[None], jnp.tile(ev_k_norm[0], 2)[None],
        cos64, sin64, ev_w_s[0].astype(BF16), bs_tab, bd)
    new_k = k[:nctx].reshape(BATCH, SEQ, B_KV_HEADS, HEAD_DIM).transpose(0, 2, 1, 3)[:, None]
    new_v = v[:nctx].reshape(BATCH, SEQ, B_KV_HEADS, HEAD_DIM).transpose(0, 2, 1, 3)[:, None]
    ck = cache_attn_k[:, 0].transpose(0, 2, 1, 3).reshape(DEC_BATCH, PAST_LEN, LANES).astype(BF16)
    k_lat = jnp.concatenate([ck, kb[nctx:].reshape(DEC_BATCH, DEC_SEQ, LANES)], axis=1)
    cv = cache_attn_v[:, 0].transpose(0, 1, 3, 2).reshape(DEC_BATCH, LANES, PAST_LEN).astype(BF16)
    vt_lat = jnp.concatenate([cv, vt[:, nctx:].reshape(LANES, DEC_BATCH, DEC_SEQ).transpose(1, 0, 2)], axis=2)
    attn_t = _gqa_attention(qt, kb, vt, k_lat, vt_lat)
    wrh, wrl = _split_router(moe_router[0])
    x1, h2, logits = _even_out(0, gated, attn_t, ev_w_out[0].astype(BF16), xc, xl, mod, norm2[0:1], wrh, wrl)
    moe0 = _route_and_moe(h2, logits, wg[0], wu[0], wd[0])

    cos32, sin32 = _rope_tables(C_ROPE)
    w_in1 = od_w_in[0]
    w_in1 = jnp.concatenate([w_in1[:, :C_Q_LORA + C_KV_LORA + C_ROPE], jnp.zeros((D_MODEL, LANES - C_ROPE), F32),
                             w_in1[:, C_Q_LORA + C_KV_LORA + C_ROPE:]], axis=1).astype(BF16)
    wuq = od_w_uq[0].reshape(C_Q_LORA, C_HEADS, C_NOPE + C_ROPE)
    wuq = jnp.concatenate([wuq[:, :, :C_NOPE].reshape(C_Q_LORA, -1), wuq[:, :, C_NOPE:].reshape(C_Q_LORA, -1)],
                          axis=1).astype(BF16)
    x2, qnt, qrt, ckv, kr, krb, gb, z = _odd_in(x1, moe0, mod, mod, norm1[1:2], w_in1, od_q_a_norm[0][None],
                                                od_kv_a_norm[0][None], wuq, cos32, sin32)
    new_ckv = ckv[:nctx].reshape(BATCH, 1, SEQ, C_KV_LORA)
    new_kr = kr[:nctx, :C_ROPE].reshape(BATCH, 1, SEQ, C_ROPE)
    wukv = od_w_ukv[0].reshape(C_KV_LORA, C_HEADS, C_NOPE + C_V)
    wukv = jnp.concatenate([wukv[:, :, :C_NOPE].reshape(C_KV_LORA, -1), wukv[:, :, C_NOPE:].reshape(C_KV_LORA, -1)],
                           axis=1).astype(BF16)
    ckv_lat = jnp.concatenate([cache_mla_ckv[:, 0], ckv[nctx:].reshape(DEC_BATCH, DEC_SEQ, C_KV_LORA)], axis=1)
    kn_c, vt_c = _kvup(ckv[:nctx], wukv)
    kn_l, vt_l = _kvup(ckv_lat.reshape(-1, C_KV_LORA), wukv)
    ckr = jnp.pad(cache_mla_krope[:, 0], ((0, 0), (0, 0), (0, LANES - C_ROPE))).astype(BF16)
    kr_l = jnp.concatenate([ckr, krb[nctx:].reshape(DEC_BATCH, DEC_SEQ, LANES)], axis=1)
    attn_t = _mla_attention(qnt, qrt, kn_c, krb, vt_c, kn_l, kr_l, vt_l)
    wrh, wrl = _split_router(moe_router[1])
    cw = jnp.concatenate([od_conv_w[0], jnp.zeros((5, D_WIDTH), F32)], axis=0)
    x3, h2, logits = _odd_out(1, attn_t, z, gb, cw, od_w_out[0].astype(BF16), x2, mod, norm2[1:2], wrh, wrl)
    moe1 = _route_and_moe(h2, logits, wg[1], wu[1], wd[1])

    y = _final(1, x3, moe1, mod, final_norm[None])
    y_prompt = y[:nctx].reshape(BATCH, SEQ, D_MODEL)
    y_sample = y[nctx:].reshape(DEC_BATCH, DEC_SEQ, D_MODEL)
    return y_prompt, y_sample, new_k, new_v, new_ckv, new_kr
```

```python
import functools
import math

import jax
import jax.numpy as jnp
from jax import lax
from jax.experimental import pallas as pl
from jax.experimental.pallas import tpu as pltpu

F32 = jnp.float32
BF16 = jnp.bfloat16

D_MODEL = 1024
BATCH = 16
SEQ = 256
DEC_BATCH = 4
DEC_SEQ = 4096
PAST_LEN = 512
GRID_W = 64
ROPE_THETA = 10000.0
EPS = 1e-6
HEAD_DIM = 64
CHUNK = 128
A_GROUPS = 8
A_WIDTH = 512
B_HEADS = 8
B_KV_HEADS = 2
B_WIDTH = 512
C_HEADS = 8
C_NOPE = 64
C_ROPE = 32
C_V = 64
C_Q_LORA = 256
C_KV_LORA = 128
D_WIDTH = 512
N_EXPERTS = 16
EXPERT_FF = 512
EC_CAPACITY = 2

LOG2E = math.log2(math.e)
LANES = 128
BF16_ROWS = 16
ATT_TK = 256
TILE = 256
NT_CTX = BATCH * SEQ // TILE
NT_SEQ = DEC_SEQ // TILE
NT_LAT = DEC_BATCH * NT_SEQ
NT = NT_CTX + NT_LAT
N_TOK = NT * TILE
GROUP = 4096
N_GROUPS = N_TOK // GROUP
GROUP_TILES = GROUP // TILE
GROUP_CAP = EC_CAPACITY * GROUP // N_EXPERTS
CHUNK_ROWS = BF16_ROWS
STACK_BLK = 256
MAX_STACK_BLKS = N_EXPERTS * TILE // STACK_BLK
FFN_BLK = 256
LIST_MAX = -(-(GROUP_CAP + GROUP_TILES * (CHUNK_ROWS - 1)) // FFN_BLK) * FFN_BLK
DUMP_ROW = N_EXPERTS * LIST_MAX
ZERO_ROW = DUMP_ROW + CHUNK_ROWS
LIST_ROWS = ZERO_ROW + CHUNK_ROWS
VMEM_LIMIT = 56 * 1024 * 1024


def _cparams(sem):
    return pltpu.CompilerParams(dimension_semantics=sem, vmem_limit_bytes=VMEM_LIMIT)


def _dot(a, b):
    return jnp.dot(a, b, preferred_element_type=F32)


def _silu(x):
    return x / (1.0 + jnp.exp(-x))


def _mod_row(i):
    return jnp.where(i < NT_CTX, 0, 1 + (i - NT_CTX) // NT_SEQ)


def _tab_blk(i):
    return jnp.where(i < NT_CTX, 0, 1 + (i - NT_CTX) % NT_SEQ)


def _mod_kernel(c_ref, w_ref, b_ref, o_ref):
    s = _silu(c_ref[...])
    o_ref[0] = _dot(s.astype(BF16), w_ref[0].astype(BF16)) + b_ref[0]


def _modulation(cvec, w_mod, b_mod):
    depth = w_mod.shape[0]
    nchunk = 6
    return pl.pallas_call(
        _mod_kernel,
        grid=(depth, nchunk),
        in_specs=[
            pl.BlockSpec((8, D_MODEL), lambda l, k: (0, 0)),
            pl.BlockSpec((1, D_MODEL, D_MODEL), lambda l, k: (l, 0, k)),
            pl.BlockSpec((1, 1, D_MODEL), lambda l, k: (l, 0, k)),
        ],
        out_specs=pl.BlockSpec((1, 8, D_MODEL), lambda l, k: (l, 0, k)),
        out_shape=jax.ShapeDtypeStruct((depth, 8, 6 * D_MODEL), F32),
        compiler_params=_cparams(("arbitrary", "arbitrary")),
        name="modulation",
    )(cvec, w_mod, b_mod.reshape(depth, 1, 6 * D_MODEL))


def _norm_mod(x, gain, scale, shift):
    ms = jnp.mean(x * x, axis=-1, keepdims=True)
    return (x * lax.rsqrt(ms + EPS) * gain) * (1.0 + scale) + shift


def _seg_mean_sq(z, bd, width):
    zz = z * z
    hi = zz.astype(BF16)
    lo = (zz - hi.astype(F32)).astype(BF16)
    return (_dot(hi, bd) + _dot(lo, bd)) * (1.0 / width)


def _rope(z, cos, sin_signed, half):
    w = z.shape[1]
    lane = lax.broadcasted_iota(jnp.int32, z.shape, 1)
    first = (lane % (2 * half)) < half
    partner = jnp.where(first, pltpu.roll(z, w - half, 1), pltpu.roll(z, half, 1))
    return z * cos + partner * sin_signed


def _tile_lanes(t, n):
    return jnp.concatenate([t] * n, axis=1) if n > 1 else t


def _even_in_kernel(xc_ref, xl_ref, mod_ref, n1_ref, w_ref, qg_ref, kg_ref, cos_ref, sin_ref, ws_ref, bs_ref, bd_ref,
                    gated_ref, qt_ref, k_ref, kb_ref, v_ref, vt_ref):
    x = _group_tile(pl.program_id(0), xc_ref, xl_ref)
    h = _norm_mod(x, n1_ref[...], mod_ref[0, 0, 1:2, :], mod_ref[0, 0, 0:1, :])
    p = _dot(h.astype(BF16), w_ref[...])
    u = p[:, 0:512]
    va = p[:, 512:1024].astype(BF16)
    q = p[:, 1024:1536]
    k = p[:, 1536:1664]
    v = p[:, 1664:1792]
    bd = bd_ref[...]
    cos = cos_ref[0]
    sin = sin_ref[0]
    qn = q * lax.rsqrt(_seg_mean_sq(q, bd, HEAD_DIM) + EPS) * qg_ref[...]
    kn = k * lax.rsqrt(_seg_mean_sq(k, bd[0:LANES, 0:LANES], HEAD_DIM) + EPS) * kg_ref[...]
    qr = _rope(qn, _tile_lanes(cos, 4), _tile_lanes(sin, 4), HEAD_DIM // 4)
    kr = _rope(kn, cos, sin, HEAD_DIM // 4)
    qt_ref[...] = (qr * (HEAD_DIM ** -0.5 * LOG2E)).T.astype(BF16)
    k_ref[...] = kr
    kb_ref[...] = kr.astype(BF16)
    v_ref[...] = v
    vt_ref[...] = v.T.astype(BF16)
    lane = lax.broadcasted_iota(jnp.int32, (CHUNK, LANES), 1)
    for ch in range(TILE // CHUNK):
        rows = slice(ch * CHUNK, (ch + 1) * CHUNK)
        cols = []
        for pair in range(A_GROUPS // 2):
            vp = va[rows, pair * LANES:(pair + 1) * LANES]
            r0 = _dot(ws_ref[2 * pair], vp)
            r1 = _dot(ws_ref[2 * pair + 1], vp)
            cols.append(jnp.where(lane < LANES // 2, r0, r1))
        s = jnp.concatenate(cols, axis=1) + bs_ref[...]
        gated_ref[rows, :] = (u[rows, :] * s).astype(BF16)


def _even_in(xc, xl, mod, n1, w_in, qg, kg, cos, sin, ws, bs_tab, bd):
    full = lambda shape: pl.BlockSpec(shape, lambda i: (0,) * len(shape))
    return pl.pallas_call(
        _even_in_kernel,
        grid=(NT,),
        in_specs=[
            pl.BlockSpec((TILE, D_MODEL), _CTX_ROW),
            pl.BlockSpec((TILE, D_MODEL), _LAT_ROW),
            pl.BlockSpec((1, 1, 6, D_MODEL), lambda i: (0, _mod_row(i), 0, 0)),
            full((1, D_MODEL)),
            full(w_in.shape),
            full((1, 512)),
            full((1, LANES)),
            pl.BlockSpec((1, TILE, LANES), lambda i: (_tab_blk(i), 0, 0)),
            pl.BlockSpec((1, TILE, LANES), lambda i: (_tab_blk(i), 0, 0)),
            full(ws.shape),
            full(bs_tab.shape),
            full(bd.shape),
        ],
        out_specs=[
            pl.BlockSpec((TILE, 512), lambda i: (i, 0)),
            pl.BlockSpec((512, TILE), lambda i: (0, i)),
            pl.BlockSpec((TILE, LANES), lambda i: (i, 0)),
            pl.BlockSpec((TILE, LANES), lambda i: (i, 0)),
            pl.BlockSpec((TILE, LANES), lambda i: (i, 0)),
            pl.BlockSpec((LANES, TILE), lambda i: (0, i)),
        ],
        out_shape=[
            jax.ShapeDtypeStruct((N_TOK, 512), BF16),
            jax.ShapeDtypeStruct((512, N_TOK), BF16),
            jax.ShapeDtypeStruct((N_TOK, LANES), F32),
            jax.ShapeDtypeStruct((N_TOK, LANES), BF16),
            jax.ShapeDtypeStruct((N_TOK, LANES), F32),
            jax.ShapeDtypeStruct((LANES, N_TOK), BF16),
        ],
        compiler_params=_cparams(("parallel",)),
        name="even_in",
    )(xc, xl, mod, n1, w_in, qg, kg, cos, sin, ws, bs_tab, bd)


def _odd_in_kernel(x_ref, mod_ref, n1_ref, w_ref, qa_ref, kva_ref, wuq_ref, cos_ref, sin_ref,
                   qnt_ref, qrt_ref, ckv_ref, kr_ref, krb_ref, gb_ref, z_ref):
    h = _norm_mod(x_ref[...], n1_ref[...], mod_ref[0, 0, 1:2, :], mod_ref[0, 0, 0:1, :])
    p = _dot(h.astype(BF16), w_ref[...])
    cq = p[:, 0:256]
    ckv = p[:, 256:384]
    kr = p[:, 384:512]
    gb_ref[...] = p[:, 512:1024]
    z_ref[...] = p[:, 1024:1536] * p[:, 1536:2048]
    cqn = cq * lax.rsqrt(jnp.mean(cq * cq, axis=-1, keepdims=True) + EPS) * qa_ref[...]
    q = _dot(cqn.astype(BF16), wuq_ref[...])
    scale = (C_NOPE + C_ROPE) ** -0.5 * LOG2E
    cos = cos_ref[0]
    sin = sin_ref[0]
    qnt_ref[...] = (q[:, 0:512] * scale).T.astype(BF16)
    qr = _rope(q[:, 512:768], _tile_lanes(cos, 2), _tile_lanes(sin, 2), C_ROPE // 4)
    qrt_ref[...] = (qr * scale).T.astype(BF16)
    ckv_ref[...] = ckv * lax.rsqrt(jnp.mean(ckv * ckv, axis=-1, keepdims=True) + EPS) * kva_ref[...]
    krr = _rope(kr, cos, sin, C_ROPE // 4)
    kr_ref[...] = krr
    krb_ref[...] = krr.astype(BF16)


def _odd_in(x, mod, n1, w_in, qa, kva, wuq, cos, sin):
    full = lambda shape: pl.BlockSpec(shape, lambda i: (0,) * len(shape))
    tile = lambda w: pl.BlockSpec((TILE, w), lambda i: (i, 0))
    return pl.pallas_call(
        _odd_in_kernel,
        grid=(NT,),
        in_specs=[
            tile(D_MODEL),
            pl.BlockSpec((1, 1, 6, D_MODEL), lambda i: (1, _mod_row(i), 0, 0)),
            full((1, D_MODEL)),
            full(w_in.shape),
            full((1, C_Q_LORA)),
            full((1, C_KV_LORA)),
            full(wuq.shape),
            pl.BlockSpec((1, TILE, LANES), lambda i: (_tab_blk(i), 0, 0)),
            pl.BlockSpec((1, TILE, LANES), lambda i: (_tab_blk(i), 0, 0)),
        ],
        out_specs=[pl.BlockSpec((512, TILE), lambda i: (0, i)),
                   pl.BlockSpec((256, TILE), lambda i: (0, i)),
                   tile(LANES), tile(LANES), tile(LANES), tile(512), tile(512)],
        out_shape=[
            jax.ShapeDtypeStruct((512, N_TOK), BF16),
            jax.ShapeDtypeStruct((256, N_TOK), BF16),
            jax.ShapeDtypeStruct((N_TOK, LANES), F32),
            jax.ShapeDtypeStruct((N_TOK, LANES), F32),
            jax.ShapeDtypeStruct((N_TOK, LANES), BF16),
            jax.ShapeDtypeStruct((N_TOK, 512), F32),
            jax.ShapeDtypeStruct((N_TOK, 512), F32),
        ],
        compiler_params=_cparams(("parallel",)),
        name="odd_in",
    )(x, mod, n1, w_in, qa, kva, wuq, cos, sin)


def _kvup_kernel(c_ref, w_ref, kn_ref, vt_ref):
    p = _dot(c_ref[...].astype(BF16), w_ref[...])
    kn_ref[...] = p[:, 0:512].astype(BF16)
    vt_ref[...] = p[:, 512:1024].T.astype(BF16)


def _kvup(ckv_all, w_ukv):
    n = ckv_all.shape[0]
    return pl.pallas_call(
        _kvup_kernel,
        grid=(n // TILE,),
        in_specs=[pl.BlockSpec((TILE, C_KV_LORA), lambda i: (i, 0)),
                  pl.BlockSpec(w_ukv.shape, lambda i: (0, 0))],
        out_specs=[pl.BlockSpec((TILE, 512), lambda i: (i, 0)), pl.BlockSpec((512, TILE), lambda i: (0, i))],
        out_shape=[jax.ShapeDtypeStruct((n, 512), BF16), jax.ShapeDtypeStruct((512, n), BF16)],
        compiler_params=_cparams(("parallel",)),
        name="mla_kv_up",
    )(ckv_all, w_ukv)


def _softmax_pv(qc, key_chunk, vt_chunks, nk):
    w = qc.shape[1]
    m = jnp.full((1, w), -jnp.inf, F32)
    acc = jnp.zeros((HEAD_DIM + BF16_ROWS, w), F32)
    ones = jnp.ones((BF16_ROWS, ATT_TK), BF16)
    s_next = _dot(key_chunk(0), qc)
    for c in range(nk):
        s = s_next
        if c + 1 < nk:
            s_next = _dot(key_chunk(c + 1), qc)
        mn = jnp.maximum(m, jnp.max(s, axis=0, keepdims=True))
        p = jnp.exp2(s - mn).astype(BF16)
        vts = vt_chunks(c)
        if len(vts) == 1:
            pv = _dot(jnp.concatenate([vts[0], ones], axis=0), p)
        else:
            tq = w // len(vts)
            pv = jnp.concatenate([_dot(jnp.concatenate([vt, ones], axis=0), p[:, i * tq:(i + 1) * tq])
                                  for i, vt in enumerate(vts)], axis=1)
        acc = jnp.exp2(m - mn) * acc + pv
        m = mn
    return acc[0:HEAD_DIM, :] / acc[HEAD_DIM:HEAD_DIM + 1, :]


def _gqa_kernel(q_ref, k_ref, vt_ref, o_ref):
    kvh = pl.program_id(1)
    nk = k_ref.shape[0] // ATT_TK
    tq = q_ref.shape[1]
    n = B_HEADS // B_KV_HEADS
    cols = []
    for hh in range(n):
        q = q_ref[hh * HEAD_DIM:(hh + 1) * HEAD_DIM, :]
        z = jnp.zeros_like(q)
        cols.append(jnp.where(kvh == 0, jnp.concatenate([q, z], axis=0), jnp.concatenate([z, q], axis=0)))
    o = _softmax_pv(jnp.concatenate(cols, axis=1), lambda c: k_ref[c * ATT_TK:(c + 1) * ATT_TK, :],
                    lambda c: [vt_ref[:, c * ATT_TK:(c + 1) * ATT_TK]], nk)
    for hh in range(n):
        o_ref[hh * HEAD_DIM:(hh + 1) * HEAD_DIM, :] = o[:, hh * tq:(hh + 1) * tq].astype(BF16)


def _mla_kernel(qn_ref, qr_ref, kn_ref, kr_ref, vt_ref, o_ref):
    nk = kn_ref.shape[0] // ATT_TK
    tq = qn_ref.shape[1]
    z = jnp.zeros((C_NOPE, tq), BF16)
    zr = jnp.zeros((LANES - C_ROPE, tq), BF16)
    cols = []
    for hh in range(2):
        qn = qn_ref[hh * C_NOPE:(hh + 1) * C_NOPE, :]
        qr = qr_ref[hh * C_ROPE:(hh + 1) * C_ROPE, :]
        cols.append(jnp.concatenate(([qn, z] if hh == 0 else [z, qn]) + [qr, zr], axis=0))
    o = _softmax_pv(
        jnp.concatenate(cols, axis=1),
        lambda c: jnp.concatenate([kn_ref[c * ATT_TK:(c + 1) * ATT_TK, :],
                                   kr_ref[c * ATT_TK:(c + 1) * ATT_TK, :]], axis=1),
        lambda c: [vt_ref[hh * C_V:(hh + 1) * C_V, c * ATT_TK:(c + 1) * ATT_TK] for hh in range(2)], nk)
    for hh in range(2):
        o_ref[hh * C_V:(hh + 1) * C_V, :] = o[:, hh * tq:(hh + 1) * tq].astype(BF16)


def _attn_call(body, grid, in_specs, out_spec, args, n_tiles):
    return pl.pallas_call(
        body,
        grid=grid,
        in_specs=in_specs,
        out_specs=out_spec,
        out_shape=jax.ShapeDtypeStruct((512, n_tiles * TILE), BF16),
        compiler_params=_cparams(("parallel",) * len(grid)),
        name="attention",
    )(*args)


def _gqa_attention(qt, kb, vt, k_lat, vt_lat):
    s = k_lat.shape[1]
    rows = (B_HEADS // B_KV_HEADS) * HEAD_DIM
    o_ctx = _attn_call(
        _gqa_kernel, (BATCH, B_KV_HEADS),
        [pl.BlockSpec((rows, TILE), lambda b, h: (h, b)),
         pl.BlockSpec((None, TILE, LANES), lambda b, h: (b, 0, 0)),
         pl.BlockSpec((HEAD_DIM, TILE), lambda b, h: (h, b))],
        pl.BlockSpec((rows, TILE), lambda b, h: (h, b)),
        [qt, kb.reshape(NT, TILE, LANES), vt], NT_CTX)
    o_lat = _attn_call(
        _gqa_kernel, (DEC_BATCH, B_KV_HEADS, NT_SEQ),
        [pl.BlockSpec((rows, TILE), lambda b, h, j: (h, NT_CTX + b * NT_SEQ + j)),
         pl.BlockSpec((None, s, LANES), lambda b, h, j: (b, 0, 0)),
         pl.BlockSpec((None, HEAD_DIM, s), lambda b, h, j: (b, h, 0))],
        pl.BlockSpec((rows, TILE), lambda b, h, j: (h, b * NT_SEQ + j)),
        [qt, k_lat, vt_lat], NT_LAT)
    return o_ctx, o_lat


def _mla_attention(qnt, qrt, kn_c, krb, vt_c, kn_l, kr_l, vt_l):
    s = kr_l.shape[1]
    pairs = C_HEADS // 2
    o_ctx = _attn_call(
        _mla_kernel, (BATCH, pairs),
        [pl.BlockSpec((2 * C_NOPE, TILE), lambda b, p: (p, b)),
         pl.BlockSpec((2 * C_ROPE, TILE), lambda b, p: (p, b)),
         pl.BlockSpec((TILE, LANES), lambda b, p: (b, p)),
         pl.BlockSpec((TILE, LANES), lambda b, p: (b, 0)),
         pl.BlockSpec((2 * C_V, TILE), lambda b, p: (p, b))],
        pl.BlockSpec((2 * C_V, TILE), lambda b, p: (p, b)),
        [qnt, qrt, kn_c, krb, vt_c], NT_CTX)
    o_lat = _attn_call(
        _mla_kernel, (DEC_BATCH, pairs, NT_SEQ),
        [pl.BlockSpec((2 * C_NOPE, TILE), lambda b, p, j: (p, NT_CTX + b * NT_SEQ + j)),
         pl.BlockSpec((2 * C_ROPE, TILE), lambda b, p, j: (p, NT_CTX + b * NT_SEQ + j)),
         pl.BlockSpec((s, LANES), lambda b, p, j: (b, p)),
         pl.BlockSpec((None, s, LANES), lambda b, p, j: (b, 0, 0)),
         pl.BlockSpec((2 * C_V, s), lambda b, p, j: (p, b))],
        pl.BlockSpec((2 * C_V, TILE), lambda b, p, j: (p, b * NT_SEQ + j)),
        [qnt, qrt, kn_l, kr_l, vt_l], NT_LAT)
    return o_ctx, o_lat


def _group_tile(i, ctx_ref, lat_ref):
    return jnp.where(i < NT_CTX, ctx_ref[...], lat_ref[...])


_CTX_COL = lambda i: (0, jnp.minimum(i, NT_CTX - 1))
_LAT_COL = lambda i: (0, jnp.maximum(i - NT_CTX, 0))
_CTX_ROW = lambda i: (jnp.minimum(i, NT_CTX - 1), 0)
_LAT_ROW = lambda i: (jnp.maximum(i - NT_CTX, 0), 0)


def _finish_out(o, x_in, mod_ref, n2_ref, wrh_ref, wrl_ref, xo_ref, h2_ref, lg_ref):
    x = x_in + mod_ref[0, 0, 2:3, :] * o
    xo_ref[...] = x
    h2 = _norm_mod(x, n2_ref[...], mod_ref[0, 0, 4:5, :], mod_ref[0, 0, 3:4, :])
    hi = h2.astype(BF16)
    lo = (h2 - hi.astype(F32)).astype(BF16)
    h2_ref[...] = hi
    lg_ref[...] = _dot(hi, wrh_ref[...]) + _dot(lo, wrh_ref[...]) + _dot(hi, wrl_ref[...])


def _dot_t(at, b):
    return lax.dot_general(at, b, (((0,), (0,)), ((), ())), preferred_element_type=F32)


def _even_out_kernel(a_ref, btc_ref, btl_ref, w_ref, xc_ref, xl_ref, mod_ref, n2_ref, wrh_ref, wrl_ref,
                     xo_ref, h2_ref, lg_ref):
    i = pl.program_id(0)
    bt = _group_tile(i, btc_ref, btl_ref)
    o = _dot(a_ref[...], w_ref[0:512, :]) + _dot_t(bt, w_ref[512:1024, :])
    _finish_out(o, _group_tile(i, xc_ref, xl_ref), mod_ref, n2_ref, wrh_ref, wrl_ref, xo_ref, h2_ref, lg_ref)


def _odd_out_kernel(atc_ref, atl_ref, z_ref, zp_ref, zn_ref, gb_ref, cw_ref, w_ref, x_ref, mod_ref, n2_ref,
                    wrh_ref, wrl_ref, xo_ref, h2_ref, lg_ref):
    i = pl.program_id(0)
    at = _group_tile(i, atc_ref, atl_ref)
    j = (i - NT_CTX) % NT_SEQ
    first = jnp.logical_or(i < NT_CTX, j == 0)
    last = jnp.logical_or(i < NT_CTX, j == NT_SEQ - 1)
    z = z_ref[...]
    row = lax.broadcasted_iota(jnp.int32, z.shape, 0)
    halo_p = jnp.where(first, 0.0, zp_ref[7:8, :])
    halo_n = jnp.where(last, 0.0, zn_ref[0:1, :])
    zprev = jnp.where(row == 0, halo_p, pltpu.roll(z, 1, 0))
    znext = jnp.where(row == TILE - 1, halo_n, pltpu.roll(z, TILE - 1, 0))
    y = zprev * cw_ref[0:1, :] + z * cw_ref[1:2, :] + znext * cw_ref[2:3, :]
    d = (gb_ref[...] * y).astype(BF16)
    o = _dot_t(at, w_ref[0:512, :]) + _dot(d, w_ref[512:1024, :])
    _finish_out(o, x_ref[...], mod_ref, n2_ref, wrh_ref, wrl_ref, xo_ref, h2_ref, lg_ref)


_OUT_SHAPES = [
    jax.ShapeDtypeStruct((N_TOK, D_MODEL), F32),
    jax.ShapeDtypeStruct((N_TOK, D_MODEL), BF16),
    jax.ShapeDtypeStruct((N_TOK, LANES), F32),
]


def _even_out(layer, a, b, w_out, xc, xl, mod, n2, wrh, wrl):
    full = lambda shape: pl.BlockSpec(shape, lambda i: (0,) * len(shape))
    tile = lambda w: pl.BlockSpec((TILE, w), lambda i: (i, 0))
    return pl.pallas_call(
        _even_out_kernel,
        grid=(NT,),
        in_specs=[tile(512), pl.BlockSpec((512, TILE), _CTX_COL), pl.BlockSpec((512, TILE), _LAT_COL),
                  full(w_out.shape), pl.BlockSpec((TILE, D_MODEL), _CTX_ROW), pl.BlockSpec((TILE, D_MODEL), _LAT_ROW),
                  pl.BlockSpec((1, 1, 6, D_MODEL), lambda i: (layer, _mod_row(i), 0, 0)),
                  full((1, D_MODEL)), full(wrh.shape), full(wrl.shape)],
        out_specs=[tile(D_MODEL), tile(D_MODEL), tile(LANES)],
        out_shape=_OUT_SHAPES,
        compiler_params=_cparams(("parallel",)),
        name="even_out",
    )(a, b[0], b[1], w_out, xc, xl, mod, n2, wrh, wrl)


def _odd_out(layer, a, z, gb, cw, w_out, x, mod, n2, wrh, wrl):
    full = lambda shape: pl.BlockSpec(shape, lambda i: (0,) * len(shape))
    tile = lambda w: pl.BlockSpec((TILE, w), lambda i: (i, 0))
    rb = TILE // 8
    return pl.pallas_call(
        _odd_out_kernel,
        grid=(NT,),
        in_specs=[pl.BlockSpec((512, TILE), _CTX_COL), pl.BlockSpec((512, TILE), _LAT_COL), tile(512),
                  pl.BlockSpec((8, 512), lambda i: (jnp.maximum(i * rb - 1, 0), 0)),
                  pl.BlockSpec((8, 512), lambda i: (jnp.minimum(i * rb + rb, NT * rb - 1), 0)),
                  tile(512), full(cw.shape), full(w_out.shape), tile(D_MODEL),
                  pl.BlockSpec((1, 1, 6, D_MODEL), lambda i: (layer, _mod_row(i), 0, 0)),
                  full((1, D_MODEL)), full(wrh.shape), full(wrl.shape)],
        out_specs=[tile(D_MODEL), tile(D_MODEL), tile(LANES)],
        out_shape=_OUT_SHAPES,
        compiler_params=_cparams(("parallel",)),
        name="odd_out",
    )(a[0], a[1], z, z, z, gb, cw, w_out, x, mod, n2, wrh, wrl)


def _excl_prefix(mask_f, tri):
    t = mask_f.shape[0]
    carry = jnp.zeros((1, LANES), F32)
    outs = []
    for blk in range(t // TILE):
        m = mask_f[blk * TILE:(blk + 1) * TILE]
        outs.append(_dot(tri, m.astype(BF16)) + carry)
        carry = carry + jnp.sum(m, axis=0, keepdims=True)
    return jnp.concatenate(outs, axis=0) if len(outs) > 1 else outs[0]


def _route_kernel(lg_ref, pos_ref, gate_ref, *, cap, per_domain_base):
    lg = lg_ref[...]
    t = lg.shape[0]
    lane = lax.broadcasted_iota(jnp.int32, lg.shape, 1)
    valid = lane < N_EXPERTS
    mx = jnp.max(jnp.where(valid, lg, -jnp.inf), axis=1, keepdims=True)
    ex = jnp.where(valid, jnp.exp(lg - mx), 0.0)
    aff = ex / jnp.sum(ex, axis=1, keepdims=True)
    thr_bits = jnp.zeros((1, LANES), jnp.int32)
    for bit in range(30, -1, -1):
        cand = thr_bits | (1 << bit)
        cnt = jnp.sum((aff >= lax.bitcast_convert_type(cand, F32)).astype(F32), axis=0, keepdims=True)
        thr_bits = jnp.where(cnt >= cap, cand, thr_bits)
    thr = lax.bitcast_convert_type(thr_bits, F32)
    gt = aff > thr
    eq = aff == thr
    need = cap - jnp.sum(gt.astype(F32), axis=0, keepdims=True)
    r = lax.broadcasted_iota(jnp.int32, (TILE, TILE), 0)
    c = lax.broadcasted_iota(jnp.int32, (TILE, TILE), 1)
    tri = (c < r).astype(BF16)
    eq_rank = _excl_prefix(eq.astype(F32), tri)
    sel = jnp.logical_and(valid, jnp.logical_or(gt, jnp.logical_and(eq, eq_rank < need)))
    pos = _excl_prefix(sel.astype(F32), tri)
    if per_domain_base:
        pos = pos + (pl.program_id(0) * cap).astype(F32)
    pos_ref[...] = jnp.where(sel, pos, -1.0)
    gate_ref[...] = jnp.where(sel, aff, 0.0)


def _route(logits, t_dom, cap, per_domain_base):
    n = logits.shape[0]
    return pl.pallas_call(
        functools.partial(_route_kernel, cap=cap, per_domain_base=per_domain_base),
        grid=(n // t_dom,),
        in_specs=[pl.BlockSpec((t_dom, LANES), lambda i: (i, 0))],
        out_specs=[pl.BlockSpec((t_dom, LANES), lambda i: (i, 0)), pl.BlockSpec((t_dom, LANES), lambda i: (i, 0))],
        out_shape=[jax.ShapeDtypeStruct((n, LANES), F32), jax.ShapeDtypeStruct((n, LANES), F32)],
        compiler_params=_cparams(("parallel",)),
        name="route",
    )(logits)


def _moe_kernel(nblk_ref, ctab_ref, etab_ref, llen_ref,
                x_ref, pos_ref, gate_ref, wg_ref, wu_ref, wd_ref, xres_ref, mod_ref, fn_ref, out_ref,
                lists_ref, gl_ref, s_ref, gs_ref, acc_ref, y_ref, *, final):
    g = pl.program_id(0)
    s = pl.program_id(1)
    chunks = STACK_BLK // CHUNK_ROWS
    rows16 = lax.broadcasted_iota(jnp.int32, (CHUNK_ROWS, TILE), 0)

    @pl.when(jnp.logical_and(g == 0, s == 0))
    def _():
        lists_ref[ZERO_ROW:ZERO_ROW + CHUNK_ROWS, :] = jnp.zeros((CHUNK_ROWS, D_MODEL), BF16)

    def chunk_rows(tbase, q, unused_row):
        d = ctab_ref[tbase + q]
        return pl.multiple_of(jnp.where(d >= 0, d, unused_row), CHUNK_ROWS)

    def build_block(j, rb, with_gate):
        tbase = ((g * GROUP_TILES + j) * MAX_STACK_BLKS + rb) * chunks
        for q in range(chunks):
            d = ctab_ref[tbase + q]
            e = etab_ref[tbase + q]
            posrow = pos_ref[0, j, pl.ds(e, 1), :]
            rowid = (rows16 + (rb * STACK_BLK + q * CHUNK_ROWS)).astype(F32)
            hit = jnp.logical_and(posrow == rowid, d >= 0)
            s_ref[q * CHUNK_ROWS:(q + 1) * CHUNK_ROWS, :] = hit.astype(BF16)
            if with_gate:
                gaterow = gate_ref[0, j, pl.ds(e, 1), :]
                gs_ref[q * CHUNK_ROWS:(q + 1) * CHUNK_ROWS, :] = jnp.where(hit, gaterow, 0.0)
        return tbase

    @pl.when(s < GROUP_TILES)
    def _gather():
        j = s

        def blk(rb, _):
            tbase = build_block(j, rb, True)
            acc_ref[...] = _dot(s_ref[...], x_ref[...])
            gcol = jnp.sum(gs_ref[...], axis=1, keepdims=True)
            for q in range(chunks):
                d = chunk_rows(tbase, q, DUMP_ROW)
                rows = slice(q * CHUNK_ROWS, (q + 1) * CHUNK_ROWS)
                lists_ref[pl.ds(d, CHUNK_ROWS), :] = acc_ref[rows, :].astype(BF16)
                gl_ref[pl.ds(d, CHUNK_ROWS), :] = gcol[rows, :]
            return 0

        lax.fori_loop(0, nblk_ref[g * GROUP_TILES + j], blk, 0)

    @pl.when(jnp.logical_and(s >= GROUP_TILES, s < 2 * GROUP_TILES))
    def _experts():
        e = s - GROUP_TILES
        ln = llen_ref[g * N_EXPERTS + e]
        nch = (ln + FFN_BLK - 1) // FFN_BLK
        base = e * LIST_MAX

        def zero_tail(k, _):
            r0 = pl.multiple_of(base + ln + k * CHUNK_ROWS, CHUNK_ROWS)
            lists_ref[pl.ds(r0, CHUNK_ROWS), :] = jnp.zeros((CHUNK_ROWS, D_MODEL), BF16)
            gl_ref[pl.ds(r0, CHUNK_ROWS), :] = jnp.zeros((CHUNK_ROWS, 1), F32)
            return 0

        lax.fori_loop(0, (nch * FFN_BLK - ln) // CHUNK_ROWS, zero_tail, 0)

        def ffn(c, _):
            r0 = pl.multiple_of(base + c * FFN_BLK, FFN_BLK)
            xs = lists_ref[pl.ds(r0, FFN_BLK), :]
            hid = _silu(_dot(xs, wg_ref[0])) * _dot(xs, wu_ref[0])
            y = _dot(hid.astype(BF16), wd_ref[0]) * gl_ref[pl.ds(r0, FFN_BLK), :]
            lists_ref[pl.ds(r0, FFN_BLK), :] = y.astype(BF16)
            return 0

        lax.fori_loop(0, nch, ffn, 0)

    @pl.when(s >= 2 * GROUP_TILES)
    def _scatter():
        j = s - 2 * GROUP_TILES
        acc_ref[...] = jnp.zeros_like(acc_ref)

        def blk(rb, _):
            tbase = build_block(j, rb, False)
            for q in range(chunks):
                d = chunk_rows(tbase, q, ZERO_ROW)
                y_ref[q * CHUNK_ROWS:(q + 1) * CHUNK_ROWS, :] = lists_ref[pl.ds(d, CHUNK_ROWS), :]
            acc_ref[...] += _dot_t(s_ref[...], y_ref[...])
            return 0

        lax.fori_loop(0, nblk_ref[g * GROUP_TILES + j], blk, 0)
        x = xres_ref[...] + mod_ref[0, 0, 5:6, :] * acc_ref[...]
        if final:
            x = x * lax.rsqrt(jnp.mean(x * x, axis=-1, keepdims=True) + EPS) * fn_ref[...]
        out_ref[...] = x


def _moe(layer, final, tables, h2, pos_t, gate_t, wg, wu, wd, xres, mod, fn):
    gt = GROUP_TILES
    tile_a = lambda g, s, *_: (g * gt + jnp.minimum(s, gt - 1), 0)
    tile_c = lambda g, s, *_: (g * gt + jnp.clip(s - 2 * gt, 0, gt - 1), 0)
    expert = lambda g, s, *_: (jnp.clip(s - gt, 0, N_EXPERTS - 1), 0, 0)
    group = lambda g, s, *_: (g, 0, 0, 0)
    grid_spec = pltpu.PrefetchScalarGridSpec(
        num_scalar_prefetch=4,
        grid=(N_GROUPS, 3 * gt),
        in_specs=[
            pl.BlockSpec((TILE, D_MODEL), tile_a),
            pl.BlockSpec((1, gt, N_EXPERTS, TILE), group),
            pl.BlockSpec((1, gt, N_EXPERTS, TILE), group),
            pl.BlockSpec((1, D_MODEL, EXPERT_FF), expert),
            pl.BlockSpec((1, D_MODEL, EXPERT_FF), expert),
            pl.BlockSpec((1, EXPERT_FF, D_MODEL), expert),
            pl.BlockSpec((TILE, D_MODEL), tile_c),
            pl.BlockSpec((1, 1, 6, D_MODEL), lambda g, s, *_: (layer, g, 0, 0)),
            pl.BlockSpec((1, D_MODEL), lambda g, s, *_: (0, 0)),
        ],
        out_specs=pl.BlockSpec((TILE, D_MODEL), tile_c),
        scratch_shapes=[
            pltpu.VMEM((LIST_ROWS, D_MODEL), BF16),
            pltpu.VMEM((LIST_ROWS, 1), F32),
            pltpu.VMEM((STACK_BLK, TILE), BF16),
            pltpu.VMEM((STACK_BLK, TILE), F32),
            pltpu.VMEM((STACK_BLK, D_MODEL), F32),
            pltpu.VMEM((STACK_BLK, D_MODEL), BF16),
        ],
    )
    return pl.pallas_call(
        functools.partial(_moe_kernel, final=final),
        grid_spec=grid_spec,
        out_shape=jax.ShapeDtypeStruct((N_TOK, D_MODEL), F32),
        compiler_params=_cparams(("arbitrary", "arbitrary")),
        name="moe",
    )(*tables, h2, pos_t, gate_t, wg, wu, wd, xres, mod, fn)


def _rope_tables(rot_dim):
    axis_dim = rot_dim // 2
    tok = jnp.arange(DEC_SEQ)
    rows = (tok // GRID_W).astype(F32)
    cols = (tok % GRID_W).astype(F32)
    inv_freq = ROPE_THETA ** (-jnp.arange(0, axis_dim, 2, dtype=F32) / axis_dim)
    ar = rows[:, None] * inv_freq
    ac = cols[:, None] * inv_freq
    cos = jnp.concatenate([jnp.cos(ar), jnp.cos(ar), jnp.cos(ac), jnp.cos(ac)], axis=1)
    sin = jnp.concatenate([-jnp.sin(ar), jnp.sin(ar), -jnp.sin(ac), jnp.sin(ac)], axis=1)
    rep = LANES // rot_dim
    cos = jnp.tile(cos, (1, rep)).reshape(NT_SEQ, TILE, LANES)
    sin = jnp.tile(sin, (1, rep)).reshape(NT_SEQ, TILE, LANES)
    cos = jnp.concatenate([jnp.ones((1, TILE, LANES), F32), cos], axis=0)
    sin = jnp.concatenate([jnp.zeros((1, TILE, LANES), F32), sin], axis=0)
    return cos, sin


def _excl_cumsum(a, axis):
    return jnp.cumsum(a, axis=axis) - a


def _route_and_moe(layer, final, h2, logits, wg, wu, wd, xres, mod, fn):
    pos_c, gate_c = _route(logits[:NT_CTX * TILE], SEQ, EC_CAPACITY * SEQ // N_EXPERTS, True)
    pos_l, gate_l = _route(logits[NT_CTX * TILE:], DEC_SEQ, EC_CAPACITY * DEC_SEQ // N_EXPERTS, False)
    pos = jnp.concatenate([pos_c, pos_l], axis=0)[:, :N_EXPERTS]
    gate = jnp.concatenate([gate_c, gate_l], axis=0)[:, :N_EXPERTS]
    pos_t = pos.reshape(N_GROUPS, GROUP_TILES, TILE, N_EXPERTS).transpose(0, 1, 3, 2)
    gate_t = gate.reshape(N_GROUPS, GROUP_TILES, TILE, N_EXPERTS).transpose(0, 1, 3, 2)
    sel = pos_t >= 0
    cnt = jnp.sum(sel, axis=-1, dtype=jnp.int32)
    npad = (cnt + CHUNK_ROWS - 1) // CHUNK_ROWS * CHUNK_ROWS
    seg_off = _excl_cumsum(npad, 2)
    list_off = _excl_cumsum(npad, 1)
    rank0 = _excl_cumsum(cnt, 1)
    stack_pos = jnp.where(sel, pos_t - rank0[..., None].astype(F32) + seg_off[..., None].astype(F32), -1.0)
    nblk = (jnp.sum(npad, axis=2) + STACK_BLK - 1) // STACK_BLK
    llen = jnp.sum(npad, axis=1)
    r = (jnp.arange(MAX_STACK_BLKS * STACK_BLK // CHUNK_ROWS, dtype=jnp.int32) * CHUNK_ROWS)[None, None, None, :]
    in_seg = jnp.logical_and(r >= seg_off[..., None], r < (seg_off + npad)[..., None])
    eidx = jnp.arange(N_EXPERTS, dtype=jnp.int32)[None, None, :, None]
    dst = jnp.sum(jnp.where(in_seg, eidx * LIST_MAX + list_off[..., None] + r - seg_off[..., None], 0), axis=2)
    used = jnp.any(in_seg, axis=2)
    ctab = jnp.where(used, dst, -1).astype(jnp.int32)
    etab = jnp.sum(jnp.where(in_seg, eidx, 0), axis=2).astype(jnp.int32)
    tables = (nblk.reshape(-1).astype(jnp.int32), ctab.reshape(-1), etab.reshape(-1),
              llen.reshape(-1).astype(jnp.int32))
    return _moe(layer, final, tables, h2, stack_pos, gate_t, wg, wu, wd, xres, mod, fn)


def _split_router(w):
    wp = jnp.zeros((D_MODEL, LANES), F32).at[:, :N_EXPERTS].set(w)
    hi = wp.astype(BF16)
    lo = (wp - hi.astype(F32)).astype(BF16)
    return hi, lo


def kernel(x_prompt, x_sample, c, cache_attn_k, cache_attn_v, cache_mla_ckv, cache_mla_krope, c_ctx, w_mod, b_mod,
           norm1, norm2, ev_w_in, ev_q_norm, ev_k_norm, ev_w_s, ev_b_s, ev_w_out, od_w_in, od_q_a_norm,
           od_kv_a_norm, od_w_uq, od_w_ukv, od_conv_w, od_w_out, moe_router, moe_w_gate, moe_w_up, moe_w_down,
           final_norm):
    nctx = NT_CTX * TILE
    xc = x_prompt.reshape(nctx, D_MODEL)
    xl = x_sample.reshape(NT_LAT * TILE, D_MODEL)
    cvec = jnp.concatenate([c_ctx[None], c, jnp.zeros((3, D_MODEL), F32)], axis=0)
    mod = _modulation(cvec, w_mod, b_mod).reshape(2, 8, 6, D_MODEL)
    wg = moe_w_gate.astype(BF16)
    wu = moe_w_up.astype(BF16)
    wd = moe_w_down.astype(BF16)

    cos64, sin64 = _rope_tables(HEAD_DIM)
    seg = jnp.arange(512) // HEAD_DIM
    bd = (seg[:, None] == seg[None, :]).astype(BF16)
    bs_tab = jnp.repeat(ev_b_s[0].T, 64, axis=1)
    gated, qt, k, kb, v, vt = _even_in(
        xc, xl, mod, norm1[0:1], ev_w_in[0].astype(BF16), jnp.tile(ev_q_norm[0], 8)[None], jnp.tile(ev_k_norm[0], 2)[None],
        cos64, sin64, ev_w_s[0].astype(BF16), bs_tab, bd)
    new_k = k[:nctx].reshape(BATCH, SEQ, B_KV_HEADS, HEAD_DIM).transpose(0, 2, 1, 3)[:, None]
    new_v = v[:nctx].reshape(BATCH, SEQ, B_KV_HEADS, HEAD_DIM).transpose(0, 2, 1, 3)[:, None]
    ck = cache_attn_k[:, 0].transpose(0, 2, 1, 3).reshape(DEC_BATCH, PAST_LEN, LANES).astype(BF16)
    k_lat = jnp.concatenate([ck, kb[nctx:].reshape(DEC_BATCH, DEC_SEQ, LANES)], axis=1)
    cv = cache_attn_v[:, 0].transpose(0, 1, 3, 2).reshape(DEC_BATCH, LANES, PAST_LEN).astype(BF16)
    vt_lat = jnp.concatenate([cv, vt[:, nctx:].reshape(LANES, DEC_BATCH, DEC_SEQ).transpose(1, 0, 2)], axis=2)
    attn_t = _gqa_attention(qt, kb, vt, k_lat, vt_lat)
    wrh, wrl = _split_router(moe_router[0])
    x1, h2, logits = _even_out(0, gated, attn_t, ev_w_out[0].astype(BF16), xc, xl, mod, norm2[0:1], wrh, wrl)
    x2 = _route_and_moe(0, False, h2, logits, wg[0], wu[0], wd[0], x1, mod, final_norm[None])

    cos32, sin32 = _rope_tables(C_ROPE)
    w_in1 = od_w_in[0]
    w_in1 = jnp.concatenate([w_in1[:, :C_Q_LORA + C_KV_LORA + C_ROPE], jnp.zeros((D_MODEL, LANES - C_ROPE), F32),
                             w_in1[:, C_Q_LORA + C_KV_LORA + C_ROPE:]], axis=1).astype(BF16)
    wuq = od_w_uq[0].reshape(C_Q_LORA, C_HEADS, C_NOPE + C_ROPE)
    wuq = jnp.concatenate([wuq[:, :, :C_NOPE].reshape(C_Q_LORA, -1), wuq[:, :, C_NOPE:].reshape(C_Q_LORA, -1)],
                          axis=1).astype(BF16)
    qnt, qrt, ckv, kr, krb, gb, z = _odd_in(x2, mod, norm1[1:2], w_in1, od_q_a_norm[0][None],
                                            od_kv_a_norm[0][None], wuq, cos32, sin32)
    new_ckv = ckv[:nctx].reshape(BATCH, 1, SEQ, C_KV_LORA)
    new_kr = kr[:nctx, :C_ROPE].reshape(BATCH, 1, SEQ, C_ROPE)
    wukv = od_w_ukv[0].reshape(C_KV_LORA, C_HEADS, C_NOPE + C_V)
    wukv = jnp.concatenate([wukv[:, :, :C_NOPE].reshape(C_KV_LORA, -1), wukv[:, :, C_NOPE:].reshape(C_KV_LORA, -1)],
                           axis=1).astype(BF16)
    ckv_lat = jnp.concatenate([cache_mla_ckv[:, 0], ckv[nctx:].reshape(DEC_BATCH, DEC_SEQ, C_KV_LORA)], axis=1)
    kn_c, vt_c = _kvup(ckv[:nctx], wukv)
    kn_l, vt_l = _kvup(ckv_lat.reshape(-1, C_KV_LORA), wukv)
    ckr = jnp.pad(cache_mla_krope[:, 0], ((0, 0), (0, 0), (0, LANES - C_ROPE))).astype(BF16)
    kr_l = jnp.concatenate([ckr, krb[nctx:].reshape(DEC_BATCH, DEC_SEQ, LANES)], axis=1)
    attn_t = _mla_attention(qnt, qrt, kn_c, krb, vt_c, kn_l, kr_l, vt_l)
    wrh, wrl = _split_router(moe_router[1])
    cw = jnp.concatenate([od_conv_w[0], jnp.zeros((5, D_WIDTH), F32)], axis=0)
    x3, h2, logits = _odd_out(1, attn_t, z, gb, cw, od_w_out[0].astype(BF16), x2, mod, norm2[1:2], wrh, wrl)
    y = _route_and_moe(1, True, h2, logits, wg[1], wu[1], wd[1], x3, mod, final_norm[None])
    y_prompt = y[:nctx].reshape(BATCH, SEQ, D_MODEL)
    y_sample = y[nctx:].reshape(DEC_BATCH, DEC_SEQ, D_MODEL)
    return y_prompt, y_sample, new_k, new_v, new_ckv, new_kr
```

```python
import functools
import math

import jax
import jax.numpy as jnp
from jax import lax
from jax.experimental import pallas as pl
from jax.experimental.pallas import tpu as pltpu

F32 = jnp.float32
BF16 = jnp.bfloat16

D_MODEL = 1024
BATCH = 16
SEQ = 256
DEC_BATCH = 4
DEC_SEQ = 4096
PAST_LEN = 512
GRID_W = 64
ROPE_THETA = 10000.0
EPS = 1e-6
HEAD_DIM = 64
CHUNK = 128
A_GROUPS = 8
A_WIDTH = 512
B_HEADS = 8
B_KV_HEADS = 2
B_WIDTH = 512
C_HEADS = 8
C_NOPE = 64
C_ROPE = 32
C_V = 64
C_Q_LORA = 256
C_KV_LORA = 128
D_WIDTH = 512
N_EXPERTS = 16
EXPERT_FF = 512
EC_CAPACITY = 2

LOG2E = math.log2(math.e)
LANES = 128
BF16_ROWS = 16
ATT_TK = 256
MLA_PAIRS = 2
TILE = 256
NT_CTX = BATCH * SEQ // TILE
NT_SEQ = DEC_SEQ // TILE
NT_LAT = DEC_BATCH * NT_SEQ
NT = NT_CTX + NT_LAT
N_TOK = NT * TILE
GROUP = 4096
N_GROUPS = N_TOK // GROUP
GROUP_TILES = GROUP // TILE
GROUP_CAP = EC_CAPACITY * GROUP // N_EXPERTS
CHUNK_ROWS = BF16_ROWS
STACK_BLK = 256
MAX_STACK_BLKS = N_EXPERTS * TILE // STACK_BLK
FFN_BLK = 256
LIST_MAX = -(-(GROUP_CAP + GROUP_TILES * (CHUNK_ROWS - 1)) // FFN_BLK) * FFN_BLK
DUMP_ROW = N_EXPERTS * LIST_MAX
ZERO_ROW = DUMP_ROW + CHUNK_ROWS
LIST_ROWS = ZERO_ROW + CHUNK_ROWS
VMEM_LIMIT = 56 * 1024 * 1024


def _cparams(sem):
    return pltpu.CompilerParams(dimension_semantics=sem, vmem_limit_bytes=VMEM_LIMIT)


def _dot(a, b):
    return jnp.dot(a, b, preferred_element_type=F32)


def _silu(x):
    return x / (1.0 + jnp.exp(-x))


def _mod_row(i):
    return jnp.where(i < NT_CTX, 0, 1 + (i - NT_CTX) // NT_SEQ)


def _tab_blk(i):
    return jnp.where(i < NT_CTX, 0, 1 + (i - NT_CTX) % NT_SEQ)


def _mod_kernel(c_ref, w_ref, b_ref, o_ref):
    s = _silu(c_ref[...])
    o_ref[0] = _dot(s.astype(BF16), w_ref[0].astype(BF16)) + b_ref[0]


def _modulation(cvec, w_mod, b_mod):
    depth = w_mod.shape[0]
    nchunk = 6
    return pl.pallas_call(
        _mod_kernel,
        grid=(depth, nchunk),
        in_specs=[
            pl.BlockSpec((8, D_MODEL), lambda l, k: (0, 0)),
            pl.BlockSpec((1, D_MODEL, D_MODEL), lambda l, k: (l, 0, k)),
            pl.BlockSpec((1, 1, D_MODEL), lambda l, k: (l, 0, k)),
        ],
        out_specs=pl.BlockSpec((1, 8, D_MODEL), lambda l, k: (l, 0, k)),
        out_shape=jax.ShapeDtypeStruct((depth, 8, 6 * D_MODEL), F32),
        compiler_params=_cparams(("arbitrary", "arbitrary")),
        name="modulation",
    )(cvec, w_mod, b_mod.reshape(depth, 1, 6 * D_MODEL))


def _norm_mod(x, gain, scale, shift):
    ms = jnp.mean(x * x, axis=-1, keepdims=True)
    return (x * lax.rsqrt(ms + EPS) * gain) * (1.0 + scale) + shift


def _seg_mean_sq(z, bd, width):
    zz = z * z
    hi = zz.astype(BF16)
    lo = (zz - hi.astype(F32)).astype(BF16)
    return (_dot(hi, bd) + _dot(lo, bd)) * (1.0 / width)


def _rope(z, cos, sin_signed, half):
    w = z.shape[1]
    lane = lax.broadcasted_iota(jnp.int32, z.shape, 1)
    first = (lane % (2 * half)) < half
    partner = jnp.where(first, pltpu.roll(z, w - half, 1), pltpu.roll(z, half, 1))
    return z * cos + partner * sin_signed


def _tile_lanes(t, n):
    return jnp.concatenate([t] * n, axis=1) if n > 1 else t


def _even_in_kernel(xc_ref, xl_ref, mod_ref, n1_ref, w_ref, qg_ref, kg_ref, cos_ref, sin_ref, ws_ref, bs_ref, bd_ref,
                    gated_ref, qt_ref, k_ref, kb_ref, v_ref, vt_ref):
    x = _group_tile(pl.program_id(0), xc_ref, xl_ref)
    h = _norm_mod(x, n1_ref[...], mod_ref[0, 0, 1:2, :], mod_ref[0, 0, 0:1, :])
    p = _dot(h.astype(BF16), w_ref[...])
    u = p[:, 0:512]
    va = p[:, 512:1024].astype(BF16)
    q = p[:, 1024:1536]
    k = p[:, 1536:1664]
    v = p[:, 1664:1792]
    bd = bd_ref[...]
    cos = cos_ref[0]
    sin = sin_ref[0]
    qn = q * lax.rsqrt(_seg_mean_sq(q, bd, HEAD_DIM) + EPS) * qg_ref[...]
    kn = k * lax.rsqrt(_seg_mean_sq(k, bd[0:LANES, 0:LANES], HEAD_DIM) + EPS) * kg_ref[...]
    qr = _rope(qn, _tile_lanes(cos, 4), _tile_lanes(sin, 4), HEAD_DIM // 4)
    kr = _rope(kn, cos, sin, HEAD_DIM // 4)
    qt_ref[...] = (qr * (HEAD_DIM ** -0.5 * LOG2E)).T.astype(BF16)
    k_ref[...] = kr
    kb_ref[...] = kr.astype(BF16)
    v_ref[...] = v
    vt_ref[...] = v.T.astype(BF16)
    lane = lax.broadcasted_iota(jnp.int32, (CHUNK, LANES), 1)
    for ch in range(TILE // CHUNK):
        rows = slice(ch * CHUNK, (ch + 1) * CHUNK)
        cols = []
        for pair in range(A_GROUPS // 2):
            vp = va[rows, pair * LANES:(pair + 1) * LANES]
            r0 = _dot(ws_ref[2 * pair], vp)
            r1 = _dot(ws_ref[2 * pair + 1], vp)
            cols.append(jnp.where(lane < LANES // 2, r0, r1))
        s = jnp.concatenate(cols, axis=1) + bs_ref[...]
        gated_ref[rows, :] = (u[rows, :] * s).astype(BF16)


def _even_in(xc, xl, mod, n1, w_in, qg, kg, cos, sin, ws, bs_tab, bd):
    full = lambda shape: pl.BlockSpec(shape, lambda i: (0,) * len(shape))
    return pl.pallas_call(
        _even_in_kernel,
        grid=(NT,),
        in_specs=[
            pl.BlockSpec((TILE, D_MODEL), _CTX_ROW),
            pl.BlockSpec((TILE, D_MODEL), _LAT_ROW),
            pl.BlockSpec((1, 1, 6, D_MODEL), lambda i: (0, _mod_row(i), 0, 0)),
            full((1, D_MODEL)),
            full(w_in.shape),
            full((1, 512)),
            full((1, LANES)),
            pl.BlockSpec((1, TILE, LANES), lambda i: (_tab_blk(i), 0, 0)),
            pl.BlockSpec((1, TILE, LANES), lambda i: (_tab_blk(i), 0, 0)),
            full(ws.shape),
            full(bs_tab.shape),
            full(bd.shape),
        ],
        out_specs=[
            pl.BlockSpec((TILE, 512), lambda i: (i, 0)),
            pl.BlockSpec((512, TILE), lambda i: (0, i)),
            pl.BlockSpec((TILE, LANES), lambda i: (i, 0)),
            pl.BlockSpec((TILE, LANES), lambda i: (i, 0)),
            pl.BlockSpec((TILE, LANES), lambda i: (i, 0)),
            pl.BlockSpec((LANES, TILE), lambda i: (0, i)),
        ],
        out_shape=[
            jax.ShapeDtypeStruct((N_TOK, 512), BF16),
            jax.ShapeDtypeStruct((512, N_TOK), BF16),
            jax.ShapeDtypeStruct((N_TOK, LANES), F32),
            jax.ShapeDtypeStruct((N_TOK, LANES), BF16),
            jax.ShapeDtypeStruct((N_TOK, LANES), F32),
            jax.ShapeDtypeStruct((LANES, N_TOK), BF16),
        ],
        compiler_params=_cparams(("parallel",)),
        name="even_in",
    )(xc, xl, mod, n1, w_in, qg, kg, cos, sin, ws, bs_tab, bd)


def _odd_in_kernel(x_ref, mod_ref, n1_ref, w_ref, qa_ref, kva_ref, wuq_ref, cos_ref, sin_ref,
                   qnt_ref, qrt_ref, ckv_ref, kr_ref, krb_ref, gb_ref, z_ref):
    h = _norm_mod(x_ref[...], n1_ref[...], mod_ref[0, 0, 1:2, :], mod_ref[0, 0, 0:1, :])
    p = _dot(h.astype(BF16), w_ref[...])
    cq = p[:, 0:256]
    ckv = p[:, 256:384]
    kr = p[:, 384:512]
    gb_ref[...] = p[:, 512:1024]
    z_ref[...] = p[:, 1024:1536] * p[:, 1536:2048]
    cqn = cq * lax.rsqrt(jnp.mean(cq * cq, axis=-1, keepdims=True) + EPS) * qa_ref[...]
    q = _dot(cqn.astype(BF16), wuq_ref[...])
    scale = (C_NOPE + C_ROPE) ** -0.5 * LOG2E
    cos = cos_ref[0]
    sin = sin_ref[0]
    qnt_ref[...] = (q[:, 0:512] * scale).T.astype(BF16)
    qr = _rope(q[:, 512:768], _tile_lanes(cos, 2), _tile_lanes(sin, 2), C_ROPE // 4)
    qrt_ref[...] = (qr * scale).T.astype(BF16)
    ckv_ref[...] = ckv * lax.rsqrt(jnp.mean(ckv * ckv, axis=-1, keepdims=True) + EPS) * kva_ref[...]
    krr = _rope(kr, cos, sin, C_ROPE // 4)
    kr_ref[...] = krr
    krb_ref[...] = krr.astype(BF16)


def _odd_in(x, mod, n1, w_in, qa, kva, wuq, cos, sin):
    full = lambda shape: pl.BlockSpec(shape, lambda i: (0,) * len(shape))
    tile = lambda w: pl.BlockSpec((TILE, w), lambda i: (i, 0))
    return pl.pallas_call(
        _odd_in_kernel,
        grid=(NT,),
        in_specs=[
            tile(D_MODEL),
            pl.BlockSpec((1, 1, 6, D_MODEL), lambda i: (1, _mod_row(i), 0, 0)),
            full((1, D_MODEL)),
            full(w_in.shape),
            full((1, C_Q_LORA)),
            full((1, C_KV_LORA)),
            full(wuq.shape),
            pl.BlockSpec((1, TILE, LANES), lambda i: (_tab_blk(i), 0, 0)),
            pl.BlockSpec((1, TILE, LANES), lambda i: (_tab_blk(i), 0, 0)),
        ],
        out_specs=[pl.BlockSpec((512, TILE), lambda i: (0, i)),
                   pl.BlockSpec((256, TILE), lambda i: (0, i)),
                   tile(LANES), tile(LANES), tile(LANES), tile(512), tile(512)],
        out_shape=[
            jax.ShapeDtypeStruct((512, N_TOK), BF16),
            jax.ShapeDtypeStruct((256, N_TOK), BF16),
            jax.ShapeDtypeStruct((N_TOK, LANES), F32),
            jax.ShapeDtypeStruct((N_TOK, LANES), F32),
            jax.ShapeDtypeStruct((N_TOK, LANES), BF16),
            jax.ShapeDtypeStruct((N_TOK, 512), F32),
            jax.ShapeDtypeStruct((N_TOK, 512), F32),
        ],
        compiler_params=_cparams(("parallel",)),
        name="odd_in",
    )(x, mod, n1, w_in, qa, kva, wuq, cos, sin)


def _kvup_kernel(c_ref, w_ref, kn_ref, vt_ref):
    p = _dot(c_ref[...].astype(BF16), w_ref[...])
    kn_ref[...] = p[:, 0:512].astype(BF16)
    vt_ref[...] = p[:, 512:1024].T.astype(BF16)


def _kvup(ckv_all, w_ukv):
    n = ckv_all.shape[0]
    return pl.pallas_call(
        _kvup_kernel,
        grid=(n // TILE,),
        in_specs=[pl.BlockSpec((TILE, C_KV_LORA), lambda i: (i, 0)),
                  pl.BlockSpec(w_ukv.shape, lambda i: (0, 0))],
        out_specs=[pl.BlockSpec((TILE, 512), lambda i: (i, 0)), pl.BlockSpec((512, TILE), lambda i: (0, i))],
        out_shape=[jax.ShapeDtypeStruct((n, 512), BF16), jax.ShapeDtypeStruct((512, n), BF16)],
        compiler_params=_cparams(("parallel",)),
        name="mla_kv_up",
    )(ckv_all, w_ukv)


def _softmax_pv(qc, key_chunk, vt_chunks, nk):
    return _softmax_pv_streams([(qc, key_chunk, vt_chunks)], nk)[0]


def _softmax_pv_streams(streams, nk):
    ones = jnp.ones((BF16_ROWS, ATT_TK), BF16)
    ms = [jnp.full((1, qc.shape[1]), -jnp.inf, F32) for qc, _, _ in streams]
    accs = [jnp.zeros((HEAD_DIM + BF16_ROWS, qc.shape[1]), F32) for qc, _, _ in streams]
    nxt = [_dot(key_chunk(0), qc) for qc, key_chunk, _ in streams]
    for c in range(nk):
        for i, (qc, key_chunk, vt_chunks) in enumerate(streams):
            s = nxt[i]
            if c + 1 < nk:
                nxt[i] = _dot(key_chunk(c + 1), qc)
            mn = jnp.maximum(ms[i], jnp.max(s, axis=0, keepdims=True))
            p = jnp.exp2(s - mn).astype(BF16)
            vts = vt_chunks(c)
            if len(vts) == 1:
                pv = _dot(jnp.concatenate([vts[0], ones], axis=0), p)
            else:
                tq = qc.shape[1] // len(vts)
                pv = jnp.concatenate([_dot(jnp.concatenate([vt, ones], axis=0), p[:, k * tq:(k + 1) * tq])
                                      for k, vt in enumerate(vts)], axis=1)
            accs[i] = jnp.exp2(ms[i] - mn) * accs[i] + pv
            ms[i] = mn
    return [acc[0:HEAD_DIM, :] / acc[HEAD_DIM:HEAD_DIM + 1, :] for acc in accs]


def _gqa_kernel(q_ref, k_ref, vt_ref, o_ref):
    kvh = pl.program_id(1)
    nk = k_ref.shape[0] // ATT_TK
    tq = q_ref.shape[1]
    n = B_HEADS // B_KV_HEADS
    cols = []
    for hh in range(n):
        q = q_ref[hh * HEAD_DIM:(hh + 1) * HEAD_DIM, :]
        z = jnp.zeros_like(q)
        cols.append(jnp.where(kvh == 0, jnp.concatenate([q, z], axis=0), jnp.concatenate([z, q], axis=0)))
    o = _softmax_pv(jnp.concatenate(cols, axis=1), lambda c: k_ref[c * ATT_TK:(c + 1) * ATT_TK, :],
                    lambda c: [vt_ref[:, c * ATT_TK:(c + 1) * ATT_TK]], nk)
    for hh in range(n):
        o_ref[hh * HEAD_DIM:(hh + 1) * HEAD_DIM, :] = o[:, hh * tq:(hh + 1) * tq].astype(BF16)


def _mla_kernel(qn_ref, qr_ref, kn_ref, kr_ref, vt_ref, o_ref):
    nk = kn_ref.shape[0] // ATT_TK
    tq = qn_ref.shape[1]
    z = jnp.zeros((C_NOPE, tq), BF16)
    zr = jnp.zeros((LANES - C_ROPE, tq), BF16)
    streams = []
    for pair in range(MLA_PAIRS):
        cols = []
        for hh in range(2):
            h = 2 * pair + hh
            qn = qn_ref[h * C_NOPE:(h + 1) * C_NOPE, :]
            qr = qr_ref[h * C_ROPE:(h + 1) * C_ROPE, :]
            cols.append(jnp.concatenate(([qn, z] if hh == 0 else [z, qn]) + [qr, zr], axis=0))

        def key_chunk(c, pair=pair):
            rows = slice(c * ATT_TK, (c + 1) * ATT_TK)
            return jnp.concatenate([kn_ref[rows, pair * LANES:(pair + 1) * LANES], kr_ref[rows, :]], axis=1)

        def vt_chunks(c, pair=pair):
            return [vt_ref[(2 * pair + hh) * C_V:(2 * pair + hh + 1) * C_V, c * ATT_TK:(c + 1) * ATT_TK]
                    for hh in range(2)]

        streams.append((jnp.concatenate(cols, axis=1), key_chunk, vt_chunks))
    outs = _softmax_pv_streams(streams, nk)
    for pair in range(MLA_PAIRS):
        for hh in range(2):
            h = 2 * pair + hh
            o_ref[h * C_V:(h + 1) * C_V, :] = outs[pair][:, hh * tq:(hh + 1) * tq].astype(BF16)


def _attn_call(body, grid, in_specs, out_spec, args, n_tiles):
    return pl.pallas_call(
        body,
        grid=grid,
        in_specs=in_specs,
        out_specs=out_spec,
        out_shape=jax.ShapeDtypeStruct((512, n_tiles * TILE), BF16),
        compiler_params=_cparams(("parallel",) * len(grid)),
        name="attention",
    )(*args)


def _gqa_attention(qt, kb, vt, k_lat, vt_lat):
    s = k_lat.shape[1]
    rows = (B_HEADS // B_KV_HEADS) * HEAD_DIM
    o_ctx = _attn_call(
        _gqa_kernel, (BATCH, B_KV_HEADS),
        [pl.BlockSpec((rows, TILE), lambda b, h: (h, b)),
         pl.BlockSpec((None, TILE, LANES), lambda b, h: (b, 0, 0)),
         pl.BlockSpec((HEAD_DIM, TILE), lambda b, h: (h, b))],
        pl.BlockSpec((rows, TILE), lambda b, h: (h, b)),
        [qt, kb.reshape(NT, TILE, LANES), vt], NT_CTX)
    o_lat = _attn_call(
        _gqa_kernel, (DEC_BATCH, B_KV_HEADS, NT_SEQ),
        [pl.BlockSpec((rows, TILE), lambda b, h, j: (h, NT_CTX + b * NT_SEQ + j)),
         pl.BlockSpec((None, s, LANES), lambda b, h, j: (b, 0, 0)),
         pl.BlockSpec((None, HEAD_DIM, s), lambda b, h, j: (b, h, 0))],
        pl.BlockSpec((rows, TILE), lambda b, h, j: (h, b * NT_SEQ + j)),
        [qt, k_lat, vt_lat], NT_LAT)
    return o_ctx, o_lat


def _mla_attention(qnt, qrt, kn_c, krb, vt_c, kn_l, kr_l, vt_l):
    s = kr_l.shape[1]
    nh = 2 * MLA_PAIRS
    steps = C_HEADS // nh
    o_ctx = _attn_call(
        _mla_kernel, (BATCH, steps),
        [pl.BlockSpec((nh * C_NOPE, TILE), lambda b, p: (p, b)),
         pl.BlockSpec((nh * C_ROPE, TILE), lambda b, p: (p, b)),
         pl.BlockSpec((TILE, MLA_PAIRS * LANES), lambda b, p: (b, p)),
         pl.BlockSpec((TILE, LANES), lambda b, p: (b, 0)),
         pl.BlockSpec((nh * C_V, TILE), lambda b, p: (p, b))],
        pl.BlockSpec((nh * C_V, TILE), lambda b, p: (p, b)),
        [qnt, qrt, kn_c, krb, vt_c], NT_CTX)
    o_lat = _attn_call(
        _mla_kernel, (DEC_BATCH, steps, NT_SEQ),
        [pl.BlockSpec((nh * C_NOPE, TILE), lambda b, p, j: (p, NT_CTX + b * NT_SEQ + j)),
         pl.BlockSpec((nh * C_ROPE, TILE), lambda b, p, j: (p, NT_CTX + b * NT_SEQ + j)),
         pl.BlockSpec((s, MLA_PAIRS * LANES), lambda b, p, j: (b, p)),
         pl.BlockSpec((None, s, LANES), lambda b, p, j: (b, 0, 0)),
         pl.BlockSpec((nh * C_V, s), lambda b, p, j: (p, b))],
        pl.BlockSpec((nh * C_V, TILE), lambda b, p, j: (p, b * NT_SEQ + j)),
        [qnt, qrt, kn_l, kr_l, vt_l], NT_LAT)
    return o_ctx, o_lat


def _group_tile(i, ctx_ref, lat_ref):
    return jnp.where(i < NT_CTX, ctx_ref[...], lat_ref[...])


_CTX_COL = lambda i: (0, jnp.minimum(i, NT_CTX - 1))
_LAT_COL = lambda i: (0, jnp.maximum(i - NT_CTX, 0))
_CTX_ROW = lambda i: (jnp.minimum(i, NT_CTX - 1), 0)
_LAT_ROW = lambda i: (jnp.maximum(i - NT_CTX, 0), 0)


def _finish_out(o, x_in, mod_ref, n2_ref, wrh_ref, wrl_ref, xo_ref, h2_ref, lg_ref):
    x = x_in + mod_ref[0, 0, 2:3, :] * o
    xo_ref[...] = x
    h2 = _norm_mod(x, n2_ref[...], mod_ref[0, 0, 4:5, :], mod_ref[0, 0, 3:4, :])
    hi = h2.astype(BF16)
    lo = (h2 - hi.astype(F32)).astype(BF16)
    h2_ref[...] = hi
    lg_ref[...] = _dot(hi, wrh_ref[...]) + _dot(lo, wrh_ref[...]) + _dot(hi, wrl_ref[...])


def _dot_t(at, b):
    return lax.dot_general(at, b, (((0,), (0,)), ((), ())), preferred_element_type=F32)


def _even_out_kernel(a_ref, btc_ref, btl_ref, w_ref, xc_ref, xl_ref, mod_ref, n2_ref, wrh_ref, wrl_ref,
                     xo_ref, h2_ref, lg_ref):
    i = pl.program_id(0)
    bt = _group_tile(i, btc_ref, btl_ref)
    o = _dot(a_ref[...], w_ref[0:512, :]) + _dot_t(bt, w_ref[512:1024, :])
    _finish_out(o, _group_tile(i, xc_ref, xl_ref), mod_ref, n2_ref, wrh_ref, wrl_ref, xo_ref, h2_ref, lg_ref)


def _odd_out_kernel(atc_ref, atl_ref, z_ref, zp_ref, zn_ref, gb_ref, cw_ref, w_ref, x_ref, mod_ref, n2_ref,
                    wrh_ref, wrl_ref, xo_ref, h2_ref, lg_ref):
    i = pl.program_id(0)
    at = _group_tile(i, atc_ref, atl_ref)
    j = (i - NT_CTX) % NT_SEQ
    first = jnp.logical_or(i < NT_CTX, j == 0)
    last = jnp.logical_or(i < NT_CTX, j == NT_SEQ - 1)
    z = z_ref[...]
    row = lax.broadcasted_iota(jnp.int32, z.shape, 0)
    halo_p = jnp.where(first, 0.0, zp_ref[7:8, :])
    halo_n = jnp.where(last, 0.0, zn_ref[0:1, :])
    zprev = jnp.where(row == 0, halo_p, pltpu.roll(z, 1, 0))
    znext = jnp.where(row == TILE - 1, halo_n, pltpu.roll(z, TILE - 1, 0))
    y = zprev * cw_ref[0:1, :] + z * cw_ref[1:2, :] + znext * cw_ref[2:3, :]
    d = (gb_ref[...] * y).astype(BF16)
    o = _dot_t(at, w_ref[0:512, :]) + _dot(d, w_ref[512:1024, :])
    _finish_out(o, x_ref[...], mod_ref, n2_ref, wrh_ref, wrl_ref, xo_ref, h2_ref, lg_ref)


_OUT_SHAPES = [
    jax.ShapeDtypeStruct((N_TOK, D_MODEL), F32),
    jax.ShapeDtypeStruct((N_TOK, D_MODEL), BF16),
    jax.ShapeDtypeStruct((N_TOK, LANES), F32),
]


def _even_out(layer, a, b, w_out, xc, xl, mod, n2, wrh, wrl):
    full = lambda shape: pl.BlockSpec(shape, lambda i: (0,) * len(shape))
    tile = lambda w: pl.BlockSpec((TILE, w), lambda i: (i, 0))
    return pl.pallas_call(
        _even_out_kernel,
        grid=(NT,),
        in_specs=[tile(512), pl.BlockSpec((512, TILE), _CTX_COL), pl.BlockSpec((512, TILE), _LAT_COL),
                  full(w_out.shape), pl.BlockSpec((TILE, D_MODEL), _CTX_ROW), pl.BlockSpec((TILE, D_MODEL), _LAT_ROW),
                  pl.BlockSpec((1, 1, 6, D_MODEL), lambda i: (layer, _mod_row(i), 0, 0)),
                  full((1, D_MODEL)), full(wrh.shape), full(wrl.shape)],
        out_specs=[tile(D_MODEL), tile(D_MODEL), tile(LANES)],
        out_shape=_OUT_SHAPES,
        compiler_params=_cparams(("parallel",)),
        name="even_out",
    )(a, b[0], b[1], w_out, xc, xl, mod, n2, wrh, wrl)


def _odd_out(layer, a, z, gb, cw, w_out, x, mod, n2, wrh, wrl):
    full = lambda shape: pl.BlockSpec(shape, lambda i: (0,) * len(shape))
    tile = lambda w: pl.BlockSpec((TILE, w), lambda i: (i, 0))
    rb = TILE // 8
    return pl.pallas_call(
        _odd_out_kernel,
        grid=(NT,),
        in_specs=[pl.BlockSpec((512, TILE), _CTX_COL), pl.BlockSpec((512, TILE), _LAT_COL), tile(512),
                  pl.BlockSpec((8, 512), lambda i: (jnp.maximum(i * rb - 1, 0), 0)),
                  pl.BlockSpec((8, 512), lambda i: (jnp.minimum(i * rb + rb, NT * rb - 1), 0)),
                  tile(512), full(cw.shape), full(w_out.shape), tile(D_MODEL),
                  pl.BlockSpec((1, 1, 6, D_MODEL), lambda i: (layer, _mod_row(i), 0, 0)),
                  full((1, D_MODEL)), full(wrh.shape), full(wrl.shape)],
        out_specs=[tile(D_MODEL), tile(D_MODEL), tile(LANES)],
        out_shape=_OUT_SHAPES,
        compiler_params=_cparams(("parallel",)),
        name="odd_out",
    )(a[0], a[1], z, z, z, gb, cw, w_out, x, mod, n2, wrh, wrl)


def _excl_prefix(mask_f, tri):
    t = mask_f.shape[0]
    carry = jnp.zeros((1, LANES), F32)
    outs = []
    for blk in range(t // TILE):
        m = mask_f[blk * TILE:(blk + 1) * TILE]
        outs.append(_dot(tri, m.astype(BF16)) + carry)
        carry = carry + jnp.sum(m, axis=0, keepdims=True)
    return jnp.concatenate(outs, axis=0) if len(outs) > 1 else outs[0]


def _route_kernel(lg_ref, pos_ref, gate_ref, *, cap, per_domain_base):
    lg = lg_ref[...]
    t = lg.shape[0]
    lane = lax.broadcasted_iota(jnp.int32, lg.shape, 1)
    valid = lane < N_EXPERTS
    mx = jnp.max(jnp.where(valid, lg, -jnp.inf), axis=1, keepdims=True)
    ex = jnp.where(valid, jnp.exp(lg - mx), 0.0)
    aff = ex / jnp.sum(ex, axis=1, keepdims=True)
    thr_bits = jnp.zeros((1, LANES), jnp.int32)
    for bit in range(30, -1, -1):
        cand = thr_bits | (1 << bit)
        cnt = jnp.sum((aff >= lax.bitcast_convert_type(cand, F32)).astype(F32), axis=0, keepdims=True)
        thr_bits = jnp.where(cnt >= cap, cand, thr_bits)
    thr = lax.bitcast_convert_type(thr_bits, F32)
    gt = aff > thr
    eq = aff == thr
    need = cap - jnp.sum(gt.astype(F32), axis=0, keepdims=True)
    r = lax.broadcasted_iota(jnp.int32, (TILE, TILE), 0)
    c = lax.broadcasted_iota(jnp.int32, (TILE, TILE), 1)
    tri = (c < r).astype(BF16)
    eq_rank = _excl_prefix(eq.astype(F32), tri)
    sel = jnp.logical_and(valid, jnp.logical_or(gt, jnp.logical_and(eq, eq_rank < need)))
    pos = _excl_prefix(sel.astype(F32), tri)
    if per_domain_base:
        pos = pos + (pl.program_id(0) * cap).astype(F32)
    pos_ref[...] = jnp.where(sel, pos, -1.0)
    gate_ref[...] = jnp.where(sel, aff, 0.0)


def _route(logits, t_dom, cap, per_domain_base):
    n = logits.shape[0]
    return pl.pallas_call(
        functools.partial(_route_kernel, cap=cap, per_domain_base=per_domain_base),
        grid=(n // t_dom,),
        in_specs=[pl.BlockSpec((t_dom, LANES), lambda i: (i, 0))],
        out_specs=[pl.BlockSpec((t_dom, LANES), lambda i: (i, 0)), pl.BlockSpec((t_dom, LANES), lambda i: (i, 0))],
        out_shape=[jax.ShapeDtypeStruct((n, LANES), F32), jax.ShapeDtypeStruct((n, LANES), F32)],
        compiler_params=_cparams(("parallel",)),
        name="route",
    )(logits)


def _moe_kernel(nblk_ref, ctab_ref, etab_ref, llen_ref,
                x_ref, pos_ref, gate_ref, wg_ref, wu_ref, wd_ref, xres_ref, mod_ref, fn_ref, out_ref,
                lists_ref, gl_ref, s_ref, gs_ref, acc_ref, y_ref, wgb_ref, wub_ref, wdb_ref, *, final):
    g = pl.program_id(0)
    s = pl.program_id(1)
    chunks = STACK_BLK // CHUNK_ROWS
    rows16 = lax.broadcasted_iota(jnp.int32, (CHUNK_ROWS, TILE), 0)

    @pl.when(jnp.logical_and(g == 0, s == 0))
    def _():
        lists_ref[ZERO_ROW:ZERO_ROW + CHUNK_ROWS, :] = jnp.zeros((CHUNK_ROWS, D_MODEL), BF16)

    def chunk_rows(tbase, q, unused_row):
        d = ctab_ref[tbase + q]
        return pl.multiple_of(jnp.where(d >= 0, d, unused_row), CHUNK_ROWS)

    def build_block(j, rb, with_gate):
        tbase = ((g * GROUP_TILES + j) * MAX_STACK_BLKS + rb) * chunks
        for q in range(chunks):
            d = ctab_ref[tbase + q]
            e = etab_ref[tbase + q]
            posrow = pos_ref[0, j, pl.ds(e, 1), :]
            rowid = (rows16 + (rb * STACK_BLK + q * CHUNK_ROWS)).astype(F32)
            hit = jnp.logical_and(posrow == rowid, d >= 0)
            s_ref[q * CHUNK_ROWS:(q + 1) * CHUNK_ROWS, :] = hit.astype(BF16)
            if with_gate:
                gaterow = gate_ref[0, j, pl.ds(e, 1), :]
                gs_ref[q * CHUNK_ROWS:(q + 1) * CHUNK_ROWS, :] = jnp.where(hit, gaterow, 0.0)
        return tbase

    @pl.when(s < GROUP_TILES)
    def _gather():
        j = s

        def blk(rb, _):
            tbase = build_block(j, rb, True)
            acc_ref[...] = _dot(s_ref[...], x_ref[...])
            gcol = jnp.sum(gs_ref[...], axis=1, keepdims=True)
            for q in range(chunks):
                d = chunk_rows(tbase, q, DUMP_ROW)
                rows = slice(q * CHUNK_ROWS, (q + 1) * CHUNK_ROWS)
                lists_ref[pl.ds(d, CHUNK_ROWS), :] = acc_ref[rows, :].astype(BF16)
                gl_ref[pl.ds(d, CHUNK_ROWS), :] = gcol[rows, :]
            return 0

        lax.fori_loop(0, nblk_ref[g * GROUP_TILES + j], blk, 0)

    @pl.when(jnp.logical_and(s >= GROUP_TILES, s < 2 * GROUP_TILES))
    def _experts():
        e = s - GROUP_TILES
        ln = llen_ref[g * N_EXPERTS + e]
        nch = (ln + FFN_BLK - 1) // FFN_BLK
        base = e * LIST_MAX

        def zero_tail(k, _):
            r0 = pl.multiple_of(base + ln + k * CHUNK_ROWS, CHUNK_ROWS)
            lists_ref[pl.ds(r0, CHUNK_ROWS), :] = jnp.zeros((CHUNK_ROWS, D_MODEL), BF16)
            gl_ref[pl.ds(r0, CHUNK_ROWS), :] = jnp.zeros((CHUNK_ROWS, 1), F32)
            return 0

        lax.fori_loop(0, (nch * FFN_BLK - ln) // CHUNK_ROWS, zero_tail, 0)
        wgb_ref[...] = wg_ref[0, 0].astype(BF16)
        wub_ref[...] = wu_ref[0, 0].astype(BF16)
        wdb_ref[...] = wd_ref[0, 0].astype(BF16)

        def ffn(c, _):
            r0 = pl.multiple_of(base + c * FFN_BLK, FFN_BLK)
            xs = lists_ref[pl.ds(r0, FFN_BLK), :]
            hid = _silu(_dot(xs, wgb_ref[...])) * _dot(xs, wub_ref[...])
            y = _dot(hid.astype(BF16), wdb_ref[...]) * gl_ref[pl.ds(r0, FFN_BLK), :]
            lists_ref[pl.ds(r0, FFN_BLK), :] = y.astype(BF16)
            return 0

        lax.fori_loop(0, nch, ffn, 0)

    @pl.when(s >= 2 * GROUP_TILES)
    def _scatter():
        j = s - 2 * GROUP_TILES
        acc_ref[...] = jnp.zeros_like(acc_ref)

        def blk(rb, _):
            tbase = build_block(j, rb, False)
            for q in range(chunks):
                d = chunk_rows(tbase, q, ZERO_ROW)
                y_ref[q * CHUNK_ROWS:(q + 1) * CHUNK_ROWS, :] = lists_ref[pl.ds(d, CHUNK_ROWS), :]
            acc_ref[...] += _dot_t(s_ref[...], y_ref[...])
            return 0

        lax.fori_loop(0, nblk_ref[g * GROUP_TILES + j], blk, 0)
        x = xres_ref[...] + mod_ref[0, 0, 5:6, :] * acc_ref[...]
        if final:
            x = x * lax.rsqrt(jnp.mean(x * x, axis=-1, keepdims=True) + EPS) * fn_ref[...]
        out_ref[...] = x


def _moe(layer, final, tables, h2, pos_t, gate_t, wg, wu, wd, xres, mod, fn):
    gt = GROUP_TILES
    tile_a = lambda g, s, *_: (g * gt + jnp.minimum(s, gt - 1), 0)
    tile_c = lambda g, s, *_: (g * gt + jnp.clip(s - 2 * gt, 0, gt - 1), 0)
    expert = lambda g, s, *_: (layer, jnp.clip(s - gt, 0, N_EXPERTS - 1), 0, 0)
    group = lambda g, s, *_: (g, 0, 0, 0)
    grid_spec = pltpu.PrefetchScalarGridSpec(
        num_scalar_prefetch=4,
        grid=(N_GROUPS, 3 * gt),
        in_specs=[
            pl.BlockSpec((TILE, D_MODEL), tile_a),
            pl.BlockSpec((1, gt, N_EXPERTS, TILE), group),
            pl.BlockSpec((1, gt, N_EXPERTS, TILE), group),
            pl.BlockSpec((1, 1, D_MODEL, EXPERT_FF), expert),
            pl.BlockSpec((1, 1, D_MODEL, EXPERT_FF), expert),
            pl.BlockSpec((1, 1, EXPERT_FF, D_MODEL), expert),
            pl.BlockSpec((TILE, D_MODEL), tile_c),
            pl.BlockSpec((1, 1, 6, D_MODEL), lambda g, s, *_: (layer, g, 0, 0)),
            pl.BlockSpec((1, D_MODEL), lambda g, s, *_: (0, 0)),
        ],
        out_specs=pl.BlockSpec((TILE, D_MODEL), tile_c),
        scratch_shapes=[
            pltpu.VMEM((LIST_ROWS, D_MODEL), BF16),
            pltpu.VMEM((LIST_ROWS, 1), F32),
            pltpu.VMEM((STACK_BLK, TILE), BF16),
            pltpu.VMEM((STACK_BLK, TILE), F32),
            pltpu.VMEM((STACK_BLK, D_MODEL), F32),
            pltpu.VMEM((STACK_BLK, D_MODEL), BF16),
            pltpu.VMEM((D_MODEL, EXPERT_FF), BF16),
            pltpu.VMEM((D_MODEL, EXPERT_FF), BF16),
            pltpu.VMEM((EXPERT_FF, D_MODEL), BF16),
        ],
    )
    return pl.pallas_call(
        functools.partial(_moe_kernel, final=final),
        grid_spec=grid_spec,
        out_shape=jax.ShapeDtypeStruct((N_TOK, D_MODEL), F32),
        compiler_params=_cparams(("arbitrary", "arbitrary")),
        name="moe",
    )(*tables, h2, pos_t, gate_t, wg, wu, wd, xres, mod, fn)


def _rope_tables(rot_dim):
    axis_dim = rot_dim // 2
    tok = jnp.arange(DEC_SEQ)
    rows = (tok // GRID_W).astype(F32)
    cols = (tok % GRID_W).astype(F32)
    inv_freq = ROPE_THETA ** (-jnp.arange(0, axis_dim, 2, dtype=F32) / axis_dim)
    ar = rows[:, None] * inv_freq
    ac = cols[:, None] * inv_freq
    cos = jnp.concatenate([jnp.cos(ar), jnp.cos(ar), jnp.cos(ac), jnp.cos(ac)], axis=1)
    sin = jnp.concatenate([-jnp.sin(ar), jnp.sin(ar), -jnp.sin(ac), jnp.sin(ac)], axis=1)
    rep = LANES // rot_dim
    cos = jnp.tile(cos, (1, rep)).reshape(NT_SEQ, TILE, LANES)
    sin = jnp.tile(sin, (1, rep)).reshape(NT_SEQ, TILE, LANES)
    cos = jnp.concatenate([jnp.ones((1, TILE, LANES), F32), cos], axis=0)
    sin = jnp.concatenate([jnp.zeros((1, TILE, LANES), F32), sin], axis=0)
    return cos, sin


def _excl_cumsum(a, axis):
    return jnp.cumsum(a, axis=axis) - a


def _route_and_moe(layer, final, h2, logits, wg, wu, wd, xres, mod, fn):
    pos_c, gate_c = _route(logits[:NT_CTX * TILE], SEQ, EC_CAPACITY * SEQ // N_EXPERTS, True)
    pos_l, gate_l = _route(logits[NT_CTX * TILE:], DEC_SEQ, EC_CAPACITY * DEC_SEQ // N_EXPERTS, False)
    pos = jnp.concatenate([pos_c, pos_l], axis=0)[:, :N_EXPERTS]
    gate = jnp.concatenate([gate_c, gate_l], axis=0)[:, :N_EXPERTS]
    pos_t = pos.reshape(N_GROUPS, GROUP_TILES, TILE, N_EXPERTS).transpose(0, 1, 3, 2)
    gate_t = gate.reshape(N_GROUPS, GROUP_TILES, TILE, N_EXPERTS).transpose(0, 1, 3, 2)
    sel = pos_t >= 0
    cnt = jnp.sum(sel, axis=-1, dtype=jnp.int32)
    npad = (cnt + CHUNK_ROWS - 1) // CHUNK_ROWS * CHUNK_ROWS
    seg_off = _excl_cumsum(npad, 2)
    list_off = _excl_cumsum(npad, 1)
    rank0 = _excl_cumsum(cnt, 1)
    stack_pos = jnp.where(sel, pos_t - rank0[..., None].astype(F32) + seg_off[..., None].astype(F32), -1.0)
    nblk = (jnp.sum(npad, axis=2) + STACK_BLK - 1) // STACK_BLK
    llen = jnp.sum(npad, axis=1)
    r = (jnp.arange(MAX_STACK_BLKS * STACK_BLK // CHUNK_ROWS, dtype=jnp.int32) * CHUNK_ROWS)[None, None, None, :]
    in_seg = jnp.logical_and(r >= seg_off[..., None], r < (seg_off + npad)[..., None])
    eidx = jnp.arange(N_EXPERTS, dtype=jnp.int32)[None, None, :, None]
    dst = jnp.sum(jnp.where(in_seg, eidx * LIST_MAX + list_off[..., None] + r - seg_off[..., None], 0), axis=2)
    used = jnp.any(in_seg, axis=2)
    ctab = jnp.where(used, dst, -1).astype(jnp.int32)
    etab = jnp.sum(jnp.where(in_seg, eidx, 0), axis=2).astype(jnp.int32)
    tables = (nblk.reshape(-1).astype(jnp.int32), ctab.reshape(-1), etab.reshape(-1),
              llen.reshape(-1).astype(jnp.int32))
    return _moe(layer, final, tables, h2, stack_pos, gate_t, wg, wu, wd, xres, mod, fn)


def _split_router(w):
    wp = jnp.zeros((D_MODEL, LANES), F32).at[:, :N_EXPERTS].set(w)
    hi = wp.astype(BF16)
    lo = (wp - hi.astype(F32)).astype(BF16)
    return hi, lo


def kernel(x_prompt, x_sample, c, cache_attn_k, cache_attn_v, cache_mla_ckv, cache_mla_krope, c_ctx, w_mod, b_mod,
           norm1, norm2, ev_w_in, ev_q_norm, ev_k_norm, ev_w_s, ev_b_s, ev_w_out, od_w_in, od_q_a_norm,
           od_kv_a_norm, od_w_uq, od_w_ukv, od_conv_w, od_w_out, moe_router, moe_w_gate, moe_w_up, moe_w_down,
           final_norm):
    nctx = NT_CTX * TILE
    xc = x_prompt.reshape(nctx, D_MODEL)
    xl = x_sample.reshape(NT_LAT * TILE, D_MODEL)
    cvec = jnp.concatenate([c_ctx[None], c, jnp.zeros((3, D_MODEL), F32)], axis=0)
    mod = _modulation(cvec, w_mod, b_mod).reshape(2, 8, 6, D_MODEL)

    cos64, sin64 = _rope_tables(HEAD_DIM)
    seg = jnp.arange(512) // HEAD_DIM
    bd = (seg[:, None] == seg[None, :]).astype(BF16)
    bs_tab = jnp.repeat(ev_b_s[0].T, 64, axis=1)
    gated, qt, k, kb, v, vt = _even_in(
        xc, xl, mod, norm1[0:1], ev_w_in[0].astype(BF16), jnp.tile(ev_q_norm[0], 8)[None], jnp.tile(ev_k_norm[0], 2)[None],
        cos64, sin64, ev_w_s[0].astype(BF16), bs_tab, bd)
    new_k = k[:nctx].reshape(BATCH, SEQ, B_KV_HEADS, HEAD_DIM).transpose(0, 2, 1, 3)[:, None]
    new_v = v[:nctx].reshape(BATCH, SEQ, B_KV_HEADS, HEAD_DIM).transpose(0, 2, 1, 3)[:, None]
    ck = cache_attn_k[:, 0].transpose(0, 2, 1, 3).reshape(DEC_BATCH, PAST_LEN, LANES).astype(BF16)
    k_lat = jnp.concatenate([ck, kb[nctx:].reshape(DEC_BATCH, DEC_SEQ, LANES)], axis=1)
    cv = cache_attn_v[:, 0].transpose(0, 1, 3, 2).reshape(DEC_BATCH, LANES, PAST_LEN).astype(BF16)
    vt_lat = jnp.concatenate([cv, vt[:, nctx:].reshape(LANES, DEC_BATCH, DEC_SEQ).transpose(1, 0, 2)], axis=2)
    attn_t = _gqa_attention(qt, kb, vt, k_lat, vt_lat)
    wrh, wrl = _split_router(moe_router[0])
    x1, h2, logits = _even_out(0, gated, attn_t, ev_w_out[0].astype(BF16), xc, xl, mod, norm2[0:1], wrh, wrl)
    x2 = _route_and_moe(0, False, h2, logits, moe_w_gate, moe_w_up, moe_w_down, x1, mod, final_norm[None])

    cos32, sin32 = _rope_tables(C_ROPE)
    w_in1 = od_w_in[0]
    w_in1 = jnp.concatenate([w_in1[:, :C_Q_LORA + C_KV_LORA + C_ROPE], jnp.zeros((D_MODEL, LANES - C_ROPE), F32),
                             w_in1[:, C_Q_LORA + C_KV_LORA + C_ROPE:]], axis=1).astype(BF16)
    wuq = od_w_uq[0].reshape(C_Q_LORA, C_HEADS, C_NOPE + C_ROPE)
    wuq = jnp.concatenate([wuq[:, :, :C_NOPE].reshape(C_Q_LORA, -1), wuq[:, :, C_NOPE:].reshape(C_Q_LORA, -1)],
                          axis=1).astype(BF16)
    qnt, qrt, ckv, kr, krb, gb, z = _odd_in(x2, mod, norm1[1:2], w_in1, od_q_a_norm[0][None],
                                            od_kv_a_norm[0][None], wuq, cos32, sin32)
    new_ckv = ckv[:nctx].reshape(BATCH, 1, SEQ, C_KV_LORA)
    new_kr = kr[:nctx, :C_ROPE].reshape(BATCH, 1, SEQ, C_ROPE)
    wukv = od_w_ukv[0].reshape(C_KV_LORA, C_HEADS, C_NOPE + C_V)
    wukv = jnp.concatenate([wukv[:, :, :C_NOPE].reshape(C_KV_LORA, -1), wukv[:, :, C_NOPE:].reshape(C_KV_LORA, -1)],
                           axis=1).astype(BF16)
    ckv_lat = jnp.concatenate([cache_mla_ckv[:, 0], ckv[nctx:].reshape(DEC_BATCH, DEC_SEQ, C_KV_LORA)], axis=1)
    kn_c, vt_c = _kvup(ckv[:nctx], wukv)
    kn_l, vt_l = _kvup(ckv_lat.reshape(-1, C_KV_LORA), wukv)
    ckr = jnp.pad(cache_mla_krope[:, 0], ((0, 0), (0, 0), (0, LANES - C_ROPE))).astype(BF16)
    kr_l = jnp.concatenate([ckr, krb[nctx:].reshape(DEC_BATCH, DEC_SEQ, LANES)], axis=1)
    attn_t = _mla_attention(qnt, qrt, kn_c, krb, vt_c, kn_l, kr_l, vt_l)
    wrh, wrl = _split_router(moe_router[1])
    cw = jnp.concatenate([od_conv_w[0], jnp.zeros((5, D_WIDTH), F32)], axis=0)
    x3, h2, logits = _odd_out(1, attn_t, z, gb, cw, od_w_out[0].astype(BF16), x2, mod, norm2[1:2], wrh, wrl)
    y = _route_and_moe(1, True, h2, logits, moe_w_gate, moe_w_up, moe_w_down, x3, mod, final_norm[None])
    y_prompt = y[:nctx].reshape(BATCH, SEQ, D_MODEL)
    y_sample = y[nctx:].reshape(DEC_BATCH, DEC_SEQ, D_MODEL)
    return y_prompt, y_sample, new_k, new_v, new_ckv, new_kr
```

```python
import functools
import math

import jax
import jax.numpy as jnp
from jax import lax
from jax.experimental import pallas as pl
from jax.experimental.pallas import tpu as pltpu

F32 = jnp.float32
BF16 = jnp.bfloat16

D_MODEL = 1024
BATCH = 16
SEQ = 256
DEC_BATCH = 4
DEC_SEQ = 4096
PAST_LEN = 512
GRID_W = 64
ROPE_THETA = 10000.0
EPS = 1e-6
HEAD_DIM = 64
CHUNK = 128
A_GROUPS = 8
A_WIDTH = 512
B_HEADS = 8
B_KV_HEADS = 2
B_WIDTH = 512
C_HEADS = 8
C_NOPE = 64
C_ROPE = 32
C_V = 64
C_Q_LORA = 256
C_KV_LORA = 128
D_WIDTH = 512
N_EXPERTS = 16
EXPERT_FF = 512
EC_CAPACITY = 2

LOG2E = math.log2(math.e)
LANES = 128
BF16_ROWS = 16
ATT_TK = 256
MLA_PAIRS = 2
GQA_STREAMS = 2
TILE = 256
NT_CTX = BATCH * SEQ // TILE
NT_SEQ = DEC_SEQ // TILE
NT_LAT = DEC_BATCH * NT_SEQ
NT = NT_CTX + NT_LAT
N_TOK = NT * TILE
GROUP = 4096
N_GROUPS = N_TOK // GROUP
GROUP_TILES = GROUP // TILE
GROUP_CAP = EC_CAPACITY * GROUP // N_EXPERTS
CHUNK_ROWS = BF16_ROWS
STACK_BLK = 256
MAX_STACK_BLKS = N_EXPERTS * TILE // STACK_BLK
FFN_BLK = 256
LIST_MAX = -(-(GROUP_CAP + GROUP_TILES * (CHUNK_ROWS - 1)) // FFN_BLK) * FFN_BLK
DUMP_ROW = N_EXPERTS * LIST_MAX
ZERO_ROW = DUMP_ROW + CHUNK_ROWS
LIST_ROWS = ZERO_ROW + CHUNK_ROWS
VMEM_LIMIT = 56 * 1024 * 1024


def _cparams(sem):
    return pltpu.CompilerParams(dimension_semantics=sem, vmem_limit_bytes=VMEM_LIMIT)


def _dot(a, b):
    return jnp.dot(a, b, preferred_element_type=F32)


def _silu(x):
    return x / (1.0 + jnp.exp(-x))


def _mod_row(i):
    return jnp.where(i < NT_CTX, 0, 1 + (i - NT_CTX) // NT_SEQ)


def _tab_blk(i):
    return jnp.where(i < NT_CTX, 0, 1 + (i - NT_CTX) % NT_SEQ)


def _mod_kernel(c_ref, w_ref, b_ref, o_ref):
    s = _silu(c_ref[...])
    o_ref[0] = _dot(s.astype(BF16), w_ref[0].astype(BF16)) + b_ref[0]


def _modulation(cvec, w_mod, b_mod):
    depth = w_mod.shape[0]
    nchunk = 6
    return pl.pallas_call(
        _mod_kernel,
        grid=(depth, nchunk),
        in_specs=[
            pl.BlockSpec((8, D_MODEL), lambda l, k: (0, 0)),
            pl.BlockSpec((1, D_MODEL, D_MODEL), lambda l, k: (l, 0, k)),
            pl.BlockSpec((1, 1, D_MODEL), lambda l, k: (l, 0, k)),
        ],
        out_specs=pl.BlockSpec((1, 8, D_MODEL), lambda l, k: (l, 0, k)),
        out_shape=jax.ShapeDtypeStruct((depth, 8, 6 * D_MODEL), F32),
        compiler_params=_cparams(("arbitrary", "arbitrary")),
        name="modulation",
    )(cvec, w_mod, b_mod.reshape(depth, 1, 6 * D_MODEL))


def _norm_mod(x, gain, scale, shift):
    ms = jnp.mean(x * x, axis=-1, keepdims=True)
    return (x * lax.rsqrt(ms + EPS) * gain) * (1.0 + scale) + shift


def _seg_mean_sq(z, bd, width):
    zz = z * z
    hi = zz.astype(BF16)
    lo = (zz - hi.astype(F32)).astype(BF16)
    return (_dot(hi, bd) + _dot(lo, bd)) * (1.0 / width)


def _rope(z, cos, sin_signed, half):
    w = z.shape[1]
    lane = lax.broadcasted_iota(jnp.int32, z.shape, 1)
    first = (lane % (2 * half)) < half
    partner = jnp.where(first, pltpu.roll(z, w - half, 1), pltpu.roll(z, half, 1))
    return z * cos + partner * sin_signed


def _tile_lanes(t, n):
    return jnp.concatenate([t] * n, axis=1) if n > 1 else t


def _even_in_kernel(xc_ref, xl_ref, mod_ref, n1_ref, w_ref, qg_ref, kg_ref, cos_ref, sin_ref, ws_ref, bs_ref, bd_ref,
                    gated_ref, qt_ref, k_ref, kb_ref, v_ref, vt_ref):
    x = _group_tile(pl.program_id(0), xc_ref, xl_ref)
    h = _norm_mod(x, n1_ref[...], mod_ref[0, 0, 1:2, :], mod_ref[0, 0, 0:1, :])
    p = _dot(h.astype(BF16), w_ref[...])
    u = p[:, 0:512]
    va = p[:, 512:1024].astype(BF16)
    q = p[:, 1024:1536]
    k = p[:, 1536:1664]
    v = p[:, 1664:1792]
    bd = bd_ref[...]
    cos = cos_ref[0]
    sin = sin_ref[0]
    qn = q * lax.rsqrt(_seg_mean_sq(q, bd, HEAD_DIM) + EPS) * qg_ref[...]
    kn = k * lax.rsqrt(_seg_mean_sq(k, bd[0:LANES, 0:LANES], HEAD_DIM) + EPS) * kg_ref[...]
    qr = _rope(qn, _tile_lanes(cos, 4), _tile_lanes(sin, 4), HEAD_DIM // 4)
    kr = _rope(kn, cos, sin, HEAD_DIM // 4)
    qt_ref[...] = (qr * (HEAD_DIM ** -0.5 * LOG2E)).T.astype(BF16)
    k_ref[...] = kr
    kb_ref[...] = kr.astype(BF16)
    v_ref[...] = v
    vt_ref[...] = v.T.astype(BF16)
    lane = lax.broadcasted_iota(jnp.int32, (CHUNK, LANES), 1)
    for ch in range(TILE // CHUNK):
        rows = slice(ch * CHUNK, (ch + 1) * CHUNK)
        cols = []
        for pair in range(A_GROUPS // 2):
            vp = va[rows, pair * LANES:(pair + 1) * LANES]
            r0 = _dot(ws_ref[2 * pair], vp)
            r1 = _dot(ws_ref[2 * pair + 1], vp)
            cols.append(jnp.where(lane < LANES // 2, r0, r1))
        s = jnp.concatenate(cols, axis=1) + bs_ref[...]
        gated_ref[rows, :] = (u[rows, :] * s).astype(BF16)


def _even_in(xc, xl, mod, n1, w_in, qg, kg, cos, sin, ws, bs_tab, bd):
    full = lambda shape: pl.BlockSpec(shape, lambda i: (0,) * len(shape))
    return pl.pallas_call(
        _even_in_kernel,
        grid=(NT,),
        in_specs=[
            pl.BlockSpec((TILE, D_MODEL), _CTX_ROW),
            pl.BlockSpec((TILE, D_MODEL), _LAT_ROW),
            pl.BlockSpec((1, 1, 6, D_MODEL), lambda i: (0, _mod_row(i), 0, 0)),
            full((1, D_MODEL)),
            full(w_in.shape),
            full((1, 512)),
            full((1, LANES)),
            pl.BlockSpec((1, TILE, LANES), lambda i: (_tab_blk(i), 0, 0)),
            pl.BlockSpec((1, TILE, LANES), lambda i: (_tab_blk(i), 0, 0)),
            full(ws.shape),
            full(bs_tab.shape),
            full(bd.shape),
        ],
        out_specs=[
            pl.BlockSpec((TILE, 512), lambda i: (i, 0)),
            pl.BlockSpec((512, TILE), lambda i: (0, i)),
            pl.BlockSpec((TILE, LANES), lambda i: (i, 0)),
            pl.BlockSpec((TILE, LANES), lambda i: (i, 0)),
            pl.BlockSpec((TILE, LANES), lambda i: (i, 0)),
            pl.BlockSpec((LANES, TILE), lambda i: (0, i)),
        ],
        out_shape=[
            jax.ShapeDtypeStruct((N_TOK, 512), BF16),
            jax.ShapeDtypeStruct((512, N_TOK), BF16),
            jax.ShapeDtypeStruct((N_TOK, LANES), F32),
            jax.ShapeDtypeStruct((N_TOK, LANES), BF16),
            jax.ShapeDtypeStruct((N_TOK, LANES), F32),
            jax.ShapeDtypeStruct((LANES, N_TOK), BF16),
        ],
        compiler_params=_cparams(("parallel",)),
        name="even_in",
    )(xc, xl, mod, n1, w_in, qg, kg, cos, sin, ws, bs_tab, bd)


def _odd_in_kernel(x_ref, mod_ref, n1_ref, w_ref, qa_ref, kva_ref, wuq_ref, cos_ref, sin_ref,
                   qnt_ref, qrt_ref, ckv_ref, kr_ref, krb_ref, gb_ref, z_ref):
    h = _norm_mod(x_ref[...], n1_ref[...], mod_ref[0, 0, 1:2, :], mod_ref[0, 0, 0:1, :])
    p = _dot(h.astype(BF16), w_ref[...])
    cq = p[:, 0:256]
    ckv = p[:, 256:384]
    kr = p[:, 384:512]
    gb_ref[...] = p[:, 512:1024]
    z_ref[...] = p[:, 1024:1536] * p[:, 1536:2048]
    cqn = cq * lax.rsqrt(jnp.mean(cq * cq, axis=-1, keepdims=True) + EPS) * qa_ref[...]
    q = _dot(cqn.astype(BF16), wuq_ref[...])
    scale = (C_NOPE + C_ROPE) ** -0.5 * LOG2E
    cos = cos_ref[0]
    sin = sin_ref[0]
    qnt_ref[...] = (q[:, 0:512] * scale).T.astype(BF16)
    qr = _rope(q[:, 512:768], _tile_lanes(cos, 2), _tile_lanes(sin, 2), C_ROPE // 4)
    qrt_ref[...] = (qr * scale).T.astype(BF16)
    ckv_ref[...] = ckv * lax.rsqrt(jnp.mean(ckv * ckv, axis=-1, keepdims=True) + EPS) * kva_ref[...]
    krr = _rope(kr, cos, sin, C_ROPE // 4)
    kr_ref[...] = krr
    krb_ref[...] = krr.astype(BF16)


def _odd_in(x, mod, n1, w_in, qa, kva, wuq, cos, sin):
    full = lambda shape: pl.BlockSpec(shape, lambda i: (0,) * len(shape))
    tile = lambda w: pl.BlockSpec((TILE, w), lambda i: (i, 0))
    return pl.pallas_call(
        _odd_in_kernel,
        grid=(NT,),
        in_specs=[
            tile(D_MODEL),
            pl.BlockSpec((1, 1, 6, D_MODEL), lambda i: (1, _mod_row(i), 0, 0)),
            full((1, D_MODEL)),
            full(w_in.shape),
            full((1, C_Q_LORA)),
            full((1, C_KV_LORA)),
            full(wuq.shape),
            pl.BlockSpec((1, TILE, LANES), lambda i: (_tab_blk(i), 0, 0)),
            pl.BlockSpec((1, TILE, LANES), lambda i: (_tab_blk(i), 0, 0)),
        ],
        out_specs=[pl.BlockSpec((512, TILE), lambda i: (0, i)),
                   pl.BlockSpec((256, TILE), lambda i: (0, i)),
                   tile(LANES), tile(LANES), tile(LANES), tile(512), tile(512)],
        out_shape=[
            jax.ShapeDtypeStruct((512, N_TOK), BF16),
            jax.ShapeDtypeStruct((256, N_TOK), BF16),
            jax.ShapeDtypeStruct((N_TOK, LANES), F32),
            jax.ShapeDtypeStruct((N_TOK, LANES), F32),
            jax.ShapeDtypeStruct((N_TOK, LANES), BF16),
            jax.ShapeDtypeStruct((N_TOK, 512), F32),
            jax.ShapeDtypeStruct((N_TOK, 512), F32),
        ],
        compiler_params=_cparams(("parallel",)),
        name="odd_in",
    )(x, mod, n1, w_in, qa, kva, wuq, cos, sin)


def _kvup_kernel(c_ref, w_ref, kn_ref, vt_ref):
    p = _dot(c_ref[...].astype(BF16), w_ref[...])
    kn_ref[...] = p[:, 0:512].astype(BF16)
    vt_ref[...] = p[:, 512:1024].T.astype(BF16)


def _kvup(ckv_all, w_ukv):
    n = ckv_all.shape[0]
    return pl.pallas_call(
        _kvup_kernel,
        grid=(n // TILE,),
        in_specs=[pl.BlockSpec((TILE, C_KV_LORA), lambda i: (i, 0)),
                  pl.BlockSpec(w_ukv.shape, lambda i: (0, 0))],
        out_specs=[pl.BlockSpec((TILE, 512), lambda i: (i, 0)), pl.BlockSpec((512, TILE), lambda i: (0, i))],
        out_shape=[jax.ShapeDtypeStruct((n, 512), BF16), jax.ShapeDtypeStruct((512, n), BF16)],
        compiler_params=_cparams(("parallel",)),
        name="mla_kv_up",
    )(ckv_all, w_ukv)


def _softmax_pv(qc, key_chunk, vt_chunks, nk):
    return _softmax_pv_streams([(qc, key_chunk, vt_chunks)], nk)[0]


def _softmax_pv_streams(streams, nk):
    ones = jnp.ones((BF16_ROWS, ATT_TK), BF16)
    ms = [jnp.full((1, qc.shape[1]), -jnp.inf, F32) for qc, _, _ in streams]
    accs = [jnp.zeros((HEAD_DIM + BF16_ROWS, qc.shape[1]), F32) for qc, _, _ in streams]
    nxt = [_dot(key_chunk(0), qc) for qc, key_chunk, _ in streams]
    for c in range(nk):
        for i, (qc, key_chunk, vt_chunks) in enumerate(streams):
            s = nxt[i]
            if c + 1 < nk:
                nxt[i] = _dot(key_chunk(c + 1), qc)
            mn = jnp.maximum(ms[i], jnp.max(s, axis=0, keepdims=True))
            p = jnp.exp2(s - mn).astype(BF16)
            vts = vt_chunks(c)
            if len(vts) == 1:
                pv = _dot(jnp.concatenate([vts[0], ones], axis=0), p)
            else:
                tq = qc.shape[1] // len(vts)
                pv = jnp.concatenate([_dot(jnp.concatenate([vt, ones], axis=0), p[:, k * tq:(k + 1) * tq])
                                      for k, vt in enumerate(vts)], axis=1)
            accs[i] = jnp.exp2(ms[i] - mn) * accs[i] + pv
            ms[i] = mn
    return [acc[0:HEAD_DIM, :] / acc[HEAD_DIM:HEAD_DIM + 1, :] for acc in accs]


def _gqa_kernel(q_ref, k_ref, vt_ref, o_ref):
    kvh = pl.program_id(1)
    nk = k_ref.shape[0] // ATT_TK
    tq = q_ref.shape[1]
    n = B_HEADS // B_KV_HEADS
    cols = []
    for hh in range(n):
        q = q_ref[hh * HEAD_DIM:(hh + 1) * HEAD_DIM, :]
        z = jnp.zeros_like(q)
        cols.append(jnp.where(kvh == 0, jnp.concatenate([q, z], axis=0), jnp.concatenate([z, q], axis=0)))
    key_chunk = lambda c: k_ref[c * ATT_TK:(c + 1) * ATT_TK, :]
    vt_chunks = lambda c: [vt_ref[:, c * ATT_TK:(c + 1) * ATT_TK]]
    per = n // GQA_STREAMS
    outs = _softmax_pv_streams(
        [(jnp.concatenate(cols[i * per:(i + 1) * per], axis=1), key_chunk, vt_chunks) for i in range(GQA_STREAMS)], nk)
    for hh in range(n):
        o = outs[hh // per][:, (hh % per) * tq:(hh % per + 1) * tq]
        o_ref[hh * HEAD_DIM:(hh + 1) * HEAD_DIM, :] = o.astype(BF16)


def _mla_kernel(qn_ref, qr_ref, kn_ref, kr_ref, vt_ref, o_ref):
    nk = kn_ref.shape[0] // ATT_TK
    tq = qn_ref.shape[1]
    z = jnp.zeros((C_NOPE, tq), BF16)
    zr = jnp.zeros((LANES - C_ROPE, tq), BF16)
    streams = []
    for pair in range(MLA_PAIRS):
        cols = []
        for hh in range(2):
            h = 2 * pair + hh
            qn = qn_ref[h * C_NOPE:(h + 1) * C_NOPE, :]
            qr = qr_ref[h * C_ROPE:(h + 1) * C_ROPE, :]
            cols.append(jnp.concatenate(([qn, z] if hh == 0 else [z, qn]) + [qr, zr], axis=0))

        def key_chunk(c, pair=pair):
            rows = slice(c * ATT_TK, (c + 1) * ATT_TK)
            return jnp.concatenate([kn_ref[rows, pair * LANES:(pair + 1) * LANES], kr_ref[rows, :]], axis=1)

        def vt_chunks(c, pair=pair):
            return [vt_ref[(2 * pair + hh) * C_V:(2 * pair + hh + 1) * C_V, c * ATT_TK:(c + 1) * ATT_TK]
                    for hh in range(2)]

        streams.append((jnp.concatenate(cols, axis=1), key_chunk, vt_chunks))
    outs = _softmax_pv_streams(streams, nk)
    for pair in range(MLA_PAIRS):
        for hh in range(2):
            h = 2 * pair + hh
            o_ref[h * C_V:(h + 1) * C_V, :] = outs[pair][:, hh * tq:(hh + 1) * tq].astype(BF16)


def _attn_call(body, grid, in_specs, out_spec, args, n_tiles):
    return pl.pallas_call(
        body,
        grid=grid,
        in_specs=in_specs,
        out_specs=out_spec,
        out_shape=jax.ShapeDtypeStruct((512, n_tiles * TILE), BF16),
        compiler_params=_cparams(("parallel",) * len(grid)),
        name="attention",
    )(*args)


def _gqa_attention(qt, kb, vt, k_lat, vt_lat):
    s = k_lat.shape[1]
    rows = (B_HEADS // B_KV_HEADS) * HEAD_DIM
    o_ctx = _attn_call(
        _gqa_kernel, (BATCH, B_KV_HEADS),
        [pl.BlockSpec((rows, TILE), lambda b, h: (h, b)),
         pl.BlockSpec((None, TILE, LANES), lambda b, h: (b, 0, 0)),
         pl.BlockSpec((HEAD_DIM, TILE), lambda b, h: (h, b))],
        pl.BlockSpec((rows, TILE), lambda b, h: (h, b)),
        [qt, kb.reshape(NT, TILE, LANES), vt], NT_CTX)
    o_lat = _attn_call(
        _gqa_kernel, (DEC_BATCH, B_KV_HEADS, NT_SEQ),
        [pl.BlockSpec((rows, TILE), lambda b, h, j: (h, NT_CTX + b * NT_SEQ + j)),
         pl.BlockSpec((None, s, LANES), lambda b, h, j: (b, 0, 0)),
         pl.BlockSpec((None, HEAD_DIM, s), lambda b, h, j: (b, h, 0))],
        pl.BlockSpec((rows, TILE), lambda b, h, j: (h, b * NT_SEQ + j)),
        [qt, k_lat, vt_lat], NT_LAT)
    return o_ctx, o_lat


def _mla_attention(qnt, qrt, kn_c, krb, vt_c, kn_l, kr_l, vt_l):
    s = kr_l.shape[1]
    nh = 2 * MLA_PAIRS
    steps = C_HEADS // nh
    o_ctx = _attn_call(
        _mla_kernel, (BATCH, steps),
        [pl.BlockSpec((nh * C_NOPE, TILE), lambda b, p: (p, b)),
         pl.BlockSpec((nh * C_ROPE, TILE), lambda b, p: (p, b)),
         pl.BlockSpec((TILE, MLA_PAIRS * LANES), lambda b, p: (b, p)),
         pl.BlockSpec((TILE, LANES), lambda b, p: (b, 0)),
         pl.BlockSpec((nh * C_V, TILE), lambda b, p: (p, b))],
        pl.BlockSpec((nh * C_V, TILE), lambda b, p: (p, b)),
        [qnt, qrt, kn_c, krb, vt_c], NT_CTX)
    o_lat = _attn_call(
        _mla_kernel, (DEC_BATCH, steps, NT_SEQ),
        [pl.BlockSpec((nh * C_NOPE, TILE), lambda b, p, j: (p, NT_CTX + b * NT_SEQ + j)),
         pl.BlockSpec((nh * C_ROPE, TILE), lambda b, p, j: (p, NT_CTX + b * NT_SEQ + j)),
         pl.BlockSpec((s, MLA_PAIRS * LANES), lambda b, p, j: (b, p)),
         pl.BlockSpec((None, s, LANES), lambda b, p, j: (b, 0, 0)),
         pl.BlockSpec((nh * C_V, s), lambda b, p, j: (p, b))],
        pl.BlockSpec((nh * C_V, TILE), lambda b, p, j: (p, b * NT_SEQ + j)),
        [qnt, qrt, kn_l, kr_l, vt_l], NT_LAT)
    return o_ctx, o_lat


def _group_tile(i, ctx_ref, lat_ref):
    return jnp.where(i < NT_CTX, ctx_ref[...], lat_ref[...])


_CTX_COL = lambda i: (0, jnp.minimum(i, NT_CTX - 1))
_LAT_COL = lambda i: (0, jnp.maximum(i - NT_CTX, 0))
_CTX_ROW = lambda i: (jnp.minimum(i, NT_CTX - 1), 0)
_LAT_ROW = lambda i: (jnp.maximum(i - NT_CTX, 0), 0)


def _finish_out(o, x_in, mod_ref, n2_ref, wrh_ref, wrl_ref, xo_ref, h2_ref, lg_ref):
    x = x_in + mod_ref[0, 0, 2:3, :] * o
    xo_ref[...] = x
    h2 = _norm_mod(x, n2_ref[...], mod_ref[0, 0, 4:5, :], mod_ref[0, 0, 3:4, :])
    hi = h2.astype(BF16)
    lo = (h2 - hi.astype(F32)).astype(BF16)
    h2_ref[...] = hi
    lg_ref[...] = _dot(hi, wrh_ref[...]) + _dot(lo, wrh_ref[...]) + _dot(hi, wrl_ref[...])


def _dot_t(at, b):
    return lax.dot_general(at, b, (((0,), (0,)), ((), ())), preferred_element_type=F32)


def _even_out_kernel(a_ref, btc_ref, btl_ref, w_ref, xc_ref, xl_ref, mod_ref, n2_ref, wrh_ref, wrl_ref,
                     xo_ref, h2_ref, lg_ref):
    i = pl.program_id(0)
    bt = _group_tile(i, btc_ref, btl_ref)
    o = _dot(a_ref[...], w_ref[0:512, :]) + _dot_t(bt, w_ref[512:1024, :])
    _finish_out(o, _group_tile(i, xc_ref, xl_ref), mod_ref, n2_ref, wrh_ref, wrl_ref, xo_ref, h2_ref, lg_ref)


def _odd_out_kernel(atc_ref, atl_ref, z_ref, zp_ref, zn_ref, gb_ref, cw_ref, w_ref, x_ref, mod_ref, n2_ref,
                    wrh_ref, wrl_ref, xo_ref, h2_ref, lg_ref):
    i = pl.program_id(0)
    at = _group_tile(i, atc_ref, atl_ref)
    j = (i - NT_CTX) % NT_SEQ
    first = jnp.logical_or(i < NT_CTX, j == 0)
    last = jnp.logical_or(i < NT_CTX, j == NT_SEQ - 1)
    z = z_ref[...]
    row = lax.broadcasted_iota(jnp.int32, z.shape, 0)
    halo_p = jnp.where(first, 0.0, zp_ref[7:8, :])
    halo_n = jnp.where(last, 0.0, zn_ref[0:1, :])
    zprev = jnp.where(row == 0, halo_p, pltpu.roll(z, 1, 0))
    znext = jnp.where(row == TILE - 1, halo_n, pltpu.roll(z, TILE - 1, 0))
    y = zprev * cw_ref[0:1, :] + z * cw_ref[1:2, :] + znext * cw_ref[2:3, :]
    d = (gb_ref[...] * y).astype(BF16)
    o = _dot_t(at, w_ref[0:512, :]) + _dot(d, w_ref[512:1024, :])
    _finish_out(o, x_ref[...], mod_ref, n2_ref, wrh_ref, wrl_ref, xo_ref, h2_ref, lg_ref)


_OUT_SHAPES = [
    jax.ShapeDtypeStruct((N_TOK, D_MODEL), F32),
    jax.ShapeDtypeStruct((N_TOK, D_MODEL), BF16),
    jax.ShapeDtypeStruct((N_TOK, LANES), F32),
]


def _even_out(layer, a, b, w_out, xc, xl, mod, n2, wrh, wrl):
    full = lambda shape: pl.BlockSpec(shape, lambda i: (0,) * len(shape))
    tile = lambda w: pl.BlockSpec((TILE, w), lambda i: (i, 0))
    return pl.pallas_call(
        _even_out_kernel,
        grid=(NT,),
        in_specs=[tile(512), pl.BlockSpec((512, TILE), _CTX_COL), pl.BlockSpec((512, TILE), _LAT_COL),
                  full(w_out.shape), pl.BlockSpec((TILE, D_MODEL), _CTX_ROW), pl.BlockSpec((TILE, D_MODEL), _LAT_ROW),
                  pl.BlockSpec((1, 1, 6, D_MODEL), lambda i: (layer, _mod_row(i), 0, 0)),
                  full((1, D_MODEL)), full(wrh.shape), full(wrl.shape)],
        out_specs=[tile(D_MODEL), tile(D_MODEL), tile(LANES)],
        out_shape=_OUT_SHAPES,
        compiler_params=_cparams(("parallel",)),
        name="even_out",
    )(a, b[0], b[1], w_out, xc, xl, mod, n2, wrh, wrl)


def _odd_out(layer, a, z, gb, cw, w_out, x, mod, n2, wrh, wrl):
    full = lambda shape: pl.BlockSpec(shape, lambda i: (0,) * len(shape))
    tile = lambda w: pl.BlockSpec((TILE, w), lambda i: (i, 0))
    rb = TILE // 8
    return pl.pallas_call(
        _odd_out_kernel,
        grid=(NT,),
        in_specs=[pl.BlockSpec((512, TILE), _CTX_COL), pl.BlockSpec((512, TILE), _LAT_COL), tile(512),
                  pl.BlockSpec((8, 512), lambda i: (jnp.maximum(i * rb - 1, 0), 0)),
                  pl.BlockSpec((8, 512), lambda i: (jnp.minimum(i * rb + rb, NT * rb - 1), 0)),
                  tile(512), full(cw.shape), full(w_out.shape), tile(D_MODEL),
                  pl.BlockSpec((1, 1, 6, D_MODEL), lambda i: (layer, _mod_row(i), 0, 0)),
                  full((1, D_MODEL)), full(wrh.shape), full(wrl.shape)],
        out_specs=[tile(D_MODEL), tile(D_MODEL), tile(LANES)],
        out_shape=_OUT_SHAPES,
        compiler_params=_cparams(("parallel",)),
        name="odd_out",
    )(a[0], a[1], z, z, z, gb, cw, w_out, x, mod, n2, wrh, wrl)


def _excl_prefix(mask_f, tri):
    t = mask_f.shape[0]
    carry = jnp.zeros((1, LANES), F32)
    outs = []
    for blk in range(t // TILE):
        m = mask_f[blk * TILE:(blk + 1) * TILE]
        outs.append(_dot(tri, m.astype(BF16)) + carry)
        carry = carry + jnp.sum(m, axis=0, keepdims=True)
    return jnp.concatenate(outs, axis=0) if len(outs) > 1 else outs[0]


def _route_kernel(lg_ref, pos_ref, gate_ref, *, cap, per_domain_base):
    lg = lg_ref[...]
    t = lg.shape[0]
    lane = lax.broadcasted_iota(jnp.int32, lg.shape, 1)
    valid = lane < N_EXPERTS
    mx = jnp.max(jnp.where(valid, lg, -jnp.inf), axis=1, keepdims=True)
    ex = jnp.where(valid, jnp.exp(lg - mx), 0.0)
    aff = ex / jnp.sum(ex, axis=1, keepdims=True)
    thr_bits = jnp.zeros((1, LANES), jnp.int32)
    for bit in range(30, -1, -1):
        cand = thr_bits | (1 << bit)
        cnt = jnp.sum((aff >= lax.bitcast_convert_type(cand, F32)).astype(F32), axis=0, keepdims=True)
        thr_bits = jnp.where(cnt >= cap, cand, thr_bits)
    thr = lax.bitcast_convert_type(thr_bits, F32)
    gt = aff > thr
    eq = aff == thr
    need = cap - jnp.sum(gt.astype(F32), axis=0, keepdims=True)
    r = lax.broadcasted_iota(jnp.int32, (TILE, TILE), 0)
    c = lax.broadcasted_iota(jnp.int32, (TILE, TILE), 1)
    tri = (c < r).astype(BF16)
    eq_rank = _excl_prefix(eq.astype(F32), tri)
    sel = jnp.logical_and(valid, jnp.logical_or(gt, jnp.logical_and(eq, eq_rank < need)))
    pos = _excl_prefix(sel.astype(F32), tri)
    if per_domain_base:
        pos = pos + (pl.program_id(0) * cap).astype(F32)
    pos_ref[...] = jnp.where(sel, pos, -1.0)
    gate_ref[...] = jnp.where(sel, aff, 0.0)


def _route(logits, t_dom, cap, per_domain_base):
    n = logits.shape[0]
    return pl.pallas_call(
        functools.partial(_route_kernel, cap=cap, per_domain_base=per_domain_base),
        grid=(n // t_dom,),
        in_specs=[pl.BlockSpec((t_dom, LANES), lambda i: (i, 0))],
        out_specs=[pl.BlockSpec((t_dom, LANES), lambda i: (i, 0)), pl.BlockSpec((t_dom, LANES), lambda i: (i, 0))],
        out_shape=[jax.ShapeDtypeStruct((n, LANES), F32), jax.ShapeDtypeStruct((n, LANES), F32)],
        compiler_params=_cparams(("parallel",)),
        name="route",
    )(logits)


def _moe_kernel(nblk_ref, ctab_ref, etab_ref, llen_ref,
                x_ref, pos_ref, gate_ref, wg_ref, wu_ref, wd_ref, xres_ref, mod_ref, fn_ref, *rest, final):
    n_out = 2 if final else 1
    out_refs = rest[:n_out]
    lists_ref, gl_ref, s_ref, gs_ref, acc_ref, wgb_ref, wub_ref, wdb_ref = rest[n_out:]
    g = pl.program_id(0)
    s = pl.program_id(1)
    chunks = STACK_BLK // CHUNK_ROWS
    rows16 = lax.broadcasted_iota(jnp.int32, (CHUNK_ROWS, TILE), 0)

    @pl.when(jnp.logical_and(g == 0, s == 0))
    def _():
        lists_ref[ZERO_ROW:ZERO_ROW + CHUNK_ROWS, :] = jnp.zeros((CHUNK_ROWS, D_MODEL), BF16)

    def chunk_rows(tbase, q, unused_row):
        d = ctab_ref[tbase + q]
        return pl.multiple_of(jnp.where(d >= 0, d, unused_row), CHUNK_ROWS)

    def build_block(j, rb, with_gate):
        tbase = ((g * GROUP_TILES + j) * MAX_STACK_BLKS + rb) * chunks
        for q in range(chunks):
            d = ctab_ref[tbase + q]
            e = etab_ref[tbase + q]
            posrow = pos_ref[0, j, pl.ds(e, 1), :]
            rowid = (rows16 + (rb * STACK_BLK + q * CHUNK_ROWS)).astype(F32)
            hit = jnp.logical_and(posrow == rowid, d >= 0)
            s_ref[q * CHUNK_ROWS:(q + 1) * CHUNK_ROWS, :] = hit.astype(BF16)
            if with_gate:
                gaterow = gate_ref[0, j, pl.ds(e, 1), :]
                gs_ref[q * CHUNK_ROWS:(q + 1) * CHUNK_ROWS, :] = jnp.where(hit, gaterow, 0.0)
        return tbase

    @pl.when(s < GROUP_TILES)
    def _gather():
        j = s

        def blk(rb, _):
            tbase = build_block(j, rb, True)
            picked = _dot(s_ref[...], x_ref[...]).astype(BF16)
            gcol = jnp.sum(gs_ref[...], axis=1, keepdims=True)
            for q in range(chunks):
                d = chunk_rows(tbase, q, DUMP_ROW)
                rows = slice(q * CHUNK_ROWS, (q + 1) * CHUNK_ROWS)
                lists_ref[pl.ds(d, CHUNK_ROWS), :] = picked[rows, :]
                gl_ref[pl.ds(d, CHUNK_ROWS), :] = gcol[rows, :]
            return 0

        lax.fori_loop(0, nblk_ref[g * GROUP_TILES + j], blk, 0)

    @pl.when(jnp.logical_and(s >= GROUP_TILES, s < 2 * GROUP_TILES))
    def _experts():
        e = s - GROUP_TILES
        ln = llen_ref[g * N_EXPERTS + e]
        base = e * LIST_MAX
        half = FFN_BLK // 2
        rem = ln % FFN_BLK
        use_half = jnp.logical_and(rem > 0, rem <= half)
        nfull = ln // FFN_BLK + jnp.where(rem > half, 1, 0)
        end = nfull * FFN_BLK + jnp.where(use_half, half, 0)

        def zero_tail(k, _):
            r0 = pl.multiple_of(base + ln + k * CHUNK_ROWS, CHUNK_ROWS)
            lists_ref[pl.ds(r0, CHUNK_ROWS), :] = jnp.zeros((CHUNK_ROWS, D_MODEL), BF16)
            gl_ref[pl.ds(r0, CHUNK_ROWS), :] = jnp.zeros((CHUNK_ROWS, 1), F32)
            return 0

        lax.fori_loop(0, (end - ln) // CHUNK_ROWS, zero_tail, 0)
        wgb_ref[...] = wg_ref[0, 0].astype(BF16)
        wub_ref[...] = wu_ref[0, 0].astype(BF16)
        wdb_ref[...] = wd_ref[0, 0].astype(BF16)

        def ffn_rows(r0, n):
            xs = lists_ref[pl.ds(r0, n), :]
            hid = _silu(_dot(xs, wgb_ref[...])) * _dot(xs, wub_ref[...])
            y = _dot(hid.astype(BF16), wdb_ref[...]) * gl_ref[pl.ds(r0, n), :]
            lists_ref[pl.ds(r0, n), :] = y.astype(BF16)

        def ffn(c, _):
            ffn_rows(pl.multiple_of(base + c * FFN_BLK, FFN_BLK), FFN_BLK)
            return 0

        lax.fori_loop(0, nfull, ffn, 0)

        @pl.when(use_half)
        def _():
            ffn_rows(pl.multiple_of(base + nfull * FFN_BLK, half), half)

    @pl.when(s >= 2 * GROUP_TILES)
    def _scatter():
        j = s - 2 * GROUP_TILES
        acc_ref[...] = jnp.zeros_like(acc_ref)

        def blk(rb, _):
            tbase = build_block(j, rb, False)
            y = jnp.concatenate([lists_ref[pl.ds(chunk_rows(tbase, q, ZERO_ROW), CHUNK_ROWS), :]
                                 for q in range(chunks)], axis=0)
            acc_ref[...] += _dot_t(s_ref[...], y)
            return 0

        lax.fori_loop(0, nblk_ref[g * GROUP_TILES + j], blk, 0)
        x = xres_ref[...] + mod_ref[0, 0, 5:6, :] * acc_ref[...]
        if final:
            y = x * lax.rsqrt(jnp.mean(x * x, axis=-1, keepdims=True) + EPS) * fn_ref[...]

            @pl.when(g == 0)
            def _():
                out_refs[0][...] = y

            @pl.when(g > 0)
            def _():
                out_refs[1][...] = y
        else:
            out_refs[0][...] = x


def _moe(layer, final, tables, h2, pos_t, gate_t, wg, wu, wd, xres, mod, fn):
    gt = GROUP_TILES
    tile_a = lambda g, s, *_: (g * gt + jnp.minimum(s, gt - 1), 0)
    tile_c = lambda g, s, *_: (g * gt + jnp.clip(s - 2 * gt, 0, gt - 1), 0)
    expert = lambda g, s, *_: (layer, jnp.clip(s - gt, 0, N_EXPERTS - 1), 0, 0)
    group = lambda g, s, *_: (g, 0, 0, 0)
    if final:
        step_c = lambda s: jnp.clip(s - 2 * gt, 0, gt - 1)
        out_specs = [
            pl.BlockSpec((TILE, D_MODEL), lambda g, s, *_: (jnp.where(g == 0, step_c(s), gt - 1), 0)),
            pl.BlockSpec((TILE, D_MODEL), lambda g, s, *_: (jnp.where(g == 0, 0, (g - 1) * gt + step_c(s)), 0)),
        ]
        out_shape = [jax.ShapeDtypeStruct((NT_CTX * TILE, D_MODEL), F32),
                     jax.ShapeDtypeStruct((NT_LAT * TILE, D_MODEL), F32)]
    else:
        out_specs = pl.BlockSpec((TILE, D_MODEL), tile_c)
        out_shape = jax.ShapeDtypeStruct((N_TOK, D_MODEL), F32)
    grid_spec = pltpu.PrefetchScalarGridSpec(
        num_scalar_prefetch=4,
        grid=(N_GROUPS, 3 * gt),
        in_specs=[
            pl.BlockSpec((TILE, D_MODEL), tile_a),
            pl.BlockSpec((1, gt, N_EXPERTS, TILE), group),
            pl.BlockSpec((1, gt, N_EXPERTS, TILE), group),
            pl.BlockSpec((1, 1, D_MODEL, EXPERT_FF), expert),
            pl.BlockSpec((1, 1, D_MODEL, EXPERT_FF), expert),
            pl.BlockSpec((1, 1, EXPERT_FF, D_MODEL), expert),
            pl.BlockSpec((TILE, D_MODEL), tile_c),
            pl.BlockSpec((1, 1, 6, D_MODEL), lambda g, s, *_: (layer, g, 0, 0)),
            pl.BlockSpec((1, D_MODEL), lambda g, s, *_: (0, 0)),
        ],
        out_specs=out_specs,
        scratch_shapes=[
            pltpu.VMEM((LIST_ROWS, D_MODEL), BF16),
            pltpu.VMEM((LIST_ROWS, 1), F32),
            pltpu.VMEM((STACK_BLK, TILE), BF16),
            pltpu.VMEM((STACK_BLK, TILE), F32),
            pltpu.VMEM((STACK_BLK, D_MODEL), F32),
            pltpu.VMEM((D_MODEL, EXPERT_FF), BF16),
            pltpu.VMEM((D_MODEL, EXPERT_FF), BF16),
            pltpu.VMEM((EXPERT_FF, D_MODEL), BF16),
        ],
    )
    return pl.pallas_call(
        functools.partial(_moe_kernel, final=final),
        grid_spec=grid_spec,
        out_shape=out_shape,
        compiler_params=_cparams(("arbitrary", "arbitrary")),
        name="moe",
    )(*tables, h2, pos_t, gate_t, wg, wu, wd, xres, mod, fn)


def _rope_tables(rot_dim):
    axis_dim = rot_dim // 2
    tok = jnp.arange(DEC_SEQ)
    rows = (tok // GRID_W).astype(F32)
    cols = (tok % GRID_W).astype(F32)
    inv_freq = ROPE_THETA ** (-jnp.arange(0, axis_dim, 2, dtype=F32) / axis_dim)
    ar = rows[:, None] * inv_freq
    ac = cols[:, None] * inv_freq
    cos = jnp.concatenate([jnp.cos(ar), jnp.cos(ar), jnp.cos(ac), jnp.cos(ac)], axis=1)
    sin = jnp.concatenate([-jnp.sin(ar), jnp.sin(ar), -jnp.sin(ac), jnp.sin(ac)], axis=1)
    rep = LANES // rot_dim
    cos = jnp.tile(cos, (1, rep)).reshape(NT_SEQ, TILE, LANES)
    sin = jnp.tile(sin, (1, rep)).reshape(NT_SEQ, TILE, LANES)
    cos = jnp.concatenate([jnp.ones((1, TILE, LANES), F32), cos], axis=0)
    sin = jnp.concatenate([jnp.zeros((1, TILE, LANES), F32), sin], axis=0)
    return cos, sin


def _excl_cumsum(a, axis):
    return jnp.cumsum(a, axis=axis) - a


def _route_and_moe(layer, final, h2, logits, wg, wu, wd, xres, mod, fn):
    pos_c, gate_c = _route(logits[:NT_CTX * TILE], SEQ, EC_CAPACITY * SEQ // N_EXPERTS, True)
    pos_l, gate_l = _route(logits[NT_CTX * TILE:], DEC_SEQ, EC_CAPACITY * DEC_SEQ // N_EXPERTS, False)
    pos = jnp.concatenate([pos_c, pos_l], axis=0)[:, :N_EXPERTS]
    gate = jnp.concatenate([gate_c, gate_l], axis=0)[:, :N_EXPERTS]
    pos_t = pos.reshape(N_GROUPS, GROUP_TILES, TILE, N_EXPERTS).transpose(0, 1, 3, 2)
    gate_t = gate.reshape(N_GROUPS, GROUP_TILES, TILE, N_EXPERTS).transpose(0, 1, 3, 2)
    sel = pos_t >= 0
    cnt = jnp.sum(sel, axis=-1, dtype=jnp.int32)
    npad = (cnt + CHUNK_ROWS - 1) // CHUNK_ROWS * CHUNK_ROWS
    seg_off = _excl_cumsum(npad, 2)
    list_off = _excl_cumsum(npad, 1)
    rank0 = _excl_cumsum(cnt, 1)
    stack_pos = jnp.where(sel, pos_t - rank0[..., None].astype(F32) + seg_off[..., None].astype(F32), -1.0)
    nblk = (jnp.sum(npad, axis=2) + STACK_BLK - 1) // STACK_BLK
    llen = jnp.sum(npad, axis=1)
    r = (jnp.arange(MAX_STACK_BLKS * STACK_BLK // CHUNK_ROWS, dtype=jnp.int32) * CHUNK_ROWS)[None, None, None, :]
    in_seg = jnp.logical_and(r >= seg_off[..., None], r < (seg_off + npad)[..., None])
    eidx = jnp.arange(N_EXPERTS, dtype=jnp.int32)[None, None, :, None]
    dst = jnp.sum(jnp.where(in_seg, eidx * LIST_MAX + list_off[..., None] + r - seg_off[..., None], 0), axis=2)
    used = jnp.any(in_seg, axis=2)
    ctab = jnp.where(used, dst, -1).astype(jnp.int32)
    etab = jnp.sum(jnp.where(in_seg, eidx, 0), axis=2).astype(jnp.int32)
    tables = (nblk.reshape(-1).astype(jnp.int32), ctab.reshape(-1), etab.reshape(-1),
              llen.reshape(-1).astype(jnp.int32))
    return _moe(layer, final, tables, h2, stack_pos, gate_t, wg, wu, wd, xres, mod, fn)


def _split_router(w):
    wp = jnp.zeros((D_MODEL, LANES), F32).at[:, :N_EXPERTS].set(w)
    hi = wp.astype(BF16)
    lo = (wp - hi.astype(F32)).astype(BF16)
    return hi, lo


def kernel(x_prompt, x_sample, c, cache_attn_k, cache_attn_v, cache_mla_ckv, cache_mla_krope, c_ctx, w_mod, b_mod,
           norm1, norm2, ev_w_in, ev_q_norm, ev_k_norm, ev_w_s, ev_b_s, ev_w_out, od_w_in, od_q_a_norm,
           od_kv_a_norm, od_w_uq, od_w_ukv, od_conv_w, od_w_out, moe_router, moe_w_gate, moe_w_up, moe_w_down,
           final_norm):
    nctx = NT_CTX * TILE
    xc = x_prompt.reshape(nctx, D_MODEL)
    xl = x_sample.reshape(NT_LAT * TILE, D_MODEL)
    cvec = jnp.concatenate([c_ctx[None], c, jnp.zeros((3, D_MODEL), F32)], axis=0)
    mod = _modulation(cvec, w_mod, b_mod).reshape(2, 8, 6, D_MODEL)

    cos64, sin64 = _rope_tables(HEAD_DIM)
    seg = jnp.arange(512) // HEAD_DIM
    bd = (seg[:, None] == seg[None, :]).astype(BF16)
    bs_tab = jnp.repeat(ev_b_s[0].T, 64, axis=1)
    gated, qt, k, kb, v, vt = _even_in(
        xc, xl, mod, norm1[0:1], ev_w_in[0].astype(BF16), jnp.tile(ev_q_norm[0], 8)[None], jnp.tile(ev_k_norm[0], 2)[None],
        cos64, sin64, ev_w_s[0].astype(BF16), bs_tab, bd)
    new_k = k[:nctx].reshape(BATCH, SEQ, B_KV_HEADS, HEAD_DIM).transpose(0, 2, 1, 3)[:, None]
    new_v = v[:nctx].reshape(BATCH, SEQ, B_KV_HEADS, HEAD_DIM).transpose(0, 2, 1, 3)[:, None]
    ck = cache_attn_k[:, 0].transpose(0, 2, 1, 3).reshape(DEC_BATCH, PAST_LEN, LANES).astype(BF16)
    k_lat = jnp.concatenate([ck, kb[nctx:].reshape(DEC_BATCH, DEC_SEQ, LANES)], axis=1)
    cv = cache_attn_v[:, 0].transpose(0, 1, 3, 2).reshape(DEC_BATCH, LANES, PAST_LEN).astype(BF16)
    vt_lat = jnp.concatenate([cv, vt[:, nctx:].reshape(LANES, DEC_BATCH, DEC_SEQ).transpose(1, 0, 2)], axis=2)
    attn_t = _gqa_attention(qt, kb, vt, k_lat, vt_lat)
    wrh, wrl = _split_router(moe_router[0])
    x1, h2, logits = _even_out(0, gated, attn_t, ev_w_out[0].astype(BF16), xc, xl, mod, norm2[0:1], wrh, wrl)
    x2 = _route_and_moe(0, False, h2, logits, moe_w_gate, moe_w_up, moe_w_down, x1, mod, final_norm[None])

    cos32, sin32 = _rope_tables(C_ROPE)
    w_in1 = od_w_in[0]
    w_in1 = jnp.concatenate([w_in1[:, :C_Q_LORA + C_KV_LORA + C_ROPE], jnp.zeros((D_MODEL, LANES - C_ROPE), F32),
                             w_in1[:, C_Q_LORA + C_KV_LORA + C_ROPE:]], axis=1).astype(BF16)
    wuq = od_w_uq[0].reshape(C_Q_LORA, C_HEADS, C_NOPE + C_ROPE)
    wuq = jnp.concatenate([wuq[:, :, :C_NOPE].reshape(C_Q_LORA, -1), wuq[:, :, C_NOPE:].reshape(C_Q_LORA, -1)],
                          axis=1).astype(BF16)
    qnt, qrt, ckv, kr, krb, gb, z = _odd_in(x2, mod, norm1[1:2], w_in1, od_q_a_norm[0][None],
                                            od_kv_a_norm[0][None], wuq, cos32, sin32)
    new_ckv = ckv[:nctx].reshape(BATCH, 1, SEQ, C_KV_LORA)
    new_kr = kr[:nctx, :C_ROPE].reshape(BATCH, 1, SEQ, C_ROPE)
    wukv = od_w_ukv[0].reshape(C_KV_LORA, C_HEADS, C_NOPE + C_V)
    wukv = jnp.concatenate([wukv[:, :, :C_NOPE].reshape(C_KV_LORA, -1), wukv[:, :, C_NOPE:].reshape(C_KV_LORA, -1)],
                           axis=1).astype(BF16)
    ckv_lat = jnp.concatenate([cache_mla_ckv[:, 0], ckv[nctx:].reshape(DEC_BATCH, DEC_SEQ, C_KV_LORA)], axis=1)
    kn_c, vt_c = _kvup(ckv[:nctx], wukv)
    kn_l, vt_l = _kvup(ckv_lat.reshape(-1, C_KV_LORA), wukv)
    ckr = jnp.pad(cache_mla_krope[:, 0], ((0, 0), (0, 0), (0, LANES - C_ROPE))).astype(BF16)
    kr_l = jnp.concatenate([ckr, krb[nctx:].reshape(DEC_BATCH, DEC_SEQ, LANES)], axis=1)
    attn_t = _mla_attention(qnt, qrt, kn_c, krb, vt_c, kn_l, kr_l, vt_l)
    wrh, wrl = _split_router(moe_router[1])
    cw = jnp.concatenate([od_conv_w[0], jnp.zeros((5, D_WIDTH), F32)], axis=0)
    x3, h2, logits = _odd_out(1, attn_t, z, gb, cw, od_w_out[0].astype(BF16), x2, mod, norm2[1:2], wrh, wrl)
    y_c, y_l = _route_and_moe(1, True, h2, logits, moe_w_gate, moe_w_up, moe_w_down, x3, mod, final_norm[None])
    y_prompt = y_c.reshape(BATCH, SEQ, D_MODEL)
    y_sample = y_l.reshape(DEC_BATCH, DEC_SEQ, D_MODEL)
    return y_prompt, y_sample, new_k, new_v, new_ckv, new_kr
```

```python
import functools
import math

import jax
import jax.numpy as jnp
from jax import lax
from jax.experimental import pallas as pl
from jax.experimental.pallas import tpu as pltpu

F32 = jnp.float32
BF16 = jnp.bfloat16

D_MODEL = 1024
BATCH = 16
SEQ = 256
DEC_BATCH = 4
DEC_SEQ = 4096
PAST_LEN = 512
GRID_W = 64
ROPE_THETA = 10000.0
EPS = 1e-6
HEAD_DIM = 64
CHUNK = 128
A_GROUPS = 8
A_WIDTH = 512
B_HEADS = 8
B_KV_HEADS = 2
B_WIDTH = 512
C_HEADS = 8
C_NOPE = 64
C_ROPE = 32
C_V = 64
C_Q_LORA = 256
C_KV_LORA = 128
D_WIDTH = 512
N_EXPERTS = 16
EXPERT_FF = 512
EC_CAPACITY = 2

LOG2E = math.log2(math.e)
LANES = 128
BF16_ROWS = 16
ATT_TK = 256
MLA_PAIRS = 2
GQA_STREAMS = 2
TILE = 256
NT_CTX = BATCH * SEQ // TILE
NT_SEQ = DEC_SEQ // TILE
NT_LAT = DEC_BATCH * NT_SEQ
NT = NT_CTX + NT_LAT
N_TOK = NT * TILE
GROUP = 4096
N_GROUPS = N_TOK // GROUP
GROUP_TILES = GROUP // TILE
GROUP_CAP = EC_CAPACITY * GROUP // N_EXPERTS
CHUNK_ROWS = BF16_ROWS
STACK_BLK = 256
MAX_STACK_BLKS = N_EXPERTS * TILE // STACK_BLK
FFN_BLK = 256
LIST_MAX = -(-(GROUP_CAP + GROUP_TILES * (CHUNK_ROWS - 1)) // FFN_BLK) * FFN_BLK
DUMP_ROW = N_EXPERTS * LIST_MAX
ZERO_ROW = DUMP_ROW + CHUNK_ROWS
LIST_ROWS = ZERO_ROW + CHUNK_ROWS
VMEM_LIMIT = 56 * 1024 * 1024


def _cparams(sem):
    return pltpu.CompilerParams(dimension_semantics=sem, vmem_limit_bytes=VMEM_LIMIT)


def _dot(a, b):
    return jnp.dot(a, b, preferred_element_type=F32)


def _silu(x):
    return x / (1.0 + jnp.exp(-x))


def _mod_row(i):
    return jnp.where(i < NT_CTX, 0, 1 + (i - NT_CTX) // NT_SEQ)


def _tab_blk(i):
    return jnp.where(i < NT_CTX, 0, 1 + (i - NT_CTX) % NT_SEQ)


def _mod_kernel(c_ref, w_ref, b_ref, o_ref):
    s = _silu(c_ref[...])
    o_ref[0] = _dot(s.astype(BF16), w_ref[0].astype(BF16)) + b_ref[0]


def _modulation(cvec, w_mod, b_mod):
    depth = w_mod.shape[0]
    nchunk = 6
    return pl.pallas_call(
        _mod_kernel,
        grid=(depth, nchunk),
        in_specs=[
            pl.BlockSpec((8, D_MODEL), lambda l, k: (0, 0)),
            pl.BlockSpec((1, D_MODEL, D_MODEL), lambda l, k: (l, 0, k)),
            pl.BlockSpec((1, 1, D_MODEL), lambda l, k: (l, 0, k)),
        ],
        out_specs=pl.BlockSpec((1, 8, D_MODEL), lambda l, k: (l, 0, k)),
        out_shape=jax.ShapeDtypeStruct((depth, 8, 6 * D_MODEL), F32),
        compiler_params=_cparams(("arbitrary", "arbitrary")),
        name="modulation",
    )(cvec, w_mod, b_mod.reshape(depth, 1, 6 * D_MODEL))


def _norm_mod(x, gain, scale, shift):
    ms = jnp.mean(x * x, axis=-1, keepdims=True)
    return (x * lax.rsqrt(ms + EPS) * gain) * (1.0 + scale) + shift


def _seg_mean_sq(z, bd, width):
    zz = z * z
    hi = zz.astype(BF16)
    lo = (zz - hi.astype(F32)).astype(BF16)
    return (_dot(hi, bd) + _dot(lo, bd)) * (1.0 / width)


def _rope(z, cos, sin_signed, half):
    w = z.shape[1]
    lane = lax.broadcasted_iota(jnp.int32, z.shape, 1)
    first = (lane % (2 * half)) < half
    partner = jnp.where(first, pltpu.roll(z, w - half, 1), pltpu.roll(z, half, 1))
    return z * cos + partner * sin_signed


def _tile_lanes(t, n):
    return jnp.concatenate([t] * n, axis=1) if n > 1 else t


def _even_in_kernel(xc_ref, xl_ref, mod_ref, n1_ref, w_ref, qg_ref, kg_ref, cos_ref, sin_ref, ws_ref, bs_ref, bd_ref,
                    gated_ref, qt_ref, k_ref, kb_ref, v_ref, vt_ref):
    x = _group_tile(pl.program_id(0), xc_ref, xl_ref)
    h = _norm_mod(x, n1_ref[...], mod_ref[0, 0, 1:2, :], mod_ref[0, 0, 0:1, :])
    p = _dot(h.astype(BF16), w_ref[...])
    u = p[:, 0:512]
    va = p[:, 512:1024].astype(BF16)
    q = p[:, 1024:1536]
    k = p[:, 1536:1664]
    v = p[:, 1664:1792]
    bd = bd_ref[...]
    cos = cos_ref[0]
    sin = sin_ref[0]
    qn = q * lax.rsqrt(_seg_mean_sq(q, bd, HEAD_DIM) + EPS) * qg_ref[...]
    kn = k * lax.rsqrt(_seg_mean_sq(k, bd[0:LANES, 0:LANES], HEAD_DIM) + EPS) * kg_ref[...]
    qr = _rope(qn, _tile_lanes(cos, 4), _tile_lanes(sin, 4), HEAD_DIM // 4)
    kr = _rope(kn, cos, sin, HEAD_DIM // 4)
    qt_ref[...] = (qr * (HEAD_DIM ** -0.5 * LOG2E)).T.astype(BF16)
    k_ref[...] = kr
    kb_ref[...] = kr.astype(BF16)
    v_ref[...] = v
    vt_ref[...] = v.T.astype(BF16)
    lane = lax.broadcasted_iota(jnp.int32, (CHUNK, LANES), 1)
    for ch in range(TILE // CHUNK):
        rows = slice(ch * CHUNK, (ch + 1) * CHUNK)
        cols = []
        for pair in range(A_GROUPS // 2):
            vp = va[rows, pair * LANES:(pair + 1) * LANES]
            r0 = _dot(ws_ref[2 * pair], vp)
            r1 = _dot(ws_ref[2 * pair + 1], vp)
            cols.append(jnp.where(lane < LANES // 2, r0, r1))
        s = jnp.concatenate(cols, axis=1) + bs_ref[...]
        gated_ref[rows, :] = (u[rows, :] * s).astype(BF16)


def _even_in(xc, xl, mod, n1, w_in, qg, kg, cos, sin, ws, bs_tab, bd):
    full = lambda shape: pl.BlockSpec(shape, lambda i: (0,) * len(shape))
    return pl.pallas_call(
        _even_in_kernel,
        grid=(NT,),
        in_specs=[
            pl.BlockSpec((TILE, D_MODEL), _CTX_ROW),
            pl.BlockSpec((TILE, D_MODEL), _LAT_ROW),
            pl.BlockSpec((1, 1, 6, D_MODEL), lambda i: (0, _mod_row(i), 0, 0)),
            full((1, D_MODEL)),
            full(w_in.shape),
            full((1, 512)),
            full((1, LANES)),
            pl.BlockSpec((1, TILE, LANES), lambda i: (_tab_blk(i), 0, 0)),
            pl.BlockSpec((1, TILE, LANES), lambda i: (_tab_blk(i), 0, 0)),
            full(ws.shape),
            full(bs_tab.shape),
            full(bd.shape),
        ],
        out_specs=[
            pl.BlockSpec((TILE, 512), lambda i: (i, 0)),
            pl.BlockSpec((512, TILE), lambda i: (0, i)),
            pl.BlockSpec((TILE, LANES), lambda i: (i, 0)),
            pl.BlockSpec((TILE, LANES), lambda i: (i, 0)),
            pl.BlockSpec((TILE, LANES), lambda i: (i, 0)),
            pl.BlockSpec((LANES, TILE), lambda i: (0, i)),
        ],
        out_shape=[
            jax.ShapeDtypeStruct((N_TOK, 512), BF16),
            jax.ShapeDtypeStruct((512, N_TOK), BF16),
            jax.ShapeDtypeStruct((N_TOK, LANES), F32),
            jax.ShapeDtypeStruct((N_TOK, LANES), BF16),
            jax.ShapeDtypeStruct((N_TOK, LANES), F32),
            jax.ShapeDtypeStruct((LANES, N_TOK), BF16),
        ],
        compiler_params=_cparams(("parallel",)),
        name="even_in",
    )(xc, xl, mod, n1, w_in, qg, kg, cos, sin, ws, bs_tab, bd)


def _odd_in_kernel(x_ref, mod_ref, n1_ref, w_ref, qa_ref, kva_ref, wuq_ref, cos_ref, sin_ref,
                   qnt_ref, qrt_ref, ckv_ref, kr_ref, krb_ref, gb_ref, z_ref):
    h = _norm_mod(x_ref[...], n1_ref[...], mod_ref[0, 0, 1:2, :], mod_ref[0, 0, 0:1, :])
    p = _dot(h.astype(BF16), w_ref[...])
    cq = p[:, 0:256]
    ckv = p[:, 256:384]
    kr = p[:, 384:512]
    gb_ref[...] = p[:, 512:1024]
    z_ref[...] = p[:, 1024:1536] * p[:, 1536:2048]
    cqn = cq * lax.rsqrt(jnp.mean(cq * cq, axis=-1, keepdims=True) + EPS) * qa_ref[...]
    q = _dot(cqn.astype(BF16), wuq_ref[...])
    scale = (C_NOPE + C_ROPE) ** -0.5 * LOG2E
    cos = cos_ref[0]
    sin = sin_ref[0]
    qnt_ref[...] = (q[:, 0:512] * scale).T.astype(BF16)
    qr = _rope(q[:, 512:768], _tile_lanes(cos, 2), _tile_lanes(sin, 2), C_ROPE // 4)
    qrt_ref[...] = (qr * scale).T.astype(BF16)
    ckv_ref[...] = ckv * lax.rsqrt(jnp.mean(ckv * ckv, axis=-1, keepdims=True) + EPS) * kva_ref[...]
    krr = _rope(kr, cos, sin, C_ROPE // 4)
    kr_ref[...] = krr
    krb_ref[...] = krr.astype(BF16)


def _odd_in(x, mod, n1, w_in, qa, kva, wuq, cos, sin):
    full = lambda shape: pl.BlockSpec(shape, lambda i: (0,) * len(shape))
    tile = lambda w: pl.BlockSpec((TILE, w), lambda i: (i, 0))
    return pl.pallas_call(
        _odd_in_kernel,
        grid=(NT,),
        in_specs=[
            tile(D_MODEL),
            pl.BlockSpec((1, 1, 6, D_MODEL), lambda i: (1, _mod_row(i), 0, 0)),
            full((1, D_MODEL)),
            full(w_in.shape),
            full((1, C_Q_LORA)),
            full((1, C_KV_LORA)),
            full(wuq.shape),
            pl.BlockSpec((1, TILE, LANES), lambda i: (_tab_blk(i), 0, 0)),
            pl.BlockSpec((1, TILE, LANES), lambda i: (_tab_blk(i), 0, 0)),
        ],
        out_specs=[pl.BlockSpec((512, TILE), lambda i: (0, i)),
                   pl.BlockSpec((256, TILE), lambda i: (0, i)),
                   tile(LANES), tile(LANES), tile(LANES), tile(512), tile(512)],
        out_shape=[
            jax.ShapeDtypeStruct((512, N_TOK), BF16),
            jax.ShapeDtypeStruct((256, N_TOK), BF16),
            jax.ShapeDtypeStruct((N_TOK, LANES), F32),
            jax.ShapeDtypeStruct((N_TOK, LANES), F32),
            jax.ShapeDtypeStruct((N_TOK, LANES), BF16),
            jax.ShapeDtypeStruct((N_TOK, 512), F32),
            jax.ShapeDtypeStruct((N_TOK, 512), F32),
        ],
        compiler_params=_cparams(("parallel",)),
        name="odd_in",
    )(x, mod, n1, w_in, qa, kva, wuq, cos, sin)


def _kvup_kernel(c_ref, w_ref, kn_ref, vt_ref):
    p = _dot(c_ref[...].astype(BF16), w_ref[...])
    kn_ref[...] = p[:, 0:512].astype(BF16)
    vt_ref[...] = p[:, 512:1024].T.astype(BF16)


def _kvup(ckv_all, w_ukv):
    n = ckv_all.shape[0]
    return pl.pallas_call(
        _kvup_kernel,
        grid=(n // TILE,),
        in_specs=[pl.BlockSpec((TILE, C_KV_LORA), lambda i: (i, 0)),
                  pl.BlockSpec(w_ukv.shape, lambda i: (0, 0))],
        out_specs=[pl.BlockSpec((TILE, 512), lambda i: (i, 0)), pl.BlockSpec((512, TILE), lambda i: (0, i))],
        out_shape=[jax.ShapeDtypeStruct((n, 512), BF16), jax.ShapeDtypeStruct((512, n), BF16)],
        compiler_params=_cparams(("parallel",)),
        name="mla_kv_up",
    )(ckv_all, w_ukv)


def _softmax_pv(qc, key_chunk, vt_chunks, nk):
    return _softmax_pv_streams([(qc, key_chunk, vt_chunks)], nk)[0]


def _softmax_pv_streams(streams, nk):
    ones = jnp.ones((BF16_ROWS, ATT_TK), BF16)
    ms = [jnp.full((1, qc.shape[1]), -jnp.inf, F32) for qc, _, _ in streams]
    accs = [jnp.zeros((HEAD_DIM + BF16_ROWS, qc.shape[1]), F32) for qc, _, _ in streams]
    nxt = [_dot(key_chunk(0), qc) for qc, key_chunk, _ in streams]
    for c in range(nk):
        for i, (qc, key_chunk, vt_chunks) in enumerate(streams):
            s = nxt[i]
            if c + 1 < nk:
                nxt[i] = _dot(key_chunk(c + 1), qc)
            mn = jnp.maximum(ms[i], jnp.max(s, axis=0, keepdims=True))
            p = jnp.exp2(s - mn).astype(BF16)
            vts = vt_chunks(c)
            if len(vts) == 1:
                pv = _dot(jnp.concatenate([vts[0], ones], axis=0), p)
            else:
                tq = qc.shape[1] // len(vts)
                pv = jnp.concatenate([_dot(jnp.concatenate([vt, ones], axis=0), p[:, k * tq:(k + 1) * tq])
                                      for k, vt in enumerate(vts)], axis=1)
            accs[i] = jnp.exp2(ms[i] - mn) * accs[i] + pv
            ms[i] = mn
    return [acc[0:HEAD_DIM, :] / acc[HEAD_DIM:HEAD_DIM + 1, :] for acc in accs]


def _gqa_kernel(q_ref, k_ref, vt_ref, o_ref):
    kvh = pl.program_id(1)
    nk = k_ref.shape[0] // ATT_TK
    tq = q_ref.shape[1]
    n = B_HEADS // B_KV_HEADS
    cols = []
    for hh in range(n):
        q = q_ref[hh * HEAD_DIM:(hh + 1) * HEAD_DIM, :]
        z = jnp.zeros_like(q)
        cols.append(jnp.where(kvh == 0, jnp.concatenate([q, z], axis=0), jnp.concatenate([z, q], axis=0)))
    key_chunk = lambda c: k_ref[c * ATT_TK:(c + 1) * ATT_TK, :]
    vt_chunks = lambda c: [vt_ref[:, c * ATT_TK:(c + 1) * ATT_TK]]
    per = n // GQA_STREAMS
    outs = _softmax_pv_streams(
        [(jnp.concatenate(cols[i * per:(i + 1) * per], axis=1), key_chunk, vt_chunks) for i in range(GQA_STREAMS)], nk)
    for hh in range(n):
        o = outs[hh // per][:, (hh % per) * tq:(hh % per + 1) * tq]
        o_ref[hh * HEAD_DIM:(hh + 1) * HEAD_DIM, :] = o.astype(BF16)


def _mla_kernel(qn_ref, qr_ref, kn_ref, kr_ref, vt_ref, o_ref):
    nk = kn_ref.shape[0] // ATT_TK
    tq = qn_ref.shape[1]
    z = jnp.zeros((C_NOPE, tq), BF16)
    zr = jnp.zeros((LANES - C_ROPE, tq), BF16)
    streams = []
    for pair in range(MLA_PAIRS):
        cols = []
        for hh in range(2):
            h = 2 * pair + hh
            qn = qn_ref[h * C_NOPE:(h + 1) * C_NOPE, :]
            qr = qr_ref[h * C_ROPE:(h + 1) * C_ROPE, :]
            cols.append(jnp.concatenate(([qn, z] if hh == 0 else [z, qn]) + [qr, zr], axis=0))

        def key_chunk(c, pair=pair):
            rows = slice(c * ATT_TK, (c + 1) * ATT_TK)
            return jnp.concatenate([kn_ref[rows, pair * LANES:(pair + 1) * LANES], kr_ref[rows, :]], axis=1)

        def vt_chunks(c, pair=pair):
            return [vt_ref[(2 * pair + hh) * C_V:(2 * pair + hh + 1) * C_V, c * ATT_TK:(c + 1) * ATT_TK]
                    for hh in range(2)]

        streams.append((jnp.concatenate(cols, axis=1), key_chunk, vt_chunks))
    outs = _softmax_pv_streams(streams, nk)
    for pair in range(MLA_PAIRS):
        for hh in range(2):
            h = 2 * pair + hh
            o_ref[h * C_V:(h + 1) * C_V, :] = outs[pair][:, hh * tq:(hh + 1) * tq].astype(BF16)


def _attn_call(body, grid, in_specs, out_spec, args, n_tiles):
    return pl.pallas_call(
        body,
        grid=grid,
        in_specs=in_specs,
        out_specs=out_spec,
        out_shape=jax.ShapeDtypeStruct((512, n_tiles * TILE), BF16),
        compiler_params=_cparams(("parallel",) * len(grid)),
        name="attention",
    )(*args)


def _gqa_attention(qt, kb, vt, k_lat, vt_lat):
    s = k_lat.shape[1]
    rows = (B_HEADS // B_KV_HEADS) * HEAD_DIM
    o_ctx = _attn_call(
        _gqa_kernel, (BATCH, B_KV_HEADS),
        [pl.BlockSpec((rows, TILE), lambda b, h: (h, b)),
         pl.BlockSpec((None, TILE, LANES), lambda b, h: (b, 0, 0)),
         pl.BlockSpec((HEAD_DIM, TILE), lambda b, h: (h, b))],
        pl.BlockSpec((rows, TILE), lambda b, h: (h, b)),
        [qt, kb.reshape(NT, TILE, LANES), vt], NT_CTX)
    o_lat = _attn_call(
        _gqa_kernel, (DEC_BATCH, B_KV_HEADS, NT_SEQ),
        [pl.BlockSpec((rows, TILE), lambda b, h, j: (h, NT_CTX + b * NT_SEQ + j)),
         pl.BlockSpec((None, s, LANES), lambda b, h, j: (b, 0, 0)),
         pl.BlockSpec((None, HEAD_DIM, s), lambda b, h, j: (b, h, 0))],
        pl.BlockSpec((rows, TILE), lambda b, h, j: (h, b * NT_SEQ + j)),
        [qt, k_lat, vt_lat], NT_LAT)
    return o_ctx, o_lat


def _mla_attention(qnt, qrt, kn_c, krb, vt_c, kn_l, kr_l, vt_l):
    s = kr_l.shape[1]
    nh = 2 * MLA_PAIRS
    steps = C_HEADS // nh
    o_ctx = _attn_call(
        _mla_kernel, (BATCH, steps),
        [pl.BlockSpec((nh * C_NOPE, TILE), lambda b, p: (p, b)),
         pl.BlockSpec((nh * C_ROPE, TILE), lambda b, p: (p, b)),
         pl.BlockSpec((TILE, MLA_PAIRS * LANES), lambda b, p: (b, p)),
         pl.BlockSpec((TILE, LANES), lambda b, p: (b, 0)),
         pl.BlockSpec((nh * C_V, TILE), lambda b, p: (p, b))],
        pl.BlockSpec((nh * C_V, TILE), lambda b, p: (p, b)),
        [qnt, qrt, kn_c, krb, vt_c], NT_CTX)
    o_lat = _attn_call(
        _mla_kernel, (DEC_BATCH, steps, NT_SEQ),
        [pl.BlockSpec((nh * C_NOPE, TILE), lambda b, p, j: (p, NT_CTX + b * NT_SEQ + j)),
         pl.BlockSpec((nh * C_ROPE, TILE), lambda b, p, j: (p, NT_CTX + b * NT_SEQ + j)),
         pl.BlockSpec((s, MLA_PAIRS * LANES), lambda b, p, j: (b, p)),
         pl.BlockSpec((None, s, LANES), lambda b, p, j: (b, 0, 0)),
         pl.BlockSpec((nh * C_V, s), lambda b, p, j: (p, b))],
        pl.BlockSpec((nh * C_V, TILE), lambda b, p, j: (p, b * NT_SEQ + j)),
        [qnt, qrt, kn_l, kr_l, vt_l], NT_LAT)
    return o_ctx, o_lat


def _group_tile(i, ctx_ref, lat_ref):
    return jnp.where(i < NT_CTX, ctx_ref[...], lat_ref[...])


_CTX_COL = lambda i: (0, jnp.minimum(i, NT_CTX - 1))
_LAT_COL = lambda i: (0, jnp.maximum(i - NT_CTX, 0))
_CTX_ROW = lambda i: (jnp.minimum(i, NT_CTX - 1), 0)
_LAT_ROW = lambda i: (jnp.maximum(i - NT_CTX, 0), 0)


def _finish_out(o, x_in, mod_ref, n2_ref, wrh_ref, wrl_ref, xo_ref, h2_ref, lg_ref):
    x = x_in + mod_ref[0, 0, 2:3, :] * o
    xo_ref[...] = x
    h2 = _norm_mod(x, n2_ref[...], mod_ref[0, 0, 4:5, :], mod_ref[0, 0, 3:4, :])
    hi = h2.astype(BF16)
    lo = (h2 - hi.astype(F32)).astype(BF16)
    h2_ref[...] = hi
    lg_ref[...] = _dot_nt(wrh_ref[...], hi) + _dot_nt(wrh_ref[...], lo) + _dot_nt(wrl_ref[...], hi)


def _dot_nt(a, bt):
    return lax.dot_general(a, bt, (((1,), (1,)), ((), ())), preferred_element_type=F32)


def _dot_t(at, b):
    return lax.dot_general(at, b, (((0,), (0,)), ((), ())), preferred_element_type=F32)


def _even_out_kernel(a_ref, btc_ref, btl_ref, w_ref, xc_ref, xl_ref, mod_ref, n2_ref, wrh_ref, wrl_ref,
                     xo_ref, h2_ref, lg_ref):
    i = pl.program_id(0)
    bt = _group_tile(i, btc_ref, btl_ref)
    o = _dot(a_ref[...], w_ref[0:512, :]) + _dot_t(bt, w_ref[512:1024, :])
    _finish_out(o, _group_tile(i, xc_ref, xl_ref), mod_ref, n2_ref, wrh_ref, wrl_ref, xo_ref, h2_ref, lg_ref)


def _odd_out_kernel(atc_ref, atl_ref, z_ref, zp_ref, zn_ref, gb_ref, cw_ref, w_ref, x_ref, mod_ref, n2_ref,
                    wrh_ref, wrl_ref, xo_ref, h2_ref, lg_ref):
    i = pl.program_id(0)
    at = _group_tile(i, atc_ref, atl_ref)
    j = (i - NT_CTX) % NT_SEQ
    first = jnp.logical_or(i < NT_CTX, j == 0)
    last = jnp.logical_or(i < NT_CTX, j == NT_SEQ - 1)
    z = z_ref[...]
    row = lax.broadcasted_iota(jnp.int32, z.shape, 0)
    halo_p = jnp.where(first, 0.0, zp_ref[7:8, :])
    halo_n = jnp.where(last, 0.0, zn_ref[0:1, :])
    zprev = jnp.where(row == 0, halo_p, pltpu.roll(z, 1, 0))
    znext = jnp.where(row == TILE - 1, halo_n, pltpu.roll(z, TILE - 1, 0))
    y = zprev * cw_ref[0:1, :] + z * cw_ref[1:2, :] + znext * cw_ref[2:3, :]
    d = (gb_ref[...] * y).astype(BF16)
    o = _dot_t(at, w_ref[0:512, :]) + _dot(d, w_ref[512:1024, :])
    _finish_out(o, x_ref[...], mod_ref, n2_ref, wrh_ref, wrl_ref, xo_ref, h2_ref, lg_ref)


_OUT_SHAPES = [
    jax.ShapeDtypeStruct((N_TOK, D_MODEL), F32),
    jax.ShapeDtypeStruct((N_TOK, D_MODEL), BF16),
    jax.ShapeDtypeStruct((N_EXPERTS, N_TOK), F32),
]
_LOGIT_SPEC = pl.BlockSpec((N_EXPERTS, TILE), lambda i: (0, i))


def _even_out(layer, a, b, w_out, xc, xl, mod, n2, wrh, wrl):
    full = lambda shape: pl.BlockSpec(shape, lambda i: (0,) * len(shape))
    tile = lambda w: pl.BlockSpec((TILE, w), lambda i: (i, 0))
    return pl.pallas_call(
        _even_out_kernel,
        grid=(NT,),
        in_specs=[tile(512), pl.BlockSpec((512, TILE), _CTX_COL), pl.BlockSpec((512, TILE), _LAT_COL),
                  full(w_out.shape), pl.BlockSpec((TILE, D_MODEL), _CTX_ROW), pl.BlockSpec((TILE, D_MODEL), _LAT_ROW),
                  pl.BlockSpec((1, 1, 6, D_MODEL), lambda i: (layer, _mod_row(i), 0, 0)),
                  full((1, D_MODEL)), full(wrh.shape), full(wrl.shape)],
        out_specs=[tile(D_MODEL), tile(D_MODEL), _LOGIT_SPEC],
        out_shape=_OUT_SHAPES,
        compiler_params=_cparams(("parallel",)),
        name="even_out",
    )(a, b[0], b[1], w_out, xc, xl, mod, n2, wrh, wrl)


def _odd_out(layer, a, z, gb, cw, w_out, x, mod, n2, wrh, wrl):
    full = lambda shape: pl.BlockSpec(shape, lambda i: (0,) * len(shape))
    tile = lambda w: pl.BlockSpec((TILE, w), lambda i: (i, 0))
    rb = TILE // 8
    return pl.pallas_call(
        _odd_out_kernel,
        grid=(NT,),
        in_specs=[pl.BlockSpec((512, TILE), _CTX_COL), pl.BlockSpec((512, TILE), _LAT_COL), tile(512),
                  pl.BlockSpec((8, 512), lambda i: (jnp.maximum(i * rb - 1, 0), 0)),
                  pl.BlockSpec((8, 512), lambda i: (jnp.minimum(i * rb + rb, NT * rb - 1), 0)),
                  tile(512), full(cw.shape), full(w_out.shape), tile(D_MODEL),
                  pl.BlockSpec((1, 1, 6, D_MODEL), lambda i: (layer, _mod_row(i), 0, 0)),
                  full((1, D_MODEL)), full(wrh.shape), full(wrl.shape)],
        out_specs=[tile(D_MODEL), tile(D_MODEL), _LOGIT_SPEC],
        out_shape=_OUT_SHAPES,
        compiler_params=_cparams(("parallel",)),
        name="odd_out",
    )(a[0], a[1], z, z, z, gb, cw, w_out, x, mod, n2, wrh, wrl)


def _select_tokens(aff, n_dom, cap, tri, pos_ref, gate_ref):
    w = aff.shape[1] // n_dom
    doms = [aff[:, d * w:(d + 1) * w] for d in range(n_dom)]
    bits = [jnp.zeros((N_EXPERTS, 1), jnp.int32) for _ in range(n_dom)]
    for bit in range(30, -1, -1):
        for d in range(n_dom):
            cand = bits[d] | (1 << bit)
            cnt = jnp.sum((doms[d] >= lax.bitcast_convert_type(cand, F32)).astype(F32), axis=1, keepdims=True)
            bits[d] = jnp.where(cnt >= cap, cand, bits[d])
    for d in range(n_dom):
        thr = lax.bitcast_convert_type(bits[d], F32)
        gt = doms[d] > thr
        eq = doms[d] == thr
        need = cap - jnp.sum(gt.astype(F32), axis=1, keepdims=True)
        eq_seen = jnp.zeros((N_EXPERTS, 1), F32)
        sel_seen = jnp.zeros((N_EXPERTS, 1), F32) + d * cap
        for blk in range(w // TILE):
            cols = slice(blk * TILE, (blk + 1) * TILE)
            eq_b = eq[:, cols].astype(F32)
            eq_rank = _dot(eq_b.astype(BF16), tri) + eq_seen
            sel = jnp.logical_or(gt[:, cols], jnp.logical_and(eq[:, cols], eq_rank < need))
            sel_f = sel.astype(F32)
            pos = _dot(sel_f.astype(BF16), tri) + sel_seen
            tile = d * (w // TILE) + blk
            pos_ref[0, tile] = jnp.where(sel, pos, -1.0)
            gate_ref[0, tile] = jnp.where(sel, doms[d][:, cols], 0.0)
            eq_seen = eq_seen + jnp.sum(eq_b, axis=1, keepdims=True)
            sel_seen = sel_seen + jnp.sum(sel_f, axis=1, keepdims=True)


def _route_kernel(lg_ref, pos_ref, gate_ref):
    lg = lg_ref[...]
    ex = jnp.exp(lg - jnp.max(lg, axis=0, keepdims=True))
    aff = ex / jnp.sum(ex, axis=0, keepdims=True)
    r = lax.broadcasted_iota(jnp.int32, (TILE, TILE), 0)
    c = lax.broadcasted_iota(jnp.int32, (TILE, TILE), 1)
    tri = (r < c).astype(BF16)

    @pl.when(pl.program_id(0) == 0)
    def _():
        _select_tokens(aff, GROUP // SEQ, EC_CAPACITY * SEQ // N_EXPERTS, tri, pos_ref, gate_ref)

    @pl.when(pl.program_id(0) > 0)
    def _():
        _select_tokens(aff, GROUP // DEC_SEQ, EC_CAPACITY * DEC_SEQ // N_EXPERTS, tri, pos_ref, gate_ref)


def _route(logits_t):
    blk = pl.BlockSpec((1, GROUP_TILES, N_EXPERTS, TILE), lambda g: (g, 0, 0, 0))
    shape = jax.ShapeDtypeStruct((N_GROUPS, GROUP_TILES, N_EXPERTS, TILE), F32)
    return pl.pallas_call(
        _route_kernel,
        grid=(N_GROUPS,),
        in_specs=[pl.BlockSpec((N_EXPERTS, GROUP), lambda g: (0, g))],
        out_specs=[blk, blk],
        out_shape=[shape, shape],
        compiler_params=_cparams(("parallel",)),
        name="route",
    )(logits_t)


def _moe_kernel(nblk_ref, ctab_ref, etab_ref, llen_ref,
                x_ref, pos_ref, gate_ref, wg_ref, wu_ref, wd_ref, xres_ref, mod_ref, fn_ref, *rest, final):
    n_out = 2 if final else 1
    out_refs = rest[:n_out]
    lists_ref, gl_ref, s_ref, gs_ref, acc_ref, wgb_ref, wub_ref, wdb_ref = rest[n_out:]
    g = pl.program_id(0)
    s = pl.program_id(1)
    chunks = STACK_BLK // CHUNK_ROWS
    rows16 = lax.broadcasted_iota(jnp.int32, (CHUNK_ROWS, TILE), 0)

    @pl.when(jnp.logical_and(g == 0, s == 0))
    def _():
        lists_ref[ZERO_ROW:ZERO_ROW + CHUNK_ROWS, :] = jnp.zeros((CHUNK_ROWS, D_MODEL), BF16)

    def chunk_rows(tbase, q, unused_row):
        d = ctab_ref[tbase + q]
        return pl.multiple_of(jnp.where(d >= 0, d, unused_row), CHUNK_ROWS)

    def build_block(j, rb, with_gate):
        tbase = ((g * GROUP_TILES + j) * MAX_STACK_BLKS + rb) * chunks
        for q in range(chunks):
            d = ctab_ref[tbase + q]
            e = etab_ref[tbase + q]
            posrow = pos_ref[0, j, pl.ds(e, 1), :]
            rowid = (rows16 + (rb * STACK_BLK + q * CHUNK_ROWS)).astype(F32)
            hit = jnp.logical_and(posrow == rowid, d >= 0)
            s_ref[q * CHUNK_ROWS:(q + 1) * CHUNK_ROWS, :] = hit.astype(BF16)
            if with_gate:
                gaterow = gate_ref[0, j, pl.ds(e, 1), :]
                gs_ref[q * CHUNK_ROWS:(q + 1) * CHUNK_ROWS, :] = jnp.where(hit, gaterow, 0.0)
        return tbase

    @pl.when(s < GROUP_TILES)
    def _gather():
        j = s

        def blk(rb, _):
            tbase = build_block(j, rb, True)
            picked = _dot(s_ref[...], x_ref[...]).astype(BF16)
            gcol = jnp.sum(gs_ref[...], axis=1, keepdims=True)
            for q in range(chunks):
                d = chunk_rows(tbase, q, DUMP_ROW)
                rows = slice(q * CHUNK_ROWS, (q + 1) * CHUNK_ROWS)
                lists_ref[pl.ds(d, CHUNK_ROWS), :] = picked[rows, :]
                gl_ref[pl.ds(d, CHUNK_ROWS), :] = gcol[rows, :]
            return 0

        lax.fori_loop(0, nblk_ref[g * GROUP_TILES + j], blk, 0)

    @pl.when(jnp.logical_and(s >= GROUP_TILES, s < 2 * GROUP_TILES))
    def _experts():
        e = s - GROUP_TILES
        ln = llen_ref[g * N_EXPERTS + e]
        base = e * LIST_MAX
        half = FFN_BLK // 2
        rem = ln % FFN_BLK
        use_half = jnp.logical_and(rem > 0, rem <= half)
        nfull = ln // FFN_BLK + jnp.where(rem > half, 1, 0)
        end = nfull * FFN_BLK + jnp.where(use_half, half, 0)

        def zero_tail(k, _):
            r0 = pl.multiple_of(base + ln + k * CHUNK_ROWS, CHUNK_ROWS)
            lists_ref[pl.ds(r0, CHUNK_ROWS), :] = jnp.zeros((CHUNK_ROWS, D_MODEL), BF16)
            gl_ref[pl.ds(r0, CHUNK_ROWS), :] = jnp.zeros((CHUNK_ROWS, 1), F32)
            return 0

        lax.fori_loop(0, (end - ln) // CHUNK_ROWS, zero_tail, 0)
        wgb_ref[...] = wg_ref[0, 0].astype(BF16)
        wub_ref[...] = wu_ref[0, 0].astype(BF16)
        wdb_ref[...] = wd_ref[0, 0].astype(BF16)

        def ffn_rows(r0, n):
            xs = lists_ref[pl.ds(r0, n), :]
            hid = _silu(_dot(xs, wgb_ref[...])) * _dot(xs, wub_ref[...])
            y = _dot(hid.astype(BF16), wdb_ref[...]) * gl_ref[pl.ds(r0, n), :]
            lists_ref[pl.ds(r0, n), :] = y.astype(BF16)

        def ffn(c, _):
            ffn_rows(pl.multiple_of(base + c * FFN_BLK, FFN_BLK), FFN_BLK)
            return 0

        lax.fori_loop(0, nfull, ffn, 0)

        @pl.when(use_half)
        def _():
            ffn_rows(pl.multiple_of(base + nfull * FFN_BLK, half), half)

    @pl.when(s >= 2 * GROUP_TILES)
    def _scatter():
        j = s - 2 * GROUP_TILES
        acc_ref[...] = jnp.zeros_like(acc_ref)

        def blk(rb, _):
            tbase = build_block(j, rb, False)
            y = jnp.concatenate([lists_ref[pl.ds(chunk_rows(tbase, q, ZERO_ROW), CHUNK_ROWS), :]
                                 for q in range(chunks)], axis=0)
            acc_ref[...] += _dot_t(s_ref[...], y)
            return 0

        lax.fori_loop(0, nblk_ref[g * GROUP_TILES + j], blk, 0)
        x = xres_ref[...] + mod_ref[0, 0, 5:6, :] * acc_ref[...]
        if final:
            y = x * lax.rsqrt(jnp.mean(x * x, axis=-1, keepdims=True) + EPS) * fn_ref[...]

            @pl.when(g == 0)
            def _():
                out_refs[0][...] = y

            @pl.when(g > 0)
            def _():
                out_refs[1][...] = y
        else:
            out_refs[0][...] = x


def _moe(layer, final, tables, h2, pos_t, gate_t, wg, wu, wd, xres, mod, fn):
    gt = GROUP_TILES
    tile_a = lambda g, s, *_: (g * gt + jnp.minimum(s, gt - 1), 0)
    tile_c = lambda g, s, *_: (g * gt + jnp.clip(s - 2 * gt, 0, gt - 1), 0)
    expert = lambda g, s, *_: (layer, jnp.clip(s - gt, 0, N_EXPERTS - 1), 0, 0)
    group = lambda g, s, *_: (g, 0, 0, 0)
    if final:
        step_c = lambda s: jnp.clip(s - 2 * gt, 0, gt - 1)
        out_specs = [
            pl.BlockSpec((TILE, D_MODEL), lambda g, s, *_: (jnp.where(g == 0, step_c(s), gt - 1), 0)),
            pl.BlockSpec((TILE, D_MODEL), lambda g, s, *_: (jnp.where(g == 0, 0, (g - 1) * gt + step_c(s)), 0)),
        ]
        out_shape = [jax.ShapeDtypeStruct((NT_CTX * TILE, D_MODEL), F32),
                     jax.ShapeDtypeStruct((NT_LAT * TILE, D_MODEL), F32)]
    else:
        out_specs = pl.BlockSpec((TILE, D_MODEL), tile_c)
        out_shape = jax.ShapeDtypeStruct((N_TOK, D_MODEL), F32)
    grid_spec = pltpu.PrefetchScalarGridSpec(
        num_scalar_prefetch=4,
        grid=(N_GROUPS, 3 * gt),
        in_specs=[
            pl.BlockSpec((TILE, D_MODEL), tile_a),
            pl.BlockSpec((1, gt, N_EXPERTS, TILE), group),
            pl.BlockSpec((1, gt, N_EXPERTS, TILE), group),
            pl.BlockSpec((1, 1, D_MODEL, EXPERT_FF), expert),
            pl.BlockSpec((1, 1, D_MODEL, EXPERT_FF), expert),
            pl.BlockSpec((1, 1, EXPERT_FF, D_MODEL), expert),
            pl.BlockSpec((TILE, D_MODEL), tile_c),
            pl.BlockSpec((1, 1, 6, D_MODEL), lambda g, s, *_: (layer, g, 0, 0)),
            pl.BlockSpec((1, D_MODEL), lambda g, s, *_: (0, 0)),
        ],
        out_specs=out_specs,
        scratch_shapes=[
            pltpu.VMEM((LIST_ROWS, D_MODEL), BF16),
            pltpu.VMEM((LIST_ROWS, 1), F32),
            pltpu.VMEM((STACK_BLK, TILE), BF16),
            pltpu.VMEM((STACK_BLK, TILE), F32),
            pltpu.VMEM((STACK_BLK, D_MODEL), F32),
            pltpu.VMEM((D_MODEL, EXPERT_FF), BF16),
            pltpu.VMEM((D_MODEL, EXPERT_FF), BF16),
            pltpu.VMEM((EXPERT_FF, D_MODEL), BF16),
        ],
    )
    return pl.pallas_call(
        functools.partial(_moe_kernel, final=final),
        grid_spec=grid_spec,
        out_shape=out_shape,
        compiler_params=_cparams(("arbitrary", "arbitrary")),
        name="moe",
    )(*tables, h2, pos_t, gate_t, wg, wu, wd, xres, mod, fn)


def _rope_tables(rot_dim):
    axis_dim = rot_dim // 2
    tok = jnp.arange(DEC_SEQ)
    rows = (tok // GRID_W).astype(F32)
    cols = (tok % GRID_W).astype(F32)
    inv_freq = ROPE_THETA ** (-jnp.arange(0, axis_dim, 2, dtype=F32) / axis_dim)
    ar = rows[:, None] * inv_freq
    ac = cols[:, None] * inv_freq
    cos = jnp.concatenate([jnp.cos(ar), jnp.cos(ar), jnp.cos(ac), jnp.cos(ac)], axis=1)
    sin = jnp.concatenate([-jnp.sin(ar), jnp.sin(ar), -jnp.sin(ac), jnp.sin(ac)], axis=1)
    rep = LANES // rot_dim
    cos = jnp.tile(cos, (1, rep)).reshape(NT_SEQ, TILE, LANES)
    sin = jnp.tile(sin, (1, rep)).reshape(NT_SEQ, TILE, LANES)
    cos = jnp.concatenate([jnp.ones((1, TILE, LANES), F32), cos], axis=0)
    sin = jnp.concatenate([jnp.zeros((1, TILE, LANES), F32), sin], axis=0)
    return cos, sin


def _excl_cumsum(a, axis):
    return jnp.cumsum(a, axis=axis) - a


def _route_and_moe(layer, final, h2, logits, wg, wu, wd, xres, mod, fn):
    pos_t, gate_t = _route(logits)
    sel = pos_t >= 0
    cnt = jnp.sum(sel, axis=-1, dtype=jnp.int32)
    npad = (cnt + CHUNK_ROWS - 1) // CHUNK_ROWS * CHUNK_ROWS
    seg_off = _excl_cumsum(npad, 2)
    list_off = _excl_cumsum(npad, 1)
    rank0 = _excl_cumsum(cnt, 1)
    stack_pos = jnp.where(sel, pos_t - rank0[..., None].astype(F32) + seg_off[..., None].astype(F32), -1.0)
    nblk = (jnp.sum(npad, axis=2) + STACK_BLK - 1) // STACK_BLK
    llen = jnp.sum(npad, axis=1)
    r = (jnp.arange(MAX_STACK_BLKS * STACK_BLK // CHUNK_ROWS, dtype=jnp.int32) * CHUNK_ROWS)[None, None, None, :]
    in_seg = jnp.logical_and(r >= seg_off[..., None], r < (seg_off + npad)[..., None])
    eidx = jnp.arange(N_EXPERTS, dtype=jnp.int32)[None, None, :, None]
    dst = jnp.sum(jnp.where(in_seg, eidx * LIST_MAX + list_off[..., None] + r - seg_off[..., None], 0), axis=2)
    used = jnp.any(in_seg, axis=2)
    ctab = jnp.where(used, dst, -1).astype(jnp.int32)
    etab = jnp.sum(jnp.where(in_seg, eidx, 0), axis=2).astype(jnp.int32)
    tables = (nblk.reshape(-1).astype(jnp.int32), ctab.reshape(-1), etab.reshape(-1),
              llen.reshape(-1).astype(jnp.int32))
    return _moe(layer, final, tables, h2, stack_pos, gate_t, wg, wu, wd, xres, mod, fn)


def _split_router(w):
    wt = w.T
    hi = wt.astype(BF16)
    lo = (wt - hi.astype(F32)).astype(BF16)
    return hi, lo


def kernel(x_prompt, x_sample, c, cache_attn_k, cache_attn_v, cache_mla_ckv, cache_mla_krope, c_ctx, w_mod, b_mod,
           norm1, norm2, ev_w_in, ev_q_norm, ev_k_norm, ev_w_s, ev_b_s, ev_w_out, od_w_in, od_q_a_norm,
           od_kv_a_norm, od_w_uq, od_w_ukv, od_conv_w, od_w_out, moe_router, moe_w_gate, moe_w_up, moe_w_down,
           final_norm):
    nctx = NT_CTX * TILE
    xc = x_prompt.reshape(nctx, D_MODEL)
    xl = x_sample.reshape(NT_LAT * TILE, D_MODEL)
    cvec = jnp.concatenate([c_ctx[None], c, jnp.zeros((3, D_MODEL), F32)], axis=0)
    mod = _modulation(cvec, w_mod, b_mod).reshape(2, 8, 6, D_MODEL)

    cos64, sin64 = _rope_tables(HEAD_DIM)
    seg = jnp.arange(512) // HEAD_DIM
    bd = (seg[:, None] == seg[None, :]).astype(BF16)
    bs_tab = jnp.repeat(ev_b_s[0].T, 64, axis=1)
    gated, qt, k, kb, v, vt = _even_in(
        xc, xl, mod, norm1[0:1], ev_w_in[0].astype(BF16), jnp.tile(ev_q_norm[0], 8)[None], jnp.tile(ev_k_norm[0], 2)[None],
        cos64, sin64, ev_w_s[0].astype(BF16), bs_tab, bd)
    new_k = k[:nctx].reshape(BATCH, SEQ, B_KV_HEADS, HEAD_DIM).transpose(0, 2, 1, 3)[:, None]
    new_v = v[:nctx].reshape(BATCH, SEQ, B_KV_HEADS, HEAD_DIM).transpose(0, 2, 1, 3)[:, None]
    ck = cache_attn_k[:, 0].transpose(0, 2, 1, 3).reshape(DEC_BATCH, PAST_LEN, LANES).astype(BF16)
    k_lat = jnp.concatenate([ck, kb[nctx:].reshape(DEC_BATCH, DEC_SEQ, LANES)], axis=1)
    cv = cache_attn_v[:, 0].transpose(0, 1, 3, 2).reshape(DEC_BATCH, LANES, PAST_LEN).astype(BF16)
    vt_lat = jnp.concatenate([cv, vt[:, nctx:].reshape(LANES, DEC_BATCH, DEC_SEQ).transpose(1, 0, 2)], axis=2)
    attn_t = _gqa_attention(qt, kb, vt, k_lat, vt_lat)
    wrh, wrl = _split_router(moe_router[0])
    x1, h2, logits = _even_out(0, gated, attn_t, ev_w_out[0].astype(BF16), xc, xl, mod, norm2[0:1], wrh, wrl)
    x2 = _route_and_moe(0, False, h2, logits, moe_w_gate, moe_w_up, moe_w_down, x1, mod, final_norm[None])

    cos32, sin32 = _rope_tables(C_ROPE)
    w_in1 = od_w_in[0]
    w_in1 = jnp.concatenate([w_in1[:, :C_Q_LORA + C_KV_LORA + C_ROPE], jnp.zeros((D_MODEL, LANES - C_ROPE), F32),
                             w_in1[:, C_Q_LORA + C_KV_LORA + C_ROPE:]], axis=1).astype(BF16)
    wuq = od_w_uq[0].reshape(C_Q_LORA, C_HEADS, C_NOPE + C_ROPE)
    wuq = jnp.concatenate([wuq[:, :, :C_NOPE].reshape(C_Q_LORA, -1), wuq[:, :, C_NOPE:].reshape(C_Q_LORA, -1)],
                          axis=1).astype(BF16)
    qnt, qrt, ckv, kr, krb, gb, z = _odd_in(x2, mod, norm1[1:2], w_in1, od_q_a_norm[0][None],
                                            od_kv_a_norm[0][None], wuq, cos32, sin32)
    new_ckv = ckv[:nctx].reshape(BATCH, 1, SEQ, C_KV_LORA)
    new_kr = kr[:nctx, :C_ROPE].reshape(BATCH, 1, SEQ, C_ROPE)
    wukv = od_w_ukv[0].reshape(C_KV_LORA, C_HEADS, C_NOPE + C_V)
    wukv = jnp.concatenate([wukv[:, :, :C_NOPE].reshape(C_KV_LORA, -1), wukv[:, :, C_NOPE:].reshape(C_KV_LORA, -1)],
                           axis=1).astype(BF16)
    ckv_lat = jnp.concatenate([cache_mla_ckv[:, 0], ckv[nctx:].reshape(DEC_BATCH, DEC_SEQ, C_KV_LORA)], axis=1)
    kn_c, vt_c = _kvup(ckv[:nctx], wukv)
    kn_l, vt_l = _kvup(ckv_lat.reshape(-1, C_KV_LORA), wukv)
    ckr = jnp.pad(cache_mla_krope[:, 0], ((0, 0), (0, 0), (0, LANES - C_ROPE))).astype(BF16)
    kr_l = jnp.concatenate([ckr, krb[nctx:].reshape(DEC_BATCH, DEC_SEQ, LANES)], axis=1)
    attn_t = _mla_attention(qnt, qrt, kn_c, krb, vt_c, kn_l, kr_l, vt_l)
    wrh, wrl = _split_router(moe_router[1])
    cw = jnp.concatenate([od_conv_w[0], jnp.zeros((5, D_WIDTH), F32)], axis=0)
    x3, h2, logits = _odd_out(1, attn_t, z, gb, cw, od_w_out[0].astype(BF16), x2, mod, norm2[1:2], wrh, wrl)
    y_c, y_l = _route_and_moe(1, True, h2, logits, moe_w_gate, moe_w_up, moe_w_down, x3, mod, final_norm[None])
    y_prompt = y_c.reshape(BATCH, SEQ, D_MODEL)
    y_sample = y_l.reshape(DEC_BATCH, DEC_SEQ, D_MODEL)
    return y_prompt, y_sample, new_k, new_v, new_ckv, new_kr
```

```python
import functools
import math

import jax
import jax.numpy as jnp
from jax import lax
from jax.experimental import pallas as pl
from jax.experimental.pallas import tpu as pltpu

F32 = jnp.float32
BF16 = jnp.bfloat16

D_MODEL = 1024
BATCH = 16
SEQ = 256
DEC_BATCH = 4
DEC_SEQ = 4096
PAST_LEN = 512
GRID_W = 64
ROPE_THETA = 10000.0
EPS = 1e-6
HEAD_DIM = 64
CHUNK = 128
A_GROUPS = 8
A_WIDTH = 512
B_HEADS = 8
B_KV_HEADS = 2
B_WIDTH = 512
C_HEADS = 8
C_NOPE = 64
C_ROPE = 32
C_V = 64
C_Q_LORA = 256
C_KV_LORA = 128
D_WIDTH = 512
N_EXPERTS = 16
EXPERT_FF = 512
EC_CAPACITY = 2

LOG2E = math.log2(math.e)
LANES = 128
BF16_ROWS = 16
ATT_TK = 256
MLA_PAIRS = 2
GQA_STREAMS = 2
TILE = 256
NT_CTX = BATCH * SEQ // TILE
NT_SEQ = DEC_SEQ // TILE
NT_LAT = DEC_BATCH * NT_SEQ
NT = NT_CTX + NT_LAT
N_TOK = NT * TILE
GROUP = 4096
N_GROUPS = N_TOK // GROUP
GROUP_TILES = GROUP // TILE
GROUP_CAP = EC_CAPACITY * GROUP // N_EXPERTS
CHUNK_ROWS = BF16_ROWS
STACK_BLK = 256
MAX_STACK_BLKS = N_EXPERTS * TILE // STACK_BLK
MOE_STEP_TILES = 2
MOE_TILE_STEPS = GROUP_TILES // MOE_STEP_TILES
SCATTER_STEP_TILES = 1
SCATTER_STEPS = GROUP_TILES // SCATTER_STEP_TILES
FFN_BLK = 256
LIST_MAX = -(-(GROUP_CAP + GROUP_TILES * (CHUNK_ROWS - 1)) // FFN_BLK) * FFN_BLK
DUMP_ROW = N_EXPERTS * LIST_MAX
ZERO_ROW = DUMP_ROW + CHUNK_ROWS
LIST_ROWS = ZERO_ROW + CHUNK_ROWS
VMEM_LIMIT = 58 * 1024 * 1024


def _cparams(sem):
    return pltpu.CompilerParams(dimension_semantics=sem, vmem_limit_bytes=VMEM_LIMIT)


def _dot(a, b):
    return jnp.dot(a, b, preferred_element_type=F32)


def _silu(x):
    return x / (1.0 + jnp.exp(-x))


def _mod_row(i):
    return jnp.where(i < NT_CTX, 0, 1 + (i - NT_CTX) // NT_SEQ)


def _tab_blk(i):
    return jnp.where(i < NT_CTX, 0, 1 + (i - NT_CTX) % NT_SEQ)


def _mod_kernel(c_ref, w_ref, b_ref, o_ref):
    s = _silu(c_ref[...])
    o_ref[0] = _dot(s.astype(BF16), w_ref[0].astype(BF16)) + b_ref[0]


def _modulation(cvec, w_mod, b_mod):
    depth = w_mod.shape[0]
    nchunk = 6
    return pl.pallas_call(
        _mod_kernel,
        grid=(depth, nchunk),
        in_specs=[
            pl.BlockSpec((8, D_MODEL), lambda l, k: (0, 0)),
            pl.BlockSpec((1, D_MODEL, D_MODEL), lambda l, k: (l, 0, k)),
            pl.BlockSpec((1, 1, D_MODEL), lambda l, k: (l, 0, k)),
        ],
        out_specs=pl.BlockSpec((1, 8, D_MODEL), lambda l, k: (l, 0, k)),
        out_shape=jax.ShapeDtypeStruct((depth, 8, 6 * D_MODEL), F32),
        compiler_params=_cparams(("arbitrary", "arbitrary")),
        name="modulation",
    )(cvec, w_mod, b_mod.reshape(depth, 1, 6 * D_MODEL))


def _norm_mod(x, gain, scale, shift):
    ms = jnp.mean(x * x, axis=-1, keepdims=True)
    return (x * lax.rsqrt(ms + EPS) * gain) * (1.0 + scale) + shift


def _seg_mean_sq(z, bd, width):
    zz = z * z
    hi = zz.astype(BF16)
    lo = (zz - hi.astype(F32)).astype(BF16)
    return (_dot(hi, bd) + _dot(lo, bd)) * (1.0 / width)


def _rope(z, cos, sin_signed, half):
    w = z.shape[1]
    lane = lax.broadcasted_iota(jnp.int32, z.shape, 1)
    first = (lane % (2 * half)) < half
    partner = jnp.where(first, pltpu.roll(z, w - half, 1), pltpu.roll(z, half, 1))
    return z * cos + partner * sin_signed


def _tile_lanes(t, n):
    return jnp.concatenate([t] * n, axis=1) if n > 1 else t


def _even_in_kernel(xc_ref, xl_ref, mod_ref, n1_ref, w_ref, qg_ref, kg_ref, cos_ref, sin_ref, ws_ref, bs_ref, bd_ref,
                    gated_ref, qt_ref, k_ref, kb_ref, v_ref, vt_ref):
    x = _group_tile(pl.program_id(0), xc_ref, xl_ref)
    h = _norm_mod(x, n1_ref[...], mod_ref[0, 0, 1:2, :], mod_ref[0, 0, 0:1, :])
    p = _dot(h.astype(BF16), w_ref[...])
    u = p[:, 0:512]
    va = p[:, 512:1024].astype(BF16)
    q = p[:, 1024:1536]
    k = p[:, 1536:1664]
    v = p[:, 1664:1792]
    bd = bd_ref[...]
    cos = cos_ref[0]
    sin = sin_ref[0]
    qn = q * lax.rsqrt(_seg_mean_sq(q, bd, HEAD_DIM) + EPS) * qg_ref[...]
    kn = k * lax.rsqrt(_seg_mean_sq(k, bd[0:LANES, 0:LANES], HEAD_DIM) + EPS) * kg_ref[...]
    qr = _rope(qn, _tile_lanes(cos, 4), _tile_lanes(sin, 4), HEAD_DIM // 4)
    kr = _rope(kn, cos, sin, HEAD_DIM // 4)
    qt_ref[...] = (qr * (HEAD_DIM ** -0.5 * LOG2E)).T.astype(BF16)
    k_ref[...] = kr
    kb_ref[...] = kr.astype(BF16)
    v_ref[...] = v
    vt_ref[...] = v.T.astype(BF16)
    lane = lax.broadcasted_iota(jnp.int32, (CHUNK, LANES), 1)
    for ch in range(TILE // CHUNK):
        rows = slice(ch * CHUNK, (ch + 1) * CHUNK)
        cols = []
        for pair in range(A_GROUPS // 2):
            vp = va[rows, pair * LANES:(pair + 1) * LANES]
            r0 = _dot(ws_ref[2 * pair], vp)
            r1 = _dot(ws_ref[2 * pair + 1], vp)
            cols.append(jnp.where(lane < LANES // 2, r0, r1))
        s = jnp.concatenate(cols, axis=1) + bs_ref[...]
        gated_ref[rows, :] = (u[rows, :] * s).astype(BF16)


def _even_in(xc, xl, mod, n1, w_in, qg, kg, cos, sin, ws, bs_tab, bd):
    full = lambda shape: pl.BlockSpec(shape, lambda i: (0,) * len(shape))
    return pl.pallas_call(
        _even_in_kernel,
        grid=(NT,),
        in_specs=[
            pl.BlockSpec((TILE, D_MODEL), _CTX_ROW),
            pl.BlockSpec((TILE, D_MODEL), _LAT_ROW),
            pl.BlockSpec((1, 1, 6, D_MODEL), lambda i: (0, _mod_row(i), 0, 0)),
            full((1, D_MODEL)),
            full(w_in.shape),
            full((1, 512)),
            full((1, LANES)),
            pl.BlockSpec((1, TILE, LANES), lambda i: (_tab_blk(i), 0, 0)),
            pl.BlockSpec((1, TILE, LANES), lambda i: (_tab_blk(i), 0, 0)),
            full(ws.shape),
            full(bs_tab.shape),
            full(bd.shape),
        ],
        out_specs=[
            pl.BlockSpec((TILE, 512), lambda i: (i, 0)),
            pl.BlockSpec((512, TILE), lambda i: (0, i)),
            pl.BlockSpec((TILE, LANES), lambda i: (i, 0)),
            pl.BlockSpec((TILE, LANES), lambda i: (i, 0)),
            pl.BlockSpec((TILE, LANES), lambda i: (i, 0)),
            pl.BlockSpec((LANES, TILE), lambda i: (0, i)),
        ],
        out_shape=[
            jax.ShapeDtypeStruct((N_TOK, 512), BF16),
            jax.ShapeDtypeStruct((512, N_TOK), BF16),
            jax.ShapeDtypeStruct((N_TOK, LANES), F32),
            jax.ShapeDtypeStruct((N_TOK, LANES), BF16),
            jax.ShapeDtypeStruct((N_TOK, LANES), F32),
            jax.ShapeDtypeStruct((LANES, N_TOK), BF16),
        ],
        compiler_params=_cparams(("parallel",)),
        name="even_in",
    )(xc, xl, mod, n1, w_in, qg, kg, cos, sin, ws, bs_tab, bd)


def _odd_in_kernel(x_ref, mod_ref, n1_ref, w_ref, qa_ref, kva_ref, wuq_ref, cos_ref, sin_ref,
                   qnt_ref, qrt_ref, ckv_ref, kr_ref, krb_ref, gb_ref, z_ref):
    h = _norm_mod(x_ref[...], n1_ref[...], mod_ref[0, 0, 1:2, :], mod_ref[0, 0, 0:1, :])
    p = _dot(h.astype(BF16), w_ref[...])
    cq = p[:, 0:256]
    ckv = p[:, 256:384]
    kr = p[:, 384:512]
    gb_ref[...] = p[:, 512:1024]
    z_ref[...] = p[:, 1024:1536] * p[:, 1536:2048]
    cqn = cq * lax.rsqrt(jnp.mean(cq * cq, axis=-1, keepdims=True) + EPS) * qa_ref[...]
    q = _dot(cqn.astype(BF16), wuq_ref[...])
    scale = (C_NOPE + C_ROPE) ** -0.5 * LOG2E
    cos = cos_ref[0]
    sin = sin_ref[0]
    qnt_ref[...] = (q[:, 0:512] * scale).T.astype(BF16)
    qr = _rope(q[:, 512:768], _tile_lanes(cos, 2), _tile_lanes(sin, 2), C_ROPE // 4)
    qrt_ref[...] = (qr * scale).T.astype(BF16)
    ckv_ref[...] = ckv * lax.rsqrt(jnp.mean(ckv * ckv, axis=-1, keepdims=True) + EPS) * kva_ref[...]
    krr = _rope(kr, cos, sin, C_ROPE // 4)
    kr_ref[...] = krr
    krb_ref[...] = krr.astype(BF16)


def _odd_in(x, mod, n1, w_in, qa, kva, wuq, cos, sin):
    full = lambda shape: pl.BlockSpec(shape, lambda i: (0,) * len(shape))
    tile = lambda w: pl.BlockSpec((TILE, w), lambda i: (i, 0))
    return pl.pallas_call(
        _odd_in_kernel,
        grid=(NT,),
        in_specs=[
            tile(D_MODEL),
            pl.BlockSpec((1, 1, 6, D_MODEL), lambda i: (1, _mod_row(i), 0, 0)),
            full((1, D_MODEL)),
            full(w_in.shape),
            full((1, C_Q_LORA)),
            full((1, C_KV_LORA)),
            full(wuq.shape),
            pl.BlockSpec((1, TILE, LANES), lambda i: (_tab_blk(i), 0, 0)),
            pl.BlockSpec((1, TILE, LANES), lambda i: (_tab_blk(i), 0, 0)),
        ],
        out_specs=[pl.BlockSpec((512, TILE), lambda i: (0, i)),
                   pl.BlockSpec((256, TILE), lambda i: (0, i)),
                   tile(LANES), tile(LANES), tile(LANES), tile(512), tile(512)],
        out_shape=[
            jax.ShapeDtypeStruct((512, N_TOK), BF16),
            jax.ShapeDtypeStruct((256, N_TOK), BF16),
            jax.ShapeDtypeStruct((N_TOK, LANES), F32),
            jax.ShapeDtypeStruct((N_TOK, LANES), F32),
            jax.ShapeDtypeStruct((N_TOK, LANES), BF16),
            jax.ShapeDtypeStruct((N_TOK, 512), F32),
            jax.ShapeDtypeStruct((N_TOK, 512), F32),
        ],
        compiler_params=_cparams(("parallel",)),
        name="odd_in",
    )(x, mod, n1, w_in, qa, kva, wuq, cos, sin)


def _kvup_kernel(c_ref, w_ref, kn_ref, vt_ref):
    p = _dot(c_ref[...].astype(BF16), w_ref[...])
    kn_ref[...] = p[:, 0:512].astype(BF16)
    vt_ref[...] = p[:, 512:1024].T.astype(BF16)


def _kvup(ckv_all, w_ukv):
    n = ckv_all.shape[0]
    return pl.pallas_call(
        _kvup_kernel,
        grid=(n // TILE,),
        in_specs=[pl.BlockSpec((TILE, C_KV_LORA), lambda i: (i, 0)),
                  pl.BlockSpec(w_ukv.shape, lambda i: (0, 0))],
        out_specs=[pl.BlockSpec((TILE, 512), lambda i: (i, 0)), pl.BlockSpec((512, TILE), lambda i: (0, i))],
        out_shape=[jax.ShapeDtypeStruct((n, 512), BF16), jax.ShapeDtypeStruct((512, n), BF16)],
        compiler_params=_cparams(("parallel",)),
        name="mla_kv_up",
    )(ckv_all, w_ukv)


def _softmax_pv(qc, key_chunk, vt_chunks, nk):
    return _softmax_pv_streams([(qc, key_chunk, vt_chunks)], nk)[0]


def _softmax_pv_streams(streams, nk):
    ones = jnp.ones((BF16_ROWS, ATT_TK), BF16)
    ms = [jnp.full((1, qc.shape[1]), -jnp.inf, F32) for qc, _, _ in streams]
    accs = [jnp.zeros((HEAD_DIM + BF16_ROWS, qc.shape[1]), F32) for qc, _, _ in streams]
    nxt = [_dot(key_chunk(0), qc) for qc, key_chunk, _ in streams]
    for c in range(nk):
        for i, (qc, key_chunk, vt_chunks) in enumerate(streams):
            s = nxt[i]
            if c + 1 < nk:
                nxt[i] = _dot(key_chunk(c + 1), qc)
            mn = jnp.maximum(ms[i], jnp.max(s, axis=0, keepdims=True))
            p = jnp.exp2(s - mn).astype(BF16)
            vts = vt_chunks(c)
            if len(vts) == 1:
                pv = _dot(jnp.concatenate([vts[0], ones], axis=0), p)
            else:
                tq = qc.shape[1] // len(vts)
                pv = jnp.concatenate([_dot(jnp.concatenate([vt, ones], axis=0), p[:, k * tq:(k + 1) * tq])
                                      for k, vt in enumerate(vts)], axis=1)
            accs[i] = jnp.exp2(ms[i] - mn) * accs[i] + pv
            ms[i] = mn
    return [acc[0:HEAD_DIM, :] / acc[HEAD_DIM:HEAD_DIM + 1, :] for acc in accs]


def _gqa_kernel(q_ref, k_ref, vt_ref, o_ref):
    kvh = pl.program_id(1)
    nk = k_ref.shape[0] // ATT_TK
    tq = q_ref.shape[1]
    n = B_HEADS // B_KV_HEADS
    cols = []
    for hh in range(n):
        q = q_ref[hh * HEAD_DIM:(hh + 1) * HEAD_DIM, :]
        z = jnp.zeros_like(q)
        cols.append(jnp.where(kvh == 0, jnp.concatenate([q, z], axis=0), jnp.concatenate([z, q], axis=0)))
    key_chunk = lambda c: k_ref[c * ATT_TK:(c + 1) * ATT_TK, :]
    vt_chunks = lambda c: [vt_ref[:, c * ATT_TK:(c + 1) * ATT_TK]]
    per = n // GQA_STREAMS
    outs = _softmax_pv_streams(
        [(jnp.concatenate(cols[i * per:(i + 1) * per], axis=1), key_chunk, vt_chunks) for i in range(GQA_STREAMS)], nk)
    for hh in range(n):
        o = outs[hh // per][:, (hh % per) * tq:(hh % per + 1) * tq]
        o_ref[hh * HEAD_DIM:(hh + 1) * HEAD_DIM, :] = o.astype(BF16)


def _mla_kernel(qn_ref, qr_ref, kn_ref, kr_ref, vt_ref, o_ref):
    nk = kn_ref.shape[0] // ATT_TK
    tq = qn_ref.shape[1]
    z = jnp.zeros((C_NOPE, tq), BF16)
    zr = jnp.zeros((LANES - C_ROPE, tq), BF16)
    streams = []
    for pair in range(MLA_PAIRS):
        cols = []
        for hh in range(2):
            h = 2 * pair + hh
            qn = qn_ref[h * C_NOPE:(h + 1) * C_NOPE, :]
            qr = qr_ref[h * C_ROPE:(h + 1) * C_ROPE, :]
            cols.append(jnp.concatenate(([qn, z] if hh == 0 else [z, qn]) + [qr, zr], axis=0))

        def key_chunk(c, pair=pair):
            rows = slice(c * ATT_TK, (c + 1) * ATT_TK)
            return jnp.concatenate([kn_ref[rows, pair * LANES:(pair + 1) * LANES], kr_ref[rows, :]], axis=1)

        def vt_chunks(c, pair=pair):
            return [vt_ref[(2 * pair + hh) * C_V:(2 * pair + hh + 1) * C_V, c * ATT_TK:(c + 1) * ATT_TK]
                    for hh in range(2)]

        streams.append((jnp.concatenate(cols, axis=1), key_chunk, vt_chunks))
    outs = _softmax_pv_streams(streams, nk)
    for pair in range(MLA_PAIRS):
        for hh in range(2):
            h = 2 * pair + hh
            o_ref[h * C_V:(h + 1) * C_V, :] = outs[pair][:, hh * tq:(hh + 1) * tq].astype(BF16)


def _attn_call(body, grid, in_specs, out_spec, args, n_tiles):
    return pl.pallas_call(
        body,
        grid=grid,
        in_specs=in_specs,
        out_specs=out_spec,
        out_shape=jax.ShapeDtypeStruct((512, n_tiles * TILE), BF16),
        compiler_params=_cparams(("parallel",) * len(grid)),
        name="attention",
    )(*args)


def _gqa_attention(qt, kb, vt, k_lat, vt_lat):
    s = k_lat.shape[1]
    rows = (B_HEADS // B_KV_HEADS) * HEAD_DIM
    o_ctx = _attn_call(
        _gqa_kernel, (BATCH, B_KV_HEADS),
        [pl.BlockSpec((rows, TILE), lambda b, h: (h, b)),
         pl.BlockSpec((None, TILE, LANES), lambda b, h: (b, 0, 0)),
         pl.BlockSpec((HEAD_DIM, TILE), lambda b, h: (h, b))],
        pl.BlockSpec((rows, TILE), lambda b, h: (h, b)),
        [qt, kb.reshape(NT, TILE, LANES), vt], NT_CTX)
    o_lat = _attn_call(
        _gqa_kernel, (DEC_BATCH, B_KV_HEADS, NT_SEQ),
        [pl.BlockSpec((rows, TILE), lambda b, h, j: (h, NT_CTX + b * NT_SEQ + j)),
         pl.BlockSpec((None, s, LANES), lambda b, h, j: (b, 0, 0)),
         pl.BlockSpec((None, HEAD_DIM, s), lambda b, h, j: (b, h, 0))],
        pl.BlockSpec((rows, TILE), lambda b, h, j: (h, b * NT_SEQ + j)),
        [qt, k_lat, vt_lat], NT_LAT)
    return o_ctx, o_lat


def _mla_attention(qnt, qrt, kn_c, krb, vt_c, kn_l, kr_l, vt_l):
    s = kr_l.shape[1]
    nh = 2 * MLA_PAIRS
    steps = C_HEADS // nh
    o_ctx = _attn_call(
        _mla_kernel, (BATCH, steps),
        [pl.BlockSpec((nh * C_NOPE, TILE), lambda b, p: (p, b)),
         pl.BlockSpec((nh * C_ROPE, TILE), lambda b, p: (p, b)),
         pl.BlockSpec((TILE, MLA_PAIRS * LANES), lambda b, p: (b, p)),
         pl.BlockSpec((TILE, LANES), lambda b, p: (b, 0)),
         pl.BlockSpec((nh * C_V, TILE), lambda b, p: (p, b))],
        pl.BlockSpec((nh * C_V, TILE), lambda b, p: (p, b)),
        [qnt, qrt, kn_c, krb, vt_c], NT_CTX)
    o_lat = _attn_call(
        _mla_kernel, (DEC_BATCH, steps, NT_SEQ),
        [pl.BlockSpec((nh * C_NOPE, TILE), lambda b, p, j: (p, NT_CTX + b * NT_SEQ + j)),
         pl.BlockSpec((nh * C_ROPE, TILE), lambda b, p, j: (p, NT_CTX + b * NT_SEQ + j)),
         pl.BlockSpec((s, MLA_PAIRS * LANES), lambda b, p, j: (b, p)),
         pl.BlockSpec((None, s, LANES), lambda b, p, j: (b, 0, 0)),
         pl.BlockSpec((nh * C_V, s), lambda b, p, j: (p, b))],
        pl.BlockSpec((nh * C_V, TILE), lambda b, p, j: (p, b * NT_SEQ + j)),
        [qnt, qrt, kn_l, kr_l, vt_l], NT_LAT)
    return o_ctx, o_lat


def _group_tile(i, ctx_ref, lat_ref):
    return jnp.where(i < NT_CTX, ctx_ref[...], lat_ref[...])


_CTX_COL = lambda i: (0, jnp.minimum(i, NT_CTX - 1))
_LAT_COL = lambda i: (0, jnp.maximum(i - NT_CTX, 0))
_CTX_ROW = lambda i: (jnp.minimum(i, NT_CTX - 1), 0)
_LAT_ROW = lambda i: (jnp.maximum(i - NT_CTX, 0), 0)


def _finish_out(o, x_in, mod_ref, n2_ref, wrh_ref, wrl_ref, xo_ref, h2_ref, lg_ref):
    x = x_in + mod_ref[0, 0, 2:3, :] * o
    xo_ref[...] = x
    h2 = _norm_mod(x, n2_ref[...], mod_ref[0, 0, 4:5, :], mod_ref[0, 0, 3:4, :])
    hi = h2.astype(BF16)
    lo = (h2 - hi.astype(F32)).astype(BF16)
    h2_ref[...] = hi
    lg = _dot(hi, wrh_ref[...]) + _dot(lo, wrh_ref[...]) + _dot(hi, wrl_ref[...])
    lg_ref[...] = lg.T[0:N_EXPERTS, :]


def _dot_nt(a, bt):
    return lax.dot_general(a, bt, (((1,), (1,)), ((), ())), preferred_element_type=F32)


def _dot_t(at, b):
    return lax.dot_general(at, b, (((0,), (0,)), ((), ())), preferred_element_type=F32)


def _even_out_kernel(a_ref, btc_ref, btl_ref, w_ref, xc_ref, xl_ref, mod_ref, n2_ref, wrh_ref, wrl_ref,
                     xo_ref, h2_ref, lg_ref):
    i = pl.program_id(0)
    bt = _group_tile(i, btc_ref, btl_ref)
    o = _dot(a_ref[...], w_ref[0:512, :]) + _dot_t(bt, w_ref[512:1024, :])
    _finish_out(o, _group_tile(i, xc_ref, xl_ref), mod_ref, n2_ref, wrh_ref, wrl_ref, xo_ref, h2_ref, lg_ref)


def _odd_out_kernel(atc_ref, atl_ref, z_ref, zp_ref, zn_ref, gb_ref, cw_ref, w_ref, x_ref, mod_ref, n2_ref,
                    wrh_ref, wrl_ref, xo_ref, h2_ref, lg_ref):
    i = pl.program_id(0)
    at = _group_tile(i, atc_ref, atl_ref)
    j = (i - NT_CTX) % NT_SEQ
    first = jnp.logical_or(i < NT_CTX, j == 0)
    last = jnp.logical_or(i < NT_CTX, j == NT_SEQ - 1)
    z = z_ref[...]
    row = lax.broadcasted_iota(jnp.int32, z.shape, 0)
    halo_p = jnp.where(first, 0.0, zp_ref[7:8, :])
    halo_n = jnp.where(last, 0.0, zn_ref[0:1, :])
    zprev = jnp.where(row == 0, halo_p, pltpu.roll(z, 1, 0))
    znext = jnp.where(row == TILE - 1, halo_n, pltpu.roll(z, TILE - 1, 0))
    y = zprev * cw_ref[0:1, :] + z * cw_ref[1:2, :] + znext * cw_ref[2:3, :]
    d = (gb_ref[...] * y).astype(BF16)
    o = _dot_t(at, w_ref[0:512, :]) + _dot(d, w_ref[512:1024, :])
    _finish_out(o, x_ref[...], mod_ref, n2_ref, wrh_ref, wrl_ref, xo_ref, h2_ref, lg_ref)


_OUT_SHAPES = [
    jax.ShapeDtypeStruct((N_TOK, D_MODEL), F32),
    jax.ShapeDtypeStruct((N_TOK, D_MODEL), BF16),
    jax.ShapeDtypeStruct((N_EXPERTS, N_TOK), F32),
]
_LOGIT_SPEC = pl.BlockSpec((N_EXPERTS, TILE), lambda i: (0, i))


def _even_out(layer, a, b, w_out, xc, xl, mod, n2, wrh, wrl):
    full = lambda shape: pl.BlockSpec(shape, lambda i: (0,) * len(shape))
    tile = lambda w: pl.BlockSpec((TILE, w), lambda i: (i, 0))
    return pl.pallas_call(
        _even_out_kernel,
        grid=(NT,),
        in_specs=[tile(512), pl.BlockSpec((512, TILE), _CTX_COL), pl.BlockSpec((512, TILE), _LAT_COL),
                  full(w_out.shape), pl.BlockSpec((TILE, D_MODEL), _CTX_ROW), pl.BlockSpec((TILE, D_MODEL), _LAT_ROW),
                  pl.BlockSpec((1, 1, 6, D_MODEL), lambda i: (layer, _mod_row(i), 0, 0)),
                  full((1, D_MODEL)), full(wrh.shape), full(wrl.shape)],
        out_specs=[tile(D_MODEL), tile(D_MODEL), _LOGIT_SPEC],
        out_shape=_OUT_SHAPES,
        compiler_params=_cparams(("parallel",)),
        name="even_out",
    )(a, b[0], b[1], w_out, xc, xl, mod, n2, wrh, wrl)


def _odd_out(layer, a, z, gb, cw, w_out, x, mod, n2, wrh, wrl):
    full = lambda shape: pl.BlockSpec(shape, lambda i: (0,) * len(shape))
    tile = lambda w: pl.BlockSpec((TILE, w), lambda i: (i, 0))
    rb = TILE // 8
    return pl.pallas_call(
        _odd_out_kernel,
        grid=(NT,),
        in_specs=[pl.BlockSpec((512, TILE), _CTX_COL), pl.BlockSpec((512, TILE), _LAT_COL), tile(512),
                  pl.BlockSpec((8, 512), lambda i: (jnp.maximum(i * rb - 1, 0), 0)),
                  pl.BlockSpec((8, 512), lambda i: (jnp.minimum(i * rb + rb, NT * rb - 1), 0)),
                  tile(512), full(cw.shape), full(w_out.shape), tile(D_MODEL),
                  pl.BlockSpec((1, 1, 6, D_MODEL), lambda i: (layer, _mod_row(i), 0, 0)),
                  full((1, D_MODEL)), full(wrh.shape), full(wrl.shape)],
        out_specs=[tile(D_MODEL), tile(D_MODEL), _LOGIT_SPEC],
        out_shape=_OUT_SHAPES,
        compiler_params=_cparams(("parallel",)),
        name="odd_out",
    )(a[0], a[1], z, z, z, gb, cw, w_out, x, mod, n2, wrh, wrl)


def _select_tokens(aff, n_dom, cap, tri, pos_ref, gate_ref):
    w = aff.shape[1] // n_dom
    doms = [aff[:, d * w:(d + 1) * w] for d in range(n_dom)]
    bits = [jnp.zeros((N_EXPERTS, 1), jnp.int32) for _ in range(n_dom)]
    for bit in range(30, -1, -1):
        for d in range(n_dom):
            cand = bits[d] | (1 << bit)
            cnt = jnp.sum((doms[d] >= lax.bitcast_convert_type(cand, F32)).astype(F32), axis=1, keepdims=True)
            bits[d] = jnp.where(cnt >= cap, cand, bits[d])
    for d in range(n_dom):
        thr = lax.bitcast_convert_type(bits[d], F32)
        gt = doms[d] > thr
        eq = doms[d] == thr
        need = cap - jnp.sum(gt.astype(F32), axis=1, keepdims=True)
        eq_seen = jnp.zeros((N_EXPERTS, 1), F32)
        sel_seen = jnp.zeros((N_EXPERTS, 1), F32) + d * cap
        for blk in range(w // TILE):
            cols = slice(blk * TILE, (blk + 1) * TILE)
            eq_b = eq[:, cols].astype(F32)
            eq_rank = _dot(eq_b.astype(BF16), tri) + eq_seen
            sel = jnp.logical_or(gt[:, cols], jnp.logical_and(eq[:, cols], eq_rank < need))
            sel_f = sel.astype(F32)
            pos = _dot(sel_f.astype(BF16), tri) + sel_seen
            tile = d * (w // TILE) + blk
            pos_ref[0, tile] = jnp.where(sel, pos, -1.0)
            gate_ref[0, tile] = jnp.where(sel, doms[d][:, cols], 0.0)
            eq_seen = eq_seen + jnp.sum(eq_b, axis=1, keepdims=True)
            sel_seen = sel_seen + jnp.sum(sel_f, axis=1, keepdims=True)


def _route_kernel(lg_ref, pos_ref, gate_ref):
    lg = lg_ref[...]
    ex = jnp.exp(lg - jnp.max(lg, axis=0, keepdims=True))
    aff = ex / jnp.sum(ex, axis=0, keepdims=True)
    r = lax.broadcasted_iota(jnp.int32, (TILE, TILE), 0)
    c = lax.broadcasted_iota(jnp.int32, (TILE, TILE), 1)
    tri = (r < c).astype(BF16)

    @pl.when(pl.program_id(0) == 0)
    def _():
        _select_tokens(aff, GROUP // SEQ, EC_CAPACITY * SEQ // N_EXPERTS, tri, pos_ref, gate_ref)

    @pl.when(pl.program_id(0) > 0)
    def _():
        _select_tokens(aff, GROUP // DEC_SEQ, EC_CAPACITY * DEC_SEQ // N_EXPERTS, tri, pos_ref, gate_ref)


def _route(logits_t):
    blk = pl.BlockSpec((1, GROUP_TILES, N_EXPERTS, TILE), lambda g: (g, 0, 0, 0))
    shape = jax.ShapeDtypeStruct((N_GROUPS, GROUP_TILES, N_EXPERTS, TILE), F32)
    return pl.pallas_call(
        _route_kernel,
        grid=(N_GROUPS,),
        in_specs=[pl.BlockSpec((N_EXPERTS, GROUP), lambda g: (0, g))],
        out_specs=[blk, blk],
        out_shape=[shape, shape],
        compiler_params=_cparams(("parallel",)),
        name="route",
    )(logits_t)


def _moe_kernel(nblk_ref, ctab_ref, etab_ref, llen_ref,
                x_ref, pos_ref, gate_ref, wg_ref, wu_ref, wd_ref, xres_ref, mod_ref, fn_ref, *rest, final):
    n_out = 2 if final else 1
    out_refs = rest[:n_out]
    lists_ref, gl_ref, s_ref, gs_ref, acc_ref, wgb_ref, wub_ref, wdb_ref = rest[n_out:]
    g = pl.program_id(0)
    s = pl.program_id(1)
    chunks = STACK_BLK // CHUNK_ROWS
    rows16 = lax.broadcasted_iota(jnp.int32, (CHUNK_ROWS, TILE), 0)

    @pl.when(jnp.logical_and(g == 0, s == 0))
    def _():
        lists_ref[ZERO_ROW:ZERO_ROW + CHUNK_ROWS, :] = jnp.zeros((CHUNK_ROWS, D_MODEL), BF16)

    def chunk_rows(tbase, q, unused_row):
        d = ctab_ref[tbase + q]
        return pl.multiple_of(jnp.where(d >= 0, d, unused_row), CHUNK_ROWS)

    def build_block(j, rb, with_gate):
        tbase = ((g * GROUP_TILES + j) * MAX_STACK_BLKS + rb) * chunks
        for q in range(chunks):
            d = ctab_ref[tbase + q]
            e = etab_ref[tbase + q]
            posrow = pos_ref[0, j, pl.ds(e, 1), :]
            rowid = (rows16 + (rb * STACK_BLK + q * CHUNK_ROWS)).astype(F32)
            hit = jnp.logical_and(posrow == rowid, d >= 0)
            s_ref[q * CHUNK_ROWS:(q + 1) * CHUNK_ROWS, :] = hit.astype(BF16)
            if with_gate:
                gaterow = gate_ref[0, j, pl.ds(e, 1), :]
                gs_ref[q * CHUNK_ROWS:(q + 1) * CHUNK_ROWS, :] = jnp.where(hit, gaterow, 0.0)
        return tbase

    @pl.when(s < MOE_TILE_STEPS)
    def _gather():
        def tile(t, _):
            j = s * MOE_STEP_TILES + t
            tok = pl.multiple_of(t * TILE, TILE)

            def blk(rb, _):
                tbase = build_block(j, rb, True)
                picked = _dot(s_ref[...], x_ref[pl.ds(tok, TILE), :]).astype(BF16)
                gcol = jnp.sum(gs_ref[...], axis=1, keepdims=True)
                for q in range(chunks):
                    d = chunk_rows(tbase, q, DUMP_ROW)
                    rows = slice(q * CHUNK_ROWS, (q + 1) * CHUNK_ROWS)
                    lists_ref[pl.ds(d, CHUNK_ROWS), :] = picked[rows, :]
                    gl_ref[pl.ds(d, CHUNK_ROWS), :] = gcol[rows, :]
                return 0

            lax.fori_loop(0, nblk_ref[g * GROUP_TILES + j], blk, 0)
            return 0

        lax.fori_loop(0, MOE_STEP_TILES, tile, 0)

    @pl.when(jnp.logical_and(s >= MOE_TILE_STEPS, s < MOE_TILE_STEPS + N_EXPERTS))
    def _experts():
        e = s - MOE_TILE_STEPS
        ln = llen_ref[g * N_EXPERTS + e]
        base = e * LIST_MAX
        half = FFN_BLK // 2
        rem = ln % FFN_BLK
        use_half = jnp.logical_and(rem > 0, rem <= half)
        nfull = ln // FFN_BLK + jnp.where(rem > half, 1, 0)
        end = nfull * FFN_BLK + jnp.where(use_half, half, 0)

        def zero_tail(k, _):
            r0 = pl.multiple_of(base + ln + k * CHUNK_ROWS, CHUNK_ROWS)
            lists_ref[pl.ds(r0, CHUNK_ROWS), :] = jnp.zeros((CHUNK_ROWS, D_MODEL), BF16)
            gl_ref[pl.ds(r0, CHUNK_ROWS), :] = jnp.zeros((CHUNK_ROWS, 1), F32)
            return 0

        lax.fori_loop(0, (end - ln) // CHUNK_ROWS, zero_tail, 0)
        wgb_ref[...] = wg_ref[0, 0].astype(BF16)
        wub_ref[...] = wu_ref[0, 0].astype(BF16)
        wdb_ref[...] = wd_ref[0, 0].astype(BF16)

        def ffn_rows(r0, n):
            xs = lists_ref[pl.ds(r0, n), :]
            hid = _silu(_dot(xs, wgb_ref[...])) * _dot(xs, wub_ref[...])
            y = _dot(hid.astype(BF16), wdb_ref[...]) * gl_ref[pl.ds(r0, n), :]
            lists_ref[pl.ds(r0, n), :] = y.astype(BF16)

        def ffn(c, _):
            ffn_rows(pl.multiple_of(base + c * FFN_BLK, FFN_BLK), FFN_BLK)
            return 0

        lax.fori_loop(0, nfull, ffn, 0)

        @pl.when(use_half)
        def _():
            ffn_rows(pl.multiple_of(base + nfull * FFN_BLK, half), half)

    @pl.when(s >= MOE_TILE_STEPS + N_EXPERTS)
    def _scatter():
        def tile(t, _):
            j = (s - MOE_TILE_STEPS - N_EXPERTS) * SCATTER_STEP_TILES + t
            rows = pl.ds(pl.multiple_of(t * TILE, TILE), TILE)
            acc_ref[...] = jnp.zeros_like(acc_ref)

            def blk(rb, _):
                tbase = build_block(j, rb, False)
                y = jnp.concatenate([lists_ref[pl.ds(chunk_rows(tbase, q, ZERO_ROW), CHUNK_ROWS), :]
                                     for q in range(chunks)], axis=0)
                acc_ref[...] += _dot_t(s_ref[...], y)
                return 0

            lax.fori_loop(0, nblk_ref[g * GROUP_TILES + j], blk, 0)
            x = xres_ref[rows, :] + mod_ref[0, 0, 5:6, :] * acc_ref[...]
            if final:
                y = x * lax.rsqrt(jnp.mean(x * x, axis=-1, keepdims=True) + EPS) * fn_ref[...]

                @pl.when(g == 0)
                def _():
                    out_refs[0][rows, :] = y

                @pl.when(g > 0)
                def _():
                    out_refs[1][rows, :] = y
            else:
                out_refs[0][rows, :] = x
            return 0

        lax.fori_loop(0, SCATTER_STEP_TILES, tile, 0)


def _moe(layer, final, tables, h2, pos_t, gate_t, wg, wu, wd, xres, mod, fn):
    gt = GROUP_TILES
    ts = MOE_TILE_STEPS
    sc = SCATTER_STEPS
    rows_a = MOE_STEP_TILES * TILE
    rows = SCATTER_STEP_TILES * TILE
    step_c = lambda s: jnp.clip(s - ts - N_EXPERTS, 0, sc - 1)
    tile_a = lambda g, s, *_: (g * ts + jnp.minimum(s, ts - 1), 0)
    tile_c = lambda g, s, *_: (g * sc + step_c(s), 0)
    expert = lambda g, s, *_: (layer, jnp.clip(s - ts, 0, N_EXPERTS - 1), 0, 0)
    group = lambda g, s, *_: (g, 0, 0, 0)
    if final:
        out_specs = [
            pl.BlockSpec((rows, D_MODEL), lambda g, s, *_: (jnp.where(g == 0, step_c(s), sc - 1), 0)),
            pl.BlockSpec((rows, D_MODEL), lambda g, s, *_: (jnp.where(g == 0, 0, (g - 1) * sc + step_c(s)), 0)),
        ]
        out_shape = [jax.ShapeDtypeStruct((NT_CTX * TILE, D_MODEL), F32),
                     jax.ShapeDtypeStruct((NT_LAT * TILE, D_MODEL), F32)]
    else:
        out_specs = pl.BlockSpec((rows, D_MODEL), tile_c)
        out_shape = jax.ShapeDtypeStruct((N_TOK, D_MODEL), F32)
    grid_spec = pltpu.PrefetchScalarGridSpec(
        num_scalar_prefetch=4,
        grid=(N_GROUPS, ts + N_EXPERTS + sc),
        in_specs=[
            pl.BlockSpec((rows_a, D_MODEL), tile_a),
            pl.BlockSpec((1, gt, N_EXPERTS, TILE), group),
            pl.BlockSpec((1, gt, N_EXPERTS, TILE), group),
            pl.BlockSpec((1, 1, D_MODEL, EXPERT_FF), expert),
            pl.BlockSpec((1, 1, D_MODEL, EXPERT_FF), expert),
            pl.BlockSpec((1, 1, EXPERT_FF, D_MODEL), expert),
            pl.BlockSpec((rows, D_MODEL), tile_c),
            pl.BlockSpec((1, 1, 6, D_MODEL), lambda g, s, *_: (layer, g, 0, 0)),
            pl.BlockSpec((1, D_MODEL), lambda g, s, *_: (0, 0)),
        ],
        out_specs=out_specs,
        scratch_shapes=[
            pltpu.VMEM((LIST_ROWS, D_MODEL), BF16),
            pltpu.VMEM((LIST_ROWS, 1), F32),
            pltpu.VMEM((STACK_BLK, TILE), BF16),
            pltpu.VMEM((STACK_BLK, TILE), F32),
            pltpu.VMEM((TILE, D_MODEL), F32),
            pltpu.VMEM((D_MODEL, EXPERT_FF), BF16),
            pltpu.VMEM((D_MODEL, EXPERT_FF), BF16),
            pltpu.VMEM((EXPERT_FF, D_MODEL), BF16),
        ],
    )
    return pl.pallas_call(
        functools.partial(_moe_kernel, final=final),
        grid_spec=grid_spec,
        out_shape=out_shape,
        compiler_params=_cparams(("arbitrary", "arbitrary")),
        name="moe",
    )(*tables, h2, pos_t, gate_t, wg, wu, wd, xres, mod, fn)


def _rope_tables(rot_dim):
    axis_dim = rot_dim // 2
    tok = jnp.arange(DEC_SEQ)
    rows = (tok // GRID_W).astype(F32)
    cols = (tok % GRID_W).astype(F32)
    inv_freq = ROPE_THETA ** (-jnp.arange(0, axis_dim, 2, dtype=F32) / axis_dim)
    ar = rows[:, None] * inv_freq
    ac = cols[:, None] * inv_freq
    cos = jnp.concatenate([jnp.cos(ar), jnp.cos(ar), jnp.cos(ac), jnp.cos(ac)], axis=1)
    sin = jnp.concatenate([-jnp.sin(ar), jnp.sin(ar), -jnp.sin(ac), jnp.sin(ac)], axis=1)
    rep = LANES // rot_dim
    cos = jnp.tile(cos, (1, rep)).reshape(NT_SEQ, TILE, LANES)
    sin = jnp.tile(sin, (1, rep)).reshape(NT_SEQ, TILE, LANES)
    cos = jnp.concatenate([jnp.ones((1, TILE, LANES), F32), cos], axis=0)
    sin = jnp.concatenate([jnp.zeros((1, TILE, LANES), F32), sin], axis=0)
    return cos, sin


def _excl_cumsum(a, axis):
    return jnp.cumsum(a, axis=axis) - a


def _route_and_moe(layer, final, h2, logits, wg, wu, wd, xres, mod, fn):
    pos_t, gate_t = _route(logits)
    sel = pos_t >= 0
    cnt = jnp.sum(sel, axis=-1, dtype=jnp.int32)
    npad = (cnt + CHUNK_ROWS - 1) // CHUNK_ROWS * CHUNK_ROWS
    seg_off = _excl_cumsum(npad, 2)
    list_off = _excl_cumsum(npad, 1)
    rank0 = _excl_cumsum(cnt, 1)
    stack_pos = jnp.where(sel, pos_t - rank0[..., None].astype(F32) + seg_off[..., None].astype(F32), -1.0)
    nblk = (jnp.sum(npad, axis=2) + STACK_BLK - 1) // STACK_BLK
    llen = jnp.sum(npad, axis=1)
    r = (jnp.arange(MAX_STACK_BLKS * STACK_BLK // CHUNK_ROWS, dtype=jnp.int32) * CHUNK_ROWS)[None, None, None, :]
    in_seg = jnp.logical_and(r >= seg_off[..., None], r < (seg_off + npad)[..., None])
    eidx = jnp.arange(N_EXPERTS, dtype=jnp.int32)[None, None, :, None]
    dst = jnp.sum(jnp.where(in_seg, eidx * LIST_MAX + list_off[..., None] + r - seg_off[..., None], 0), axis=2)
    used = jnp.any(in_seg, axis=2)
    ctab = jnp.where(used, dst, -1).astype(jnp.int32)
    etab = jnp.sum(jnp.where(in_seg, eidx, 0), axis=2).astype(jnp.int32)
    tables = (nblk.reshape(-1).astype(jnp.int32), ctab.reshape(-1), etab.reshape(-1),
              llen.reshape(-1).astype(jnp.int32))
    return _moe(layer, final, tables, h2, stack_pos, gate_t, wg, wu, wd, xres, mod, fn)


def _split_router(w):
    wp = jnp.zeros((D_MODEL, LANES), F32).at[:, :N_EXPERTS].set(w)
    hi = wp.astype(BF16)
    lo = (wp - hi.astype(F32)).astype(BF16)
    return hi, lo


def kernel(x_prompt, x_sample, c, cache_attn_k, cache_attn_v, cache_mla_ckv, cache_mla_krope, c_ctx, w_mod, b_mod,
           norm1, norm2, ev_w_in, ev_q_norm, ev_k_norm, ev_w_s, ev_b_s, ev_w_out, od_w_in, od_q_a_norm,
           od_kv_a_norm, od_w_uq, od_w_ukv, od_conv_w, od_w_out, moe_router, moe_w_gate, moe_w_up, moe_w_down,
           final_norm):
    nctx = NT_CTX * TILE
    xc = x_prompt.reshape(nctx, D_MODEL)
    xl = x_sample.reshape(NT_LAT * TILE, D_MODEL)
    cvec = jnp.concatenate([c_ctx[None], c, jnp.zeros((3, D_MODEL), F32)], axis=0)
    mod = _modulation(cvec, w_mod, b_mod).reshape(2, 8, 6, D_MODEL)

    cos64, sin64 = _rope_tables(HEAD_DIM)
    seg = jnp.arange(512) // HEAD_DIM
    bd = (seg[:, None] == seg[None, :]).astype(BF16)
    bs_tab = jnp.repeat(ev_b_s[0].T, 64, axis=1)
    gated, qt, k, kb, v, vt = _even_in(
        xc, xl, mod, norm1[0:1], ev_w_in[0].astype(BF16), jnp.tile(ev_q_norm[0], 8)[None], jnp.tile(ev_k_norm[0], 2)[None],
        cos64, sin64, ev_w_s[0].astype(BF16), bs_tab, bd)
    new_k = k[:nctx].reshape(BATCH, SEQ, B_KV_HEADS, HEAD_DIM).transpose(0, 2, 1, 3)[:, None]
    new_v = v[:nctx].reshape(BATCH, SEQ, B_KV_HEADS, HEAD_DIM).transpose(0, 2, 1, 3)[:, None]
    ck = cache_attn_k[:, 0].transpose(0, 2, 1, 3).reshape(DEC_BATCH, PAST_LEN, LANES).astype(BF16)
    k_lat = jnp.concatenate([ck, kb[nctx:].reshape(DEC_BATCH, DEC_SEQ, LANES)], axis=1)
    cv = cache_attn_v[:, 0].transpose(0, 1, 3, 2).reshape(DEC_BATCH, LANES, PAST_LEN).astype(BF16)
    vt_lat = jnp.concatenate([cv, vt[:, nctx:].reshape(LANES, DEC_BATCH, DEC_SEQ).transpose(1, 0, 2)], axis=2)
    attn_t = _gqa_attention(qt, kb, vt, k_lat, vt_lat)
    wrh, wrl = _split_router(moe_router[0])
    x1, h2, logits = _even_out(0, gated, attn_t, ev_w_out[0].astype(BF16), xc, xl, mod, norm2[0:1], wrh, wrl)
    x2 = _route_and_moe(0, False, h2, logits, moe_w_gate, moe_w_up, moe_w_down, x1, mod, final_norm[None])

    cos32, sin32 = _rope_tables(C_ROPE)
    w_in1 = od_w_in[0]
    w_in1 = jnp.concatenate([w_in1[:, :C_Q_LORA + C_KV_LORA + C_ROPE], jnp.zeros((D_MODEL, LANES - C_ROPE), F32),
                             w_in1[:, C_Q_LORA + C_KV_LORA + C_ROPE:]], axis=1).astype(BF16)
    wuq = od_w_uq[0].reshape(C_Q_LORA, C_HEADS, C_NOPE + C_ROPE)
    wuq = jnp.concatenate([wuq[:, :, :C_NOPE].reshape(C_Q_LORA, -1), wuq[:, :, C_NOPE:].reshape(C_Q_LORA, -1)],
                          axis=1).astype(BF16)
    qnt, qrt, ckv, kr, krb, gb, z = _odd_in(x2, mod, norm1[1:2], w_in1, od_q_a_norm[0][None],
                                            od_kv_a_norm[0][None], wuq, cos32, sin32)
    new_ckv = ckv[:nctx].reshape(BATCH, 1, SEQ, C_KV_LORA)
    new_kr = kr[:nctx, :C_ROPE].reshape(BATCH, 1, SEQ, C_ROPE)
    wukv = od_w_ukv[0].reshape(C_KV_LORA, C_HEADS, C_NOPE + C_V)
    wukv = jnp.concatenate([wukv[:, :, :C_NOPE].reshape(C_KV_LORA, -1), wukv[:, :, C_NOPE:].reshape(C_KV_LORA, -1)],
                           axis=1).astype(BF16)
    ckv_lat = jnp.concatenate([cache_mla_ckv[:, 0], ckv[nctx:].reshape(DEC_BATCH, DEC_SEQ, C_KV_LORA)], axis=1)
    kn_c, vt_c = _kvup(ckv[:nctx], wukv)
    kn_l, vt_l = _kvup(ckv_lat.reshape(-1, C_KV_LORA), wukv)
    ckr = jnp.pad(cache_mla_krope[:, 0], ((0, 0), (0, 0), (0, LANES - C_ROPE))).astype(BF16)
    kr_l = jnp.concatenate([ckr, krb[nctx:].reshape(DEC_BATCH, DEC_SEQ, LANES)], axis=1)
    attn_t = _mla_attention(qnt, qrt, kn_c, krb, vt_c, kn_l, kr_l, vt_l)
    wrh, wrl = _split_router(moe_router[1])
    cw = jnp.concatenate([od_conv_w[0], jnp.zeros((5, D_WIDTH), F32)], axis=0)
    x3, h2, logits = _odd_out(1, attn_t, z, gb, cw, od_w_out[0].astype(BF16), x2, mod, norm2[1:2], wrh, wrl)
    y_c, y_l = _route_and_moe(1, True, h2, logits, moe_w_gate, moe_w_up, moe_w_down, x3, mod, final_norm[None])
    y_prompt = y_c.reshape(BATCH, SEQ, D_MODEL)
    y_sample = y_l.reshape(DEC_BATCH, DEC_SEQ, D_MODEL)
    return y_prompt, y_sample, new_k, new_v, new_ckv, new_kr
```

```python
import functools
import math

import jax
import jax.numpy as jnp
from jax import lax
from jax.experimental import pallas as pl
from jax.experimental.pallas import tpu as pltpu

F32 = jnp.float32
BF16 = jnp.bfloat16

D_MODEL = 1024
BATCH = 16
SEQ = 256
DEC_BATCH = 4
DEC_SEQ = 4096
PAST_LEN = 512
GRID_W = 64
ROPE_THETA = 10000.0
EPS = 1e-6
HEAD_DIM = 64
CHUNK = 128
A_GROUPS = 8
A_WIDTH = 512
B_HEADS = 8
B_KV_HEADS = 2
B_WIDTH = 512
C_HEADS = 8
C_NOPE = 64
C_ROPE = 32
C_V = 64
C_Q_LORA = 256
C_KV_LORA = 128
D_WIDTH = 512
N_EXPERTS = 16
EXPERT_FF = 512
EC_CAPACITY = 2

LOG2E = math.log2(math.e)
LANES = 128
BF16_ROWS = 16
ATT_TK = 256
MLA_PAIRS = 2
GQA_STREAMS = 2
TILE = 256
NT_CTX = BATCH * SEQ // TILE
NT_SEQ = DEC_SEQ // TILE
NT_LAT = DEC_BATCH * NT_SEQ
NT = NT_CTX + NT_LAT
N_TOK = NT * TILE
GROUP = 4096
N_GROUPS = N_TOK // GROUP
GROUP_TILES = GROUP // TILE
GROUP_CAP = EC_CAPACITY * GROUP // N_EXPERTS
CHUNK_ROWS = BF16_ROWS
STACK_BLK = 256
MAX_STACK_BLKS = N_EXPERTS * TILE // STACK_BLK
MOE_STEP_TILES = 2
MOE_TILE_STEPS = GROUP_TILES // MOE_STEP_TILES
SCATTER_STEP_TILES = 1
SCATTER_STEPS = GROUP_TILES // SCATTER_STEP_TILES
FFN_BLK = 256
LIST_MAX = -(-(GROUP_CAP + GROUP_TILES * (CHUNK_ROWS - 1)) // FFN_BLK) * FFN_BLK
DUMP_ROW = N_EXPERTS * LIST_MAX
ZERO_ROW = DUMP_ROW + CHUNK_ROWS
LIST_ROWS = ZERO_ROW + CHUNK_ROWS
VMEM_LIMIT = 58 * 1024 * 1024


def _cparams(sem):
    return pltpu.CompilerParams(dimension_semantics=sem, vmem_limit_bytes=VMEM_LIMIT)


def _dot(a, b):
    return jnp.dot(a, b, preferred_element_type=F32)


def _silu(x):
    return x / (1.0 + jnp.exp(-x))


def _mod_row(i):
    return jnp.where(i < NT_CTX, 0, 1 + (i - NT_CTX) // NT_SEQ)


def _tab_blk(i):
    return jnp.where(i < NT_CTX, 0, 1 + (i - NT_CTX) % NT_SEQ)


def _mod_kernel(c_ref, w_ref, b_ref, o_ref):
    s = _silu(c_ref[...])
    o_ref[0] = _dot(s.astype(BF16), w_ref[0].astype(BF16)) + b_ref[0]


def _modulation(cvec, w_mod, b_mod):
    depth = w_mod.shape[0]
    nchunk = 6
    return pl.pallas_call(
        _mod_kernel,
        grid=(depth, nchunk),
        in_specs=[
            pl.BlockSpec((8, D_MODEL), lambda l, k: (0, 0)),
            pl.BlockSpec((1, D_MODEL, D_MODEL), lambda l, k: (l, 0, k)),
            pl.BlockSpec((1, 1, D_MODEL), lambda l, k: (l, 0, k)),
        ],
        out_specs=pl.BlockSpec((1, 8, D_MODEL), lambda l, k: (l, 0, k)),
        out_shape=jax.ShapeDtypeStruct((depth, 8, 6 * D_MODEL), F32),
        compiler_params=_cparams(("arbitrary", "arbitrary")),
        name="modulation",
    )(cvec, w_mod, b_mod.reshape(depth, 1, 6 * D_MODEL))


def _norm_mod(x, gain, scale, shift):
    ms = jnp.mean(x * x, axis=-1, keepdims=True)
    return (x * lax.rsqrt(ms + EPS) * gain) * (1.0 + scale) + shift


def _seg_mean_sq(z, bd, width):
    zz = z * z
    hi = zz.astype(BF16)
    lo = (zz - hi.astype(F32)).astype(BF16)
    return (_dot(hi, bd) + _dot(lo, bd)) * (1.0 / width)


def _rope(z, cos, sin_signed, half):
    w = z.shape[1]
    lane = lax.broadcasted_iota(jnp.int32, z.shape, 1)
    first = (lane % (2 * half)) < half
    partner = jnp.where(first, pltpu.roll(z, w - half, 1), pltpu.roll(z, half, 1))
    return z * cos + partner * sin_signed


def _tile_lanes(t, n):
    return jnp.concatenate([t] * n, axis=1) if n > 1 else t


def _even_in_kernel(xc_ref, xl_ref, mod_ref, n1_ref, w_ref, qg_ref, kg_ref, cos_ref, sin_ref, ws_ref, bs_ref, bd_ref,
                    gated_ref, qt_ref, k_ref, kb_ref, v_ref, vt_ref):
    x = _group_tile(pl.program_id(0), xc_ref, xl_ref)
    h = _norm_mod(x, n1_ref[...], mod_ref[0, 0, 1:2, :], mod_ref[0, 0, 0:1, :])
    p = _dot(h.astype(BF16), w_ref[...])
    u = p[:, 0:512]
    va = p[:, 512:1024].astype(BF16)
    q = p[:, 1024:1536]
    k = p[:, 1536:1664]
    v = p[:, 1664:1792]
    bd = bd_ref[...]
    cos = cos_ref[0]
    sin = sin_ref[0]
    qn = q * lax.rsqrt(_seg_mean_sq(q, bd, HEAD_DIM) + EPS) * qg_ref[...]
    kn = k * lax.rsqrt(_seg_mean_sq(k, bd[0:LANES, 0:LANES], HEAD_DIM) + EPS) * kg_ref[...]
    qr = _rope(qn, _tile_lanes(cos, 4), _tile_lanes(sin, 4), HEAD_DIM // 4)
    kr = _rope(kn, cos, sin, HEAD_DIM // 4)
    qt_ref[...] = (qr * (HEAD_DIM ** -0.5 * LOG2E)).T.astype(BF16)
    k_ref[...] = kr
    kb_ref[...] = kr.astype(BF16)
    v_ref[...] = v
    vt_ref[...] = v.T.astype(BF16)
    lane = lax.broadcasted_iota(jnp.int32, (CHUNK, LANES), 1)
    for ch in range(TILE // CHUNK):
        rows = slice(ch * CHUNK, (ch + 1) * CHUNK)
        cols = []
        for pair in range(A_GROUPS // 2):
            vp = va[rows, pair * LANES:(pair + 1) * LANES]
            r0 = _dot(ws_ref[2 * pair], vp)
            r1 = _dot(ws_ref[2 * pair + 1], vp)
            cols.append(jnp.where(lane < LANES // 2, r0, r1))
        s = jnp.concatenate(cols, axis=1) + bs_ref[...]
        gated_ref[rows, :] = (u[rows, :] * s).astype(BF16)


def _even_in(xc, xl, mod, n1, w_in, qg, kg, cos, sin, ws, bs_tab, bd):
    full = lambda shape: pl.BlockSpec(shape, lambda i: (0,) * len(shape))
    return pl.pallas_call(
        _even_in_kernel,
        grid=(NT,),
        in_specs=[
            pl.BlockSpec((TILE, D_MODEL), _CTX_ROW),
            pl.BlockSpec((TILE, D_MODEL), _LAT_ROW),
            pl.BlockSpec((1, 1, 6, D_MODEL), lambda i: (0, _mod_row(i), 0, 0)),
            full((1, D_MODEL)),
            full(w_in.shape),
            full((1, 512)),
            full((1, LANES)),
            pl.BlockSpec((1, TILE, LANES), lambda i: (_tab_blk(i), 0, 0)),
            pl.BlockSpec((1, TILE, LANES), lambda i: (_tab_blk(i), 0, 0)),
            full(ws.shape),
            full(bs_tab.shape),
            full(bd.shape),
        ],
        out_specs=[
            pl.BlockSpec((TILE, 512), lambda i: (i, 0)),
            pl.BlockSpec((512, TILE), lambda i: (0, i)),
            pl.BlockSpec((TILE, LANES), lambda i: (i, 0)),
            pl.BlockSpec((TILE, LANES), lambda i: (i, 0)),
            pl.BlockSpec((TILE, LANES), lambda i: (i, 0)),
            pl.BlockSpec((LANES, TILE), lambda i: (0, i)),
        ],
        out_shape=[
            jax.ShapeDtypeStruct((N_TOK, 512), BF16),
            jax.ShapeDtypeStruct((512, N_TOK), BF16),
            jax.ShapeDtypeStruct((N_TOK, LANES), F32),
            jax.ShapeDtypeStruct((N_TOK, LANES), BF16),
            jax.ShapeDtypeStruct((N_TOK, LANES), F32),
            jax.ShapeDtypeStruct((LANES, N_TOK), BF16),
        ],
        compiler_params=_cparams(("parallel",)),
        name="even_in",
    )(xc, xl, mod, n1, w_in, qg, kg, cos, sin, ws, bs_tab, bd)


def _odd_in_kernel(x_ref, mod_ref, n1_ref, w_ref, qa_ref, kva_ref, wuq_ref, wukv_ref, cos_ref, sin_ref,
                   qnt_ref, qrt_ref, ckv_ref, kr_ref, krb_ref, gb_ref, z_ref, kn_ref, vt_ref):
    h = _norm_mod(x_ref[...], n1_ref[...], mod_ref[0, 0, 1:2, :], mod_ref[0, 0, 0:1, :])
    p = _dot(h.astype(BF16), w_ref[...])
    cq = p[:, 0:256]
    ckv = p[:, 256:384]
    kr = p[:, 384:512]
    gb_ref[...] = p[:, 512:1024]
    z_ref[...] = p[:, 1024:1536] * p[:, 1536:2048]
    cqn = cq * lax.rsqrt(jnp.mean(cq * cq, axis=-1, keepdims=True) + EPS) * qa_ref[...]
    q = _dot(cqn.astype(BF16), wuq_ref[...])
    scale = (C_NOPE + C_ROPE) ** -0.5 * LOG2E
    cos = cos_ref[0]
    sin = sin_ref[0]
    qnt_ref[...] = (q[:, 0:512] * scale).T.astype(BF16)
    qr = _rope(q[:, 512:768], _tile_lanes(cos, 2), _tile_lanes(sin, 2), C_ROPE // 4)
    qrt_ref[...] = (qr * scale).T.astype(BF16)
    ckvn = ckv * lax.rsqrt(jnp.mean(ckv * ckv, axis=-1, keepdims=True) + EPS) * kva_ref[...]
    ckv_ref[...] = ckvn
    kv = _dot(ckvn.astype(BF16), wukv_ref[...])
    kn_ref[...] = kv[:, 0:512].astype(BF16)
    vt_ref[...] = kv[:, 512:1024].T.astype(BF16)
    krr = _rope(kr, cos, sin, C_ROPE // 4)
    kr_ref[...] = krr
    krb_ref[...] = krr.astype(BF16)


def _odd_in(x, mod, n1, w_in, qa, kva, wuq, wukv, cos, sin):
    full = lambda shape: pl.BlockSpec(shape, lambda i: (0,) * len(shape))
    tile = lambda w: pl.BlockSpec((TILE, w), lambda i: (i, 0))
    return pl.pallas_call(
        _odd_in_kernel,
        grid=(NT,),
        in_specs=[
            tile(D_MODEL),
            pl.BlockSpec((1, 1, 6, D_MODEL), lambda i: (1, _mod_row(i), 0, 0)),
            full((1, D_MODEL)),
            full(w_in.shape),
            full((1, C_Q_LORA)),
            full((1, C_KV_LORA)),
            full(wuq.shape),
            full(wukv.shape),
            pl.BlockSpec((1, TILE, LANES), lambda i: (_tab_blk(i), 0, 0)),
            pl.BlockSpec((1, TILE, LANES), lambda i: (_tab_blk(i), 0, 0)),
        ],
        out_specs=[pl.BlockSpec((512, TILE), lambda i: (0, i)),
                   pl.BlockSpec((256, TILE), lambda i: (0, i)),
                   tile(LANES), tile(LANES), tile(LANES), tile(512), tile(512),
                   tile(512), pl.BlockSpec((512, TILE), lambda i: (0, i))],
        out_shape=[
            jax.ShapeDtypeStruct((512, N_TOK), BF16),
            jax.ShapeDtypeStruct((256, N_TOK), BF16),
            jax.ShapeDtypeStruct((N_TOK, LANES), F32),
            jax.ShapeDtypeStruct((N_TOK, LANES), F32),
            jax.ShapeDtypeStruct((N_TOK, LANES), BF16),
            jax.ShapeDtypeStruct((N_TOK, 512), F32),
            jax.ShapeDtypeStruct((N_TOK, 512), F32),
            jax.ShapeDtypeStruct((N_TOK, 512), BF16),
            jax.ShapeDtypeStruct((512, N_TOK), BF16),
        ],
        compiler_params=_cparams(("parallel",)),
        name="odd_in",
    )(x, mod, n1, w_in, qa, kva, wuq, wukv, cos, sin)


def _kvup_kernel(c_ref, w_ref, kn_ref, vt_ref):
    p = _dot(c_ref[...].astype(BF16), w_ref[...])
    kn_ref[...] = p[:, 0:512].astype(BF16)
    vt_ref[...] = p[:, 512:1024].T.astype(BF16)


def _kvup(ckv_all, w_ukv):
    n = ckv_all.shape[0]
    return pl.pallas_call(
        _kvup_kernel,
        grid=(n // TILE,),
        in_specs=[pl.BlockSpec((TILE, C_KV_LORA), lambda i: (i, 0)),
                  pl.BlockSpec(w_ukv.shape, lambda i: (0, 0))],
        out_specs=[pl.BlockSpec((TILE, 512), lambda i: (i, 0)), pl.BlockSpec((512, TILE), lambda i: (0, i))],
        out_shape=[jax.ShapeDtypeStruct((n, 512), BF16), jax.ShapeDtypeStruct((512, n), BF16)],
        compiler_params=_cparams(("parallel",)),
        name="mla_kv_up",
    )(ckv_all, w_ukv)


def _softmax_pv(qc, key_chunk, vt_chunks, nk):
    return _softmax_pv_streams([(qc, key_chunk, vt_chunks)], nk)[0]


def _softmax_pv_streams(streams, nk):
    ones = jnp.ones((BF16_ROWS, ATT_TK), BF16)
    ms = [jnp.full((1, qc.shape[1]), -jnp.inf, F32) for qc, _, _ in streams]
    accs = [jnp.zeros((HEAD_DIM + BF16_ROWS, qc.shape[1]), F32) for qc, _, _ in streams]
    nxt = [_dot(key_chunk(0), qc) for qc, key_chunk, _ in streams]
    for c in range(nk):
        for i, (qc, key_chunk, vt_chunks) in enumerate(streams):
            s = nxt[i]
            if c + 1 < nk:
                nxt[i] = _dot(key_chunk(c + 1), qc)
            mn = jnp.maximum(ms[i], jnp.max(s, axis=0, keepdims=True))
            p = jnp.exp2(s - mn).astype(BF16)
            vts = vt_chunks(c)
            if len(vts) == 1:
                pv = _dot(jnp.concatenate([vts[0], ones], axis=0), p)
            else:
                tq = qc.shape[1] // len(vts)
                pv = jnp.concatenate([_dot(jnp.concatenate([vt, ones], axis=0), p[:, k * tq:(k + 1) * tq])
                                      for k, vt in enumerate(vts)], axis=1)
            accs[i] = jnp.exp2(ms[i] - mn) * accs[i] + pv
            ms[i] = mn
    return [acc[0:HEAD_DIM, :] / acc[HEAD_DIM:HEAD_DIM + 1, :] for acc in accs]


def _chunk_locator(segments):
    counts = [seg[0].shape[0] // ATT_TK for seg in segments]

    def locate(c):
        for seg, n in zip(segments, counts):
            if c < n:
                return seg, slice(c * ATT_TK, (c + 1) * ATT_TK)
            c -= n
        raise IndexError(c)

    return locate, sum(counts)


def _gqa_kernel(q_ref, *refs):
    o_ref = refs[-1]
    locate, nk = _chunk_locator([refs[i:i + 2] for i in range(0, len(refs) - 1, 2)])
    kvh = pl.program_id(1)
    tq = q_ref.shape[1]
    n = B_HEADS // B_KV_HEADS
    cols = []
    for hh in range(n):
        q = q_ref[hh * HEAD_DIM:(hh + 1) * HEAD_DIM, :]
        z = jnp.zeros_like(q)
        cols.append(jnp.where(kvh == 0, jnp.concatenate([q, z], axis=0), jnp.concatenate([z, q], axis=0)))
    def key_chunk(c):
        (k_ref, _), rows = locate(c)
        return k_ref[rows, :]

    def vt_chunks(c):
        (_, vt_ref), cols_c = locate(c)
        return [vt_ref[:, cols_c]]

    per = n // GQA_STREAMS
    outs = _softmax_pv_streams(
        [(jnp.concatenate(cols[i * per:(i + 1) * per], axis=1), key_chunk, vt_chunks) for i in range(GQA_STREAMS)], nk)
    for hh in range(n):
        o = outs[hh // per][:, (hh % per) * tq:(hh % per + 1) * tq]
        o_ref[hh * HEAD_DIM:(hh + 1) * HEAD_DIM, :] = o.astype(BF16)


def _mla_kernel(qn_ref, qr_ref, *refs):
    o_ref = refs[-1]
    locate, nk = _chunk_locator([refs[i:i + 3] for i in range(0, len(refs) - 1, 3)])
    tq = qn_ref.shape[1]
    z = jnp.zeros((C_NOPE, tq), BF16)
    zr = jnp.zeros((LANES - C_ROPE, tq), BF16)
    streams = []
    for pair in range(MLA_PAIRS):
        cols = []
        for hh in range(2):
            h = 2 * pair + hh
            qn = qn_ref[h * C_NOPE:(h + 1) * C_NOPE, :]
            qr = qr_ref[h * C_ROPE:(h + 1) * C_ROPE, :]
            cols.append(jnp.concatenate(([qn, z] if hh == 0 else [z, qn]) + [qr, zr], axis=0))

        def key_chunk(c, pair=pair):
            (kn_ref, kr_ref, _), rows = locate(c)
            return jnp.concatenate([kn_ref[rows, pair * LANES:(pair + 1) * LANES], kr_ref[rows, :]], axis=1)

        def vt_chunks(c, pair=pair):
            (_, _, vt_ref), cols_c = locate(c)
            return [vt_ref[(2 * pair + hh) * C_V:(2 * pair + hh + 1) * C_V, cols_c] for hh in range(2)]

        streams.append((jnp.concatenate(cols, axis=1), key_chunk, vt_chunks))
    outs = _softmax_pv_streams(streams, nk)
    for pair in range(MLA_PAIRS):
        for hh in range(2):
            h = 2 * pair + hh
            o_ref[h * C_V:(h + 1) * C_V, :] = outs[pair][:, hh * tq:(hh + 1) * tq].astype(BF16)


def _attn_call(body, grid, in_specs, out_spec, args, n_tiles):
    return pl.pallas_call(
        body,
        grid=grid,
        in_specs=in_specs,
        out_specs=out_spec,
        out_shape=jax.ShapeDtypeStruct((512, n_tiles * TILE), BF16),
        compiler_params=_cparams(("parallel",) * len(grid)),
        name="attention",
    )(*args)


def _gqa_attention(qt, kb, vt, k_cache, vt_cache):
    assert NT_CTX * TILE == DEC_SEQ
    rows = (B_HEADS // B_KV_HEADS) * HEAD_DIM
    o_ctx = _attn_call(
        _gqa_kernel, (BATCH, B_KV_HEADS),
        [pl.BlockSpec((rows, TILE), lambda b, h: (h, b)),
         pl.BlockSpec((None, TILE, LANES), lambda b, h: (b, 0, 0)),
         pl.BlockSpec((HEAD_DIM, TILE), lambda b, h: (h, b))],
        pl.BlockSpec((rows, TILE), lambda b, h: (h, b)),
        [qt, kb.reshape(NT, TILE, LANES), vt], NT_CTX)
    o_lat = _attn_call(
        _gqa_kernel, (DEC_BATCH, B_KV_HEADS, NT_SEQ),
        [pl.BlockSpec((rows, TILE), lambda b, h, j: (h, NT_CTX + b * NT_SEQ + j)),
         pl.BlockSpec((None, PAST_LEN, LANES), lambda b, h, j: (b, 0, 0)),
         pl.BlockSpec((None, HEAD_DIM, PAST_LEN), lambda b, h, j: (b, h, 0)),
         pl.BlockSpec((DEC_SEQ, LANES), lambda b, h, j: (1 + b, 0)),
         pl.BlockSpec((HEAD_DIM, DEC_SEQ), lambda b, h, j: (h, 1 + b))],
        pl.BlockSpec((rows, TILE), lambda b, h, j: (h, b * NT_SEQ + j)),
        [qt, k_cache, vt_cache, kb, vt], NT_LAT)
    return o_ctx, o_lat


def _mla_attention(qnt, qrt, kn, krb, vt, kn_cache, kr_cache, vt_cache):
    nh = 2 * MLA_PAIRS
    steps = C_HEADS // nh
    o_ctx = _attn_call(
        _mla_kernel, (BATCH, steps),
        [pl.BlockSpec((nh * C_NOPE, TILE), lambda b, p: (p, b)),
         pl.BlockSpec((nh * C_ROPE, TILE), lambda b, p: (p, b)),
         pl.BlockSpec((TILE, MLA_PAIRS * LANES), lambda b, p: (b, p)),
         pl.BlockSpec((TILE, LANES), lambda b, p: (b, 0)),
         pl.BlockSpec((nh * C_V, TILE), lambda b, p: (p, b))],
        pl.BlockSpec((nh * C_V, TILE), lambda b, p: (p, b)),
        [qnt, qrt, kn, krb, vt], NT_CTX)
    o_lat = _attn_call(
        _mla_kernel, (DEC_BATCH, steps, NT_SEQ),
        [pl.BlockSpec((nh * C_NOPE, TILE), lambda b, p, j: (p, NT_CTX + b * NT_SEQ + j)),
         pl.BlockSpec((nh * C_ROPE, TILE), lambda b, p, j: (p, NT_CTX + b * NT_SEQ + j)),
         pl.BlockSpec((PAST_LEN, MLA_PAIRS * LANES), lambda b, p, j: (b, p)),
         pl.BlockSpec((None, PAST_LEN, LANES), lambda b, p, j: (b, 0, 0)),
         pl.BlockSpec((nh * C_V, PAST_LEN), lambda b, p, j: (p, b)),
         pl.BlockSpec((DEC_SEQ, MLA_PAIRS * LANES), lambda b, p, j: (1 + b, p)),
         pl.BlockSpec((DEC_SEQ, LANES), lambda b, p, j: (1 + b, 0)),
         pl.BlockSpec((nh * C_V, DEC_SEQ), lambda b, p, j: (p, 1 + b))],
        pl.BlockSpec((nh * C_V, TILE), lambda b, p, j: (p, b * NT_SEQ + j)),
        [qnt, qrt, kn_cache, kr_cache, vt_cache, kn, krb, vt], NT_LAT)
    return o_ctx, o_lat


def _group_tile(i, ctx_ref, lat_ref):
    return jnp.where(i < NT_CTX, ctx_ref[...], lat_ref[...])


_CTX_COL = lambda i: (0, jnp.minimum(i, NT_CTX - 1))
_LAT_COL = lambda i: (0, jnp.maximum(i - NT_CTX, 0))
_CTX_ROW = lambda i: (jnp.minimum(i, NT_CTX - 1), 0)
_LAT_ROW = lambda i: (jnp.maximum(i - NT_CTX, 0), 0)


def _finish_out(o, x_in, mod_ref, n2_ref, wrh_ref, wrl_ref, xo_ref, h2_ref, lg_ref):
    x = x_in + mod_ref[0, 0, 2:3, :] * o
    xo_ref[...] = x
    h2 = _norm_mod(x, n2_ref[...], mod_ref[0, 0, 4:5, :], mod_ref[0, 0, 3:4, :])
    hi = h2.astype(BF16)
    lo = (h2 - hi.astype(F32)).astype(BF16)
    h2_ref[...] = hi
    lg = _dot(hi, wrh_ref[...]) + _dot(lo, wrh_ref[...]) + _dot(hi, wrl_ref[...])
    lg_ref[...] = lg.T[0:N_EXPERTS, :]


def _dot_nt(a, bt):
    return lax.dot_general(a, bt, (((1,), (1,)), ((), ())), preferred_element_type=F32)


def _dot_t(at, b):
    return lax.dot_general(at, b, (((0,), (0,)), ((), ())), preferred_element_type=F32)


def _even_out_kernel(a_ref, btc_ref, btl_ref, w_ref, xc_ref, xl_ref, mod_ref, n2_ref, wrh_ref, wrl_ref,
                     xo_ref, h2_ref, lg_ref):
    i = pl.program_id(0)
    bt = _group_tile(i, btc_ref, btl_ref)
    o = _dot(a_ref[...], w_ref[0:512, :]) + _dot_t(bt, w_ref[512:1024, :])
    _finish_out(o, _group_tile(i, xc_ref, xl_ref), mod_ref, n2_ref, wrh_ref, wrl_ref, xo_ref, h2_ref, lg_ref)


def _odd_out_kernel(atc_ref, atl_ref, z_ref, zp_ref, zn_ref, gb_ref, cw_ref, w_ref, x_ref, mod_ref, n2_ref,
                    wrh_ref, wrl_ref, xo_ref, h2_ref, lg_ref):
    i = pl.program_id(0)
    at = _group_tile(i, atc_ref, atl_ref)
    j = (i - NT_CTX) % NT_SEQ
    first = jnp.logical_or(i < NT_CTX, j == 0)
    last = jnp.logical_or(i < NT_CTX, j == NT_SEQ - 1)
    z = z_ref[...]
    row = lax.broadcasted_iota(jnp.int32, z.shape, 0)
    halo_p = jnp.where(first, 0.0, zp_ref[7:8, :])
    halo_n = jnp.where(last, 0.0, zn_ref[0:1, :])
    zprev = jnp.where(row == 0, halo_p, pltpu.roll(z, 1, 0))
    znext = jnp.where(row == TILE - 1, halo_n, pltpu.roll(z, TILE - 1, 0))
    y = zprev * cw_ref[0:1, :] + z * cw_ref[1:2, :] + znext * cw_ref[2:3, :]
    d = (gb_ref[...] * y).astype(BF16)
    o = _dot_t(at, w_ref[0:512, :]) + _dot(d, w_ref[512:1024, :])
    _finish_out(o, x_ref[...], mod_ref, n2_ref, wrh_ref, wrl_ref, xo_ref, h2_ref, lg_ref)


_OUT_SHAPES = [
    jax.ShapeDtypeStruct((N_TOK, D_MODEL), F32),
    jax.ShapeDtypeStruct((N_TOK, D_MODEL), BF16),
    jax.ShapeDtypeStruct((N_EXPERTS, N_TOK), F32),
]
_LOGIT_SPEC = pl.BlockSpec((N_EXPERTS, TILE), lambda i: (0, i))


def _even_out(layer, a, b, w_out, xc, xl, mod, n2, wrh, wrl):
    full = lambda shape: pl.BlockSpec(shape, lambda i: (0,) * len(shape))
    tile = lambda w: pl.BlockSpec((TILE, w), lambda i: (i, 0))
    return pl.pallas_call(
        _even_out_kernel,
        grid=(NT,),
        in_specs=[tile(512), pl.BlockSpec((512, TILE), _CTX_COL), pl.BlockSpec((512, TILE), _LAT_COL),
                  full(w_out.shape), pl.BlockSpec((TILE, D_MODEL), _CTX_ROW), pl.BlockSpec((TILE, D_MODEL), _LAT_ROW),
                  pl.BlockSpec((1, 1, 6, D_MODEL), lambda i: (layer, _mod_row(i), 0, 0)),
                  full((1, D_MODEL)), full(wrh.shape), full(wrl.shape)],
        out_specs=[tile(D_MODEL), tile(D_MODEL), _LOGIT_SPEC],
        out_shape=_OUT_SHAPES,
        compiler_params=_cparams(("parallel",)),
        name="even_out",
    )(a, b[0], b[1], w_out, xc, xl, mod, n2, wrh, wrl)


def _odd_out(layer, a, z, gb, cw, w_out, x, mod, n2, wrh, wrl):
    full = lambda shape: pl.BlockSpec(shape, lambda i: (0,) * len(shape))
    tile = lambda w: pl.BlockSpec((TILE, w), lambda i: (i, 0))
    rb = TILE // 8
    return pl.pallas_call(
        _odd_out_kernel,
        grid=(NT,),
        in_specs=[pl.BlockSpec((512, TILE), _CTX_COL), pl.BlockSpec((512, TILE), _LAT_COL), tile(512),
                  pl.BlockSpec((8, 512), lambda i: (jnp.maximum(i * rb - 1, 0), 0)),
                  pl.BlockSpec((8, 512), lambda i: (jnp.minimum(i * rb + rb, NT * rb - 1), 0)),
                  tile(512), full(cw.shape), full(w_out.shape), tile(D_MODEL),
                  pl.BlockSpec((1, 1, 6, D_MODEL), lambda i: (layer, _mod_row(i), 0, 0)),
                  full((1, D_MODEL)), full(wrh.shape), full(wrl.shape)],
        out_specs=[tile(D_MODEL), tile(D_MODEL), _LOGIT_SPEC],
        out_shape=_OUT_SHAPES,
        compiler_params=_cparams(("parallel",)),
        name="odd_out",
    )(a[0], a[1], z, z, z, gb, cw, w_out, x, mod, n2, wrh, wrl)


def _select_tokens(aff, n_dom, cap, tri, pos_ref, gate_ref):
    w = aff.shape[1] // n_dom
    doms = [aff[:, d * w:(d + 1) * w] for d in range(n_dom)]
    bits = [jnp.zeros((N_EXPERTS, 1), jnp.int32) for _ in range(n_dom)]
    for bit in range(30, -1, -1):
        for d in range(n_dom):
            cand = bits[d] | (1 << bit)
            cnt = jnp.sum((doms[d] >= lax.bitcast_convert_type(cand, F32)).astype(F32), axis=1, keepdims=True)
            bits[d] = jnp.where(cnt >= cap, cand, bits[d])
    for d in range(n_dom):
        thr = lax.bitcast_convert_type(bits[d], F32)
        gt = doms[d] > thr
        eq = doms[d] == thr
        need = cap - jnp.sum(gt.astype(F32), axis=1, keepdims=True)
        eq_seen = jnp.zeros((N_EXPERTS, 1), F32)
        sel_seen = jnp.zeros((N_EXPERTS, 1), F32) + d * cap
        for blk in range(w // TILE):
            cols = slice(blk * TILE, (blk + 1) * TILE)
            eq_b = eq[:, cols].astype(F32)
            eq_rank = _dot(eq_b.astype(BF16), tri) + eq_seen
            sel = jnp.logical_or(gt[:, cols], jnp.logical_and(eq[:, cols], eq_rank < need))
            sel_f = sel.astype(F32)
            pos = _dot(sel_f.astype(BF16), tri) + sel_seen
            tile = d * (w // TILE) + blk
            pos_ref[0, tile] = jnp.where(sel, pos, -1.0)
            gate_ref[0, tile] = jnp.where(sel, doms[d][:, cols], 0.0)
            eq_seen = eq_seen + jnp.sum(eq_b, axis=1, keepdims=True)
            sel_seen = sel_seen + jnp.sum(sel_f, axis=1, keepdims=True)


def _route_kernel(lg_ref, pos_ref, gate_ref):
    lg = lg_ref[...]
    ex = jnp.exp(lg - jnp.max(lg, axis=0, keepdims=True))
    aff = ex / jnp.sum(ex, axis=0, keepdims=True)
    r = lax.broadcasted_iota(jnp.int32, (TILE, TILE), 0)
    c = lax.broadcasted_iota(jnp.int32, (TILE, TILE), 1)
    tri = (r < c).astype(BF16)

    @pl.when(pl.program_id(0) == 0)
    def _():
        _select_tokens(aff, GROUP // SEQ, EC_CAPACITY * SEQ // N_EXPERTS, tri, pos_ref, gate_ref)

    @pl.when(pl.program_id(0) > 0)
    def _():
        _select_tokens(aff, GROUP // DEC_SEQ, EC_CAPACITY * DEC_SEQ // N_EXPERTS, tri, pos_ref, gate_ref)


def _route(logits_t):
    blk = pl.BlockSpec((1, GROUP_TILES, N_EXPERTS, TILE), lambda g: (g, 0, 0, 0))
    shape = jax.ShapeDtypeStruct((N_GROUPS, GROUP_TILES, N_EXPERTS, TILE), F32)
    return pl.pallas_call(
        _route_kernel,
        grid=(N_GROUPS,),
        in_specs=[pl.BlockSpec((N_EXPERTS, GROUP), lambda g: (0, g))],
        out_specs=[blk, blk],
        out_shape=[shape, shape],
        compiler_params=_cparams(("parallel",)),
        name="route",
    )(logits_t)


def _moe_kernel(nblk_ref, ctab_ref, etab_ref, llen_ref,
                x_ref, pos_ref, gate_ref, wg_ref, wu_ref, wd_ref, xres_ref, mod_ref, fn_ref, *rest, final):
    n_out = 2 if final else 1
    out_refs = rest[:n_out]
    lists_ref, gl_ref, s_ref, gs_ref, acc_ref, wgb_ref, wub_ref, wdb_ref = rest[n_out:]
    g = pl.program_id(0)
    s = pl.program_id(1)
    chunks = STACK_BLK // CHUNK_ROWS
    rows16 = lax.broadcasted_iota(jnp.int32, (CHUNK_ROWS, TILE), 0)

    @pl.when(jnp.logical_and(g == 0, s == 0))
    def _():
        lists_ref[ZERO_ROW:ZERO_ROW + CHUNK_ROWS, :] = jnp.zeros((CHUNK_ROWS, D_MODEL), BF16)

    def chunk_rows(tbase, q, unused_row):
        d = ctab_ref[tbase + q]
        return pl.multiple_of(jnp.where(d >= 0, d, unused_row), CHUNK_ROWS)

    def build_block(j, rb, with_gate):
        tbase = ((g * GROUP_TILES + j) * MAX_STACK_BLKS + rb) * chunks
        for q in range(chunks):
            d = ctab_ref[tbase + q]
            e = etab_ref[tbase + q]
            posrow = pos_ref[0, j, pl.ds(e, 1), :]
            rowid = (rows16 + (rb * STACK_BLK + q * CHUNK_ROWS)).astype(F32)
            hit = jnp.logical_and(posrow == rowid, d >= 0)
            s_ref[q * CHUNK_ROWS:(q + 1) * CHUNK_ROWS, :] = hit.astype(BF16)
            if with_gate:
                gaterow = gate_ref[0, j, pl.ds(e, 1), :]
                gs_ref[q * CHUNK_ROWS:(q + 1) * CHUNK_ROWS, :] = jnp.where(hit, gaterow, 0.0)
        return tbase

    @pl.when(s < MOE_TILE_STEPS)
    def _gather():
        def tile(t, _):
            j = s * MOE_STEP_TILES + t
            tok = pl.multiple_of(t * TILE, TILE)

            def blk(rb, _):
                tbase = build_block(j, rb, True)
                picked = _dot(s_ref[...], x_ref[pl.ds(tok, TILE), :]).astype(BF16)
                gcol = jnp.sum(gs_ref[...], axis=1, keepdims=True)
                for q in range(chunks):
                    d = chunk_rows(tbase, q, DUMP_ROW)
                    rows = slice(q * CHUNK_ROWS, (q + 1) * CHUNK_ROWS)
                    lists_ref[pl.ds(d, CHUNK_ROWS), :] = picked[rows, :]
                    gl_ref[pl.ds(d, CHUNK_ROWS), :] = gcol[rows, :]
                return 0

            lax.fori_loop(0, nblk_ref[g * GROUP_TILES + j], blk, 0)
            return 0

        lax.fori_loop(0, MOE_STEP_TILES, tile, 0)

    @pl.when(jnp.logical_and(s >= MOE_TILE_STEPS, s < MOE_TILE_STEPS + N_EXPERTS))
    def _experts():
        e = s - MOE_TILE_STEPS
        ln = llen_ref[g * N_EXPERTS + e]
        base = e * LIST_MAX
        half = FFN_BLK // 2
        rem = ln % FFN_BLK
        use_half = jnp.logical_and(rem > 0, rem <= half)
        nfull = ln // FFN_BLK + jnp.where(rem > half, 1, 0)
        end = nfull * FFN_BLK + jnp.where(use_half, half, 0)

        def zero_tail(k, _):
            r0 = pl.multiple_of(base + ln + k * CHUNK_ROWS, CHUNK_ROWS)
            lists_ref[pl.ds(r0, CHUNK_ROWS), :] = jnp.zeros((CHUNK_ROWS, D_MODEL), BF16)
            gl_ref[pl.ds(r0, CHUNK_ROWS), :] = jnp.zeros((CHUNK_ROWS, 1), F32)
            return 0

        lax.fori_loop(0, (end - ln) // CHUNK_ROWS, zero_tail, 0)
        wgb_ref[...] = wg_ref[0, 0].astype(BF16)
        wub_ref[...] = wu_ref[0, 0].astype(BF16)
        wdb_ref[...] = wd_ref[0, 0].astype(BF16)

        def ffn_rows(r0, n):
            xs = lists_ref[pl.ds(r0, n), :]
            hid = _silu(_dot(xs, wgb_ref[...])) * _dot(xs, wub_ref[...])
            y = _dot(hid.astype(BF16), wdb_ref[...]) * gl_ref[pl.ds(r0, n), :]
            lists_ref[pl.ds(r0, n), :] = y.astype(BF16)

        def ffn(c, _):
            ffn_rows(pl.multiple_of(base + c * FFN_BLK, FFN_BLK), FFN_BLK)
            return 0

        lax.fori_loop(0, nfull, ffn, 0)

        @pl.when(use_half)
        def _():
            ffn_rows(pl.multiple_of(base + nfull * FFN_BLK, half), half)

    @pl.when(s >= MOE_TILE_STEPS + N_EXPERTS)
    def _scatter():
        def tile(t, _):
            j = (s - MOE_TILE_STEPS - N_EXPERTS) * SCATTER_STEP_TILES + t
            rows = pl.ds(pl.multiple_of(t * TILE, TILE), TILE)
            acc_ref[...] = jnp.zeros_like(acc_ref)

            def blk(rb, _):
                tbase = build_block(j, rb, False)
                y = jnp.concatenate([lists_ref[pl.ds(chunk_rows(tbase, q, ZERO_ROW), CHUNK_ROWS), :]
                                     for q in range(chunks)], axis=0)
                acc_ref[...] += _dot_t(s_ref[...], y)
                return 0

            lax.fori_loop(0, nblk_ref[g * GROUP_TILES + j], blk, 0)
            x = xres_ref[rows, :] + mod_ref[0, 0, 5:6, :] * acc_ref[...]
            if final:
                y = x * lax.rsqrt(jnp.mean(x * x, axis=-1, keepdims=True) + EPS) * fn_ref[...]

                @pl.when(g == 0)
                def _():
                    out_refs[0][rows, :] = y

                @pl.when(g > 0)
                def _():
                    out_refs[1][rows, :] = y
            else:
                out_refs[0][rows, :] = x
            return 0

        lax.fori_loop(0, SCATTER_STEP_TILES, tile, 0)


def _moe(layer, final, tables, h2, pos_t, gate_t, wg, wu, wd, xres, mod, fn):
    gt = GROUP_TILES
    ts = MOE_TILE_STEPS
    sc = SCATTER_STEPS
    rows_a = MOE_STEP_TILES * TILE
    rows = SCATTER_STEP_TILES * TILE
    step_c = lambda s: jnp.clip(s - ts - N_EXPERTS, 0, sc - 1)
    tile_a = lambda g, s, *_: (g * ts + jnp.minimum(s, ts - 1), 0)
    tile_c = lambda g, s, *_: (g * sc + step_c(s), 0)
    expert = lambda g, s, *_: (layer, jnp.clip(s - ts, 0, N_EXPERTS - 1), 0, 0)
    group = lambda g, s, *_: (g, 0, 0, 0)
    if final:
        out_specs = [
            pl.BlockSpec((rows, D_MODEL), lambda g, s, *_: (jnp.where(g == 0, step_c(s), sc - 1), 0)),
            pl.BlockSpec((rows, D_MODEL), lambda g, s, *_: (jnp.where(g == 0, 0, (g - 1) * sc + step_c(s)), 0)),
        ]
        out_shape = [jax.ShapeDtypeStruct((NT_CTX * TILE, D_MODEL), F32),
                     jax.ShapeDtypeStruct((NT_LAT * TILE, D_MODEL), F32)]
    else:
        out_specs = pl.BlockSpec((rows, D_MODEL), tile_c)
        out_shape = jax.ShapeDtypeStruct((N_TOK, D_MODEL), F32)
    grid_spec = pltpu.PrefetchScalarGridSpec(
        num_scalar_prefetch=4,
        grid=(N_GROUPS, ts + N_EXPERTS + sc),
        in_specs=[
            pl.BlockSpec((rows_a, D_MODEL), tile_a),
            pl.BlockSpec((1, gt, N_EXPERTS, TILE), group),
            pl.BlockSpec((1, gt, N_EXPERTS, TILE), group),
            pl.BlockSpec((1, 1, D_MODEL, EXPERT_FF), expert),
            pl.BlockSpec((1, 1, D_MODEL, EXPERT_FF), expert),
            pl.BlockSpec((1, 1, EXPERT_FF, D_MODEL), expert),
            pl.BlockSpec((rows, D_MODEL), tile_c),
            pl.BlockSpec((1, 1, 6, D_MODEL), lambda g, s, *_: (layer, g, 0, 0)),
            pl.BlockSpec((1, D_MODEL), lambda g, s, *_: (0, 0)),
        ],
        out_specs=out_specs,
        scratch_shapes=[
            pltpu.VMEM((LIST_ROWS, D_MODEL), BF16),
            pltpu.VMEM((LIST_ROWS, 1), F32),
            pltpu.VMEM((STACK_BLK, TILE), BF16),
            pltpu.VMEM((STACK_BLK, TILE), F32),
            pltpu.VMEM((TILE, D_MODEL), F32),
            pltpu.VMEM((D_MODEL, EXPERT_FF), BF16),
            pltpu.VMEM((D_MODEL, EXPERT_FF), BF16),
            pltpu.VMEM((EXPERT_FF, D_MODEL), BF16),
        ],
    )
    return pl.pallas_call(
        functools.partial(_moe_kernel, final=final),
        grid_spec=grid_spec,
        out_shape=out_shape,
        compiler_params=_cparams(("arbitrary", "arbitrary")),
        name="moe",
    )(*tables, h2, pos_t, gate_t, wg, wu, wd, xres, mod, fn)


def _rope_tables(rot_dim):
    axis_dim = rot_dim // 2
    tok = jnp.arange(DEC_SEQ)
    rows = (tok // GRID_W).astype(F32)
    cols = (tok % GRID_W).astype(F32)
    inv_freq = ROPE_THETA ** (-jnp.arange(0, axis_dim, 2, dtype=F32) / axis_dim)
    ar = rows[:, None] * inv_freq
    ac = cols[:, None] * inv_freq
    cos = jnp.concatenate([jnp.cos(ar), jnp.cos(ar), jnp.cos(ac), jnp.cos(ac)], axis=1)
    sin = jnp.concatenate([-jnp.sin(ar), jnp.sin(ar), -jnp.sin(ac), jnp.sin(ac)], axis=1)
    rep = LANES // rot_dim
    cos = jnp.tile(cos, (1, rep)).reshape(NT_SEQ, TILE, LANES)
    sin = jnp.tile(sin, (1, rep)).reshape(NT_SEQ, TILE, LANES)
    cos = jnp.concatenate([jnp.ones((1, TILE, LANES), F32), cos], axis=0)
    sin = jnp.concatenate([jnp.zeros((1, TILE, LANES), F32), sin], axis=0)
    return cos, sin


def _excl_cumsum(a, axis):
    return jnp.cumsum(a, axis=axis) - a


def _route_and_moe(layer, final, h2, logits, wg, wu, wd, xres, mod, fn):
    pos_t, gate_t = _route(logits)
    sel = pos_t >= 0
    cnt = jnp.sum(sel, axis=-1, dtype=jnp.int32)
    npad = (cnt + CHUNK_ROWS - 1) // CHUNK_ROWS * CHUNK_ROWS
    seg_off = _excl_cumsum(npad, 2)
    list_off = _excl_cumsum(npad, 1)
    rank0 = _excl_cumsum(cnt, 1)
    stack_pos = jnp.where(sel, pos_t - rank0[..., None].astype(F32) + seg_off[..., None].astype(F32), -1.0)
    nblk = (jnp.sum(npad, axis=2) + STACK_BLK - 1) // STACK_BLK
    llen = jnp.sum(npad, axis=1)
    r = (jnp.arange(MAX_STACK_BLKS * STACK_BLK // CHUNK_ROWS, dtype=jnp.int32) * CHUNK_ROWS)[None, None, None, :]
    in_seg = jnp.logical_and(r >= seg_off[..., None], r < (seg_off + npad)[..., None])
    eidx = jnp.arange(N_EXPERTS, dtype=jnp.int32)[None, None, :, None]
    dst = jnp.sum(jnp.where(in_seg, eidx * LIST_MAX + list_off[..., None] + r - seg_off[..., None], 0), axis=2)
    used = jnp.any(in_seg, axis=2)
    ctab = jnp.where(used, dst, -1).astype(jnp.int32)
    etab = jnp.sum(jnp.where(in_seg, eidx, 0), axis=2).astype(jnp.int32)
    tables = (nblk.reshape(-1).astype(jnp.int32), ctab.reshape(-1), etab.reshape(-1),
              llen.reshape(-1).astype(jnp.int32))
    return _moe(layer, final, tables, h2, stack_pos, gate_t, wg, wu, wd, xres, mod, fn)


def _split_router(w):
    wp = jnp.zeros((D_MODEL, LANES), F32).at[:, :N_EXPERTS].set(w)
    hi = wp.astype(BF16)
    lo = (wp - hi.astype(F32)).astype(BF16)
    return hi, lo


def kernel(x_prompt, x_sample, c, cache_attn_k, cache_attn_v, cache_mla_ckv, cache_mla_krope, c_ctx, w_mod, b_mod,
           norm1, norm2, ev_w_in, ev_q_norm, ev_k_norm, ev_w_s, ev_b_s, ev_w_out, od_w_in, od_q_a_norm,
           od_kv_a_norm, od_w_uq, od_w_ukv, od_conv_w, od_w_out, moe_router, moe_w_gate, moe_w_up, moe_w_down,
           final_norm):
    nctx = NT_CTX * TILE
    xc = x_prompt.reshape(nctx, D_MODEL)
    xl = x_sample.reshape(NT_LAT * TILE, D_MODEL)
    cvec = jnp.concatenate([c_ctx[None], c, jnp.zeros((3, D_MODEL), F32)], axis=0)
    mod = _modulation(cvec, w_mod, b_mod).reshape(2, 8, 6, D_MODEL)

    cos64, sin64 = _rope_tables(HEAD_DIM)
    seg = jnp.arange(512) // HEAD_DIM
    bd = (seg[:, None] == seg[None, :]).astype(BF16)
    bs_tab = jnp.repeat(ev_b_s[0].T, 64, axis=1)
    gated, qt, k, kb, v, vt = _even_in(
        xc, xl, mod, norm1[0:1], ev_w_in[0].astype(BF16), jnp.tile(ev_q_norm[0], 8)[None], jnp.tile(ev_k_norm[0], 2)[None],
        cos64, sin64, ev_w_s[0].astype(BF16), bs_tab, bd)
    new_k = k[:nctx].reshape(BATCH, SEQ, B_KV_HEADS, HEAD_DIM).transpose(0, 2, 1, 3)[:, None]
    new_v = v[:nctx].reshape(BATCH, SEQ, B_KV_HEADS, HEAD_DIM).transpose(0, 2, 1, 3)[:, None]
    ck = cache_attn_k[:, 0].transpose(0, 2, 1, 3).reshape(DEC_BATCH, PAST_LEN, LANES).astype(BF16)
    cv = cache_attn_v[:, 0].transpose(0, 1, 3, 2).reshape(DEC_BATCH, LANES, PAST_LEN).astype(BF16)
    attn_t = _gqa_attention(qt, kb, vt, ck, cv)
    wrh, wrl = _split_router(moe_router[0])
    x1, h2, logits = _even_out(0, gated, attn_t, ev_w_out[0].astype(BF16), xc, xl, mod, norm2[0:1], wrh, wrl)
    x2 = _route_and_moe(0, False, h2, logits, moe_w_gate, moe_w_up, moe_w_down, x1, mod, final_norm[None])

    cos32, sin32 = _rope_tables(C_ROPE)
    w_in1 = od_w_in[0]
    w_in1 = jnp.concatenate([w_in1[:, :C_Q_LORA + C_KV_LORA + C_ROPE], jnp.zeros((D_MODEL, LANES - C_ROPE), F32),
                             w_in1[:, C_Q_LORA + C_KV_LORA + C_ROPE:]], axis=1).astype(BF16)
    wuq = od_w_uq[0].reshape(C_Q_LORA, C_HEADS, C_NOPE + C_ROPE)
    wuq = jnp.concatenate([wuq[:, :, :C_NOPE].reshape(C_Q_LORA, -1), wuq[:, :, C_NOPE:].reshape(C_Q_LORA, -1)],
                          axis=1).astype(BF16)
    wukv = od_w_ukv[0].reshape(C_KV_LORA, C_HEADS, C_NOPE + C_V)
    wukv = jnp.concatenate([wukv[:, :, :C_NOPE].reshape(C_KV_LORA, -1), wukv[:, :, C_NOPE:].reshape(C_KV_LORA, -1)],
                           axis=1).astype(BF16)
    qnt, qrt, ckv, kr, krb, gb, z, kn, vt1 = _odd_in(x2, mod, norm1[1:2], w_in1, od_q_a_norm[0][None],
                                                     od_kv_a_norm[0][None], wuq, wukv, cos32, sin32)
    new_ckv = ckv[:nctx].reshape(BATCH, 1, SEQ, C_KV_LORA)
    new_kr = kr[:nctx, :C_ROPE].reshape(BATCH, 1, SEQ, C_ROPE)
    kn_cache, vt_cache = _kvup(cache_mla_ckv[:, 0].reshape(DEC_BATCH * PAST_LEN, C_KV_LORA), wukv)
    ckr = jnp.pad(cache_mla_krope[:, 0], ((0, 0), (0, 0), (0, LANES - C_ROPE))).astype(BF16)
    attn_t = _mla_attention(qnt, qrt, kn, krb, vt1, kn_cache, ckr, vt_cache)
    wrh, wrl = _split_router(moe_router[1])
    cw = jnp.concatenate([od_conv_w[0], jnp.zeros((5, D_WIDTH), F32)], axis=0)
    x3, h2, logits = _odd_out(1, attn_t, z, gb, cw, od_w_out[0].astype(BF16), x2, mod, norm2[1:2], wrh, wrl)
    y_c, y_l = _route_and_moe(1, True, h2, logits, moe_w_gate, moe_w_up, moe_w_down, x3, mod, final_norm[None])
    y_prompt = y_c.reshape(BATCH, SEQ, D_MODEL)
    y_sample = y_l.reshape(DEC_BATCH, DEC_SEQ, D_MODEL)
    return y_prompt, y_sample, new_k, new_v, new_ckv, new_kr
```

```python
import functools
import math

import jax
import jax.numpy as jnp
from jax import lax
from jax.experimental import pallas as pl
from jax.experimental.pallas import tpu as pltpu

F32 = jnp.float32
BF16 = jnp.bfloat16

D_MODEL = 1024
BATCH = 16
SEQ = 256
DEC_BATCH = 4
DEC_SEQ = 4096
PAST_LEN = 512
GRID_W = 64
ROPE_THETA = 10000.0
EPS = 1e-6
HEAD_DIM = 64
CHUNK = 128
A_GROUPS = 8
A_WIDTH = 512
B_HEADS = 8
B_KV_HEADS = 2
B_WIDTH = 512
C_HEADS = 8
C_NOPE = 64
C_ROPE = 32
C_V = 64
C_Q_LORA = 256
C_KV_LORA = 128
D_WIDTH = 512
N_EXPERTS = 16
EXPERT_FF = 512
EC_CAPACITY = 2

LOG2E = math.log2(math.e)
LANES = 128
BF16_ROWS = 16
ATT_TK = 256
MLA_PAIRS = 2
GQA_STREAMS = 2
TILE = 256
NT_CTX = BATCH * SEQ // TILE
NT_SEQ = DEC_SEQ // TILE
NT_LAT = DEC_BATCH * NT_SEQ
NT = NT_CTX + NT_LAT
N_TOK = NT * TILE
GROUP = 4096
N_GROUPS = N_TOK // GROUP
GROUP_TILES = GROUP // TILE
GROUP_CAP = EC_CAPACITY * GROUP // N_EXPERTS
CHUNK_ROWS = BF16_ROWS
STACK_BLK = 256
MAX_STACK_BLKS = -(-N_EXPERTS * TILE // STACK_BLK)
MOE_STEP_TILES = 2
MOE_TILE_STEPS = GROUP_TILES // MOE_STEP_TILES
SCATTER_STEP_TILES = 1
SCATTER_STEPS = GROUP_TILES // SCATTER_STEP_TILES
FFN_BLK = 256
LIST_MAX = -(-(GROUP_CAP + GROUP_TILES * (CHUNK_ROWS - 1)) // FFN_BLK) * FFN_BLK
DUMP_ROW = N_EXPERTS * LIST_MAX
ZERO_ROW = DUMP_ROW + CHUNK_ROWS
LIST_ROWS = ZERO_ROW + CHUNK_ROWS
VMEM_LIMIT = 58 * 1024 * 1024


def _cparams(sem):
    return pltpu.CompilerParams(dimension_semantics=sem, vmem_limit_bytes=VMEM_LIMIT)


def _dot(a, b):
    return jnp.dot(a, b, preferred_element_type=F32)


def _silu(x):
    return x / (1.0 + jnp.exp(-x))


def _mod_row(i):
    return jnp.where(i < NT_CTX, 0, 1 + (i - NT_CTX) // NT_SEQ)


def _tab_blk(i):
    return jnp.where(i < NT_CTX, 0, 1 + (i - NT_CTX) % NT_SEQ)


def _mod_kernel(c_ref, w_ref, b_ref, o_ref):
    s = _silu(c_ref[...])
    o_ref[0] = _dot(s.astype(BF16), w_ref[0].astype(BF16)) + b_ref[0]


def _modulation(cvec, w_mod, b_mod):
    depth = w_mod.shape[0]
    nchunk = 6
    return pl.pallas_call(
        _mod_kernel,
        grid=(depth, nchunk),
        in_specs=[
            pl.BlockSpec((8, D_MODEL), lambda l, k: (0, 0)),
            pl.BlockSpec((1, D_MODEL, D_MODEL), lambda l, k: (l, 0, k)),
            pl.BlockSpec((1, 1, D_MODEL), lambda l, k: (l, 0, k)),
        ],
        out_specs=pl.BlockSpec((1, 8, D_MODEL), lambda l, k: (l, 0, k)),
        out_shape=jax.ShapeDtypeStruct((depth, 8, 6 * D_MODEL), F32),
        compiler_params=_cparams(("arbitrary", "arbitrary")),
        name="modulation",
    )(cvec, w_mod, b_mod.reshape(depth, 1, 6 * D_MODEL))


def _norm_mod(x, gain, scale, shift):
    ms = jnp.mean(x * x, axis=-1, keepdims=True)
    return (x * lax.rsqrt(ms + EPS) * gain) * (1.0 + scale) + shift


def _seg_mean_sq(z, bd, width):
    zz = z * z
    hi = zz.astype(BF16)
    lo = (zz - hi.astype(F32)).astype(BF16)
    return (_dot(hi, bd) + _dot(lo, bd)) * (1.0 / width)


def _rope(z, cos, sin_signed, half):
    w = z.shape[1]
    lane = lax.broadcasted_iota(jnp.int32, z.shape, 1)
    first = (lane % (2 * half)) < half
    partner = jnp.where(first, pltpu.roll(z, w - half, 1), pltpu.roll(z, half, 1))
    return z * cos + partner * sin_signed


def _tile_lanes(t, n):
    return jnp.concatenate([t] * n, axis=1) if n > 1 else t


def _even_in_kernel(xc_ref, xl_ref, mod_ref, n1_ref, w_ref, qg_ref, kg_ref, cos_ref, sin_ref, ws_ref, bs_ref, bd_ref,
                    gated_ref, qt_ref, k_ref, kb_ref, v_ref, vt_ref):
    x = _group_tile(pl.program_id(0), xc_ref, xl_ref)
    h = _norm_mod(x, n1_ref[...], mod_ref[0, 0, 1:2, :], mod_ref[0, 0, 0:1, :])
    p = _dot(h.astype(BF16), w_ref[...])
    u = p[:, 0:512]
    va = p[:, 512:1024].astype(BF16)
    q = p[:, 1024:1536]
    k = p[:, 1536:1664]
    v = p[:, 1664:1792]
    bd = bd_ref[...]
    cos = cos_ref[0]
    sin = sin_ref[0]
    qn = q * lax.rsqrt(_seg_mean_sq(q, bd, HEAD_DIM) + EPS) * qg_ref[...]
    kn = k * lax.rsqrt(_seg_mean_sq(k, bd[0:LANES, 0:LANES], HEAD_DIM) + EPS) * kg_ref[...]
    qr = _rope(qn, _tile_lanes(cos, 4), _tile_lanes(sin, 4), HEAD_DIM // 4)
    kr = _rope(kn, cos, sin, HEAD_DIM // 4)
    qt_ref[...] = (qr * (HEAD_DIM ** -0.5 * LOG2E)).T.astype(BF16)
    k_ref[...] = kr
    kb_ref[...] = kr.astype(BF16)
    v_ref[...] = v
    vt_ref[...] = v.T.astype(BF16)
    lane = lax.broadcasted_iota(jnp.int32, (CHUNK, LANES), 1)
    for ch in range(TILE // CHUNK):
        rows = slice(ch * CHUNK, (ch + 1) * CHUNK)
        cols = []
        for pair in range(A_GROUPS // 2):
            vp = va[rows, pair * LANES:(pair + 1) * LANES]
            r0 = _dot(ws_ref[2 * pair], vp)
            r1 = _dot(ws_ref[2 * pair + 1], vp)
            cols.append(jnp.where(lane < LANES // 2, r0, r1))
        s = jnp.concatenate(cols, axis=1) + bs_ref[...]
        gated_ref[rows, :] = (u[rows, :] * s).astype(BF16)


def _even_in(xc, xl, mod, n1, w_in, qg, kg, cos, sin, ws, bs_tab, bd):
    full = lambda shape: pl.BlockSpec(shape, lambda i: (0,) * len(shape))
    return pl.pallas_call(
        _even_in_kernel,
        grid=(NT,),
        in_specs=[
            pl.BlockSpec((TILE, D_MODEL), _CTX_ROW),
            pl.BlockSpec((TILE, D_MODEL), _LAT_ROW),
            pl.BlockSpec((1, 1, 6, D_MODEL), lambda i: (0, _mod_row(i), 0, 0)),
            full((1, D_MODEL)),
            full(w_in.shape),
            full((1, 512)),
            full((1, LANES)),
            pl.BlockSpec((1, TILE, LANES), lambda i: (_tab_blk(i), 0, 0)),
            pl.BlockSpec((1, TILE, LANES), lambda i: (_tab_blk(i), 0, 0)),
            full(ws.shape),
            full(bs_tab.shape),
            full(bd.shape),
        ],
        out_specs=[
            pl.BlockSpec((TILE, 512), lambda i: (i, 0)),
            pl.BlockSpec((512, TILE), lambda i: (0, i)),
            pl.BlockSpec((TILE, LANES), lambda i: (i, 0)),
            pl.BlockSpec((TILE, LANES), lambda i: (i, 0)),
            pl.BlockSpec((TILE, LANES), lambda i: (i, 0)),
            pl.BlockSpec((LANES, TILE), lambda i: (0, i)),
        ],
        out_shape=[
            jax.ShapeDtypeStruct((N_TOK, 512), BF16),
            jax.ShapeDtypeStruct((512, N_TOK), BF16),
            jax.ShapeDtypeStruct((N_TOK, LANES), F32),
            jax.ShapeDtypeStruct((N_TOK, LANES), BF16),
            jax.ShapeDtypeStruct((N_TOK, LANES), F32),
            jax.ShapeDtypeStruct((LANES, N_TOK), BF16),
        ],
        compiler_params=_cparams(("parallel",)),
        name="even_in",
    )(xc, xl, mod, n1, w_in, qg, kg, cos, sin, ws, bs_tab, bd)


def _odd_in_kernel(x_ref, mod_ref, n1_ref, w_ref, qa_ref, kva_ref, wuq_ref, wukv_ref, cos_ref, sin_ref,
                   qnt_ref, qrt_ref, ckv_ref, kr_ref, krb_ref, gb_ref, z_ref, kn_ref, vt_ref):
    h = _norm_mod(x_ref[...], n1_ref[...], mod_ref[0, 0, 1:2, :], mod_ref[0, 0, 0:1, :])
    p = _dot(h.astype(BF16), w_ref[...])
    cq = p[:, 0:256]
    ckv = p[:, 256:384]
    kr = p[:, 384:512]
    gb_ref[...] = p[:, 512:1024]
    z_ref[...] = p[:, 1024:1536] * p[:, 1536:2048]
    cqn = cq * lax.rsqrt(jnp.mean(cq * cq, axis=-1, keepdims=True) + EPS) * qa_ref[...]
    q = _dot(cqn.astype(BF16), wuq_ref[...])
    scale = (C_NOPE + C_ROPE) ** -0.5 * LOG2E
    cos = cos_ref[0]
    sin = sin_ref[0]
    qnt_ref[...] = (q[:, 0:512] * scale).T.astype(BF16)
    qr = _rope(q[:, 512:768], _tile_lanes(cos, 2), _tile_lanes(sin, 2), C_ROPE // 4)
    qrt_ref[...] = (qr * scale).T.astype(BF16)
    ckvn = ckv * lax.rsqrt(jnp.mean(ckv * ckv, axis=-1, keepdims=True) + EPS) * kva_ref[...]
    ckv_ref[...] = ckvn
    kv = _dot(ckvn.astype(BF16), wukv_ref[...])
    kn_ref[...] = kv[:, 0:512].astype(BF16)
    vt_ref[...] = kv[:, 512:1024].T.astype(BF16)
    krr = _rope(kr, cos, sin, C_ROPE // 4)
    kr_ref[...] = krr
    krb_ref[...] = krr.astype(BF16)


def _odd_in(x, mod, n1, w_in, qa, kva, wuq, wukv, cos, sin):
    full = lambda shape: pl.BlockSpec(shape, lambda i: (0,) * len(shape))
    tile = lambda w: pl.BlockSpec((TILE, w), lambda i: (i, 0))
    return pl.pallas_call(
        _odd_in_kernel,
        grid=(NT,),
        in_specs=[
            tile(D_MODEL),
            pl.BlockSpec((1, 1, 6, D_MODEL), lambda i: (1, _mod_row(i), 0, 0)),
            full((1, D_MODEL)),
            full(w_in.shape),
            full((1, C_Q_LORA)),
            full((1, C_KV_LORA)),
            full(wuq.shape),
            full(wukv.shape),
            pl.BlockSpec((1, TILE, LANES), lambda i: (_tab_blk(i), 0, 0)),
            pl.BlockSpec((1, TILE, LANES), lambda i: (_tab_blk(i), 0, 0)),
        ],
        out_specs=[pl.BlockSpec((512, TILE), lambda i: (0, i)),
                   pl.BlockSpec((256, TILE), lambda i: (0, i)),
                   tile(LANES), tile(LANES), tile(LANES), tile(512), tile(512),
                   tile(512), pl.BlockSpec((512, TILE), lambda i: (0, i))],
        out_shape=[
            jax.ShapeDtypeStruct((512, N_TOK), BF16),
            jax.ShapeDtypeStruct((256, N_TOK), BF16),
            jax.ShapeDtypeStruct((N_TOK, LANES), F32),
            jax.ShapeDtypeStruct((N_TOK, LANES), F32),
            jax.ShapeDtypeStruct((N_TOK, LANES), BF16),
            jax.ShapeDtypeStruct((N_TOK, 512), F32),
            jax.ShapeDtypeStruct((N_TOK, 512), F32),
            jax.ShapeDtypeStruct((N_TOK, 512), BF16),
            jax.ShapeDtypeStruct((512, N_TOK), BF16),
        ],
        compiler_params=_cparams(("parallel",)),
        name="odd_in",
    )(x, mod, n1, w_in, qa, kva, wuq, wukv, cos, sin)


def _kvup_kernel(c_ref, w_ref, kn_ref, vt_ref):
    p = _dot(c_ref[...].astype(BF16), w_ref[...])
    kn_ref[...] = p[:, 0:512].astype(BF16)
    vt_ref[...] = p[:, 512:1024].T.astype(BF16)


def _kvup(ckv_all, w_ukv):
    n = ckv_all.shape[0]
    return pl.pallas_call(
        _kvup_kernel,
        grid=(n // TILE,),
        in_specs=[pl.BlockSpec((TILE, C_KV_LORA), lambda i: (i, 0)),
                  pl.BlockSpec(w_ukv.shape, lambda i: (0, 0))],
        out_specs=[pl.BlockSpec((TILE, 512), lambda i: (i, 0)), pl.BlockSpec((512, TILE), lambda i: (0, i))],
        out_shape=[jax.ShapeDtypeStruct((n, 512), BF16), jax.ShapeDtypeStruct((512, n), BF16)],
        compiler_params=_cparams(("parallel",)),
        name="mla_kv_up",
    )(ckv_all, w_ukv)


def _softmax_pv(qc, key_chunk, vt_chunks, nk):
    return _softmax_pv_streams([(qc, key_chunk, vt_chunks)], nk)[0]


def _softmax_pv_streams(streams, nk):
    ones = jnp.ones((BF16_ROWS, ATT_TK), BF16)
    ms = [jnp.full((1, qc.shape[1]), -jnp.inf, F32) for qc, _, _ in streams]
    accs = [jnp.zeros((HEAD_DIM + BF16_ROWS, qc.shape[1]), F32) for qc, _, _ in streams]
    nxt = [_dot(key_chunk(0), qc) for qc, key_chunk, _ in streams]
    for c in range(nk):
        for i, (qc, key_chunk, vt_chunks) in enumerate(streams):
            s = nxt[i]
            if c + 1 < nk:
                nxt[i] = _dot(key_chunk(c + 1), qc)
            mn = jnp.maximum(ms[i], jnp.max(s, axis=0, keepdims=True))
            p = jnp.exp2(s - mn).astype(BF16)
            vts = vt_chunks(c)
            if len(vts) == 1:
                pv = _dot(jnp.concatenate([vts[0], ones], axis=0), p)
            else:
                tq = qc.shape[1] // len(vts)
                pv = jnp.concatenate([_dot(jnp.concatenate([vt, ones], axis=0), p[:, k * tq:(k + 1) * tq])
                                      for k, vt in enumerate(vts)], axis=1)
            accs[i] = jnp.exp2(ms[i] - mn) * accs[i] + pv
            ms[i] = mn
    return [acc[0:HEAD_DIM, :] / acc[HEAD_DIM:HEAD_DIM + 1, :] for acc in accs]


def _chunk_locator(segments):
    counts = [seg[0].shape[0] // ATT_TK for seg in segments]

    def locate(c):
        for seg, n in zip(segments, counts):
            if c < n:
                return seg, slice(c * ATT_TK, (c + 1) * ATT_TK)
            c -= n
        raise IndexError(c)

    return locate, sum(counts)


def _gqa_kernel(q_ref, *refs):
    o_ref = refs[-1]
    locate, nk = _chunk_locator([refs[i:i + 2] for i in range(0, len(refs) - 1, 2)])
    kvh = pl.program_id(1)
    tq = q_ref.shape[1]
    n = B_HEADS // B_KV_HEADS
    cols = []
    for hh in range(n):
        q = q_ref[hh * HEAD_DIM:(hh + 1) * HEAD_DIM, :]
        z = jnp.zeros_like(q)
        cols.append(jnp.where(kvh == 0, jnp.concatenate([q, z], axis=0), jnp.concatenate([z, q], axis=0)))
    def key_chunk(c):
        (k_ref, _), rows = locate(c)
        return k_ref[rows, :]

    def vt_chunks(c):
        (_, vt_ref), cols_c = locate(c)
        return [vt_ref[:, cols_c]]

    per = n // GQA_STREAMS
    outs = _softmax_pv_streams(
        [(jnp.concatenate(cols[i * per:(i + 1) * per], axis=1), key_chunk, vt_chunks) for i in range(GQA_STREAMS)], nk)
    for hh in range(n):
        o = outs[hh // per][:, (hh % per) * tq:(hh % per + 1) * tq]
        o_ref[hh * HEAD_DIM:(hh + 1) * HEAD_DIM, :] = o.astype(BF16)


def _mla_kernel(qn_ref, qr_ref, *refs):
    o_ref = refs[-1]
    locate, nk = _chunk_locator([refs[i:i + 3] for i in range(0, len(refs) - 1, 3)])
    tq = qn_ref.shape[1]
    z = jnp.zeros((C_NOPE, tq), BF16)
    zr = jnp.zeros((LANES - C_ROPE, tq), BF16)
    streams = []
    for pair in range(MLA_PAIRS):
        cols = []
        for hh in range(2):
            h = 2 * pair + hh
            qn = qn_ref[h * C_NOPE:(h + 1) * C_NOPE, :]
            qr = qr_ref[h * C_ROPE:(h + 1) * C_ROPE, :]
            cols.append(jnp.concatenate(([qn, z] if hh == 0 else [z, qn]) + [qr, zr], axis=0))

        def key_chunk(c, pair=pair):
            (kn_ref, kr_ref, _), rows = locate(c)
            return jnp.concatenate([kn_ref[rows, pair * LANES:(pair + 1) * LANES], kr_ref[rows, :]], axis=1)

        def vt_chunks(c, pair=pair):
            (_, _, vt_ref), cols_c = locate(c)
            return [vt_ref[(2 * pair + hh) * C_V:(2 * pair + hh + 1) * C_V, cols_c] for hh in range(2)]

        streams.append((jnp.concatenate(cols, axis=1), key_chunk, vt_chunks))
    outs = _softmax_pv_streams(streams, nk)
    for pair in range(MLA_PAIRS):
        for hh in range(2):
            h = 2 * pair + hh
            o_ref[h * C_V:(h + 1) * C_V, :] = outs[pair][:, hh * tq:(hh + 1) * tq].astype(BF16)


def _attn_call(body, grid, in_specs, out_spec, args, n_tiles):
    return pl.pallas_call(
        body,
        grid=grid,
        in_specs=in_specs,
        out_specs=out_spec,
        out_shape=jax.ShapeDtypeStruct((512, n_tiles * TILE), BF16),
        compiler_params=_cparams(("parallel",) * len(grid)),
        name="attention",
    )(*args)


def _gqa_attention(qt, kb, vt, k_cache, vt_cache):
    assert NT_CTX * TILE == DEC_SEQ
    rows = (B_HEADS // B_KV_HEADS) * HEAD_DIM
    o_ctx = _attn_call(
        _gqa_kernel, (BATCH, B_KV_HEADS),
        [pl.BlockSpec((rows, TILE), lambda b, h: (h, b)),
         pl.BlockSpec((None, TILE, LANES), lambda b, h: (b, 0, 0)),
         pl.BlockSpec((HEAD_DIM, TILE), lambda b, h: (h, b))],
        pl.BlockSpec((rows, TILE), lambda b, h: (h, b)),
        [qt, kb.reshape(NT, TILE, LANES), vt], NT_CTX)
    o_lat = _attn_call(
        _gqa_kernel, (DEC_BATCH, B_KV_HEADS, NT_SEQ),
        [pl.BlockSpec((rows, TILE), lambda b, h, j: (h, NT_CTX + b * NT_SEQ + j)),
         pl.BlockSpec((None, PAST_LEN, LANES), lambda b, h, j: (b, 0, 0)),
         pl.BlockSpec((None, HEAD_DIM, PAST_LEN), lambda b, h, j: (b, h, 0)),
         pl.BlockSpec((DEC_SEQ, LANES), lambda b, h, j: (1 + b, 0)),
         pl.BlockSpec((HEAD_DIM, DEC_SEQ), lambda b, h, j: (h, 1 + b))],
        pl.BlockSpec((rows, TILE), lambda b, h, j: (h, b * NT_SEQ + j)),
        [qt, k_cache, vt_cache, kb, vt], NT_LAT)
    return o_ctx, o_lat


def _mla_attention(qnt, qrt, kn, krb, vt, kn_cache, kr_cache, vt_cache):
    nh = 2 * MLA_PAIRS
    steps = C_HEADS // nh
    o_ctx = _attn_call(
        _mla_kernel, (BATCH, steps),
        [pl.BlockSpec((nh * C_NOPE, TILE), lambda b, p: (p, b)),
         pl.BlockSpec((nh * C_ROPE, TILE), lambda b, p: (p, b)),
         pl.BlockSpec((TILE, MLA_PAIRS * LANES), lambda b, p: (b, p)),
         pl.BlockSpec((TILE, LANES), lambda b, p: (b, 0)),
         pl.BlockSpec((nh * C_V, TILE), lambda b, p: (p, b))],
        pl.BlockSpec((nh * C_V, TILE), lambda b, p: (p, b)),
        [qnt, qrt, kn, krb, vt], NT_CTX)
    o_lat = _attn_call(
        _mla_kernel, (DEC_BATCH, steps, NT_SEQ),
        [pl.BlockSpec((nh * C_NOPE, TILE), lambda b, p, j: (p, NT_CTX + b * NT_SEQ + j)),
         pl.BlockSpec((nh * C_ROPE, TILE), lambda b, p, j: (p, NT_CTX + b * NT_SEQ + j)),
         pl.BlockSpec((PAST_LEN, MLA_PAIRS * LANES), lambda b, p, j: (b, p)),
         pl.BlockSpec((None, PAST_LEN, LANES), lambda b, p, j: (b, 0, 0)),
         pl.BlockSpec((nh * C_V, PAST_LEN), lambda b, p, j: (p, b)),
         pl.BlockSpec((DEC_SEQ, MLA_PAIRS * LANES), lambda b, p, j: (1 + b, p)),
         pl.BlockSpec((DEC_SEQ, LANES), lambda b, p, j: (1 + b, 0)),
         pl.BlockSpec((nh * C_V, DEC_SEQ), lambda b, p, j: (p, 1 + b))],
        pl.BlockSpec((nh * C_V, TILE), lambda b, p, j: (p, b * NT_SEQ + j)),
        [qnt, qrt, kn_cache, kr_cache, vt_cache, kn, krb, vt], NT_LAT)
    return o_ctx, o_lat


def _group_tile(i, ctx_ref, lat_ref):
    return jnp.where(i < NT_CTX, ctx_ref[...], lat_ref[...])


_CTX_COL = lambda i: (0, jnp.minimum(i, NT_CTX - 1))
_LAT_COL = lambda i: (0, jnp.maximum(i - NT_CTX, 0))
_CTX_ROW = lambda i: (jnp.minimum(i, NT_CTX - 1), 0)
_LAT_ROW = lambda i: (jnp.maximum(i - NT_CTX, 0), 0)


def _finish_out(o, x_in, mod_ref, n2_ref, wrh_ref, wrl_ref, xo_ref, h2_ref, lg_ref):
    x = x_in + mod_ref[0, 0, 2:3, :] * o
    xo_ref[...] = x
    h2 = _norm_mod(x, n2_ref[...], mod_ref[0, 0, 4:5, :], mod_ref[0, 0, 3:4, :])
    hi = h2.astype(BF16)
    lo = (h2 - hi.astype(F32)).astype(BF16)
    h2_ref[...] = hi
    lg = _dot(hi, wrh_ref[...]) + _dot(lo, wrh_ref[...]) + _dot(hi, wrl_ref[...])
    lg_ref[...] = lg.T[0:N_EXPERTS, :]


def _dot_nt(a, bt):
    return lax.dot_general(a, bt, (((1,), (1,)), ((), ())), preferred_element_type=F32)


def _dot_t(at, b):
    return lax.dot_general(at, b, (((0,), (0,)), ((), ())), preferred_element_type=F32)


def _even_out_kernel(a_ref, btc_ref, btl_ref, w_ref, xc_ref, xl_ref, mod_ref, n2_ref, wrh_ref, wrl_ref,
                     xo_ref, h2_ref, lg_ref):
    i = pl.program_id(0)
    bt = _group_tile(i, btc_ref, btl_ref)
    o = _dot(a_ref[...], w_ref[0:512, :]) + _dot_t(bt, w_ref[512:1024, :])
    _finish_out(o, _group_tile(i, xc_ref, xl_ref), mod_ref, n2_ref, wrh_ref, wrl_ref, xo_ref, h2_ref, lg_ref)


def _odd_out_kernel(atc_ref, atl_ref, z_ref, zp_ref, zn_ref, gb_ref, cw_ref, w_ref, x_ref, mod_ref, n2_ref,
                    wrh_ref, wrl_ref, xo_ref, h2_ref, lg_ref):
    i = pl.program_id(0)
    at = _group_tile(i, atc_ref, atl_ref)
    j = (i - NT_CTX) % NT_SEQ
    first = jnp.logical_or(i < NT_CTX, j == 0)
    last = jnp.logical_or(i < NT_CTX, j == NT_SEQ - 1)
    z = z_ref[...]
    row = lax.broadcasted_iota(jnp.int32, z.shape, 0)
    halo_p = jnp.where(first, 0.0, zp_ref[7:8, :])
    halo_n = jnp.where(last, 0.0, zn_ref[0:1, :])
    zprev = jnp.where(row == 0, halo_p, pltpu.roll(z, 1, 0))
    znext = jnp.where(row == TILE - 1, halo_n, pltpu.roll(z, TILE - 1, 0))
    y = zprev * cw_ref[0:1, :] + z * cw_ref[1:2, :] + znext * cw_ref[2:3, :]
    d = (gb_ref[...] * y).astype(BF16)
    o = _dot_t(at, w_ref[0:512, :]) + _dot(d, w_ref[512:1024, :])
    _finish_out(o, x_ref[...], mod_ref, n2_ref, wrh_ref, wrl_ref, xo_ref, h2_ref, lg_ref)


_OUT_SHAPES = [
    jax.ShapeDtypeStruct((N_TOK, D_MODEL), F32),
    jax.ShapeDtypeStruct((N_TOK, D_MODEL), BF16),
    jax.ShapeDtypeStruct((N_EXPERTS, N_TOK), F32),
]
_LOGIT_SPEC = pl.BlockSpec((N_EXPERTS, TILE), lambda i: (0, i))


def _even_out(layer, a, b, w_out, xc, xl, mod, n2, wrh, wrl):
    full = lambda shape: pl.BlockSpec(shape, lambda i: (0,) * len(shape))
    tile = lambda w: pl.BlockSpec((TILE, w), lambda i: (i, 0))
    return pl.pallas_call(
        _even_out_kernel,
        grid=(NT,),
        in_specs=[tile(512), pl.BlockSpec((512, TILE), _CTX_COL), pl.BlockSpec((512, TILE), _LAT_COL),
                  full(w_out.shape), pl.BlockSpec((TILE, D_MODEL), _CTX_ROW), pl.BlockSpec((TILE, D_MODEL), _LAT_ROW),
                  pl.BlockSpec((1, 1, 6, D_MODEL), lambda i: (layer, _mod_row(i), 0, 0)),
                  full((1, D_MODEL)), full(wrh.shape), full(wrl.shape)],
        out_specs=[tile(D_MODEL), tile(D_MODEL), _LOGIT_SPEC],
        out_shape=_OUT_SHAPES,
        compiler_params=_cparams(("parallel",)),
        name="even_out",
    )(a, b[0], b[1], w_out, xc, xl, mod, n2, wrh, wrl)


def _odd_out(layer, a, z, gb, cw, w_out, x, mod, n2, wrh, wrl):
    full = lambda shape: pl.BlockSpec(shape, lambda i: (0,) * len(shape))
    tile = lambda w: pl.BlockSpec((TILE, w), lambda i: (i, 0))
    rb = TILE // 8
    return pl.pallas_call(
        _odd_out_kernel,
        grid=(NT,),
        in_specs=[pl.BlockSpec((512, TILE), _CTX_COL), pl.BlockSpec((512, TILE), _LAT_COL), tile(512),
                  pl.BlockSpec((8, 512), lambda i: (jnp.maximum(i * rb - 1, 0), 0)),
                  pl.BlockSpec((8, 512), lambda i: (jnp.minimum(i * rb + rb, NT * rb - 1), 0)),
                  tile(512), full(cw.shape), full(w_out.shape), tile(D_MODEL),
                  pl.BlockSpec((1, 1, 6, D_MODEL), lambda i: (layer, _mod_row(i), 0, 0)),
                  full((1, D_MODEL)), full(wrh.shape), full(wrl.shape)],
        out_specs=[tile(D_MODEL), tile(D_MODEL), _LOGIT_SPEC],
        out_shape=_OUT_SHAPES,
        compiler_params=_cparams(("parallel",)),
        name="odd_out",
    )(a[0], a[1], z, z, z, gb, cw, w_out, x, mod, n2, wrh, wrl)


def _select_tokens(aff, n_dom, cap, tri, pos_ref, gate_ref):
    w = aff.shape[1] // n_dom
    doms = [aff[:, d * w:(d + 1) * w] for d in range(n_dom)]
    bits = [jnp.zeros((N_EXPERTS, 1), jnp.int32) for _ in range(n_dom)]
    for bit in range(30, -1, -1):
        for d in range(n_dom):
            cand = bits[d] | (1 << bit)
            cnt = jnp.sum((doms[d] >= lax.bitcast_convert_type(cand, F32)).astype(F32), axis=1, keepdims=True)
            bits[d] = jnp.where(cnt >= cap, cand, bits[d])
    for d in range(n_dom):
        thr = lax.bitcast_convert_type(bits[d], F32)
        gt = doms[d] > thr
        eq = doms[d] == thr
        need = cap - jnp.sum(gt.astype(F32), axis=1, keepdims=True)
        eq_seen = jnp.zeros((N_EXPERTS, 1), F32)
        sel_seen = jnp.zeros((N_EXPERTS, 1), F32) + d * cap
        for blk in range(w // TILE):
            cols = slice(blk * TILE, (blk + 1) * TILE)
            eq_b = eq[:, cols].astype(F32)
            eq_rank = _dot(eq_b.astype(BF16), tri) + eq_seen
            sel = jnp.logical_or(gt[:, cols], jnp.logical_and(eq[:, cols], eq_rank < need))
            sel_f = sel.astype(F32)
            pos = _dot(sel_f.astype(BF16), tri) + sel_seen
            tile = d * (w // TILE) + blk
            pos_ref[0, tile] = jnp.where(sel, pos, -1.0)
            gate_ref[0, tile] = jnp.where(sel, doms[d][:, cols], 0.0)
            eq_seen = eq_seen + jnp.sum(eq_b, axis=1, keepdims=True)
            sel_seen = sel_seen + jnp.sum(sel_f, axis=1, keepdims=True)


def _route_kernel(lg_ref, pos_ref, gate_ref):
    lg = lg_ref[...]
    ex = jnp.exp(lg - jnp.max(lg, axis=0, keepdims=True))
    aff = ex / jnp.sum(ex, axis=0, keepdims=True)
    r = lax.broadcasted_iota(jnp.int32, (TILE, TILE), 0)
    c = lax.broadcasted_iota(jnp.int32, (TILE, TILE), 1)
    tri = (r < c).astype(BF16)

    @pl.when(pl.program_id(0) == 0)
    def _():
        _select_tokens(aff, GROUP // SEQ, EC_CAPACITY * SEQ // N_EXPERTS, tri, pos_ref, gate_ref)

    @pl.when(pl.program_id(0) > 0)
    def _():
        _select_tokens(aff, GROUP // DEC_SEQ, EC_CAPACITY * DEC_SEQ // N_EXPERTS, tri, pos_ref, gate_ref)


def _route(logits_t):
    blk = pl.BlockSpec((1, GROUP_TILES, N_EXPERTS, TILE), lambda g: (g, 0, 0, 0))
    shape = jax.ShapeDtypeStruct((N_GROUPS, GROUP_TILES, N_EXPERTS, TILE), F32)
    return pl.pallas_call(
        _route_kernel,
        grid=(N_GROUPS,),
        in_specs=[pl.BlockSpec((N_EXPERTS, GROUP), lambda g: (0, g))],
        out_specs=[blk, blk],
        out_shape=[shape, shape],
        compiler_params=_cparams(("parallel",)),
        name="route",
    )(logits_t)


def _moe_kernel(nblk_ref, ctab_ref, etab_ref, llen_ref,
                x_ref, pos_ref, gate_ref, wg_ref, wu_ref, wd_ref, xres_ref, mod_ref, fn_ref, *rest, final):
    n_out = 2 if final else 1
    out_refs = rest[:n_out]
    lists_ref, gl_ref, s_ref, gs_ref, acc_ref, wgb_ref, wub_ref, wdb_ref = rest[n_out:]
    g = pl.program_id(0)
    s = pl.program_id(1)
    chunks = STACK_BLK // CHUNK_ROWS
    rows16 = lax.broadcasted_iota(jnp.int32, (CHUNK_ROWS, TILE), 0)

    @pl.when(jnp.logical_and(g == 0, s == 0))
    def _():
        lists_ref[ZERO_ROW:ZERO_ROW + CHUNK_ROWS, :] = jnp.zeros((CHUNK_ROWS, D_MODEL), BF16)

    def chunk_rows(tbase, q, unused_row):
        d = ctab_ref[tbase + q]
        return pl.multiple_of(jnp.where(d >= 0, d, unused_row), CHUNK_ROWS)

    def build_block(j, rb, with_gate, slot):
        tbase = ((g * GROUP_TILES + j) * MAX_STACK_BLKS + rb) * chunks
        for q in range(chunks):
            d = ctab_ref[tbase + q]
            e = etab_ref[tbase + q]
            posrow = pos_ref[0, j, pl.ds(e, 1), :]
            rowid = (rows16 + (rb * STACK_BLK + q * CHUNK_ROWS)).astype(F32)
            hit = jnp.logical_and(posrow == rowid, d >= 0)
            s_ref[slot, q * CHUNK_ROWS:(q + 1) * CHUNK_ROWS, :] = hit.astype(BF16)
            if with_gate:
                gaterow = gate_ref[0, j, pl.ds(e, 1), :]
                gs_ref[slot, q * CHUNK_ROWS:(q + 1) * CHUNK_ROWS, :] = jnp.where(hit, gaterow, 0.0)
        return tbase

    def for_blocks(nb, build, finish):
        def pair(i, _):
            t0 = build(2 * i, 0)
            t1 = build(2 * i + 1, 1)
            finish(t0, 0)
            finish(t1, 1)
            return 0

        lax.fori_loop(0, nb // 2, pair, 0)

        @pl.when(nb % 2 == 1)
        def _():
            finish(build(nb - 1, 0), 0)

    @pl.when(s < MOE_TILE_STEPS)
    def _gather():
        def tile(t, _):
            j = s * MOE_STEP_TILES + t
            tok = pl.multiple_of(t * TILE, TILE)

            def finish(tbase, slot):
                picked = _dot(s_ref[slot], x_ref[pl.ds(tok, TILE), :]).astype(BF16)
                gcol = jnp.sum(gs_ref[slot], axis=1, keepdims=True)
                for q in range(chunks):
                    d = chunk_rows(tbase, q, DUMP_ROW)
                    rows = slice(q * CHUNK_ROWS, (q + 1) * CHUNK_ROWS)
                    lists_ref[pl.ds(d, CHUNK_ROWS), :] = picked[rows, :]
                    gl_ref[pl.ds(d, CHUNK_ROWS), :] = gcol[rows, :]

            for_blocks(nblk_ref[g * GROUP_TILES + j], lambda rb, slot: build_block(j, rb, True, slot), finish)
            return 0

        lax.fori_loop(0, MOE_STEP_TILES, tile, 0)

    @pl.when(jnp.logical_and(s >= MOE_TILE_STEPS, s < MOE_TILE_STEPS + N_EXPERTS))
    def _experts():
        e = s - MOE_TILE_STEPS
        ln = llen_ref[g * N_EXPERTS + e]
        base = e * LIST_MAX
        half = FFN_BLK // 2
        rem = ln % FFN_BLK
        use_half = jnp.logical_and(rem > 0, rem <= half)
        nfull = ln // FFN_BLK + jnp.where(rem > half, 1, 0)
        end = nfull * FFN_BLK + jnp.where(use_half, half, 0)

        def zero_tail(k, _):
            r0 = pl.multiple_of(base + ln + k * CHUNK_ROWS, CHUNK_ROWS)
            lists_ref[pl.ds(r0, CHUNK_ROWS), :] = jnp.zeros((CHUNK_ROWS, D_MODEL), BF16)
            gl_ref[pl.ds(r0, CHUNK_ROWS), :] = jnp.zeros((CHUNK_ROWS, 1), F32)
            return 0

        lax.fori_loop(0, (end - ln) // CHUNK_ROWS, zero_tail, 0)
        wgb_ref[...] = wg_ref[0, 0].astype(BF16)
        wub_ref[...] = wu_ref[0, 0].astype(BF16)
        wdb_ref[...] = wd_ref[0, 0].astype(BF16)

        def ffn_rows(r0, n):
            xs = lists_ref[pl.ds(r0, n), :]
            hid = _silu(_dot(xs, wgb_ref[...])) * _dot(xs, wub_ref[...])
            y = _dot(hid.astype(BF16), wdb_ref[...]) * gl_ref[pl.ds(r0, n), :]
            lists_ref[pl.ds(r0, n), :] = y.astype(BF16)

        def ffn(c, _):
            ffn_rows(pl.multiple_of(base + c * FFN_BLK, FFN_BLK), FFN_BLK)
            return 0

        lax.fori_loop(0, nfull, ffn, 0)

        @pl.when(use_half)
        def _():
            ffn_rows(pl.multiple_of(base + nfull * FFN_BLK, half), half)

    @pl.when(s >= MOE_TILE_STEPS + N_EXPERTS)
    def _scatter():
        def tile(t, _):
            j = (s - MOE_TILE_STEPS - N_EXPERTS) * SCATTER_STEP_TILES + t
            rows = pl.ds(pl.multiple_of(t * TILE, TILE), TILE)
            acc_ref[...] = jnp.zeros_like(acc_ref)

            def finish(tbase, slot):
                y = jnp.concatenate([lists_ref[pl.ds(chunk_rows(tbase, q, ZERO_ROW), CHUNK_ROWS), :]
                                     for q in range(chunks)], axis=0)
                acc_ref[...] += _dot_t(s_ref[slot], y)

            for_blocks(nblk_ref[g * GROUP_TILES + j], lambda rb, slot: build_block(j, rb, False, slot), finish)
            x = xres_ref[rows, :] + mod_ref[0, 0, 5:6, :] * acc_ref[...]
            if final:
                y = x * lax.rsqrt(jnp.mean(x * x, axis=-1, keepdims=True) + EPS) * fn_ref[...]

                @pl.when(g == 0)
                def _():
                    out_refs[0][rows, :] = y

                @pl.when(g > 0)
                def _():
                    out_refs[1][rows, :] = y
            else:
                out_refs[0][rows, :] = x
            return 0

        lax.fori_loop(0, SCATTER_STEP_TILES, tile, 0)


def _moe(layer, final, tables, h2, pos_t, gate_t, wg, wu, wd, xres, mod, fn):
    gt = GROUP_TILES
    ts = MOE_TILE_STEPS
    sc = SCATTER_STEPS
    rows_a = MOE_STEP_TILES * TILE
    rows = SCATTER_STEP_TILES * TILE
    step_c = lambda s: jnp.clip(s - ts - N_EXPERTS, 0, sc - 1)
    tile_a = lambda g, s, *_: (g * ts + jnp.minimum(s, ts - 1), 0)
    tile_c = lambda g, s, *_: (g * sc + step_c(s), 0)
    expert = lambda g, s, *_: (layer, jnp.clip(s - ts, 0, N_EXPERTS - 1), 0, 0)
    group = lambda g, s, *_: (g, 0, 0, 0)
    if final:
        out_specs = [
            pl.BlockSpec((rows, D_MODEL), lambda g, s, *_: (jnp.where(g == 0, step_c(s), sc - 1), 0)),
            pl.BlockSpec((rows, D_MODEL), lambda g, s, *_: (jnp.where(g == 0, 0, (g - 1) * sc + step_c(s)), 0)),
        ]
        out_shape = [jax.ShapeDtypeStruct((NT_CTX * TILE, D_MODEL), F32),
                     jax.ShapeDtypeStruct((NT_LAT * TILE, D_MODEL), F32)]
    else:
        out_specs = pl.BlockSpec((rows, D_MODEL), tile_c)
        out_shape = jax.ShapeDtypeStruct((N_TOK, D_MODEL), F32)
    grid_spec = pltpu.PrefetchScalarGridSpec(
        num_scalar_prefetch=4,
        grid=(N_GROUPS, ts + N_EXPERTS + sc),
        in_specs=[
            pl.BlockSpec((rows_a, D_MODEL), tile_a),
            pl.BlockSpec((1, gt, N_EXPERTS, TILE), group),
            pl.BlockSpec((1, gt, N_EXPERTS, TILE), group),
            pl.BlockSpec((1, 1, D_MODEL, EXPERT_FF), expert),
            pl.BlockSpec((1, 1, D_MODEL, EXPERT_FF), expert),
            pl.BlockSpec((1, 1, EXPERT_FF, D_MODEL), expert),
            pl.BlockSpec((rows, D_MODEL), tile_c),
            pl.BlockSpec((1, 1, 6, D_MODEL), lambda g, s, *_: (layer, g, 0, 0)),
            pl.BlockSpec((1, D_MODEL), lambda g, s, *_: (0, 0)),
        ],
        out_specs=out_specs,
        scratch_shapes=[
            pltpu.VMEM((LIST_ROWS, D_MODEL), BF16),
            pltpu.VMEM((LIST_ROWS, 1), F32),
            pltpu.VMEM((2, STACK_BLK, TILE), BF16),
            pltpu.VMEM((2, STACK_BLK, TILE), F32),
            pltpu.VMEM((TILE, D_MODEL), F32),
            pltpu.VMEM((D_MODEL, EXPERT_FF), BF16),
            pltpu.VMEM((D_MODEL, EXPERT_FF), BF16),
            pltpu.VMEM((EXPERT_FF, D_MODEL), BF16),
        ],
    )
    return pl.pallas_call(
        functools.partial(_moe_kernel, final=final),
        grid_spec=grid_spec,
        out_shape=out_shape,
        compiler_params=_cparams(("arbitrary", "arbitrary")),
        name="moe",
    )(*tables, h2, pos_t, gate_t, wg, wu, wd, xres, mod, fn)


def _rope_tables(rot_dim):
    axis_dim = rot_dim // 2
    tok = jnp.arange(DEC_SEQ)
    rows = (tok // GRID_W).astype(F32)
    cols = (tok % GRID_W).astype(F32)
    inv_freq = ROPE_THETA ** (-jnp.arange(0, axis_dim, 2, dtype=F32) / axis_dim)
    ar = rows[:, None] * inv_freq
    ac = cols[:, None] * inv_freq
    cos = jnp.concatenate([jnp.cos(ar), jnp.cos(ar), jnp.cos(ac), jnp.cos(ac)], axis=1)
    sin = jnp.concatenate([-jnp.sin(ar), jnp.sin(ar), -jnp.sin(ac), jnp.sin(ac)], axis=1)
    rep = LANES // rot_dim
    cos = jnp.tile(cos, (1, rep)).reshape(NT_SEQ, TILE, LANES)
    sin = jnp.tile(sin, (1, rep)).reshape(NT_SEQ, TILE, LANES)
    cos = jnp.concatenate([jnp.ones((1, TILE, LANES), F32), cos], axis=0)
    sin = jnp.concatenate([jnp.zeros((1, TILE, LANES), F32), sin], axis=0)
    return cos, sin


def _excl_cumsum(a, axis):
    return jnp.cumsum(a, axis=axis) - a


def _route_and_moe(layer, final, h2, logits, wg, wu, wd, xres, mod, fn):
    pos_t, gate_t = _route(logits)
    sel = pos_t >= 0
    cnt = jnp.sum(sel, axis=-1, dtype=jnp.int32)
    npad = (cnt + CHUNK_ROWS - 1) // CHUNK_ROWS * CHUNK_ROWS
    seg_off = _excl_cumsum(npad, 2)
    list_off = _excl_cumsum(npad, 1)
    rank0 = _excl_cumsum(cnt, 1)
    stack_pos = jnp.where(sel, pos_t - rank0[..., None].astype(F32) + seg_off[..., None].astype(F32), -1.0)
    nblk = (jnp.sum(npad, axis=2) + STACK_BLK - 1) // STACK_BLK
    llen = jnp.sum(npad, axis=1)
    r = (jnp.arange(MAX_STACK_BLKS * STACK_BLK // CHUNK_ROWS, dtype=jnp.int32) * CHUNK_ROWS)[None, None, None, :]
    in_seg = jnp.logical_and(r >= seg_off[..., None], r < (seg_off + npad)[..., None])
    eidx = jnp.arange(N_EXPERTS, dtype=jnp.int32)[None, None, :, None]
    dst = jnp.sum(jnp.where(in_seg, eidx * LIST_MAX + list_off[..., None] + r - seg_off[..., None], 0), axis=2)
    used = jnp.any(in_seg, axis=2)
    ctab = jnp.where(used, dst, -1).astype(jnp.int32)
    etab = jnp.sum(jnp.where(in_seg, eidx, 0), axis=2).astype(jnp.int32)
    tables = (nblk.reshape(-1).astype(jnp.int32), ctab.reshape(-1), etab.reshape(-1),
              llen.reshape(-1).astype(jnp.int32))
    return _moe(layer, final, tables, h2, stack_pos, gate_t, wg, wu, wd, xres, mod, fn)


def _split_router(w):
    wp = jnp.zeros((D_MODEL, LANES), F32).at[:, :N_EXPERTS].set(w)
    hi = wp.astype(BF16)
    lo = (wp - hi.astype(F32)).astype(BF16)
    return hi, lo


def kernel(x_prompt, x_sample, c, cache_attn_k, cache_attn_v, cache_mla_ckv, cache_mla_krope, c_ctx, w_mod, b_mod,
           norm1, norm2, ev_w_in, ev_q_norm, ev_k_norm, ev_w_s, ev_b_s, ev_w_out, od_w_in, od_q_a_norm,
           od_kv_a_norm, od_w_uq, od_w_ukv, od_conv_w, od_w_out, moe_router, moe_w_gate, moe_w_up, moe_w_down,
           final_norm):
    nctx = NT_CTX * TILE
    xc = x_prompt.reshape(nctx, D_MODEL)
    xl = x_sample.reshape(NT_LAT * TILE, D_MODEL)
    cvec = jnp.concatenate([c_ctx[None], c, jnp.zeros((3, D_MODEL), F32)], axis=0)
    mod = _modulation(cvec, w_mod, b_mod).reshape(2, 8, 6, D_MODEL)

    cos64, sin64 = _rope_tables(HEAD_DIM)
    seg = jnp.arange(512) // HEAD_DIM
    bd = (seg[:, None] == seg[None, :]).astype(BF16)
    bs_tab = jnp.repeat(ev_b_s[0].T, 64, axis=1)
    gated, qt, k, kb, v, vt = _even_in(
        xc, xl, mod, norm1[0:1], ev_w_in[0].astype(BF16), jnp.tile(ev_q_norm[0], 8)[None], jnp.tile(ev_k_norm[0], 2)[None],
        cos64, sin64, ev_w_s[0].astype(BF16), bs_tab, bd)
    new_k = k[:nctx].reshape(BATCH, SEQ, B_KV_HEADS, HEAD_DIM).transpose(0, 2, 1, 3)[:, None]
    new_v = v[:nctx].reshape(BATCH, SEQ, B_KV_HEADS, HEAD_DIM).transpose(0, 2, 1, 3)[:, None]
    ck = cache_attn_k[:, 0].transpose(0, 2, 1, 3).reshape(DEC_BATCH, PAST_LEN, LANES).astype(BF16)
    cv = cache_attn_v[:, 0].transpose(0, 1, 3, 2).reshape(DEC_BATCH, LANES, PAST_LEN).astype(BF16)
    attn_t = _gqa_attention(qt, kb, vt, ck, cv)
    wrh, wrl = _split_router(moe_router[0])
    x1, h2, logits = _even_out(0, gated, attn_t, ev_w_out[0].astype(BF16), xc, xl, mod, norm2[0:1], wrh, wrl)
    x2 = _route_and_moe(0, False, h2, logits, moe_w_gate, moe_w_up, moe_w_down, x1, mod, final_norm[None])

    cos32, sin32 = _rope_tables(C_ROPE)
    w_in1 = od_w_in[0]
    w_in1 = jnp.concatenate([w_in1[:, :C_Q_LORA + C_KV_LORA + C_ROPE], jnp.zeros((D_MODEL, LANES - C_ROPE), F32),
                             w_in1[:, C_Q_LORA + C_KV_LORA + C_ROPE:]], axis=1).astype(BF16)
    wuq = od_w_uq[0].reshape(C_Q_LORA, C_HEADS, C_NOPE + C_ROPE)
    wuq = jnp.concatenate([wuq[:, :, :C_NOPE].reshape(C_Q_LORA, -1), wuq[:, :, C_NOPE:].reshape(C_Q_LORA, -1)],
                          axis=1).astype(BF16)
    wukv = od_w_ukv[0].reshape(C_KV_LORA, C_HEADS, C_NOPE + C_V)
    wukv = jnp.concatenate([wukv[:, :, :C_NOPE].reshape(C_KV_LORA, -1), wukv[:, :, C_NOPE:].reshape(C_KV_LORA, -1)],
                           axis=1).astype(BF16)
    qnt, qrt, ckv, kr, krb, gb, z, kn, vt1 = _odd_in(x2, mod, norm1[1:2], w_in1, od_q_a_norm[0][None],
                                                     od_kv_a_norm[0][None], wuq, wukv, cos32, sin32)
    new_ckv = ckv[:nctx].reshape(BATCH, 1, SEQ, C_KV_LORA)
    new_kr = kr[:nctx, :C_ROPE].reshape(BATCH, 1, SEQ, C_ROPE)
    kn_cache, vt_cache = _kvup(cache_mla_ckv[:, 0].reshape(DEC_BATCH * PAST_LEN, C_KV_LORA), wukv)
    ckr = jnp.pad(cache_mla_krope[:, 0], ((0, 0), (0, 0), (0, LANES - C_ROPE))).astype(BF16)
    attn_t = _mla_attention(qnt, qrt, kn, krb, vt1, kn_cache, ckr, vt_cache)
    wrh, wrl = _split_router(moe_router[1])
    cw = jnp.concatenate([od_conv_w[0], jnp.zeros((5, D_WIDTH), F32)], axis=0)
    x3, h2, logits = _odd_out(1, attn_t, z, gb, cw, od_w_out[0].astype(BF16), x2, mod, norm2[1:2], wrh, wrl)
    y_c, y_l = _route_and_moe(1, True, h2, logits, moe_w_gate, moe_w_up, moe_w_down, x3, mod, final_norm[None])
    y_prompt = y_c.reshape(BATCH, SEQ, D_MODEL)
    y_sample = y_l.reshape(DEC_BATCH, DEC_SEQ, D_MODEL)
    return y_prompt, y_sample, new_k, new_v, new_ckv, new_kr
```

```python
import functools
import math

import jax
import jax.numpy as jnp
from jax import lax
from jax.experimental import pallas as pl
from jax.experimental.pallas import tpu as pltpu

F32 = jnp.float32
BF16 = jnp.bfloat16

D_MODEL = 1024
BATCH = 16
SEQ = 256
DEC_BATCH = 4
DEC_SEQ = 4096
PAST_LEN = 512
GRID_W = 64
ROPE_THETA = 10000.0
EPS = 1e-6
HEAD_DIM = 64
CHUNK = 128
A_GROUPS = 8
A_WIDTH = 512
B_HEADS = 8
B_KV_HEADS = 2
B_WIDTH = 512
C_HEADS = 8
C_NOPE = 64
C_ROPE = 32
C_V = 64
C_Q_LORA = 256
C_KV_LORA = 128
D_WIDTH = 512
N_EXPERTS = 16
EXPERT_FF = 512
EC_CAPACITY = 2

LOG2E = math.log2(math.e)
LANES = 128
BF16_ROWS = 16
ATT_TK = 256
MLA_PAIRS = 2
GQA_STREAMS = 2
TILE = 256
NT_CTX = BATCH * SEQ // TILE
NT_SEQ = DEC_SEQ // TILE
NT_LAT = DEC_BATCH * NT_SEQ
NT = NT_CTX + NT_LAT
N_TOK = NT * TILE
GROUP = 4096
N_GROUPS = N_TOK // GROUP
GROUP_TILES = GROUP // TILE
GROUP_CAP = EC_CAPACITY * GROUP // N_EXPERTS
CHUNK_ROWS = BF16_ROWS
STACK_BLK = 256
MAX_STACK_BLKS = -(-N_EXPERTS * TILE // STACK_BLK)
MOE_STEP_TILES = 2
MOE_TILE_STEPS = GROUP_TILES // MOE_STEP_TILES
SCATTER_STEP_TILES = 1
SCATTER_STEPS = GROUP_TILES // SCATTER_STEP_TILES
BLOCK_SLOTS = 3
FFN_BLK = 256
LIST_MAX = -(-(GROUP_CAP + GROUP_TILES * (CHUNK_ROWS - 1)) // FFN_BLK) * FFN_BLK
DUMP_ROW = N_EXPERTS * LIST_MAX
ZERO_ROW = DUMP_ROW + CHUNK_ROWS
LIST_ROWS = ZERO_ROW + CHUNK_ROWS
VMEM_LIMIT = 58 * 1024 * 1024


def _cparams(sem):
    return pltpu.CompilerParams(dimension_semantics=sem, vmem_limit_bytes=VMEM_LIMIT)


def _dot(a, b):
    return jnp.dot(a, b, preferred_element_type=F32)


def _silu(x):
    return x / (1.0 + jnp.exp(-x))


def _mod_row(i):
    return jnp.where(i < NT_CTX, 0, 1 + (i - NT_CTX) // NT_SEQ)


def _tab_blk(i):
    return jnp.where(i < NT_CTX, 0, 1 + (i - NT_CTX) % NT_SEQ)


def _mod_kernel(c_ref, w_ref, b_ref, o_ref):
    s = _silu(c_ref[...])
    o_ref[0] = _dot(s.astype(BF16), w_ref[0].astype(BF16)) + b_ref[0]


def _modulation(cvec, w_mod, b_mod):
    depth = w_mod.shape[0]
    nchunk = 6
    return pl.pallas_call(
        _mod_kernel,
        grid=(depth, nchunk),
        in_specs=[
            pl.BlockSpec((8, D_MODEL), lambda l, k: (0, 0)),
            pl.BlockSpec((1, D_MODEL, D_MODEL), lambda l, k: (l, 0, k)),
            pl.BlockSpec((1, 1, D_MODEL), lambda l, k: (l, 0, k)),
        ],
        out_specs=pl.BlockSpec((1, 8, D_MODEL), lambda l, k: (l, 0, k)),
        out_shape=jax.ShapeDtypeStruct((depth, 8, 6 * D_MODEL), F32),
        compiler_params=_cparams(("arbitrary", "arbitrary")),
        name="modulation",
    )(cvec, w_mod, b_mod.reshape(depth, 1, 6 * D_MODEL))


def _norm_mod(x, gain, scale, shift):
    ms = jnp.mean(x * x, axis=-1, keepdims=True)
    return (x * lax.rsqrt(ms + EPS) * gain) * (1.0 + scale) + shift


def _seg_mean_sq(z, bd, width):
    zz = z * z
    hi = zz.astype(BF16)
    lo = (zz - hi.astype(F32)).astype(BF16)
    return (_dot(hi, bd) + _dot(lo, bd)) * (1.0 / width)


def _rope(z, cos, sin_signed, half):
    w = z.shape[1]
    lane = lax.broadcasted_iota(jnp.int32, z.shape, 1)
    first = (lane % (2 * half)) < half
    partner = jnp.where(first, pltpu.roll(z, w - half, 1), pltpu.roll(z, half, 1))
    return z * cos + partner * sin_signed


def _tile_lanes(t, n):
    return jnp.concatenate([t] * n, axis=1) if n > 1 else t


def _even_in_kernel(xc_ref, xl_ref, mod_ref, n1_ref, w_ref, qg_ref, kg_ref, cos_ref, sin_ref, ws_ref, bs_ref, bd_ref,
                    gated_ref, qt_ref, k_ref, kb_ref, v_ref, vt_ref):
    x = _group_tile(pl.program_id(0), xc_ref, xl_ref)
    h = _norm_mod(x, n1_ref[...], mod_ref[0, 0, 1:2, :], mod_ref[0, 0, 0:1, :])
    p = _dot(h.astype(BF16), w_ref[...])
    u = p[:, 0:512]
    va = p[:, 512:1024].astype(BF16)
    q = p[:, 1024:1536]
    k = p[:, 1536:1664]
    v = p[:, 1664:1792]
    bd = bd_ref[...]
    cos = cos_ref[0]
    sin = sin_ref[0]
    qn = q * lax.rsqrt(_seg_mean_sq(q, bd, HEAD_DIM) + EPS) * qg_ref[...]
    kn = k * lax.rsqrt(_seg_mean_sq(k, bd[0:LANES, 0:LANES], HEAD_DIM) + EPS) * kg_ref[...]
    qr = _rope(qn, _tile_lanes(cos, 4), _tile_lanes(sin, 4), HEAD_DIM // 4)
    kr = _rope(kn, cos, sin, HEAD_DIM // 4)
    qt_ref[...] = (qr * (HEAD_DIM ** -0.5 * LOG2E)).T.astype(BF16)
    k_ref[...] = kr
    kb_ref[...] = kr.astype(BF16)
    v_ref[...] = v
    vt_ref[...] = v.T.astype(BF16)
    lane = lax.broadcasted_iota(jnp.int32, (CHUNK, LANES), 1)
    for ch in range(TILE // CHUNK):
        rows = slice(ch * CHUNK, (ch + 1) * CHUNK)
        cols = []
        for pair in range(A_GROUPS // 2):
            vp = va[rows, pair * LANES:(pair + 1) * LANES]
            zero = jnp.zeros_like(vp)
            rhs = jnp.concatenate([jnp.where(lane < LANES // 2, vp, zero), jnp.where(lane < LANES // 2, zero, vp)],
                                  axis=0)
            cols.append(_dot(ws_ref[pair], rhs))
        s = jnp.concatenate(cols, axis=1) + bs_ref[...]
        gated_ref[rows, :] = (u[rows, :] * s).astype(BF16)


def _even_in(xc, xl, mod, n1, w_in, qg, kg, cos, sin, ws, bs_tab, bd):
    full = lambda shape: pl.BlockSpec(shape, lambda i: (0,) * len(shape))
    return pl.pallas_call(
        _even_in_kernel,
        grid=(NT,),
        in_specs=[
            pl.BlockSpec((TILE, D_MODEL), _CTX_ROW),
            pl.BlockSpec((TILE, D_MODEL), _LAT_ROW),
            pl.BlockSpec((1, 1, 6, D_MODEL), lambda i: (0, _mod_row(i), 0, 0)),
            full((1, D_MODEL)),
            full(w_in.shape),
            full((1, 512)),
            full((1, LANES)),
            pl.BlockSpec((1, TILE, LANES), lambda i: (_tab_blk(i), 0, 0)),
            pl.BlockSpec((1, TILE, LANES), lambda i: (_tab_blk(i), 0, 0)),
            full(ws.shape),
            full(bs_tab.shape),
            full(bd.shape),
        ],
        out_specs=[
            pl.BlockSpec((TILE, 512), lambda i: (i, 0)),
            pl.BlockSpec((512, TILE), lambda i: (0, i)),
            pl.BlockSpec((TILE, LANES), lambda i: (i, 0)),
            pl.BlockSpec((TILE, LANES), lambda i: (i, 0)),
            pl.BlockSpec((TILE, LANES), lambda i: (i, 0)),
            pl.BlockSpec((LANES, TILE), lambda i: (0, i)),
        ],
        out_shape=[
            jax.ShapeDtypeStruct((N_TOK, 512), BF16),
            jax.ShapeDtypeStruct((512, N_TOK), BF16),
            jax.ShapeDtypeStruct((N_TOK, LANES), F32),
            jax.ShapeDtypeStruct((N_TOK, LANES), BF16),
            jax.ShapeDtypeStruct((N_TOK, LANES), F32),
            jax.ShapeDtypeStruct((LANES, N_TOK), BF16),
        ],
        compiler_params=_cparams(("parallel",)),
        name="even_in",
    )(xc, xl, mod, n1, w_in, qg, kg, cos, sin, ws, bs_tab, bd)


def _odd_in_kernel(x_ref, mod_ref, n1_ref, w_ref, qa_ref, kva_ref, wuq_ref, wukv_ref, cos_ref, sin_ref,
                   qnt_ref, qrt_ref, ckv_ref, kr_ref, krb_ref, gb_ref, z_ref, kn_ref, vt_ref):
    h = _norm_mod(x_ref[...], n1_ref[...], mod_ref[0, 0, 1:2, :], mod_ref[0, 0, 0:1, :])
    p = _dot(h.astype(BF16), w_ref[...])
    cq = p[:, 0:256]
    ckv = p[:, 256:384]
    kr = p[:, 384:512]
    gb_ref[...] = p[:, 512:1024]
    z_ref[...] = p[:, 1024:1536] * p[:, 1536:2048]
    cqn = cq * lax.rsqrt(jnp.mean(cq * cq, axis=-1, keepdims=True) + EPS) * qa_ref[...]
    q = _dot(cqn.astype(BF16), wuq_ref[...])
    scale = (C_NOPE + C_ROPE) ** -0.5 * LOG2E
    cos = cos_ref[0]
    sin = sin_ref[0]
    qnt_ref[...] = (q[:, 0:512] * scale).T.astype(BF16)
    qr = _rope(q[:, 512:768], _tile_lanes(cos, 2), _tile_lanes(sin, 2), C_ROPE // 4)
    qrt_ref[...] = (qr * scale).T.astype(BF16)
    ckvn = ckv * lax.rsqrt(jnp.mean(ckv * ckv, axis=-1, keepdims=True) + EPS) * kva_ref[...]
    ckv_ref[...] = ckvn
    kv = _dot(ckvn.astype(BF16), wukv_ref[...])
    kn_ref[...] = kv[:, 0:512].astype(BF16)
    vt_ref[...] = kv[:, 512:1024].T.astype(BF16)
    krr = _rope(kr, cos, sin, C_ROPE // 4)
    kr_ref[...] = krr
    krb_ref[...] = krr.astype(BF16)


def _odd_in(x, mod, n1, w_in, qa, kva, wuq, wukv, cos, sin):
    full = lambda shape: pl.BlockSpec(shape, lambda i: (0,) * len(shape))
    tile = lambda w: pl.BlockSpec((TILE, w), lambda i: (i, 0))
    return pl.pallas_call(
        _odd_in_kernel,
        grid=(NT,),
        in_specs=[
            tile(D_MODEL),
            pl.BlockSpec((1, 1, 6, D_MODEL), lambda i: (1, _mod_row(i), 0, 0)),
            full((1, D_MODEL)),
            full(w_in.shape),
            full((1, C_Q_LORA)),
            full((1, C_KV_LORA)),
            full(wuq.shape),
            full(wukv.shape),
            pl.BlockSpec((1, TILE, LANES), lambda i: (_tab_blk(i), 0, 0)),
            pl.BlockSpec((1, TILE, LANES), lambda i: (_tab_blk(i), 0, 0)),
        ],
        out_specs=[pl.BlockSpec((512, TILE), lambda i: (0, i)),
                   pl.BlockSpec((256, TILE), lambda i: (0, i)),
                   tile(LANES), tile(LANES), tile(LANES), tile(512), tile(512),
                   tile(512), pl.BlockSpec((512, TILE), lambda i: (0, i))],
        out_shape=[
            jax.ShapeDtypeStruct((512, N_TOK), BF16),
            jax.ShapeDtypeStruct((256, N_TOK), BF16),
            jax.ShapeDtypeStruct((N_TOK, LANES), F32),
            jax.ShapeDtypeStruct((N_TOK, LANES), F32),
            jax.ShapeDtypeStruct((N_TOK, LANES), BF16),
            jax.ShapeDtypeStruct((N_TOK, 512), F32),
            jax.ShapeDtypeStruct((N_TOK, 512), F32),
            jax.ShapeDtypeStruct((N_TOK, 512), BF16),
            jax.ShapeDtypeStruct((512, N_TOK), BF16),
        ],
        compiler_params=_cparams(("parallel",)),
        name="odd_in",
    )(x, mod, n1, w_in, qa, kva, wuq, wukv, cos, sin)


def _kvup_kernel(c_ref, w_ref, kn_ref, vt_ref):
    p = _dot(c_ref[...].astype(BF16), w_ref[...])
    kn_ref[...] = p[:, 0:512].astype(BF16)
    vt_ref[...] = p[:, 512:1024].T.astype(BF16)


def _kvup(ckv_all, w_ukv):
    n = ckv_all.shape[0]
    return pl.pallas_call(
        _kvup_kernel,
        grid=(n // TILE,),
        in_specs=[pl.BlockSpec((TILE, C_KV_LORA), lambda i: (i, 0)),
                  pl.BlockSpec(w_ukv.shape, lambda i: (0, 0))],
        out_specs=[pl.BlockSpec((TILE, 512), lambda i: (i, 0)), pl.BlockSpec((512, TILE), lambda i: (0, i))],
        out_shape=[jax.ShapeDtypeStruct((n, 512), BF16), jax.ShapeDtypeStruct((512, n), BF16)],
        compiler_params=_cparams(("parallel",)),
        name="mla_kv_up",
    )(ckv_all, w_ukv)


def _softmax_pv(qc, key_chunk, vt_chunks, nk):
    return _softmax_pv_streams([(qc, key_chunk, vt_chunks)], nk)[0]


def _softmax_pv_streams(streams, nk):
    ones = jnp.ones((BF16_ROWS, ATT_TK), BF16)
    ms = [jnp.full((1, qc.shape[1]), -jnp.inf, F32) for qc, _, _ in streams]
    accs = [jnp.zeros((HEAD_DIM + BF16_ROWS, qc.shape[1]), F32) for qc, _, _ in streams]
    nxt = [_dot(key_chunk(0), qc) for qc, key_chunk, _ in streams]
    for c in range(nk):
        for i, (qc, key_chunk, vt_chunks) in enumerate(streams):
            s = nxt[i]
            if c + 1 < nk:
                nxt[i] = _dot(key_chunk(c + 1), qc)
            mn = jnp.maximum(ms[i], jnp.max(s, axis=0, keepdims=True))
            p = jnp.exp2(s - mn).astype(BF16)
            vts = vt_chunks(c)
            if len(vts) == 1:
                pv = _dot(jnp.concatenate([vts[0], ones], axis=0), p)
            else:
                tq = qc.shape[1] // len(vts)
                pv = jnp.concatenate([_dot(jnp.concatenate([vt, ones], axis=0), p[:, k * tq:(k + 1) * tq])
                                      for k, vt in enumerate(vts)], axis=1)
            accs[i] = jnp.exp2(ms[i] - mn) * accs[i] + pv
            ms[i] = mn
    return [acc[0:HEAD_DIM, :] / acc[HEAD_DIM:HEAD_DIM + 1, :] for acc in accs]


def _chunk_locator(segments):
    counts = [seg[0].shape[0] // ATT_TK for seg in segments]

    def locate(c):
        for seg, n in zip(segments, counts):
            if c < n:
                return seg, slice(c * ATT_TK, (c + 1) * ATT_TK)
            c -= n
        raise IndexError(c)

    return locate, sum(counts)


def _gqa_kernel(q_ref, *refs):
    o_ref = refs[-1]
    locate, nk = _chunk_locator([refs[i:i + 2] for i in range(0, len(refs) - 1, 2)])
    kvh = pl.program_id(1)
    tq = q_ref.shape[1]
    n = B_HEADS // B_KV_HEADS
    cols = []
    for hh in range(n):
        q = q_ref[hh * HEAD_DIM:(hh + 1) * HEAD_DIM, :]
        z = jnp.zeros_like(q)
        cols.append(jnp.where(kvh == 0, jnp.concatenate([q, z], axis=0), jnp.concatenate([z, q], axis=0)))
    def key_chunk(c):
        (k_ref, _), rows = locate(c)
        return k_ref[rows, :]

    def vt_chunks(c):
        (_, vt_ref), cols_c = locate(c)
        return [vt_ref[:, cols_c]]

    per = n // GQA_STREAMS
    outs = _softmax_pv_streams(
        [(jnp.concatenate(cols[i * per:(i + 1) * per], axis=1), key_chunk, vt_chunks) for i in range(GQA_STREAMS)], nk)
    for hh in range(n):
        o = outs[hh // per][:, (hh % per) * tq:(hh % per + 1) * tq]
        o_ref[hh * HEAD_DIM:(hh + 1) * HEAD_DIM, :] = o.astype(BF16)


def _mla_kernel(qn_ref, qr_ref, *refs):
    o_ref = refs[-1]
    locate, nk = _chunk_locator([refs[i:i + 3] for i in range(0, len(refs) - 1, 3)])
    tq = qn_ref.shape[1]
    z = jnp.zeros((C_NOPE, tq), BF16)
    zr = jnp.zeros((LANES - C_ROPE, tq), BF16)
    streams = []
    for pair in range(MLA_PAIRS):
        cols = []
        for hh in range(2):
            h = 2 * pair + hh
            qn = qn_ref[h * C_NOPE:(h + 1) * C_NOPE, :]
            qr = qr_ref[h * C_ROPE:(h + 1) * C_ROPE, :]
            cols.append(jnp.concatenate(([qn, z] if hh == 0 else [z, qn]) + [qr, zr], axis=0))

        def key_chunk(c, pair=pair):
            (kn_ref, kr_ref, _), rows = locate(c)
            return jnp.concatenate([kn_ref[rows, pair * LANES:(pair + 1) * LANES], kr_ref[rows, :]], axis=1)

        def vt_chunks(c, pair=pair):
            (_, _, vt_ref), cols_c = locate(c)
            return [vt_ref[(2 * pair + hh) * C_V:(2 * pair + hh + 1) * C_V, cols_c] for hh in range(2)]

        streams.append((jnp.concatenate(cols, axis=1), key_chunk, vt_chunks))
    outs = _softmax_pv_streams(streams, nk)
    for pair in range(MLA_PAIRS):
        for hh in range(2):
            h = 2 * pair + hh
            o_ref[h * C_V:(h + 1) * C_V, :] = outs[pair][:, hh * tq:(hh + 1) * tq].astype(BF16)


def _attn_call(body, grid, in_specs, out_spec, args, n_tiles):
    return pl.pallas_call(
        body,
        grid=grid,
        in_specs=in_specs,
        out_specs=out_spec,
        out_shape=jax.ShapeDtypeStruct((512, n_tiles * TILE), BF16),
        compiler_params=_cparams(("parallel",) * len(grid)),
        name="attention",
    )(*args)


def _gqa_attention(qt, kb, vt, k_cache, vt_cache):
    assert NT_CTX * TILE == DEC_SEQ
    rows = (B_HEADS // B_KV_HEADS) * HEAD_DIM
    o_ctx = _attn_call(
        _gqa_kernel, (BATCH, B_KV_HEADS),
        [pl.BlockSpec((rows, TILE), lambda b, h: (h, b)),
         pl.BlockSpec((None, TILE, LANES), lambda b, h: (b, 0, 0)),
         pl.BlockSpec((HEAD_DIM, TILE), lambda b, h: (h, b))],
        pl.BlockSpec((rows, TILE), lambda b, h: (h, b)),
        [qt, kb.reshape(NT, TILE, LANES), vt], NT_CTX)
    o_lat = _attn_call(
        _gqa_kernel, (DEC_BATCH, B_KV_HEADS, NT_SEQ),
        [pl.BlockSpec((rows, TILE), lambda b, h, j: (h, NT_CTX + b * NT_SEQ + j)),
         pl.BlockSpec((None, PAST_LEN, LANES), lambda b, h, j: (b, 0, 0)),
         pl.BlockSpec((None, HEAD_DIM, PAST_LEN), lambda b, h, j: (b, h, 0)),
         pl.BlockSpec((DEC_SEQ, LANES), lambda b, h, j: (1 + b, 0)),
         pl.BlockSpec((HEAD_DIM, DEC_SEQ), lambda b, h, j: (h, 1 + b))],
        pl.BlockSpec((rows, TILE), lambda b, h, j: (h, b * NT_SEQ + j)),
        [qt, k_cache, vt_cache, kb, vt], NT_LAT)
    return o_ctx, o_lat


def _mla_attention(qnt, qrt, kn, krb, vt, kn_cache, kr_cache, vt_cache):
    nh = 2 * MLA_PAIRS
    steps = C_HEADS // nh
    o_ctx = _attn_call(
        _mla_kernel, (BATCH, steps),
        [pl.BlockSpec((nh * C_NOPE, TILE), lambda b, p: (p, b)),
         pl.BlockSpec((nh * C_ROPE, TILE), lambda b, p: (p, b)),
         pl.BlockSpec((TILE, MLA_PAIRS * LANES), lambda b, p: (b, p)),
         pl.BlockSpec((TILE, LANES), lambda b, p: (b, 0)),
         pl.BlockSpec((nh * C_V, TILE), lambda b, p: (p, b))],
        pl.BlockSpec((nh * C_V, TILE), lambda b, p: (p, b)),
        [qnt, qrt, kn, krb, vt], NT_CTX)
    o_lat = _attn_call(
        _mla_kernel, (DEC_BATCH, steps, NT_SEQ),
        [pl.BlockSpec((nh * C_NOPE, TILE), lambda b, p, j: (p, NT_CTX + b * NT_SEQ + j)),
         pl.BlockSpec((nh * C_ROPE, TILE), lambda b, p, j: (p, NT_CTX + b * NT_SEQ + j)),
         pl.BlockSpec((PAST_LEN, MLA_PAIRS * LANES), lambda b, p, j: (b, p)),
         pl.BlockSpec((None, PAST_LEN, LANES), lambda b, p, j: (b, 0, 0)),
         pl.BlockSpec((nh * C_V, PAST_LEN), lambda b, p, j: (p, b)),
         pl.BlockSpec((DEC_SEQ, MLA_PAIRS * LANES), lambda b, p, j: (1 + b, p)),
         pl.BlockSpec((DEC_SEQ, LANES), lambda b, p, j: (1 + b, 0)),
         pl.BlockSpec((nh * C_V, DEC_SEQ), lambda b, p, j: (p, 1 + b))],
        pl.BlockSpec((nh * C_V, TILE), lambda b, p, j: (p, b * NT_SEQ + j)),
        [qnt, qrt, kn_cache, kr_cache, vt_cache, kn, krb, vt], NT_LAT)
    return o_ctx, o_lat


def _group_tile(i, ctx_ref, lat_ref):
    return jnp.where(i < NT_CTX, ctx_ref[...], lat_ref[...])


_CTX_COL = lambda i: (0, jnp.minimum(i, NT_CTX - 1))
_LAT_COL = lambda i: (0, jnp.maximum(i - NT_CTX, 0))
_CTX_ROW = lambda i: (jnp.minimum(i, NT_CTX - 1), 0)
_LAT_ROW = lambda i: (jnp.maximum(i - NT_CTX, 0), 0)


def _finish_out(o, x_in, mod_ref, n2_ref, wr_ref, xo_ref, h2_ref, lg_ref):
    x = x_in + mod_ref[0, 0, 2:3, :] * o
    xo_ref[...] = x
    h2 = _norm_mod(x, n2_ref[...], mod_ref[0, 0, 4:5, :], mod_ref[0, 0, 3:4, :])
    hi = h2.astype(BF16)
    lo = (h2 - hi.astype(F32)).astype(BF16)
    h2_ref[...] = hi
    parts = (_dot(hi, wr_ref[...]) + _dot(lo, wr_ref[...])).T
    lg_ref[...] = parts[0:N_EXPERTS, :] + parts[N_EXPERTS:2 * N_EXPERTS, :]


def _dot_nt(a, bt):
    return lax.dot_general(a, bt, (((1,), (1,)), ((), ())), preferred_element_type=F32)


def _dot_t(at, b):
    return lax.dot_general(at, b, (((0,), (0,)), ((), ())), preferred_element_type=F32)


def _even_out_kernel(a_ref, btc_ref, btl_ref, w_ref, xc_ref, xl_ref, mod_ref, n2_ref, wr_ref,
                     xo_ref, h2_ref, lg_ref):
    i = pl.program_id(0)
    bt = _group_tile(i, btc_ref, btl_ref)
    o = _dot(a_ref[...], w_ref[0:512, :]) + _dot_t(bt, w_ref[512:1024, :])
    _finish_out(o, _group_tile(i, xc_ref, xl_ref), mod_ref, n2_ref, wr_ref, xo_ref, h2_ref, lg_ref)


def _odd_out_kernel(atc_ref, atl_ref, z_ref, zp_ref, zn_ref, gb_ref, cw_ref, w_ref, x_ref, mod_ref, n2_ref,
                    wr_ref, xo_ref, h2_ref, lg_ref):
    i = pl.program_id(0)
    at = _group_tile(i, atc_ref, atl_ref)
    j = (i - NT_CTX) % NT_SEQ
    first = jnp.logical_or(i < NT_CTX, j == 0)
    last = jnp.logical_or(i < NT_CTX, j == NT_SEQ - 1)
    z = z_ref[...]
    row = lax.broadcasted_iota(jnp.int32, z.shape, 0)
    halo_p = jnp.where(first, 0.0, zp_ref[7:8, :])
    halo_n = jnp.where(last, 0.0, zn_ref[0:1, :])
    zprev = jnp.where(row == 0, halo_p, pltpu.roll(z, 1, 0))
    znext = jnp.where(row == TILE - 1, halo_n, pltpu.roll(z, TILE - 1, 0))
    y = zprev * cw_ref[0:1, :] + z * cw_ref[1:2, :] + znext * cw_ref[2:3, :]
    d = (gb_ref[...] * y).astype(BF16)
    o = _dot_t(at, w_ref[0:512, :]) + _dot(d, w_ref[512:1024, :])
    _finish_out(o, x_ref[...], mod_ref, n2_ref, wr_ref, xo_ref, h2_ref, lg_ref)


_OUT_SHAPES = [
    jax.ShapeDtypeStruct((N_TOK, D_MODEL), F32),
    jax.ShapeDtypeStruct((N_TOK, D_MODEL), BF16),
    jax.ShapeDtypeStruct((N_EXPERTS, N_TOK), F32),
]
_LOGIT_SPEC = pl.BlockSpec((N_EXPERTS, TILE), lambda i: (0, i))


def _even_out(layer, a, b, w_out, xc, xl, mod, n2, wr):
    full = lambda shape: pl.BlockSpec(shape, lambda i: (0,) * len(shape))
    tile = lambda w: pl.BlockSpec((TILE, w), lambda i: (i, 0))
    return pl.pallas_call(
        _even_out_kernel,
        grid=(NT,),
        in_specs=[tile(512), pl.BlockSpec((512, TILE), _CTX_COL), pl.BlockSpec((512, TILE), _LAT_COL),
                  full(w_out.shape), pl.BlockSpec((TILE, D_MODEL), _CTX_ROW), pl.BlockSpec((TILE, D_MODEL), _LAT_ROW),
                  pl.BlockSpec((1, 1, 6, D_MODEL), lambda i: (layer, _mod_row(i), 0, 0)),
                  full((1, D_MODEL)), full(wr.shape)],
        out_specs=[tile(D_MODEL), tile(D_MODEL), _LOGIT_SPEC],
        out_shape=_OUT_SHAPES,
        compiler_params=_cparams(("parallel",)),
        name="even_out",
    )(a, b[0], b[1], w_out, xc, xl, mod, n2, wr)


def _odd_out(layer, a, z, gb, cw, w_out, x, mod, n2, wr):
    full = lambda shape: pl.BlockSpec(shape, lambda i: (0,) * len(shape))
    tile = lambda w: pl.BlockSpec((TILE, w), lambda i: (i, 0))
    rb = TILE // 8
    return pl.pallas_call(
        _odd_out_kernel,
        grid=(NT,),
        in_specs=[pl.BlockSpec((512, TILE), _CTX_COL), pl.BlockSpec((512, TILE), _LAT_COL), tile(512),
                  pl.BlockSpec((8, 512), lambda i: (jnp.maximum(i * rb - 1, 0), 0)),
                  pl.BlockSpec((8, 512), lambda i: (jnp.minimum(i * rb + rb, NT * rb - 1), 0)),
                  tile(512), full(cw.shape), full(w_out.shape), tile(D_MODEL),
                  pl.BlockSpec((1, 1, 6, D_MODEL), lambda i: (layer, _mod_row(i), 0, 0)),
                  full((1, D_MODEL)), full(wr.shape)],
        out_specs=[tile(D_MODEL), tile(D_MODEL), _LOGIT_SPEC],
        out_shape=_OUT_SHAPES,
        compiler_params=_cparams(("parallel",)),
        name="odd_out",
    )(a[0], a[1], z, z, z, gb, cw, w_out, x, mod, n2, wr)


def _select_tokens(aff, n_dom, cap, tri, pos_ref, gate_ref):
    w = aff.shape[1] // n_dom
    doms = [aff[:, d * w:(d + 1) * w] for d in range(n_dom)]
    bits = [jnp.zeros((N_EXPERTS, 1), jnp.int32) for _ in range(n_dom)]
    for bit in range(30, -1, -1):
        for d in range(n_dom):
            cand = bits[d] | (1 << bit)
            cnt = jnp.sum((doms[d] >= lax.bitcast_convert_type(cand, F32)).astype(F32), axis=1, keepdims=True)
            bits[d] = jnp.where(cnt >= cap, cand, bits[d])
    for d in range(n_dom):
        thr = lax.bitcast_convert_type(bits[d], F32)
        gt = doms[d] > thr
        eq = doms[d] == thr
        need = cap - jnp.sum(gt.astype(F32), axis=1, keepdims=True)
        eq_seen = jnp.zeros((N_EXPERTS, 1), F32)
        sel_seen = jnp.zeros((N_EXPERTS, 1), F32) + d * cap
        for blk in range(w // TILE):
            cols = slice(blk * TILE, (blk + 1) * TILE)
            eq_b = eq[:, cols].astype(F32)
            eq_rank = _dot(eq_b.astype(BF16), tri) + eq_seen
            sel = jnp.logical_or(gt[:, cols], jnp.logical_and(eq[:, cols], eq_rank < need))
            sel_f = sel.astype(F32)
            pos = _dot(sel_f.astype(BF16), tri) + sel_seen
            tile = d * (w // TILE) + blk
            pos_ref[0, tile] = jnp.where(sel, pos, -1.0)
            gate_ref[0, tile] = jnp.where(sel, doms[d][:, cols], 0.0)
            eq_seen = eq_seen + jnp.sum(eq_b, axis=1, keepdims=True)
            sel_seen = sel_seen + jnp.sum(sel_f, axis=1, keepdims=True)


def _route_kernel(lg_ref, pos_ref, gate_ref):
    lg = lg_ref[...]
    ex = jnp.exp(lg - jnp.max(lg, axis=0, keepdims=True))
    aff = ex / jnp.sum(ex, axis=0, keepdims=True)
    r = lax.broadcasted_iota(jnp.int32, (TILE, TILE), 0)
    c = lax.broadcasted_iota(jnp.int32, (TILE, TILE), 1)
    tri = (r < c).astype(BF16)

    @pl.when(pl.program_id(0) == 0)
    def _():
        _select_tokens(aff, GROUP // SEQ, EC_CAPACITY * SEQ // N_EXPERTS, tri, pos_ref, gate_ref)

    @pl.when(pl.program_id(0) > 0)
    def _():
        _select_tokens(aff, GROUP // DEC_SEQ, EC_CAPACITY * DEC_SEQ // N_EXPERTS, tri, pos_ref, gate_ref)


def _route(logits_t):
    blk = pl.BlockSpec((1, GROUP_TILES, N_EXPERTS, TILE), lambda g: (g, 0, 0, 0))
    shape = jax.ShapeDtypeStruct((N_GROUPS, GROUP_TILES, N_EXPERTS, TILE), F32)
    return pl.pallas_call(
        _route_kernel,
        grid=(N_GROUPS,),
        in_specs=[pl.BlockSpec((N_EXPERTS, GROUP), lambda g: (0, g))],
        out_specs=[blk, blk],
        out_shape=[shape, shape],
        compiler_params=_cparams(("parallel",)),
        name="route",
    )(logits_t)


def _moe_kernel(nblk_ref, ctab_ref, etab_ref, llen_ref,
                x_ref, pos_ref, gate_ref, wg_ref, wu_ref, wd_ref, xres_ref, mod_ref, fn_ref, *rest, final):
    n_out = 2 if final else 1
    out_refs = rest[:n_out]
    lists_ref, gl_ref, s_ref, gs_ref, acc_ref, wgb_ref, wub_ref, wdb_ref = rest[n_out:]
    g = pl.program_id(0)
    s = pl.program_id(1)
    chunks = STACK_BLK // CHUNK_ROWS
    rows16 = lax.broadcasted_iota(jnp.int32, (CHUNK_ROWS, TILE), 0)

    @pl.when(jnp.logical_and(g == 0, s == 0))
    def _():
        lists_ref[ZERO_ROW:ZERO_ROW + CHUNK_ROWS, :] = jnp.zeros((CHUNK_ROWS, D_MODEL), BF16)

    def chunk_rows(tbase, q, unused_row):
        d = ctab_ref[tbase + q]
        return pl.multiple_of(jnp.where(d >= 0, d, unused_row), CHUNK_ROWS)

    def build_block(j, rb, with_gate, slot):
        tbase = ((g * GROUP_TILES + j) * MAX_STACK_BLKS + rb) * chunks
        for q in range(chunks):
            d = ctab_ref[tbase + q]
            e = etab_ref[tbase + q]
            posrow = pos_ref[0, j, pl.ds(e, 1), :]
            rowid = (rows16 + (rb * STACK_BLK + q * CHUNK_ROWS)).astype(F32)
            hit = jnp.logical_and(posrow == rowid, d >= 0)
            s_ref[slot, q * CHUNK_ROWS:(q + 1) * CHUNK_ROWS, :] = hit.astype(BF16)
            if with_gate:
                gaterow = gate_ref[0, j, pl.ds(e, 1), :]
                gs_ref[slot, q * CHUNK_ROWS:(q + 1) * CHUNK_ROWS, :] = jnp.where(hit, gaterow, 0.0)
        return tbase

    def for_blocks(nb, build, finish):
        def run(first, n):
            tbases = [build(first + k, k) for k in range(n)]
            for k in range(n):
                finish(tbases[k], k)

        def group(i, _):
            run(BLOCK_SLOTS * i, BLOCK_SLOTS)
            return 0

        lax.fori_loop(0, nb // BLOCK_SLOTS, group, 0)
        for left in range(1, BLOCK_SLOTS):
            @pl.when(nb % BLOCK_SLOTS == left)
            def _(left=left):
                run(nb - left, left)

    @pl.when(s < MOE_TILE_STEPS)
    def _gather():
        def tile(t, _):
            j = s * MOE_STEP_TILES + t
            tok = pl.multiple_of(t * TILE, TILE)

            def finish(tbase, slot):
                picked = _dot(s_ref[slot], x_ref[pl.ds(tok, TILE), :]).astype(BF16)
                gcol = jnp.sum(gs_ref[slot], axis=1, keepdims=True)
                for q in range(chunks):
                    d = chunk_rows(tbase, q, DUMP_ROW)
                    rows = slice(q * CHUNK_ROWS, (q + 1) * CHUNK_ROWS)
                    lists_ref[pl.ds(d, CHUNK_ROWS), :] = picked[rows, :]
                    gl_ref[pl.ds(d, CHUNK_ROWS), :] = gcol[rows, :]

            for_blocks(nblk_ref[g * GROUP_TILES + j], lambda rb, slot: build_block(j, rb, True, slot), finish)
            return 0

        lax.fori_loop(0, MOE_STEP_TILES, tile, 0)

    @pl.when(jnp.logical_and(s >= MOE_TILE_STEPS, s < MOE_TILE_STEPS + N_EXPERTS))
    def _experts():
        e = s - MOE_TILE_STEPS
        ln = llen_ref[g * N_EXPERTS + e]
        base = e * LIST_MAX
        half = FFN_BLK // 2
        rem = ln % FFN_BLK
        use_half = jnp.logical_and(rem > 0, rem <= half)
        nfull = ln // FFN_BLK + jnp.where(rem > half, 1, 0)
        end = nfull * FFN_BLK + jnp.where(use_half, half, 0)

        def zero_tail(k, _):
            r0 = pl.multiple_of(base + ln + k * CHUNK_ROWS, CHUNK_ROWS)
            lists_ref[pl.ds(r0, CHUNK_ROWS), :] = jnp.zeros((CHUNK_ROWS, D_MODEL), BF16)
            gl_ref[pl.ds(r0, CHUNK_ROWS), :] = jnp.zeros((CHUNK_ROWS, 1), F32)
            return 0

        lax.fori_loop(0, (end - ln) // CHUNK_ROWS, zero_tail, 0)
        wgb_ref[...] = wg_ref[0, 0].astype(BF16)
        wub_ref[...] = wu_ref[0, 0].astype(BF16)
        wdb_ref[...] = wd_ref[0, 0].astype(BF16)

        def ffn_rows(r0, n):
            xs = lists_ref[pl.ds(r0, n), :]
            hid = _silu(_dot(xs, wgb_ref[...])) * _dot(xs, wub_ref[...])
            y = _dot(hid.astype(BF16), wdb_ref[...]) * gl_ref[pl.ds(r0, n), :]
            lists_ref[pl.ds(r0, n), :] = y.astype(BF16)

        def ffn(c, _):
            ffn_rows(pl.multiple_of(base + c * FFN_BLK, FFN_BLK), FFN_BLK)
            return 0

        lax.fori_loop(0, nfull, ffn, 0)

        @pl.when(use_half)
        def _():
            ffn_rows(pl.multiple_of(base + nfull * FFN_BLK, half), half)

    @pl.when(s >= MOE_TILE_STEPS + N_EXPERTS)
    def _scatter():
        def tile(t, _):
            j = (s - MOE_TILE_STEPS - N_EXPERTS) * SCATTER_STEP_TILES + t
            rows = pl.ds(pl.multiple_of(t * TILE, TILE), TILE)
            acc_ref[...] = jnp.zeros_like(acc_ref)

            def finish(tbase, slot):
                y = jnp.concatenate([lists_ref[pl.ds(chunk_rows(tbase, q, ZERO_ROW), CHUNK_ROWS), :]
                                     for q in range(chunks)], axis=0)
                acc_ref[...] += _dot_t(s_ref[slot], y)

            for_blocks(nblk_ref[g * GROUP_TILES + j], lambda rb, slot: build_block(j, rb, False, slot), finish)
            x = xres_ref[rows, :] + mod_ref[0, 0, 5:6, :] * acc_ref[...]
            if final:
                y = x * lax.rsqrt(jnp.mean(x * x, axis=-1, keepdims=True) + EPS) * fn_ref[...]

                @pl.when(g == 0)
                def _():
                    out_refs[0][rows, :] = y

                @pl.when(g > 0)
                def _():
                    out_refs[1][rows, :] = y
            else:
                out_refs[0][rows, :] = x
            return 0

        lax.fori_loop(0, SCATTER_STEP_TILES, tile, 0)


def _moe(layer, final, tables, h2, pos_t, gate_t, wg, wu, wd, xres, mod, fn):
    gt = GROUP_TILES
    ts = MOE_TILE_STEPS
    sc = SCATTER_STEPS
    rows_a = MOE_STEP_TILES * TILE
    rows = SCATTER_STEP_TILES * TILE
    step_c = lambda s: jnp.clip(s - ts - N_EXPERTS, 0, sc - 1)
    tile_a = lambda g, s, *_: (g * ts + jnp.minimum(s, ts - 1), 0)
    tile_c = lambda g, s, *_: (g * sc + step_c(s), 0)
    expert = lambda g, s, *_: (layer, jnp.clip(s - ts, 0, N_EXPERTS - 1), 0, 0)
    group = lambda g, s, *_: (g, 0, 0, 0)
    if final:
        out_specs = [
            pl.BlockSpec((rows, D_MODEL), lambda g, s, *_: (jnp.where(g == 0, step_c(s), sc - 1), 0)),
            pl.BlockSpec((rows, D_MODEL), lambda g, s, *_: (jnp.where(g == 0, 0, (g - 1) * sc + step_c(s)), 0)),
        ]
        out_shape = [jax.ShapeDtypeStruct((NT_CTX * TILE, D_MODEL), F32),
                     jax.ShapeDtypeStruct((NT_LAT * TILE, D_MODEL), F32)]
    else:
        out_specs = pl.BlockSpec((rows, D_MODEL), tile_c)
        out_shape = jax.ShapeDtypeStruct((N_TOK, D_MODEL), F32)
    grid_spec = pltpu.PrefetchScalarGridSpec(
        num_scalar_prefetch=4,
        grid=(N_GROUPS, ts + N_EXPERTS + sc),
        in_specs=[
            pl.BlockSpec((rows_a, D_MODEL), tile_a),
            pl.BlockSpec((1, gt, N_EXPERTS, TILE), group),
            pl.BlockSpec((1, gt, N_EXPERTS, TILE), group),
            pl.BlockSpec((1, 1, D_MODEL, EXPERT_FF), expert),
            pl.BlockSpec((1, 1, D_MODEL, EXPERT_FF), expert),
            pl.BlockSpec((1, 1, EXPERT_FF, D_MODEL), expert),
            pl.BlockSpec((rows, D_MODEL), tile_c),
            pl.BlockSpec((1, 1, 6, D_MODEL), lambda g, s, *_: (layer, g, 0, 0)),
            pl.BlockSpec((1, D_MODEL), lambda g, s, *_: (0, 0)),
        ],
        out_specs=out_specs,
        scratch_shapes=[
            pltpu.VMEM((LIST_ROWS, D_MODEL), BF16),
            pltpu.VMEM((LIST_ROWS, 1), F32),
            pltpu.VMEM((BLOCK_SLOTS, STACK_BLK, TILE), BF16),
            pltpu.VMEM((BLOCK_SLOTS, STACK_BLK, TILE), F32),
            pltpu.VMEM((TILE, D_MODEL), F32),
            pltpu.VMEM((D_MODEL, EXPERT_FF), BF16),
            pltpu.VMEM((D_MODEL, EXPERT_FF), BF16),
            pltpu.VMEM((EXPERT_FF, D_MODEL), BF16),
        ],
    )
    return pl.pallas_call(
        functools.partial(_moe_kernel, final=final),
        grid_spec=grid_spec,
        out_shape=out_shape,
        compiler_params=_cparams(("arbitrary", "arbitrary")),
        name="moe",
    )(*tables, h2, pos_t, gate_t, wg, wu, wd, xres, mod, fn)


def _rope_tables(rot_dim):
    axis_dim = rot_dim // 2
    tok = jnp.arange(DEC_SEQ)
    rows = (tok // GRID_W).astype(F32)
    cols = (tok % GRID_W).astype(F32)
    inv_freq = ROPE_THETA ** (-jnp.arange(0, axis_dim, 2, dtype=F32) / axis_dim)
    ar = rows[:, None] * inv_freq
    ac = cols[:, None] * inv_freq
    cos = jnp.concatenate([jnp.cos(ar), jnp.cos(ar), jnp.cos(ac), jnp.cos(ac)], axis=1)
    sin = jnp.concatenate([-jnp.sin(ar), jnp.sin(ar), -jnp.sin(ac), jnp.sin(ac)], axis=1)
    rep = LANES // rot_dim
    cos = jnp.tile(cos, (1, rep)).reshape(NT_SEQ, TILE, LANES)
    sin = jnp.tile(sin, (1, rep)).reshape(NT_SEQ, TILE, LANES)
    cos = jnp.concatenate([jnp.ones((1, TILE, LANES), F32), cos], axis=0)
    sin = jnp.concatenate([jnp.zeros((1, TILE, LANES), F32), sin], axis=0)
    return cos, sin


def _excl_cumsum(a, axis):
    return jnp.cumsum(a, axis=axis) - a


def _route_and_moe(layer, final, h2, logits, wg, wu, wd, xres, mod, fn):
    pos_t, gate_t = _route(logits)
    sel = pos_t >= 0
    cnt = jnp.sum(sel, axis=-1, dtype=jnp.int32)
    npad = (cnt + CHUNK_ROWS - 1) // CHUNK_ROWS * CHUNK_ROWS
    seg_off = _excl_cumsum(npad, 2)
    list_off = _excl_cumsum(npad, 1)
    rank0 = _excl_cumsum(cnt, 1)
    stack_pos = jnp.where(sel, pos_t - rank0[..., None].astype(F32) + seg_off[..., None].astype(F32), -1.0)
    nblk = (jnp.sum(npad, axis=2) + STACK_BLK - 1) // STACK_BLK
    llen = jnp.sum(npad, axis=1)
    r = (jnp.arange(MAX_STACK_BLKS * STACK_BLK // CHUNK_ROWS, dtype=jnp.int32) * CHUNK_ROWS)[None, None, None, :]
    in_seg = jnp.logical_and(r >= seg_off[..., None], r < (seg_off + npad)[..., None])
    eidx = jnp.arange(N_EXPERTS, dtype=jnp.int32)[None, None, :, None]
    dst = jnp.sum(jnp.where(in_seg, eidx * LIST_MAX + list_off[..., None] + r - seg_off[..., None], 0), axis=2)
    used = jnp.any(in_seg, axis=2)
    ctab = jnp.where(used, dst, -1).astype(jnp.int32)
    etab = jnp.sum(jnp.where(in_seg, eidx, 0), axis=2).astype(jnp.int32)
    tables = (nblk.reshape(-1).astype(jnp.int32), ctab.reshape(-1), etab.reshape(-1),
              llen.reshape(-1).astype(jnp.int32))
    return _moe(layer, final, tables, h2, stack_pos, gate_t, wg, wu, wd, xres, mod, fn)


def _split_router(w):
    hi = w.astype(BF16)
    lo = (w - hi.astype(F32)).astype(BF16)
    pad = jnp.zeros((D_MODEL, LANES - 2 * N_EXPERTS), BF16)
    return jnp.concatenate([hi, lo, pad], axis=1)


def kernel(x_prompt, x_sample, c, cache_attn_k, cache_attn_v, cache_mla_ckv, cache_mla_krope, c_ctx, w_mod, b_mod,
           norm1, norm2, ev_w_in, ev_q_norm, ev_k_norm, ev_w_s, ev_b_s, ev_w_out, od_w_in, od_q_a_norm,
           od_kv_a_norm, od_w_uq, od_w_ukv, od_conv_w, od_w_out, moe_router, moe_w_gate, moe_w_up, moe_w_down,
           final_norm):
    nctx = NT_CTX * TILE
    xc = x_prompt.reshape(nctx, D_MODEL)
    xl = x_sample.reshape(NT_LAT * TILE, D_MODEL)
    cvec = jnp.concatenate([c_ctx[None], c, jnp.zeros((3, D_MODEL), F32)], axis=0)
    mod = _modulation(cvec, w_mod, b_mod).reshape(2, 8, 6, D_MODEL)

    cos64, sin64 = _rope_tables(HEAD_DIM)
    seg = jnp.arange(512) // HEAD_DIM
    bd = (seg[:, None] == seg[None, :]).astype(BF16)
    bs_tab = jnp.repeat(ev_b_s[0].T, 64, axis=1)
    gated, qt, k, kb, v, vt = _even_in(
        xc, xl, mod, norm1[0:1], ev_w_in[0].astype(BF16), jnp.tile(ev_q_norm[0], 8)[None], jnp.tile(ev_k_norm[0], 2)[None],
        cos64, sin64, jnp.concatenate([ev_w_s[0, 0::2], ev_w_s[0, 1::2]], axis=2).astype(BF16), bs_tab, bd)
    new_k = k[:nctx].reshape(BATCH, SEQ, B_KV_HEADS, HEAD_DIM).transpose(0, 2, 1, 3)[:, None]
    new_v = v[:nctx].reshape(BATCH, SEQ, B_KV_HEADS, HEAD_DIM).transpose(0, 2, 1, 3)[:, None]
    ck = cache_attn_k[:, 0].transpose(0, 2, 1, 3).reshape(DEC_BATCH, PAST_LEN, LANES).astype(BF16)
    cv = cache_attn_v[:, 0].transpose(0, 1, 3, 2).reshape(DEC_BATCH, LANES, PAST_LEN).astype(BF16)
    attn_t = _gqa_attention(qt, kb, vt, ck, cv)
    x1, h2, logits = _even_out(0, gated, attn_t, ev_w_out[0].astype(BF16), xc, xl, mod, norm2[0:1],
                               _split_router(moe_router[0]))
    x2 = _route_and_moe(0, False, h2, logits, moe_w_gate, moe_w_up, moe_w_down, x1, mod, final_norm[None])

    cos32, sin32 = _rope_tables(C_ROPE)
    w_in1 = od_w_in[0]
    w_in1 = jnp.concatenate([w_in1[:, :C_Q_LORA + C_KV_LORA + C_ROPE], jnp.zeros((D_MODEL, LANES - C_ROPE), F32),
                             w_in1[:, C_Q_LORA + C_KV_LORA + C_ROPE:]], axis=1).astype(BF16)
    wuq = od_w_uq[0].reshape(C_Q_LORA, C_HEADS, C_NOPE + C_ROPE)
    wuq = jnp.concatenate([wuq[:, :, :C_NOPE].reshape(C_Q_LORA, -1), wuq[:, :, C_NOPE:].reshape(C_Q_LORA, -1)],
                          axis=1).astype(BF16)
    wukv = od_w_ukv[0].reshape(C_KV_LORA, C_HEADS, C_NOPE + C_V)
    wukv = jnp.concatenate([wukv[:, :, :C_NOPE].reshape(C_KV_LORA, -1), wukv[:, :, C_NOPE:].reshape(C_KV_LORA, -1)],
                           axis=1).astype(BF16)
    qnt, qrt, ckv, kr, krb, gb, z, kn, vt1 = _odd_in(x2, mod, norm1[1:2], w_in1, od_q_a_norm[0][None],
                                                     od_kv_a_norm[0][None], wuq, wukv, cos32, sin32)
    new_ckv = ckv[:nctx].reshape(BATCH, 1, SEQ, C_KV_LORA)
    new_kr = kr[:nctx, :C_ROPE].reshape(BATCH, 1, SEQ, C_ROPE)
    kn_cache, vt_cache = _kvup(cache_mla_ckv[:, 0].reshape(DEC_BATCH * PAST_LEN, C_KV_LORA), wukv)
    ckr = jnp.pad(cache_mla_krope[:, 0], ((0, 0), (0, 0), (0, LANES - C_ROPE))).astype(BF16)
    attn_t = _mla_attention(qnt, qrt, kn, krb, vt1, kn_cache, ckr, vt_cache)
    cw = jnp.concatenate([od_conv_w[0], jnp.zeros((5, D_WIDTH), F32)], axis=0)
    x3, h2, logits = _odd_out(1, attn_t, z, gb, cw, od_w_out[0].astype(BF16), x2, mod, norm2[1:2],
                              _split_router(moe_router[1]))
    y_c, y_l = _route_and_moe(1, True, h2, logits, moe_w_gate, moe_w_up, moe_w_down, x3, mod, final_norm[None])
    y_prompt = y_c.reshape(BATCH, SEQ, D_MODEL)
    y_sample = y_l.reshape(DEC_BATCH, DEC_SEQ, D_MODEL)
    return y_prompt, y_sample, new_k, new_v, new_ckv, new_kr
```

```python
import functools
import math

import jax
import jax.numpy as jnp
from jax import lax
from jax.experimental import pallas as pl
from jax.experimental.pallas import tpu as pltpu

F32 = jnp.float32
BF16 = jnp.bfloat16

D_MODEL = 1024
BATCH = 16
SEQ = 256
DEC_BATCH = 4
DEC_SEQ = 4096
PAST_LEN = 512
GRID_W = 64
ROPE_THETA = 10000.0
EPS = 1e-6
HEAD_DIM = 64
CHUNK = 128
A_GROUPS = 8
A_WIDTH = 512
B_HEADS = 8
B_KV_HEADS = 2
B_WIDTH = 512
C_HEADS = 8
C_NOPE = 64
C_ROPE = 32
C_V = 64
C_Q_LORA = 256
C_KV_LORA = 128
D_WIDTH = 512
N_EXPERTS = 16
EXPERT_FF = 512
EC_CAPACITY = 2

LOG2E = math.log2(math.e)
LANES = 128
BF16_ROWS = 16
ATT_TK = 256
MLA_PAIRS = 2
GQA_STREAMS = 2
TILE = 256
NT_CTX = BATCH * SEQ // TILE
NT_SEQ = DEC_SEQ // TILE
NT_LAT = DEC_BATCH * NT_SEQ
NT = NT_CTX + NT_LAT
N_TOK = NT * TILE
IO_TILE = 512
NT_CTX_IO = BATCH * SEQ // IO_TILE
NT_SEQ_IO = DEC_SEQ // IO_TILE
NT_IO = N_TOK // IO_TILE
GROUP = 4096
N_GROUPS = N_TOK // GROUP
GROUP_TILES = GROUP // TILE
GROUP_CAP = EC_CAPACITY * GROUP // N_EXPERTS
CHUNK_ROWS = BF16_ROWS
STACK_BLK = 256
MAX_STACK_BLKS = -(-N_EXPERTS * TILE // STACK_BLK)
MOE_STEP_TILES = 2
MOE_TILE_STEPS = GROUP_TILES // MOE_STEP_TILES
SCATTER_STEP_TILES = 1
SCATTER_STEPS = GROUP_TILES // SCATTER_STEP_TILES
BLOCK_SLOTS = 3
FFN_BLK = 256
LIST_MAX = -(-(GROUP_CAP + GROUP_TILES * (CHUNK_ROWS - 1)) // FFN_BLK) * FFN_BLK
DUMP_ROW = N_EXPERTS * LIST_MAX
ZERO_ROW = DUMP_ROW + CHUNK_ROWS
LIST_ROWS = ZERO_ROW + CHUNK_ROWS
VMEM_LIMIT = 58 * 1024 * 1024


def _cparams(sem):
    return pltpu.CompilerParams(dimension_semantics=sem, vmem_limit_bytes=VMEM_LIMIT)


def _dot(a, b):
    return jnp.dot(a, b, preferred_element_type=F32)


def _silu(x):
    return x / (1.0 + jnp.exp(-x))


def _mod_row(i):
    return jnp.where(i < NT_CTX_IO, 0, 1 + (i - NT_CTX_IO) // NT_SEQ_IO)


def _tab_blk(i):
    return jnp.where(i < NT_CTX_IO, 0, 1 + (i - NT_CTX_IO) % NT_SEQ_IO)


def _mod_kernel(c_ref, w_ref, b_ref, o_ref):
    s = _silu(c_ref[...])
    o_ref[0] = _dot(s.astype(BF16), w_ref[0].astype(BF16)) + b_ref[0]


def _modulation(cvec, w_mod, b_mod):
    depth = w_mod.shape[0]
    nchunk = 6
    return pl.pallas_call(
        _mod_kernel,
        grid=(depth, nchunk),
        in_specs=[
            pl.BlockSpec((8, D_MODEL), lambda l, k: (0, 0)),
            pl.BlockSpec((1, D_MODEL, D_MODEL), lambda l, k: (l, 0, k)),
            pl.BlockSpec((1, 1, D_MODEL), lambda l, k: (l, 0, k)),
        ],
        out_specs=pl.BlockSpec((1, 8, D_MODEL), lambda l, k: (l, 0, k)),
        out_shape=jax.ShapeDtypeStruct((depth, 8, 6 * D_MODEL), F32),
        compiler_params=_cparams(("arbitrary", "arbitrary")),
        name="modulation",
    )(cvec, w_mod, b_mod.reshape(depth, 1, 6 * D_MODEL))


def _norm_mod(x, gain, scale, shift):
    ms = jnp.mean(x * x, axis=-1, keepdims=True)
    return (x * lax.rsqrt(ms + EPS) * gain) * (1.0 + scale) + shift


def _seg_mean_sq(z, bd, width):
    zz = z * z
    hi = zz.astype(BF16)
    lo = (zz - hi.astype(F32)).astype(BF16)
    return (_dot(hi, bd) + _dot(lo, bd)) * (1.0 / width)


def _rope(z, cos, sin_signed, half):
    w = z.shape[1]
    lane = lax.broadcasted_iota(jnp.int32, z.shape, 1)
    first = (lane % (2 * half)) < half
    partner = jnp.where(first, pltpu.roll(z, w - half, 1), pltpu.roll(z, half, 1))
    return z * cos + partner * sin_signed


def _tile_lanes(t, n):
    return jnp.concatenate([t] * n, axis=1) if n > 1 else t


def _even_in_kernel(xc_ref, xl_ref, mod_ref, n1_ref, w_ref, qg_ref, kg_ref, cos_ref, sin_ref, ws_ref, bs_ref, bd_ref,
                    gated_ref, qt_ref, k_ref, kb_ref, v_ref, vt_ref):
    x = _group_tile(pl.program_id(0), xc_ref, xl_ref)
    h = _norm_mod(x, n1_ref[...], mod_ref[0, 0, 1:2, :], mod_ref[0, 0, 0:1, :])
    p = _dot(h.astype(BF16), w_ref[...])
    u = p[:, 0:512]
    va = p[:, 512:1024].astype(BF16)
    q = p[:, 1024:1536]
    k = p[:, 1536:1664]
    v = p[:, 1664:1792]
    bd = bd_ref[...]
    cos = cos_ref[0]
    sin = sin_ref[0]
    qn = q * lax.rsqrt(_seg_mean_sq(q, bd, HEAD_DIM) + EPS) * qg_ref[...]
    kn = k * lax.rsqrt(_seg_mean_sq(k, bd[0:LANES, 0:LANES], HEAD_DIM) + EPS) * kg_ref[...]
    qr = _rope(qn, _tile_lanes(cos, 4), _tile_lanes(sin, 4), HEAD_DIM // 4)
    kr = _rope(kn, cos, sin, HEAD_DIM // 4)
    qt_ref[...] = (qr * (HEAD_DIM ** -0.5 * LOG2E)).T.astype(BF16)
    k_ref[...] = kr
    kb_ref[...] = kr.astype(BF16)
    v_ref[...] = v
    vt_ref[...] = v.T.astype(BF16)
    lane = lax.broadcasted_iota(jnp.int32, (CHUNK, LANES), 1)
    for ch in range(IO_TILE // CHUNK):
        rows = slice(ch * CHUNK, (ch + 1) * CHUNK)
        cols = []
        for pair in range(A_GROUPS // 2):
            vp = va[rows, pair * LANES:(pair + 1) * LANES]
            zero = jnp.zeros_like(vp)
            rhs = jnp.concatenate([jnp.where(lane < LANES // 2, vp, zero), jnp.where(lane < LANES // 2, zero, vp)],
                                  axis=0)
            cols.append(_dot(ws_ref[pair], rhs))
        s = jnp.concatenate(cols, axis=1) + bs_ref[...]
        gated_ref[rows, :] = (u[rows, :] * s).astype(BF16)


def _even_in(xc, xl, mod, n1, w_in, qg, kg, cos, sin, ws, bs_tab, bd):
    full = lambda shape: pl.BlockSpec(shape, lambda i: (0,) * len(shape))
    return pl.pallas_call(
        _even_in_kernel,
        grid=(NT_IO,),
        in_specs=[
            pl.BlockSpec((IO_TILE, D_MODEL), _CTX_ROW),
            pl.BlockSpec((IO_TILE, D_MODEL), _LAT_ROW),
            pl.BlockSpec((1, 1, 6, D_MODEL), lambda i: (0, _mod_row(i), 0, 0)),
            full((1, D_MODEL)),
            full(w_in.shape),
            full((1, 512)),
            full((1, LANES)),
            pl.BlockSpec((1, IO_TILE, LANES), lambda i: (_tab_blk(i), 0, 0)),
            pl.BlockSpec((1, IO_TILE, LANES), lambda i: (_tab_blk(i), 0, 0)),
            full(ws.shape),
            full(bs_tab.shape),
            full(bd.shape),
        ],
        out_specs=[
            pl.BlockSpec((IO_TILE, 512), lambda i: (i, 0)),
            pl.BlockSpec((512, IO_TILE), lambda i: (0, i)),
            pl.BlockSpec((IO_TILE, LANES), lambda i: (i, 0)),
            pl.BlockSpec((IO_TILE, LANES), lambda i: (i, 0)),
            pl.BlockSpec((IO_TILE, LANES), lambda i: (i, 0)),
            pl.BlockSpec((LANES, IO_TILE), lambda i: (0, i)),
        ],
        out_shape=[
            jax.ShapeDtypeStruct((N_TOK, 512), BF16),
            jax.ShapeDtypeStruct((512, N_TOK), BF16),
            jax.ShapeDtypeStruct((N_TOK, LANES), F32),
            jax.ShapeDtypeStruct((N_TOK, LANES), BF16),
            jax.ShapeDtypeStruct((N_TOK, LANES), F32),
            jax.ShapeDtypeStruct((LANES, N_TOK), BF16),
        ],
        compiler_params=_cparams(("parallel",)),
        name="even_in",
    )(xc, xl, mod, n1, w_in, qg, kg, cos, sin, ws, bs_tab, bd)


def _odd_in_kernel(x_ref, mod_ref, n1_ref, w_ref, qa_ref, kva_ref, wuq_ref, wukv_ref, cos_ref, sin_ref,
                   qnt_ref, qrt_ref, ckv_ref, kr_ref, krb_ref, gb_ref, z_ref, kn_ref, vt_ref):
    h = _norm_mod(x_ref[...], n1_ref[...], mod_ref[0, 0, 1:2, :], mod_ref[0, 0, 0:1, :])
    p = _dot(h.astype(BF16), w_ref[...])
    cq = p[:, 0:256]
    ckv = p[:, 256:384]
    kr = p[:, 384:512]
    gb_ref[...] = p[:, 512:1024]
    z_ref[...] = p[:, 1024:1536] * p[:, 1536:2048]
    cqn = cq * lax.rsqrt(jnp.mean(cq * cq, axis=-1, keepdims=True) + EPS) * qa_ref[...]
    q = _dot(cqn.astype(BF16), wuq_ref[...])
    scale = (C_NOPE + C_ROPE) ** -0.5 * LOG2E
    cos = cos_ref[0]
    sin = sin_ref[0]
    qnt_ref[...] = (q[:, 0:512] * scale).T.astype(BF16)
    qr = _rope(q[:, 512:768], _tile_lanes(cos, 2), _tile_lanes(sin, 2), C_ROPE // 4)
    qrt_ref[...] = (qr * scale).T.astype(BF16)
    ckvn = ckv * lax.rsqrt(jnp.mean(ckv * ckv, axis=-1, keepdims=True) + EPS) * kva_ref[...]
    ckv_ref[...] = ckvn
    kv = _dot(ckvn.astype(BF16), wukv_ref[...])
    kn_ref[...] = kv[:, 0:512].astype(BF16)
    vt_ref[...] = kv[:, 512:1024].T.astype(BF16)
    krr = _rope(kr, cos, sin, C_ROPE // 4)
    kr_ref[...] = krr
    krb_ref[...] = krr.astype(BF16)


def _odd_in(x, mod, n1, w_in, qa, kva, wuq, wukv, cos, sin):
    full = lambda shape: pl.BlockSpec(shape, lambda i: (0,) * len(shape))
    tile = lambda w: pl.BlockSpec((IO_TILE, w), lambda i: (i, 0))
    return pl.pallas_call(
        _odd_in_kernel,
        grid=(NT_IO,),
        in_specs=[
            tile(D_MODEL),
            pl.BlockSpec((1, 1, 6, D_MODEL), lambda i: (1, _mod_row(i), 0, 0)),
            full((1, D_MODEL)),
            full(w_in.shape),
            full((1, C_Q_LORA)),
            full((1, C_KV_LORA)),
            full(wuq.shape),
            full(wukv.shape),
            pl.BlockSpec((1, IO_TILE, LANES), lambda i: (_tab_blk(i), 0, 0)),
            pl.BlockSpec((1, IO_TILE, LANES), lambda i: (_tab_blk(i), 0, 0)),
        ],
        out_specs=[pl.BlockSpec((512, IO_TILE), lambda i: (0, i)),
                   pl.BlockSpec((256, IO_TILE), lambda i: (0, i)),
                   tile(LANES), tile(LANES), tile(LANES), tile(512), tile(512),
                   tile(512), pl.BlockSpec((512, IO_TILE), lambda i: (0, i))],
        out_shape=[
            jax.ShapeDtypeStruct((512, N_TOK), BF16),
            jax.ShapeDtypeStruct((256, N_TOK), BF16),
            jax.ShapeDtypeStruct((N_TOK, LANES), F32),
            jax.ShapeDtypeStruct((N_TOK, LANES), F32),
            jax.ShapeDtypeStruct((N_TOK, LANES), BF16),
            jax.ShapeDtypeStruct((N_TOK, 512), F32),
            jax.ShapeDtypeStruct((N_TOK, 512), F32),
            jax.ShapeDtypeStruct((N_TOK, 512), BF16),
            jax.ShapeDtypeStruct((512, N_TOK), BF16),
        ],
        compiler_params=_cparams(("parallel",)),
        name="odd_in",
    )(x, mod, n1, w_in, qa, kva, wuq, wukv, cos, sin)


def _kvup_kernel(c_ref, w_ref, kn_ref, vt_ref):
    p = _dot(c_ref[...].astype(BF16), w_ref[...])
    kn_ref[...] = p[:, 0:512].astype(BF16)
    vt_ref[...] = p[:, 512:1024].T.astype(BF16)


def _kvup(ckv_all, w_ukv):
    n = ckv_all.shape[0]
    return pl.pallas_call(
        _kvup_kernel,
        grid=(n // TILE,),
        in_specs=[pl.BlockSpec((TILE, C_KV_LORA), lambda i: (i, 0)),
                  pl.BlockSpec(w_ukv.shape, lambda i: (0, 0))],
        out_specs=[pl.BlockSpec((TILE, 512), lambda i: (i, 0)), pl.BlockSpec((512, TILE), lambda i: (0, i))],
        out_shape=[jax.ShapeDtypeStruct((n, 512), BF16), jax.ShapeDtypeStruct((512, n), BF16)],
        compiler_params=_cparams(("parallel",)),
        name="mla_kv_up",
    )(ckv_all, w_ukv)


def _softmax_pv(qc, key_chunk, vt_chunks, nk):
    return _softmax_pv_streams([(qc, key_chunk, vt_chunks)], nk)[0]


def _softmax_pv_streams(streams, nk):
    ones = jnp.ones((BF16_ROWS, ATT_TK), BF16)
    ms = [jnp.full((1, qc.shape[1]), -jnp.inf, F32) for qc, _, _ in streams]
    accs = [jnp.zeros((HEAD_DIM + BF16_ROWS, qc.shape[1]), F32) for qc, _, _ in streams]
    nxt = [_dot(key_chunk(0), qc) for qc, key_chunk, _ in streams]
    for c in range(nk):
        for i, (qc, key_chunk, vt_chunks) in enumerate(streams):
            s = nxt[i]
            if c + 1 < nk:
                nxt[i] = _dot(key_chunk(c + 1), qc)
            mn = jnp.maximum(ms[i], jnp.max(s, axis=0, keepdims=True))
            p = jnp.exp2(s - mn).astype(BF16)
            vts = vt_chunks(c)
            if len(vts) == 1:
                pv = _dot(jnp.concatenate([vts[0], ones], axis=0), p)
            else:
                tq = qc.shape[1] // len(vts)
                pv = jnp.concatenate([_dot(jnp.concatenate([vt, ones], axis=0), p[:, k * tq:(k + 1) * tq])
                                      for k, vt in enumerate(vts)], axis=1)
            accs[i] = jnp.exp2(ms[i] - mn) * accs[i] + pv
            ms[i] = mn
    return [acc[0:HEAD_DIM, :] / acc[HEAD_DIM:HEAD_DIM + 1, :] for acc in accs]


def _chunk_locator(segments):
    counts = [seg[0].shape[0] // ATT_TK for seg in segments]

    def locate(c):
        for seg, n in zip(segments, counts):
            if c < n:
                return seg, slice(c * ATT_TK, (c + 1) * ATT_TK)
            c -= n
        raise IndexError(c)

    return locate, sum(counts)


def _gqa_kernel(q_ref, *refs):
    o_ref = refs[-1]
    locate, nk = _chunk_locator([refs[i:i + 2] for i in range(0, len(refs) - 1, 2)])
    kvh = pl.program_id(1)
    tq = q_ref.shape[1]
    n = B_HEADS // B_KV_HEADS
    cols = []
    for hh in range(n):
        q = q_ref[hh * HEAD_DIM:(hh + 1) * HEAD_DIM, :]
        z = jnp.zeros_like(q)
        cols.append(jnp.where(kvh == 0, jnp.concatenate([q, z], axis=0), jnp.concatenate([z, q], axis=0)))
    def key_chunk(c):
        (k_ref, _), rows = locate(c)
        return k_ref[rows, :]

    def vt_chunks(c):
        (_, vt_ref), cols_c = locate(c)
        return [vt_ref[:, cols_c]]

    per = n // GQA_STREAMS
    outs = _softmax_pv_streams(
        [(jnp.concatenate(cols[i * per:(i + 1) * per], axis=1), key_chunk, vt_chunks) for i in range(GQA_STREAMS)], nk)
    for hh in range(n):
        o = outs[hh // per][:, (hh % per) * tq:(hh % per + 1) * tq]
        o_ref[hh * HEAD_DIM:(hh + 1) * HEAD_DIM, :] = o.astype(BF16)


def _mla_kernel(qn_ref, qr_ref, *refs):
    o_ref = refs[-1]
    locate, nk = _chunk_locator([refs[i:i + 3] for i in range(0, len(refs) - 1, 3)])
    tq = qn_ref.shape[1]
    z = jnp.zeros((C_NOPE, tq), BF16)
    zr = jnp.zeros((LANES - C_ROPE, tq), BF16)
    streams = []
    for pair in range(MLA_PAIRS):
        cols = []
        for hh in range(2):
            h = 2 * pair + hh
            qn = qn_ref[h * C_NOPE:(h + 1) * C_NOPE, :]
            qr = qr_ref[h * C_ROPE:(h + 1) * C_ROPE, :]
            cols.append(jnp.concatenate(([qn, z] if hh == 0 else [z, qn]) + [qr, zr], axis=0))

        def key_chunk(c, pair=pair):
            (kn_ref, kr_ref, _), rows = locate(c)
            return jnp.concatenate([kn_ref[rows, pair * LANES:(pair + 1) * LANES], kr_ref[rows, :]], axis=1)

        def vt_chunks(c, pair=pair):
            (_, _, vt_ref), cols_c = locate(c)
            return [vt_ref[(2 * pair + hh) * C_V:(2 * pair + hh + 1) * C_V, cols_c] for hh in range(2)]

        streams.append((jnp.concatenate(cols, axis=1), key_chunk, vt_chunks))
    outs = _softmax_pv_streams(streams, nk)
    for pair in range(MLA_PAIRS):
        for hh in range(2):
            h = 2 * pair + hh
            o_ref[h * C_V:(h + 1) * C_V, :] = outs[pair][:, hh * tq:(hh + 1) * tq].astype(BF16)


def _attn_call(body, grid, in_specs, out_spec, args, n_tiles):
    return pl.pallas_call(
        body,
        grid=grid,
        in_specs=in_specs,
        out_specs=out_spec,
        out_shape=jax.ShapeDtypeStruct((512, n_tiles * TILE), BF16),
        compiler_params=_cparams(("parallel",) * len(grid)),
        name="attention",
    )(*args)


def _gqa_attention(qt, kb, vt, k_cache, vt_cache):
    assert NT_CTX * TILE == DEC_SEQ
    rows = (B_HEADS // B_KV_HEADS) * HEAD_DIM
    o_ctx = _attn_call(
        _gqa_kernel, (BATCH, B_KV_HEADS),
        [pl.BlockSpec((rows, TILE), lambda b, h: (h, b)),
         pl.BlockSpec((None, TILE, LANES), lambda b, h: (b, 0, 0)),
         pl.BlockSpec((HEAD_DIM, TILE), lambda b, h: (h, b))],
        pl.BlockSpec((rows, TILE), lambda b, h: (h, b)),
        [qt, kb.reshape(NT, TILE, LANES), vt], NT_CTX)
    o_lat = _attn_call(
        _gqa_kernel, (DEC_BATCH, B_KV_HEADS, NT_SEQ),
        [pl.BlockSpec((rows, TILE), lambda b, h, j: (h, NT_CTX + b * NT_SEQ + j)),
         pl.BlockSpec((None, PAST_LEN, LANES), lambda b, h, j: (b, 0, 0)),
         pl.BlockSpec((None, HEAD_DIM, PAST_LEN), lambda b, h, j: (b, h, 0)),
         pl.BlockSpec((DEC_SEQ, LANES), lambda b, h, j: (1 + b, 0)),
         pl.BlockSpec((HEAD_DIM, DEC_SEQ), lambda b, h, j: (h, 1 + b))],
        pl.BlockSpec((rows, TILE), lambda b, h, j: (h, b * NT_SEQ + j)),
        [qt, k_cache, vt_cache, kb, vt], NT_LAT)
    return o_ctx, o_lat


def _mla_attention(qnt, qrt, kn, krb, vt, kn_cache, kr_cache, vt_cache):
    nh = 2 * MLA_PAIRS
    steps = C_HEADS // nh
    o_ctx = _attn_call(
        _mla_kernel, (BATCH, steps),
        [pl.BlockSpec((nh * C_NOPE, TILE), lambda b, p: (p, b)),
         pl.BlockSpec((nh * C_ROPE, TILE), lambda b, p: (p, b)),
         pl.BlockSpec((TILE, MLA_PAIRS * LANES), lambda b, p: (b, p)),
         pl.BlockSpec((TILE, LANES), lambda b, p: (b, 0)),
         pl.BlockSpec((nh * C_V, TILE), lambda b, p: (p, b))],
        pl.BlockSpec((nh * C_V, TILE), lambda b, p: (p, b)),
        [qnt, qrt, kn, krb, vt], NT_CTX)
    o_lat = _attn_call(
        _mla_kernel, (DEC_BATCH, steps, NT_SEQ),
        [pl.BlockSpec((nh * C_NOPE, TILE), lambda b, p, j: (p, NT_CTX + b * NT_SEQ + j)),
         pl.BlockSpec((nh * C_ROPE, TILE), lambda b, p, j: (p, NT_CTX + b * NT_SEQ + j)),
         pl.BlockSpec((PAST_LEN, MLA_PAIRS * LANES), lambda b, p, j: (b, p)),
         pl.BlockSpec((None, PAST_LEN, LANES), lambda b, p, j: (b, 0, 0)),
         pl.BlockSpec((nh * C_V, PAST_LEN), lambda b, p, j: (p, b)),
         pl.BlockSpec((DEC_SEQ, MLA_PAIRS * LANES), lambda b, p, j: (1 + b, p)),
         pl.BlockSpec((DEC_SEQ, LANES), lambda b, p, j: (1 + b, 0)),
         pl.BlockSpec((nh * C_V, DEC_SEQ), lambda b, p, j: (p, 1 + b))],
        pl.BlockSpec((nh * C_V, TILE), lambda b, p, j: (p, b * NT_SEQ + j)),
        [qnt, qrt, kn_cache, kr_cache, vt_cache, kn, krb, vt], NT_LAT)
    return o_ctx, o_lat


def _group_tile(i, ctx_ref, lat_ref):
    return jnp.where(i < NT_CTX_IO, ctx_ref[...], lat_ref[...])


_CTX_COL = lambda i: (0, jnp.minimum(i, NT_CTX_IO - 1))
_LAT_COL = lambda i: (0, jnp.maximum(i - NT_CTX_IO, 0))
_CTX_ROW = lambda i: (jnp.minimum(i, NT_CTX_IO - 1), 0)
_LAT_ROW = lambda i: (jnp.maximum(i - NT_CTX_IO, 0), 0)


def _finish_out(o, x_in, mod_ref, n2_ref, wr_ref, xo_ref, h2_ref, lg_ref):
    x = x_in + mod_ref[0, 0, 2:3, :] * o
    xo_ref[...] = x
    h2 = _norm_mod(x, n2_ref[...], mod_ref[0, 0, 4:5, :], mod_ref[0, 0, 3:4, :])
    hi = h2.astype(BF16)
    lo = (h2 - hi.astype(F32)).astype(BF16)
    h2_ref[...] = hi
    parts = (_dot(hi, wr_ref[...]) + _dot(lo, wr_ref[...])).T
    lg_ref[...] = parts[0:N_EXPERTS, :] + parts[N_EXPERTS:2 * N_EXPERTS, :]


def _dot_nt(a, bt):
    return lax.dot_general(a, bt, (((1,), (1,)), ((), ())), preferred_element_type=F32)


def _dot_t(at, b):
    return lax.dot_general(at, b, (((0,), (0,)), ((), ())), preferred_element_type=F32)


def _even_out_kernel(a_ref, btc_ref, btl_ref, w_ref, xc_ref, xl_ref, mod_ref, n2_ref, wr_ref,
                     xo_ref, h2_ref, lg_ref):
    i = pl.program_id(0)
    bt = _group_tile(i, btc_ref, btl_ref)
    o = _dot(a_ref[...], w_ref[0:512, :]) + _dot_t(bt, w_ref[512:1024, :])
    _finish_out(o, _group_tile(i, xc_ref, xl_ref), mod_ref, n2_ref, wr_ref, xo_ref, h2_ref, lg_ref)


def _odd_out_kernel(atc_ref, atl_ref, z_ref, zp_ref, zn_ref, gb_ref, cw_ref, w_ref, x_ref, mod_ref, n2_ref,
                    wr_ref, xo_ref, h2_ref, lg_ref):
    i = pl.program_id(0)
    at = _group_tile(i, atc_ref, atl_ref)
    ctx = i < NT_CTX_IO
    j = (i - NT_CTX_IO) % NT_SEQ_IO
    first = jnp.logical_or(ctx, j == 0)
    last = jnp.logical_or(ctx, j == NT_SEQ_IO - 1)
    z = z_ref[...]
    row = lax.broadcasted_iota(jnp.int32, z.shape, 0)
    halo_p = jnp.where(first, 0.0, zp_ref[7:8, :])
    halo_n = jnp.where(last, 0.0, zn_ref[0:1, :])
    zprev = jnp.where(row == 0, halo_p, pltpu.roll(z, 1, 0))
    znext = jnp.where(row == IO_TILE - 1, halo_n, pltpu.roll(z, IO_TILE - 1, 0))
    zprev = jnp.where(jnp.logical_and(ctx, row % SEQ == 0), 0.0, zprev)
    znext = jnp.where(jnp.logical_and(ctx, row % SEQ == SEQ - 1), 0.0, znext)
    y = zprev * cw_ref[0:1, :] + z * cw_ref[1:2, :] + znext * cw_ref[2:3, :]
    d = (gb_ref[...] * y).astype(BF16)
    o = _dot_t(at, w_ref[0:512, :]) + _dot(d, w_ref[512:1024, :])
    _finish_out(o, x_ref[...], mod_ref, n2_ref, wr_ref, xo_ref, h2_ref, lg_ref)


_OUT_SHAPES = [
    jax.ShapeDtypeStruct((N_TOK, D_MODEL), F32),
    jax.ShapeDtypeStruct((N_TOK, D_MODEL), BF16),
    jax.ShapeDtypeStruct((N_EXPERTS, N_TOK), F32),
]
_LOGIT_SPEC = pl.BlockSpec((N_EXPERTS, IO_TILE), lambda i: (0, i))


def _even_out(layer, a, b, w_out, xc, xl, mod, n2, wr):
    full = lambda shape: pl.BlockSpec(shape, lambda i: (0,) * len(shape))
    tile = lambda w: pl.BlockSpec((IO_TILE, w), lambda i: (i, 0))
    return pl.pallas_call(
        _even_out_kernel,
        grid=(NT_IO,),
        in_specs=[tile(512), pl.BlockSpec((512, IO_TILE), _CTX_COL), pl.BlockSpec((512, IO_TILE), _LAT_COL),
                  full(w_out.shape), pl.BlockSpec((IO_TILE, D_MODEL), _CTX_ROW),
                  pl.BlockSpec((IO_TILE, D_MODEL), _LAT_ROW),
                  pl.BlockSpec((1, 1, 6, D_MODEL), lambda i: (layer, _mod_row(i), 0, 0)),
                  full((1, D_MODEL)), full(wr.shape)],
        out_specs=[tile(D_MODEL), tile(D_MODEL), _LOGIT_SPEC],
        out_shape=_OUT_SHAPES,
        compiler_params=_cparams(("parallel",)),
        name="even_out",
    )(a, b[0], b[1], w_out, xc, xl, mod, n2, wr)


def _odd_out(layer, a, z, gb, cw, w_out, x, mod, n2, wr):
    full = lambda shape: pl.BlockSpec(shape, lambda i: (0,) * len(shape))
    tile = lambda w: pl.BlockSpec((IO_TILE, w), lambda i: (i, 0))
    rb = IO_TILE // 8
    return pl.pallas_call(
        _odd_out_kernel,
        grid=(NT_IO,),
        in_specs=[pl.BlockSpec((512, IO_TILE), _CTX_COL), pl.BlockSpec((512, IO_TILE), _LAT_COL), tile(512),
                  pl.BlockSpec((8, 512), lambda i: (jnp.maximum(i * rb - 1, 0), 0)),
                  pl.BlockSpec((8, 512), lambda i: (jnp.minimum(i * rb + rb, NT_IO * rb - 1), 0)),
                  tile(512), full(cw.shape), full(w_out.shape), tile(D_MODEL),
                  pl.BlockSpec((1, 1, 6, D_MODEL), lambda i: (layer, _mod_row(i), 0, 0)),
                  full((1, D_MODEL)), full(wr.shape)],
        out_specs=[tile(D_MODEL), tile(D_MODEL), _LOGIT_SPEC],
        out_shape=_OUT_SHAPES,
        compiler_params=_cparams(("parallel",)),
        name="odd_out",
    )(a[0], a[1], z, z, z, gb, cw, w_out, x, mod, n2, wr)


def _select_tokens(aff, n_dom, cap, tri, pos_ref, gate_ref):
    w = aff.shape[1] // n_dom
    doms = [aff[:, d * w:(d + 1) * w] for d in range(n_dom)]
    bits = [jnp.zeros((N_EXPERTS, 1), jnp.int32) for _ in range(n_dom)]
    for bit in range(30, -1, -1):
        for d in range(n_dom):
            cand = bits[d] | (1 << bit)
            cnt = jnp.sum((doms[d] >= lax.bitcast_convert_type(cand, F32)).astype(F32), axis=1, keepdims=True)
            bits[d] = jnp.where(cnt >= cap, cand, bits[d])
    for d in range(n_dom):
        thr = lax.bitcast_convert_type(bits[d], F32)
        gt = doms[d] > thr
        eq = doms[d] == thr
        need = cap - jnp.sum(gt.astype(F32), axis=1, keepdims=True)
        eq_seen = jnp.zeros((N_EXPERTS, 1), F32)
        sel_seen = jnp.zeros((N_EXPERTS, 1), F32) + d * cap
        for blk in range(w // TILE):
            cols = slice(blk * TILE, (blk + 1) * TILE)
            eq_b = eq[:, cols].astype(F32)
            eq_rank = _dot(eq_b.astype(BF16), tri) + eq_seen
            sel = jnp.logical_or(gt[:, cols], jnp.logical_and(eq[:, cols], eq_rank < need))
            sel_f = sel.astype(F32)
            pos = _dot(sel_f.astype(BF16), tri) + sel_seen
            tile = d * (w // TILE) + blk
            pos_ref[0, tile] = jnp.where(sel, pos, -1.0)
            gate_ref[0, tile] = jnp.where(sel, doms[d][:, cols], 0.0)
            eq_seen = eq_seen + jnp.sum(eq_b, axis=1, keepdims=True)
            sel_seen = sel_seen + jnp.sum(sel_f, axis=1, keepdims=True)


def _route_kernel(lg_ref, pos_ref, gate_ref):
    lg = lg_ref[...]
    ex = jnp.exp(lg - jnp.max(lg, axis=0, keepdims=True))
    aff = ex / jnp.sum(ex, axis=0, keepdims=True)
    r = lax.broadcasted_iota(jnp.int32, (TILE, TILE), 0)
    c = lax.broadcasted_iota(jnp.int32, (TILE, TILE), 1)
    tri = (r < c).astype(BF16)

    @pl.when(pl.program_id(0) == 0)
    def _():
        _select_tokens(aff, GROUP // SEQ, EC_CAPACITY * SEQ // N_EXPERTS, tri, pos_ref, gate_ref)

    @pl.when(pl.program_id(0) > 0)
    def _():
        _select_tokens(aff, GROUP // DEC_SEQ, EC_CAPACITY * DEC_SEQ // N_EXPERTS, tri, pos_ref, gate_ref)


def _route(logits_t):
    blk = pl.BlockSpec((1, GROUP_TILES, N_EXPERTS, TILE), lambda g: (g, 0, 0, 0))
    shape = jax.ShapeDtypeStruct((N_GROUPS, GROUP_TILES, N_EXPERTS, TILE), F32)
    return pl.pallas_call(
        _route_kernel,
        grid=(N_GROUPS,),
        in_specs=[pl.BlockSpec((N_EXPERTS, GROUP), lambda g: (0, g))],
        out_specs=[blk, blk],
        out_shape=[shape, shape],
        compiler_params=_cparams(("parallel",)),
        name="route",
    )(logits_t)


def _moe_kernel(nblk_ref, ctab_ref, etab_ref, llen_ref,
                x_ref, pos_ref, gate_ref, wg_ref, wu_ref, wd_ref, xres_ref, mod_ref, fn_ref, *rest, final):
    n_out = 2 if final else 1
    out_refs = rest[:n_out]
    lists_ref, gl_ref, s_ref, gs_ref, acc_ref, wgb_ref, wub_ref, wdb_ref = rest[n_out:]
    g = pl.program_id(0)
    s = pl.program_id(1)
    chunks = STACK_BLK // CHUNK_ROWS
    rows16 = lax.broadcasted_iota(jnp.int32, (CHUNK_ROWS, TILE), 0)

    @pl.when(jnp.logical_and(g == 0, s == 0))
    def _():
        lists_ref[ZERO_ROW:ZERO_ROW + CHUNK_ROWS, :] = jnp.zeros((CHUNK_ROWS, D_MODEL), BF16)

    def chunk_rows(tbase, q, unused_row):
        d = ctab_ref[tbase + q]
        return pl.multiple_of(jnp.where(d >= 0, d, unused_row), CHUNK_ROWS)

    def build_block(j, rb, with_gate, slot):
        tbase = ((g * GROUP_TILES + j) * MAX_STACK_BLKS + rb) * chunks
        for q in range(chunks):
            d = ctab_ref[tbase + q]
            e = etab_ref[tbase + q]
            posrow = pos_ref[0, j, pl.ds(e, 1), :]
            rowid = (rows16 + (rb * STACK_BLK + q * CHUNK_ROWS)).astype(F32)
            hit = jnp.logical_and(posrow == rowid, d >= 0)
            s_ref[slot, q * CHUNK_ROWS:(q + 1) * CHUNK_ROWS, :] = hit.astype(BF16)
            if with_gate:
                gaterow = gate_ref[0, j, pl.ds(e, 1), :]
                gs_ref[slot, q * CHUNK_ROWS:(q + 1) * CHUNK_ROWS, :] = jnp.where(hit, gaterow, 0.0)
        return tbase

    def for_blocks(nb, build, finish):
        def run(first, n):
            tbases = [build(first + k, k) for k in range(n)]
            for k in range(n):
                finish(tbases[k], k)

        def group(i, _):
            run(BLOCK_SLOTS * i, BLOCK_SLOTS)
            return 0

        lax.fori_loop(0, nb // BLOCK_SLOTS, group, 0)
        for left in range(1, BLOCK_SLOTS):
            @pl.when(nb % BLOCK_SLOTS == left)
            def _(left=left):
                run(nb - left, left)

    @pl.when(s < MOE_TILE_STEPS)
    def _gather():
        def tile(t, _):
            j = s * MOE_STEP_TILES + t
            tok = pl.multiple_of(t * TILE, TILE)

            def finish(tbase, slot):
                picked = _dot(s_ref[slot], x_ref[pl.ds(tok, TILE), :]).astype(BF16)
                gcol = jnp.sum(gs_ref[slot], axis=1, keepdims=True)
                for q in range(chunks):
                    d = chunk_rows(tbase, q, DUMP_ROW)
                    rows = slice(q * CHUNK_ROWS, (q + 1) * CHUNK_ROWS)
                    lists_ref[pl.ds(d, CHUNK_ROWS), :] = picked[rows, :]
                    gl_ref[pl.ds(d, CHUNK_ROWS), :] = gcol[rows, :]

            for_blocks(nblk_ref[g * GROUP_TILES + j], lambda rb, slot: build_block(j, rb, True, slot), finish)
            return 0

        lax.fori_loop(0, MOE_STEP_TILES, tile, 0)

    @pl.when(jnp.logical_and(s >= MOE_TILE_STEPS, s < MOE_TILE_STEPS + N_EXPERTS))
    def _experts():
        e = s - MOE_TILE_STEPS
        ln = llen_ref[g * N_EXPERTS + e]
        base = e * LIST_MAX
        half = FFN_BLK // 2
        rem = ln % FFN_BLK
        use_half = jnp.logical_and(rem > 0, rem <= half)
        nfull = ln // FFN_BLK + jnp.where(rem > half, 1, 0)
        end = nfull * FFN_BLK + jnp.where(use_half, half, 0)

        def zero_tail(k, _):
            r0 = pl.multiple_of(base + ln + k * CHUNK_ROWS, CHUNK_ROWS)
            lists_ref[pl.ds(r0, CHUNK_ROWS), :] = jnp.zeros((CHUNK_ROWS, D_MODEL), BF16)
            gl_ref[pl.ds(r0, CHUNK_ROWS), :] = jnp.zeros((CHUNK_ROWS, 1), F32)
            return 0

        lax.fori_loop(0, (end - ln) // CHUNK_ROWS, zero_tail, 0)
        wgb_ref[...] = wg_ref[0, 0].astype(BF16)
        wub_ref[...] = wu_ref[0, 0].astype(BF16)
        wdb_ref[...] = wd_ref[0, 0].astype(BF16)

        def ffn_rows(r0, n):
            xs = lists_ref[pl.ds(r0, n), :]
            hid = _silu(_dot(xs, wgb_ref[...])) * _dot(xs, wub_ref[...])
            y = _dot(hid.astype(BF16), wdb_ref[...]) * gl_ref[pl.ds(r0, n), :]
            lists_ref[pl.ds(r0, n), :] = y.astype(BF16)

        def ffn(c, _):
            ffn_rows(pl.multiple_of(base + c * FFN_BLK, FFN_BLK), FFN_BLK)
            return 0

        lax.fori_loop(0, nfull, ffn, 0)

        @pl.when(use_half)
        def _():
            ffn_rows(pl.multiple_of(base + nfull * FFN_BLK, half), half)

    @pl.when(s >= MOE_TILE_STEPS + N_EXPERTS)
    def _scatter():
        def tile(t, _):
            j = (s - MOE_TILE_STEPS - N_EXPERTS) * SCATTER_STEP_TILES + t
            rows = pl.ds(pl.multiple_of(t * TILE, TILE), TILE)
            acc_ref[...] = jnp.zeros_like(acc_ref)

            def finish(tbase, slot):
                y = jnp.concatenate([lists_ref[pl.ds(chunk_rows(tbase, q, ZERO_ROW), CHUNK_ROWS), :]
                                     for q in range(chunks)], axis=0)
                acc_ref[...] += _dot_t(s_ref[slot], y)

            for_blocks(nblk_ref[g * GROUP_TILES + j], lambda rb, slot: build_block(j, rb, False, slot), finish)
            x = xres_ref[rows, :] + mod_ref[0, 0, 5:6, :] * acc_ref[...]
            if final:
                y = x * lax.rsqrt(jnp.mean(x * x, axis=-1, keepdims=True) + EPS) * fn_ref[...]

                @pl.when(g == 0)
                def _():
                    out_refs[0][rows, :] = y

                @pl.when(g > 0)
                def _():
                    out_refs[1][rows, :] = y
            else:
                out_refs[0][rows, :] = x
            return 0

        lax.fori_loop(0, SCATTER_STEP_TILES, tile, 0)


def _moe(layer, final, tables, h2, pos_t, gate_t, wg, wu, wd, xres, mod, fn):
    gt = GROUP_TILES
    ts = MOE_TILE_STEPS
    sc = SCATTER_STEPS
    rows_a = MOE_STEP_TILES * TILE
    rows = SCATTER_STEP_TILES * TILE
    step_c = lambda s: jnp.clip(s - ts - N_EXPERTS, 0, sc - 1)
    tile_a = lambda g, s, *_: (g * ts + jnp.minimum(s, ts - 1), 0)
    tile_c = lambda g, s, *_: (g * sc + step_c(s), 0)
    expert = lambda g, s, *_: (layer, jnp.clip(s - ts, 0, N_EXPERTS - 1), 0, 0)
    group = lambda g, s, *_: (g, 0, 0, 0)
    if final:
        out_specs = [
            pl.BlockSpec((rows, D_MODEL), lambda g, s, *_: (jnp.where(g == 0, step_c(s), sc - 1), 0)),
            pl.BlockSpec((rows, D_MODEL), lambda g, s, *_: (jnp.where(g == 0, 0, (g - 1) * sc + step_c(s)), 0)),
        ]
        out_shape = [jax.ShapeDtypeStruct((NT_CTX * TILE, D_MODEL), F32),
                     jax.ShapeDtypeStruct((NT_LAT * TILE, D_MODEL), F32)]
    else:
        out_specs = pl.BlockSpec((rows, D_MODEL), tile_c)
        out_shape = jax.ShapeDtypeStruct((N_TOK, D_MODEL), F32)
    grid_spec = pltpu.PrefetchScalarGridSpec(
        num_scalar_prefetch=4,
        grid=(N_GROUPS, ts + N_EXPERTS + sc),
        in_specs=[
            pl.BlockSpec((rows_a, D_MODEL), tile_a),
            pl.BlockSpec((1, gt, N_EXPERTS, TILE), group),
            pl.BlockSpec((1, gt, N_EXPERTS, TILE), group),
            pl.BlockSpec((1, 1, D_MODEL, EXPERT_FF), expert),
            pl.BlockSpec((1, 1, D_MODEL, EXPERT_FF), expert),
            pl.BlockSpec((1, 1, EXPERT_FF, D_MODEL), expert),
            pl.BlockSpec((rows, D_MODEL), tile_c),
            pl.BlockSpec((1, 1, 6, D_MODEL), lambda g, s, *_: (layer, g, 0, 0)),
            pl.BlockSpec((1, D_MODEL), lambda g, s, *_: (0, 0)),
        ],
        out_specs=out_specs,
        scratch_shapes=[
            pltpu.VMEM((LIST_ROWS, D_MODEL), BF16),
            pltpu.VMEM((LIST_ROWS, 1), F32),
            pltpu.VMEM((BLOCK_SLOTS, STACK_BLK, TILE), BF16),
            pltpu.VMEM((BLOCK_SLOTS, STACK_BLK, TILE), F32),
            pltpu.VMEM((TILE, D_MODEL), F32),
            pltpu.VMEM((D_MODEL, EXPERT_FF), BF16),
            pltpu.VMEM((D_MODEL, EXPERT_FF), BF16),
            pltpu.VMEM((EXPERT_FF, D_MODEL), BF16),
        ],
    )
    return pl.pallas_call(
        functools.partial(_moe_kernel, final=final),
        grid_spec=grid_spec,
        out_shape=out_shape,
        compiler_params=_cparams(("arbitrary", "arbitrary")),
        name="moe",
    )(*tables, h2, pos_t, gate_t, wg, wu, wd, xres, mod, fn)


def _rope_tables(rot_dim):
    axis_dim = rot_dim // 2
    tok = jnp.arange(DEC_SEQ)
    rows = (tok // GRID_W).astype(F32)
    cols = (tok % GRID_W).astype(F32)
    inv_freq = ROPE_THETA ** (-jnp.arange(0, axis_dim, 2, dtype=F32) / axis_dim)
    ar = rows[:, None] * inv_freq
    ac = cols[:, None] * inv_freq
    cos = jnp.concatenate([jnp.cos(ar), jnp.cos(ar), jnp.cos(ac), jnp.cos(ac)], axis=1)
    sin = jnp.concatenate([-jnp.sin(ar), jnp.sin(ar), -jnp.sin(ac), jnp.sin(ac)], axis=1)
    rep = LANES // rot_dim
    cos = jnp.tile(cos, (1, rep)).reshape(NT_SEQ_IO, IO_TILE, LANES)
    sin = jnp.tile(sin, (1, rep)).reshape(NT_SEQ_IO, IO_TILE, LANES)
    cos = jnp.concatenate([jnp.ones((1, IO_TILE, LANES), F32), cos], axis=0)
    sin = jnp.concatenate([jnp.zeros((1, IO_TILE, LANES), F32), sin], axis=0)
    return cos, sin


def _excl_cumsum(a, axis):
    return jnp.cumsum(a, axis=axis) - a


def _route_and_moe(layer, final, h2, logits, wg, wu, wd, xres, mod, fn):
    pos_t, gate_t = _route(logits)
    sel = pos_t >= 0
    cnt = jnp.sum(sel, axis=-1, dtype=jnp.int32)
    npad = (cnt + CHUNK_ROWS - 1) // CHUNK_ROWS * CHUNK_ROWS
    seg_off = _excl_cumsum(npad, 2)
    list_off = _excl_cumsum(npad, 1)
    rank0 = _excl_cumsum(cnt, 1)
    stack_pos = jnp.where(sel, pos_t - rank0[..., None].astype(F32) + seg_off[..., None].astype(F32), -1.0)
    nblk = (jnp.sum(npad, axis=2) + STACK_BLK - 1) // STACK_BLK
    llen = jnp.sum(npad, axis=1)
    r = (jnp.arange(MAX_STACK_BLKS * STACK_BLK // CHUNK_ROWS, dtype=jnp.int32) * CHUNK_ROWS)[None, None, None, :]
    in_seg = jnp.logical_and(r >= seg_off[..., None], r < (seg_off + npad)[..., None])
    eidx = jnp.arange(N_EXPERTS, dtype=jnp.int32)[None, None, :, None]
    dst = jnp.sum(jnp.where(in_seg, eidx * LIST_MAX + list_off[..., None] + r - seg_off[..., None], 0), axis=2)
    used = jnp.any(in_seg, axis=2)
    ctab = jnp.where(used, dst, -1).astype(jnp.int32)
    etab = jnp.sum(jnp.where(in_seg, eidx, 0), axis=2).astype(jnp.int32)
    tables = (nblk.reshape(-1).astype(jnp.int32), ctab.reshape(-1), etab.reshape(-1),
              llen.reshape(-1).astype(jnp.int32))
    return _moe(layer, final, tables, h2, stack_pos, gate_t, wg, wu, wd, xres, mod, fn)


def _split_router(w):
    hi = w.astype(BF16)
    lo = (w - hi.astype(F32)).astype(BF16)
    pad = jnp.zeros((D_MODEL, LANES - 2 * N_EXPERTS), BF16)
    return jnp.concatenate([hi, lo, pad], axis=1)


def kernel(x_prompt, x_sample, c, cache_attn_k, cache_attn_v, cache_mla_ckv, cache_mla_krope, c_ctx, w_mod, b_mod,
           norm1, norm2, ev_w_in, ev_q_norm, ev_k_norm, ev_w_s, ev_b_s, ev_w_out, od_w_in, od_q_a_norm,
           od_kv_a_norm, od_w_uq, od_w_ukv, od_conv_w, od_w_out, moe_router, moe_w_gate, moe_w_up, moe_w_down,
           final_norm):
    nctx = NT_CTX * TILE
    xc = x_prompt.reshape(nctx, D_MODEL)
    xl = x_sample.reshape(NT_LAT * TILE, D_MODEL)
    cvec = jnp.concatenate([c_ctx[None], c, jnp.zeros((3, D_MODEL), F32)], axis=0)
    mod = _modulation(cvec, w_mod, b_mod).reshape(2, 8, 6, D_MODEL)

    cos64, sin64 = _rope_tables(HEAD_DIM)
    seg = jnp.arange(512) // HEAD_DIM
    bd = (seg[:, None] == seg[None, :]).astype(BF16)
    bs_tab = jnp.repeat(ev_b_s[0].T, 64, axis=1)
    gated, qt, k, kb, v, vt = _even_in(
        xc, xl, mod, norm1[0:1], ev_w_in[0].astype(BF16), jnp.tile(ev_q_norm[0], 8)[None], jnp.tile(ev_k_norm[0], 2)[None],
        cos64, sin64, jnp.concatenate([ev_w_s[0, 0::2], ev_w_s[0, 1::2]], axis=2).astype(BF16), bs_tab, bd)
    new_k = k[:nctx].reshape(BATCH, SEQ, B_KV_HEADS, HEAD_DIM).transpose(0, 2, 1, 3)[:, None]
    new_v = v[:nctx].reshape(BATCH, SEQ, B_KV_HEADS, HEAD_DIM).transpose(0, 2, 1, 3)[:, None]
    ck = cache_attn_k[:, 0].transpose(0, 2, 1, 3).reshape(DEC_BATCH, PAST_LEN, LANES).astype(BF16)
    cv = cache_attn_v[:, 0].transpose(0, 1, 3, 2).reshape(DEC_BATCH, LANES, PAST_LEN).astype(BF16)
    attn_t = _gqa_attention(qt, kb, vt, ck, cv)
    x1, h2, logits = _even_out(0, gated, attn_t, ev_w_out[0].astype(BF16), xc, xl, mod, norm2[0:1],
                               _split_router(moe_router[0]))
    x2 = _route_and_moe(0, False, h2, logits, moe_w_gate, moe_w_up, moe_w_down, x1, mod, final_norm[None])

    cos32, sin32 = _rope_tables(C_ROPE)
    w_in1 = od_w_in[0]
    w_in1 = jnp.concatenate([w_in1[:, :C_Q_LORA + C_KV_LORA + C_ROPE], jnp.zeros((D_MODEL, LANES - C_ROPE), F32),
                             w_in1[:, C_Q_LORA + C_KV_LORA + C_ROPE:]], axis=1).astype(BF16)
    wuq = od_w_uq[0].reshape(C_Q_LORA, C_HEADS, C_NOPE + C_ROPE)
    wuq = jnp.concatenate([wuq[:, :, :C_NOPE].reshape(C_Q_LORA, -1), wuq[:, :, C_NOPE:].reshape(C_Q_LORA, -1)],
                          axis=1).astype(BF16)
    wukv = od_w_ukv[0].reshape(C_KV_LORA, C_HEADS, C_NOPE + C_V)
    wukv = jnp.concatenate([wukv[:, :, :C_NOPE].reshape(C_KV_LORA, -1), wukv[:, :, C_NOPE:].reshape(C_KV_LORA, -1)],
                           axis=1).astype(BF16)
    qnt, qrt, ckv, kr, krb, gb, z, kn, vt1 = _odd_in(x2, mod, norm1[1:2], w_in1, od_q_a_norm[0][None],
                                                     od_kv_a_norm[0][None], wuq, wukv, cos32, sin32)
    new_ckv = ckv[:nctx].reshape(BATCH, 1, SEQ, C_KV_LORA)
    new_kr = kr[:nctx, :C_ROPE].reshape(BATCH, 1, SEQ, C_ROPE)
    kn_cache, vt_cache = _kvup(cache_mla_ckv[:, 0].reshape(DEC_BATCH * PAST_LEN, C_KV_LORA), wukv)
    ckr = jnp.pad(cache_mla_krope[:, 0], ((0, 0), (0, 0), (0, LANES - C_ROPE))).astype(BF16)
    attn_t = _mla_attention(qnt, qrt, kn, krb, vt1, kn_cache, ckr, vt_cache)
    cw = jnp.concatenate([od_conv_w[0], jnp.zeros((5, D_WIDTH), F32)], axis=0)
    x3, h2, logits = _odd_out(1, attn_t, z, gb, cw, od_w_out[0].astype(BF16), x2, mod, norm2[1:2],
                              _split_router(moe_router[1]))
    y_c, y_l = _route_and_moe(1, True, h2, logits, moe_w_gate, moe_w_up, moe_w_down, x3, mod, final_norm[None])
    y_prompt = y_c.reshape(BATCH, SEQ, D_MODEL)
    y_sample = y_l.reshape(DEC_BATCH, DEC_SEQ, D_MODEL)
    return y_prompt, y_sample, new_k, new_v, new_ckv, new_kr
```

```python
import functools
import math

import jax
import jax.numpy as jnp
from jax import lax
from jax.experimental import pallas as pl
from jax.experimental.pallas import tpu as pltpu

F32 = jnp.float32
BF16 = jnp.bfloat16

D_MODEL = 1024
BATCH = 16
SEQ = 256
DEC_BATCH = 4
DEC_SEQ = 4096
PAST_LEN = 512
GRID_W = 64
ROPE_THETA = 10000.0
EPS = 1e-6
HEAD_DIM = 64
CHUNK = 128
A_GROUPS = 8
A_WIDTH = 512
B_HEADS = 8
B_KV_HEADS = 2
B_WIDTH = 512
C_HEADS = 8
C_NOPE = 64
C_ROPE = 32
C_V = 64
C_Q_LORA = 256
C_KV_LORA = 128
D_WIDTH = 512
N_EXPERTS = 16
EXPERT_FF = 512
EC_CAPACITY = 2

LOG2E = math.log2(math.e)
LANES = 128
BF16_ROWS = 16
ATT_TK = 256
MLA_PAIRS = 2
GQA_STREAMS = 2
TILE = 256
NT_CTX = BATCH * SEQ // TILE
NT_SEQ = DEC_SEQ // TILE
NT_LAT = DEC_BATCH * NT_SEQ
NT = NT_CTX + NT_LAT
N_TOK = NT * TILE
IO_TILE = 1024
NT_CTX_IO = BATCH * SEQ // IO_TILE
NT_SEQ_IO = DEC_SEQ // IO_TILE
NT_IO = N_TOK // IO_TILE
GROUP = 4096
N_GROUPS = N_TOK // GROUP
GROUP_TILES = GROUP // TILE
GROUP_CAP = EC_CAPACITY * GROUP // N_EXPERTS
CHUNK_ROWS = BF16_ROWS
STACK_BLK = 256
MAX_STACK_BLKS = -(-N_EXPERTS * TILE // STACK_BLK)
MOE_STEP_TILES = 2
MOE_TILE_STEPS = GROUP_TILES // MOE_STEP_TILES
SCATTER_STEP_TILES = 1
SCATTER_STEPS = GROUP_TILES // SCATTER_STEP_TILES
BLOCK_SLOTS = 3
FFN_BLK = 256
LIST_MAX = -(-(GROUP_CAP + GROUP_TILES * (CHUNK_ROWS - 1)) // FFN_BLK) * FFN_BLK
DUMP_ROW = N_EXPERTS * LIST_MAX
ZERO_ROW = DUMP_ROW + CHUNK_ROWS
LIST_ROWS = ZERO_ROW + CHUNK_ROWS
VMEM_LIMIT = 58 * 1024 * 1024


def _cparams(sem):
    return pltpu.CompilerParams(dimension_semantics=sem, vmem_limit_bytes=VMEM_LIMIT)


def _dot(a, b):
    return jnp.dot(a, b, preferred_element_type=F32)


def _silu(x):
    return x / (1.0 + jnp.exp(-x))


def _mod_row(i):
    return jnp.where(i < NT_CTX_IO, 0, 1 + (i - NT_CTX_IO) // NT_SEQ_IO)


def _tab_blk(i):
    return jnp.where(i < NT_CTX_IO, 0, 1 + (i - NT_CTX_IO) % NT_SEQ_IO)


def _mod_kernel(c_ref, w_ref, b_ref, o_ref):
    s = _silu(c_ref[...])
    o_ref[0] = _dot(s.astype(BF16), w_ref[0].astype(BF16)) + b_ref[0]


def _modulation(cvec, w_mod, b_mod):
    depth = w_mod.shape[0]
    nchunk = 6
    return pl.pallas_call(
        _mod_kernel,
        grid=(depth, nchunk),
        in_specs=[
            pl.BlockSpec((8, D_MODEL), lambda l, k: (0, 0)),
            pl.BlockSpec((1, D_MODEL, D_MODEL), lambda l, k: (l, 0, k)),
            pl.BlockSpec((1, 1, D_MODEL), lambda l, k: (l, 0, k)),
        ],
        out_specs=pl.BlockSpec((1, 8, D_MODEL), lambda l, k: (l, 0, k)),
        out_shape=jax.ShapeDtypeStruct((depth, 8, 6 * D_MODEL), F32),
        compiler_params=_cparams(("arbitrary", "arbitrary")),
        name="modulation",
    )(cvec, w_mod, b_mod.reshape(depth, 1, 6 * D_MODEL))


def _norm_mod(x, gain, scale, shift):
    ms = jnp.mean(x * x, axis=-1, keepdims=True)
    return (x * lax.rsqrt(ms + EPS) * gain) * (1.0 + scale) + shift


def _seg_mean_sq(z, bd, width):
    zz = z * z
    hi = zz.astype(BF16)
    lo = (zz - hi.astype(F32)).astype(BF16)
    return (_dot(hi, bd) + _dot(lo, bd)) * (1.0 / width)


def _rope(z, cos, sin_signed, half):
    w = z.shape[1]
    lane = lax.broadcasted_iota(jnp.int32, z.shape, 1)
    first = (lane % (2 * half)) < half
    partner = jnp.where(first, pltpu.roll(z, w - half, 1), pltpu.roll(z, half, 1))
    return z * cos + partner * sin_signed


def _tile_lanes(t, n):
    return jnp.concatenate([t] * n, axis=1) if n > 1 else t


def _even_in_kernel(xc_ref, xl_ref, mod_ref, n1_ref, w_ref, qg_ref, kg_ref, cos_ref, sin_ref, ws_ref, bs_ref, bd_ref,
                    gated_ref, qt_ref, k_ref, kb_ref, v_ref, vt_ref):
    x = _group_tile(pl.program_id(0), xc_ref, xl_ref)
    h = _norm_mod(x, n1_ref[...], mod_ref[0, 0, 1:2, :], mod_ref[0, 0, 0:1, :])
    p = _dot(h.astype(BF16), w_ref[...])
    u = p[:, 0:512]
    va = p[:, 512:1024].astype(BF16)
    q = p[:, 1024:1536]
    k = p[:, 1536:1664]
    v = p[:, 1664:1792]
    bd = bd_ref[...]
    cos = cos_ref[0]
    sin = sin_ref[0]
    qn = q * lax.rsqrt(_seg_mean_sq(q, bd, HEAD_DIM) + EPS) * qg_ref[...]
    kn = k * lax.rsqrt(_seg_mean_sq(k, bd[0:LANES, 0:LANES], HEAD_DIM) + EPS) * kg_ref[...]
    qr = _rope(qn, _tile_lanes(cos, 4), _tile_lanes(sin, 4), HEAD_DIM // 4)
    kr = _rope(kn, cos, sin, HEAD_DIM // 4)
    qt_ref[...] = (qr * (HEAD_DIM ** -0.5 * LOG2E)).T.astype(BF16)
    k_ref[...] = kr
    kb_ref[...] = kr.astype(BF16)
    v_ref[...] = v
    vt_ref[...] = v.T.astype(BF16)
    lane = lax.broadcasted_iota(jnp.int32, (CHUNK, LANES), 1)
    for ch in range(IO_TILE // CHUNK):
        rows = slice(ch * CHUNK, (ch + 1) * CHUNK)
        cols = []
        for pair in range(A_GROUPS // 2):
            vp = va[rows, pair * LANES:(pair + 1) * LANES]
            zero = jnp.zeros_like(vp)
            rhs = jnp.concatenate([jnp.where(lane < LANES // 2, vp, zero), jnp.where(lane < LANES // 2, zero, vp)],
                                  axis=0)
            cols.append(_dot(ws_ref[pair], rhs))
        s = jnp.concatenate(cols, axis=1) + bs_ref[...]
        gated_ref[rows, :] = (u[rows, :] * s).astype(BF16)


def _even_in(xc, xl, mod, n1, w_in, qg, kg, cos, sin, ws, bs_tab, bd):
    full = lambda shape: pl.BlockSpec(shape, lambda i: (0,) * len(shape))
    return pl.pallas_call(
        _even_in_kernel,
        grid=(NT_IO,),
        in_specs=[
            pl.BlockSpec((IO_TILE, D_MODEL), _CTX_ROW),
            pl.BlockSpec((IO_TILE, D_MODEL), _LAT_ROW),
            pl.BlockSpec((1, 1, 6, D_MODEL), lambda i: (0, _mod_row(i), 0, 0)),
            full((1, D_MODEL)),
            full(w_in.shape),
            full((1, 512)),
            full((1, LANES)),
            pl.BlockSpec((1, IO_TILE, LANES), lambda i: (_tab_blk(i), 0, 0)),
            pl.BlockSpec((1, IO_TILE, LANES), lambda i: (_tab_blk(i), 0, 0)),
            full(ws.shape),
            full(bs_tab.shape),
            full(bd.shape),
        ],
        out_specs=[
            pl.BlockSpec((IO_TILE, 512), lambda i: (i, 0)),
            pl.BlockSpec((512, IO_TILE), lambda i: (0, i)),
            pl.BlockSpec((IO_TILE, LANES), lambda i: (i, 0)),
            pl.BlockSpec((IO_TILE, LANES), lambda i: (i, 0)),
            pl.BlockSpec((IO_TILE, LANES), lambda i: (i, 0)),
            pl.BlockSpec((LANES, IO_TILE), lambda i: (0, i)),
        ],
        out_shape=[
            jax.ShapeDtypeStruct((N_TOK, 512), BF16),
            jax.ShapeDtypeStruct((512, N_TOK), BF16),
            jax.ShapeDtypeStruct((N_TOK, LANES), F32),
            jax.ShapeDtypeStruct((N_TOK, LANES), BF16),
            jax.ShapeDtypeStruct((N_TOK, LANES), F32),
            jax.ShapeDtypeStruct((LANES, N_TOK), BF16),
        ],
        compiler_params=_cparams(("parallel",)),
        name="even_in",
    )(xc, xl, mod, n1, w_in, qg, kg, cos, sin, ws, bs_tab, bd)


def _odd_in_kernel(x_ref, mod_ref, n1_ref, w_ref, qa_ref, kva_ref, wuq_ref, wukv_ref, cos_ref, sin_ref,
                   qnt_ref, qrt_ref, ckv_ref, kr_ref, krb_ref, gb_ref, z_ref, kn_ref, vt_ref):
    h = _norm_mod(x_ref[...], n1_ref[...], mod_ref[0, 0, 1:2, :], mod_ref[0, 0, 0:1, :])
    p = _dot(h.astype(BF16), w_ref[...])
    cq = p[:, 0:256]
    ckv = p[:, 256:384]
    kr = p[:, 384:512]
    gb_ref[...] = p[:, 512:1024]
    z_ref[...] = p[:, 1024:1536] * p[:, 1536:2048]
    cqn = cq * lax.rsqrt(jnp.mean(cq * cq, axis=-1, keepdims=True) + EPS) * qa_ref[...]
    q = _dot(cqn.astype(BF16), wuq_ref[...])
    scale = (C_NOPE + C_ROPE) ** -0.5 * LOG2E
    cos = cos_ref[0]
    sin = sin_ref[0]
    qnt_ref[...] = (q[:, 0:512] * scale).T.astype(BF16)
    qr = _rope(q[:, 512:768], _tile_lanes(cos, 2), _tile_lanes(sin, 2), C_ROPE // 4)
    qrt_ref[...] = (qr * scale).T.astype(BF16)
    ckvn = ckv * lax.rsqrt(jnp.mean(ckv * ckv, axis=-1, keepdims=True) + EPS) * kva_ref[...]
    ckv_ref[...] = ckvn
    kv = _dot(ckvn.astype(BF16), wukv_ref[...])
    kn_ref[...] = kv[:, 0:512].astype(BF16)
    vt_ref[...] = kv[:, 512:1024].T.astype(BF16)
    krr = _rope(kr, cos, sin, C_ROPE // 4)
    kr_ref[...] = krr
    krb_ref[...] = krr.astype(BF16)


def _odd_in(x, mod, n1, w_in, qa, kva, wuq, wukv, cos, sin):
    full = lambda shape: pl.BlockSpec(shape, lambda i: (0,) * len(shape))
    tile = lambda w: pl.BlockSpec((IO_TILE, w), lambda i: (i, 0))
    return pl.pallas_call(
        _odd_in_kernel,
        grid=(NT_IO,),
        in_specs=[
            tile(D_MODEL),
            pl.BlockSpec((1, 1, 6, D_MODEL), lambda i: (1, _mod_row(i), 0, 0)),
            full((1, D_MODEL)),
            full(w_in.shape),
            full((1, C_Q_LORA)),
            full((1, C_KV_LORA)),
            full(wuq.shape),
            full(wukv.shape),
            pl.BlockSpec((1, IO_TILE, LANES), lambda i: (_tab_blk(i), 0, 0)),
            pl.BlockSpec((1, IO_TILE, LANES), lambda i: (_tab_blk(i), 0, 0)),
        ],
        out_specs=[pl.BlockSpec((512, IO_TILE), lambda i: (0, i)),
                   pl.BlockSpec((256, IO_TILE), lambda i: (0, i)),
                   tile(LANES), tile(LANES), tile(LANES), tile(512), tile(512),
                   tile(512), pl.BlockSpec((512, IO_TILE), lambda i: (0, i))],
        out_shape=[
            jax.ShapeDtypeStruct((512, N_TOK), BF16),
            jax.ShapeDtypeStruct((256, N_TOK), BF16),
            jax.ShapeDtypeStruct((N_TOK, LANES), F32),
            jax.ShapeDtypeStruct((N_TOK, LANES), F32),
            jax.ShapeDtypeStruct((N_TOK, LANES), BF16),
            jax.ShapeDtypeStruct((N_TOK, 512), F32),
            jax.ShapeDtypeStruct((N_TOK, 512), F32),
            jax.ShapeDtypeStruct((N_TOK, 512), BF16),
            jax.ShapeDtypeStruct((512, N_TOK), BF16),
        ],
        compiler_params=_cparams(("parallel",)),
        name="odd_in",
    )(x, mod, n1, w_in, qa, kva, wuq, wukv, cos, sin)


def _kvup_kernel(c_ref, w_ref, kn_ref, vt_ref):
    p = _dot(c_ref[...].astype(BF16), w_ref[...])
    kn_ref[...] = p[:, 0:512].astype(BF16)
    vt_ref[...] = p[:, 512:1024].T.astype(BF16)


def _kvup(ckv_all, w_ukv):
    n = ckv_all.shape[0]
    return pl.pallas_call(
        _kvup_kernel,
        grid=(n // TILE,),
        in_specs=[pl.BlockSpec((TILE, C_KV_LORA), lambda i: (i, 0)),
                  pl.BlockSpec(w_ukv.shape, lambda i: (0, 0))],
        out_specs=[pl.BlockSpec((TILE, 512), lambda i: (i, 0)), pl.BlockSpec((512, TILE), lambda i: (0, i))],
        out_shape=[jax.ShapeDtypeStruct((n, 512), BF16), jax.ShapeDtypeStruct((512, n), BF16)],
        compiler_params=_cparams(("parallel",)),
        name="mla_kv_up",
    )(ckv_all, w_ukv)


def _softmax_pv(qc, key_chunk, vt_chunks, nk):
    return _softmax_pv_streams([(qc, key_chunk, vt_chunks)], nk)[0]


def _softmax_pv_streams(streams, nk):
    ones = jnp.ones((BF16_ROWS, ATT_TK), BF16)
    ms = [jnp.full((1, qc.shape[1]), -jnp.inf, F32) for qc, _, _ in streams]
    accs = [jnp.zeros((HEAD_DIM + BF16_ROWS, qc.shape[1]), F32) for qc, _, _ in streams]
    nxt = [_dot(key_chunk(0), qc) for qc, key_chunk, _ in streams]
    for c in range(nk):
        for i, (qc, key_chunk, vt_chunks) in enumerate(streams):
            s = nxt[i]
            if c + 1 < nk:
                nxt[i] = _dot(key_chunk(c + 1), qc)
            mn = jnp.maximum(ms[i], jnp.max(s, axis=0, keepdims=True))
            p = jnp.exp2(s - mn).astype(BF16)
            vts = vt_chunks(c)
            if len(vts) == 1:
                pv = _dot(jnp.concatenate([vts[0], ones], axis=0), p)
            else:
                tq = qc.shape[1] // len(vts)
                pv = jnp.concatenate([_dot(jnp.concatenate([vt, ones], axis=0), p[:, k * tq:(k + 1) * tq])
                                      for k, vt in enumerate(vts)], axis=1)
            accs[i] = jnp.exp2(ms[i] - mn) * accs[i] + pv
            ms[i] = mn
    return [acc[0:HEAD_DIM, :] / acc[HEAD_DIM:HEAD_DIM + 1, :] for acc in accs]


def _chunk_locator(segments):
    counts = [seg[0].shape[0] // ATT_TK for seg in segments]

    def locate(c):
        for seg, n in zip(segments, counts):
            if c < n:
                return seg, slice(c * ATT_TK, (c + 1) * ATT_TK)
            c -= n
        raise IndexError(c)

    return locate, sum(counts)


def _gqa_kernel(q_ref, *refs):
    o_ref = refs[-1]
    locate, nk = _chunk_locator([refs[i:i + 2] for i in range(0, len(refs) - 1, 2)])
    kvh = pl.program_id(1)
    tq = q_ref.shape[1]
    n = B_HEADS // B_KV_HEADS
    cols = []
    for hh in range(n):
        q = q_ref[hh * HEAD_DIM:(hh + 1) * HEAD_DIM, :]
        z = jnp.zeros_like(q)
        cols.append(jnp.where(kvh == 0, jnp.concatenate([q, z], axis=0), jnp.concatenate([z, q], axis=0)))
    def key_chunk(c):
        (k_ref, _), rows = locate(c)
        return k_ref[rows, :]

    def vt_chunks(c):
        (_, vt_ref), cols_c = locate(c)
        return [vt_ref[:, cols_c]]

    per = n // GQA_STREAMS
    outs = _softmax_pv_streams(
        [(jnp.concatenate(cols[i * per:(i + 1) * per], axis=1), key_chunk, vt_chunks) for i in range(GQA_STREAMS)], nk)
    for hh in range(n):
        o = outs[hh // per][:, (hh % per) * tq:(hh % per + 1) * tq]
        o_ref[hh * HEAD_DIM:(hh + 1) * HEAD_DIM, :] = o.astype(BF16)


def _mla_kernel(qn_ref, qr_ref, *refs):
    o_ref = refs[-1]
    locate, nk = _chunk_locator([refs[i:i + 3] for i in range(0, len(refs) - 1, 3)])
    tq = qn_ref.shape[1]
    z = jnp.zeros((C_NOPE, tq), BF16)
    zr = jnp.zeros((LANES - C_ROPE, tq), BF16)
    streams = []
    for pair in range(MLA_PAIRS):
        cols = []
        for hh in range(2):
            h = 2 * pair + hh
            qn = qn_ref[h * C_NOPE:(h + 1) * C_NOPE, :]
            qr = qr_ref[h * C_ROPE:(h + 1) * C_ROPE, :]
            cols.append(jnp.concatenate(([qn, z] if hh == 0 else [z, qn]) + [qr, zr], axis=0))

        def key_chunk(c, pair=pair):
            (kn_ref, kr_ref, _), rows = locate(c)
            return jnp.concatenate([kn_ref[rows, pair * LANES:(pair + 1) * LANES], kr_ref[rows, :]], axis=1)

        def vt_chunks(c, pair=pair):
            (_, _, vt_ref), cols_c = locate(c)
            return [vt_ref[(2 * pair + hh) * C_V:(2 * pair + hh + 1) * C_V, cols_c] for hh in range(2)]

        streams.append((jnp.concatenate(cols, axis=1), key_chunk, vt_chunks))
    outs = _softmax_pv_streams(streams, nk)
    for pair in range(MLA_PAIRS):
        for hh in range(2):
            h = 2 * pair + hh
            o_ref[h * C_V:(h + 1) * C_V, :] = outs[pair][:, hh * tq:(hh + 1) * tq].astype(BF16)


def _attn_call(body, grid, in_specs, out_spec, args, n_tiles):
    return pl.pallas_call(
        body,
        grid=grid,
        in_specs=in_specs,
        out_specs=out_spec,
        out_shape=jax.ShapeDtypeStruct((512, n_tiles * TILE), BF16),
        compiler_params=_cparams(("parallel",) * len(grid)),
        name="attention",
    )(*args)


def _gqa_attention(qt, kb, vt, k_cache, vt_cache):
    assert NT_CTX * TILE == DEC_SEQ
    rows = (B_HEADS // B_KV_HEADS) * HEAD_DIM
    o_ctx = _attn_call(
        _gqa_kernel, (BATCH, B_KV_HEADS),
        [pl.BlockSpec((rows, TILE), lambda b, h: (h, b)),
         pl.BlockSpec((None, TILE, LANES), lambda b, h: (b, 0, 0)),
         pl.BlockSpec((HEAD_DIM, TILE), lambda b, h: (h, b))],
        pl.BlockSpec((rows, TILE), lambda b, h: (h, b)),
        [qt, kb.reshape(NT, TILE, LANES), vt], NT_CTX)
    o_lat = _attn_call(
        _gqa_kernel, (DEC_BATCH, B_KV_HEADS, NT_SEQ),
        [pl.BlockSpec((rows, TILE), lambda b, h, j: (h, NT_CTX + b * NT_SEQ + j)),
         pl.BlockSpec((None, PAST_LEN, LANES), lambda b, h, j: (b, 0, 0)),
         pl.BlockSpec((None, HEAD_DIM, PAST_LEN), lambda b, h, j: (b, h, 0)),
         pl.BlockSpec((DEC_SEQ, LANES), lambda b, h, j: (1 + b, 0)),
         pl.BlockSpec((HEAD_DIM, DEC_SEQ), lambda b, h, j: (h, 1 + b))],
        pl.BlockSpec((rows, TILE), lambda b, h, j: (h, b * NT_SEQ + j)),
        [qt, k_cache, vt_cache, kb, vt], NT_LAT)
    return o_ctx, o_lat


def _mla_attention(qnt, qrt, kn, krb, vt, kn_cache, kr_cache, vt_cache):
    nh = 2 * MLA_PAIRS
    steps = C_HEADS // nh
    o_ctx = _attn_call(
        _mla_kernel, (BATCH, steps),
        [pl.BlockSpec((nh * C_NOPE, TILE), lambda b, p: (p, b)),
         pl.BlockSpec((nh * C_ROPE, TILE), lambda b, p: (p, b)),
         pl.BlockSpec((TILE, MLA_PAIRS * LANES), lambda b, p: (b, p)),
         pl.BlockSpec((TILE, LANES), lambda b, p: (b, 0)),
         pl.BlockSpec((nh * C_V, TILE), lambda b, p: (p, b))],
        pl.BlockSpec((nh * C_V, TILE), lambda b, p: (p, b)),
        [qnt, qrt, kn, krb, vt], NT_CTX)
    o_lat = _attn_call(
        _mla_kernel, (DEC_BATCH, steps, NT_SEQ),
        [pl.BlockSpec((nh * C_NOPE, TILE), lambda b, p, j: (p, NT_CTX + b * NT_SEQ + j)),
         pl.BlockSpec((nh * C_ROPE, TILE), lambda b, p, j: (p, NT_CTX + b * NT_SEQ + j)),
         pl.BlockSpec((PAST_LEN, MLA_PAIRS * LANES), lambda b, p, j: (b, p)),
         pl.BlockSpec((None, PAST_LEN, LANES), lambda b, p, j: (b, 0, 0)),
         pl.BlockSpec((nh * C_V, PAST_LEN), lambda b, p, j: (p, b)),
         pl.BlockSpec((DEC_SEQ, MLA_PAIRS * LANES), lambda b, p, j: (1 + b, p)),
         pl.BlockSpec((DEC_SEQ, LANES), lambda b, p, j: (1 + b, 0)),
         pl.BlockSpec((nh * C_V, DEC_SEQ), lambda b, p, j: (p, 1 + b))],
        pl.BlockSpec((nh * C_V, TILE), lambda b, p, j: (p, b * NT_SEQ + j)),
        [qnt, qrt, kn_cache, kr_cache, vt_cache, kn, krb, vt], NT_LAT)
    return o_ctx, o_lat


def _group_tile(i, ctx_ref, lat_ref):
    return jnp.where(i < NT_CTX_IO, ctx_ref[...], lat_ref[...])


_CTX_COL = lambda i: (0, jnp.minimum(i, NT_CTX_IO - 1))
_LAT_COL = lambda i: (0, jnp.maximum(i - NT_CTX_IO, 0))
_CTX_ROW = lambda i: (jnp.minimum(i, NT_CTX_IO - 1), 0)
_LAT_ROW = lambda i: (jnp.maximum(i - NT_CTX_IO, 0), 0)


def _finish_out(o, x_in, mod_ref, n2_ref, wr_ref, xo_ref, h2_ref, lg_ref):
    x = x_in + mod_ref[0, 0, 2:3, :] * o
    xo_ref[...] = x
    h2 = _norm_mod(x, n2_ref[...], mod_ref[0, 0, 4:5, :], mod_ref[0, 0, 3:4, :])
    hi = h2.astype(BF16)
    lo = (h2 - hi.astype(F32)).astype(BF16)
    h2_ref[...] = hi
    parts = (_dot(hi, wr_ref[...]) + _dot(lo, wr_ref[...])).T
    lg_ref[...] = parts[0:N_EXPERTS, :] + parts[N_EXPERTS:2 * N_EXPERTS, :]


def _dot_nt(a, bt):
    return lax.dot_general(a, bt, (((1,), (1,)), ((), ())), preferred_element_type=F32)


def _dot_t(at, b):
    return lax.dot_general(at, b, (((0,), (0,)), ((), ())), preferred_element_type=F32)


def _even_out_kernel(a_ref, btc_ref, btl_ref, w_ref, xc_ref, xl_ref, mod_ref, n2_ref, wr_ref,
                     xo_ref, h2_ref, lg_ref):
    i = pl.program_id(0)
    bt = _group_tile(i, btc_ref, btl_ref)
    o = _dot(a_ref[...], w_ref[0:512, :]) + _dot_t(bt, w_ref[512:1024, :])
    _finish_out(o, _group_tile(i, xc_ref, xl_ref), mod_ref, n2_ref, wr_ref, xo_ref, h2_ref, lg_ref)


def _odd_out_kernel(atc_ref, atl_ref, z_ref, zp_ref, zn_ref, gb_ref, cw_ref, w_ref, x_ref, mod_ref, n2_ref,
                    wr_ref, xo_ref, h2_ref, lg_ref):
    i = pl.program_id(0)
    at = _group_tile(i, atc_ref, atl_ref)
    ctx = i < NT_CTX_IO
    j = (i - NT_CTX_IO) % NT_SEQ_IO
    first = jnp.logical_or(ctx, j == 0)
    last = jnp.logical_or(ctx, j == NT_SEQ_IO - 1)
    z = z_ref[...]
    row = lax.broadcasted_iota(jnp.int32, z.shape, 0)
    halo_p = jnp.where(first, 0.0, zp_ref[7:8, :])
    halo_n = jnp.where(last, 0.0, zn_ref[0:1, :])
    zprev = jnp.where(row == 0, halo_p, pltpu.roll(z, 1, 0))
    znext = jnp.where(row == IO_TILE - 1, halo_n, pltpu.roll(z, IO_TILE - 1, 0))
    zprev = jnp.where(jnp.logical_and(ctx, row % SEQ == 0), 0.0, zprev)
    znext = jnp.where(jnp.logical_and(ctx, row % SEQ == SEQ - 1), 0.0, znext)
    y = zprev * cw_ref[0:1, :] + z * cw_ref[1:2, :] + znext * cw_ref[2:3, :]
    d = (gb_ref[...] * y).astype(BF16)
    o = _dot_t(at, w_ref[0:512, :]) + _dot(d, w_ref[512:1024, :])
    _finish_out(o, x_ref[...], mod_ref, n2_ref, wr_ref, xo_ref, h2_ref, lg_ref)


_OUT_SHAPES = [
    jax.ShapeDtypeStruct((N_TOK, D_MODEL), F32),
    jax.ShapeDtypeStruct((N_TOK, D_MODEL), BF16),
    jax.ShapeDtypeStruct((N_EXPERTS, N_TOK), F32),
]
_LOGIT_SPEC = pl.BlockSpec((N_EXPERTS, IO_TILE), lambda i: (0, i))


def _even_out(layer, a, b, w_out, xc, xl, mod, n2, wr):
    full = lambda shape: pl.BlockSpec(shape, lambda i: (0,) * len(shape))
    tile = lambda w: pl.BlockSpec((IO_TILE, w), lambda i: (i, 0))
    return pl.pallas_call(
        _even_out_kernel,
        grid=(NT_IO,),
        in_specs=[tile(512), pl.BlockSpec((512, IO_TILE), _CTX_COL), pl.BlockSpec((512, IO_TILE), _LAT_COL),
                  full(w_out.shape), pl.BlockSpec((IO_TILE, D_MODEL), _CTX_ROW),
                  pl.BlockSpec((IO_TILE, D_MODEL), _LAT_ROW),
                  pl.BlockSpec((1, 1, 6, D_MODEL), lambda i: (layer, _mod_row(i), 0, 0)),
                  full((1, D_MODEL)), full(wr.shape)],
        out_specs=[tile(D_MODEL), tile(D_MODEL), _LOGIT_SPEC],
        out_shape=_OUT_SHAPES,
        compiler_params=_cparams(("parallel",)),
        name="even_out",
    )(a, b[0], b[1], w_out, xc, xl, mod, n2, wr)


def _odd_out(layer, a, z, gb, cw, w_out, x, mod, n2, wr):
    full = lambda shape: pl.BlockSpec(shape, lambda i: (0,) * len(shape))
    tile = lambda w: pl.BlockSpec((IO_TILE, w), lambda i: (i, 0))
    rb = IO_TILE // 8
    return pl.pallas_call(
        _odd_out_kernel,
        grid=(NT_IO,),
        in_specs=[pl.BlockSpec((512, IO_TILE), _CTX_COL), pl.BlockSpec((512, IO_TILE), _LAT_COL), tile(512),
                  pl.BlockSpec((8, 512), lambda i: (jnp.maximum(i * rb - 1, 0), 0)),
                  pl.BlockSpec((8, 512), lambda i: (jnp.minimum(i * rb + rb, NT_IO * rb - 1), 0)),
                  tile(512), full(cw.shape), full(w_out.shape), tile(D_MODEL),
                  pl.BlockSpec((1, 1, 6, D_MODEL), lambda i: (layer, _mod_row(i), 0, 0)),
                  full((1, D_MODEL)), full(wr.shape)],
        out_specs=[tile(D_MODEL), tile(D_MODEL), _LOGIT_SPEC],
        out_shape=_OUT_SHAPES,
        compiler_params=_cparams(("parallel",)),
        name="odd_out",
    )(a[0], a[1], z, z, z, gb, cw, w_out, x, mod, n2, wr)


def _select_tokens(aff, n_dom, cap, tri, pos_ref, gate_ref):
    w = aff.shape[1] // n_dom
    doms = [aff[:, d * w:(d + 1) * w] for d in range(n_dom)]
    bits = [jnp.zeros((N_EXPERTS, 1), jnp.int32) for _ in range(n_dom)]
    for bit in range(30, -1, -1):
        for d in range(n_dom):
            cand = bits[d] | (1 << bit)
            cnt = jnp.sum((doms[d] >= lax.bitcast_convert_type(cand, F32)).astype(F32), axis=1, keepdims=True)
            bits[d] = jnp.where(cnt >= cap, cand, bits[d])
    for d in range(n_dom):
        thr = lax.bitcast_convert_type(bits[d], F32)
        gt = doms[d] > thr
        eq = doms[d] == thr
        need = cap - jnp.sum(gt.astype(F32), axis=1, keepdims=True)
        eq_seen = jnp.zeros((N_EXPERTS, 1), F32)
        sel_seen = jnp.zeros((N_EXPERTS, 1), F32) + d * cap
        for blk in range(w // TILE):
            cols = slice(blk * TILE, (blk + 1) * TILE)
            eq_b = eq[:, cols].astype(F32)
            eq_rank = _dot(eq_b.astype(BF16), tri) + eq_seen
            sel = jnp.logical_or(gt[:, cols], jnp.logical_and(eq[:, cols], eq_rank < need))
            sel_f = sel.astype(F32)
            pos = _dot(sel_f.astype(BF16), tri) + sel_seen
            tile = d * (w // TILE) + blk
            pos_ref[0, tile] = jnp.where(sel, pos, -1.0)
            gate_ref[0, tile] = jnp.where(sel, doms[d][:, cols], 0.0)
            eq_seen = eq_seen + jnp.sum(eq_b, axis=1, keepdims=True)
            sel_seen = sel_seen + jnp.sum(sel_f, axis=1, keepdims=True)


def _route_kernel(lg_ref, pos_ref, gate_ref):
    lg = lg_ref[...]
    ex = jnp.exp(lg - jnp.max(lg, axis=0, keepdims=True))
    aff = ex / jnp.sum(ex, axis=0, keepdims=True)
    r = lax.broadcasted_iota(jnp.int32, (TILE, TILE), 0)
    c = lax.broadcasted_iota(jnp.int32, (TILE, TILE), 1)
    tri = (r < c).astype(BF16)

    @pl.when(pl.program_id(0) == 0)
    def _():
        _select_tokens(aff, GROUP // SEQ, EC_CAPACITY * SEQ // N_EXPERTS, tri, pos_ref, gate_ref)

    @pl.when(pl.program_id(0) > 0)
    def _():
        _select_tokens(aff, GROUP // DEC_SEQ, EC_CAPACITY * DEC_SEQ // N_EXPERTS, tri, pos_ref, gate_ref)


def _route(logits_t):
    blk = pl.BlockSpec((1, GROUP_TILES, N_EXPERTS, TILE), lambda g: (g, 0, 0, 0))
    shape = jax.ShapeDtypeStruct((N_GROUPS, GROUP_TILES, N_EXPERTS, TILE), F32)
    return pl.pallas_call(
        _route_kernel,
        grid=(N_GROUPS,),
        in_specs=[pl.BlockSpec((N_EXPERTS, GROUP), lambda g: (0, g))],
        out_specs=[blk, blk],
        out_shape=[shape, shape],
        compiler_params=_cparams(("parallel",)),
        name="route",
    )(logits_t)


def _moe_kernel(nblk_ref, ctab_ref, etab_ref, llen_ref,
                x_ref, pos_ref, gate_ref, wg_ref, wu_ref, wd_ref, xres_ref, mod_ref, fn_ref, *rest, final):
    n_out = 2 if final else 1
    out_refs = rest[:n_out]
    lists_ref, gl_ref, s_ref, gs_ref, acc_ref, wgb_ref, wub_ref, wdb_ref = rest[n_out:]
    g = pl.program_id(0)
    s = pl.program_id(1)
    chunks = STACK_BLK // CHUNK_ROWS
    rows16 = lax.broadcasted_iota(jnp.int32, (CHUNK_ROWS, TILE), 0)

    @pl.when(jnp.logical_and(g == 0, s == 0))
    def _():
        lists_ref[ZERO_ROW:ZERO_ROW + CHUNK_ROWS, :] = jnp.zeros((CHUNK_ROWS, D_MODEL), BF16)

    def chunk_rows(tbase, q, unused_row):
        d = ctab_ref[tbase + q]
        return pl.multiple_of(jnp.where(d >= 0, d, unused_row), CHUNK_ROWS)

    def build_block(j, rb, with_gate, slot):
        tbase = ((g * GROUP_TILES + j) * MAX_STACK_BLKS + rb) * chunks
        for q in range(chunks):
            d = ctab_ref[tbase + q]
            e = etab_ref[tbase + q]
            posrow = pos_ref[0, j, pl.ds(e, 1), :]
            rowid = (rows16 + (rb * STACK_BLK + q * CHUNK_ROWS)).astype(F32)
            hit = jnp.logical_and(posrow == rowid, d >= 0)
            s_ref[slot, q * CHUNK_ROWS:(q + 1) * CHUNK_ROWS, :] = hit.astype(BF16)
            if with_gate:
                gaterow = gate_ref[0, j, pl.ds(e, 1), :]
                gs_ref[slot, q * CHUNK_ROWS:(q + 1) * CHUNK_ROWS, :] = jnp.where(hit, gaterow, 0.0)
        return tbase

    def for_blocks(nb, build, finish):
        def run(first, n):
            tbases = [build(first + k, k) for k in range(n)]
            for k in range(n):
                finish(tbases[k], k)

        def group(i, _):
            run(BLOCK_SLOTS * i, BLOCK_SLOTS)
            return 0

        lax.fori_loop(0, nb // BLOCK_SLOTS, group, 0)
        for left in range(1, BLOCK_SLOTS):
            @pl.when(nb % BLOCK_SLOTS == left)
            def _(left=left):
                run(nb - left, left)

    @pl.when(s < MOE_TILE_STEPS)
    def _gather():
        def tile(t, _):
            j = s * MOE_STEP_TILES + t
            tok = pl.multiple_of(t * TILE, TILE)

            def finish(tbase, slot):
                picked = _dot(s_ref[slot], x_ref[pl.ds(tok, TILE), :]).astype(BF16)
                gcol = jnp.sum(gs_ref[slot], axis=1, keepdims=True)
                for q in range(chunks):
                    d = chunk_rows(tbase, q, DUMP_ROW)
                    rows = slice(q * CHUNK_ROWS, (q + 1) * CHUNK_ROWS)
                    lists_ref[pl.ds(d, CHUNK_ROWS), :] = picked[rows, :]
                    gl_ref[pl.ds(d, CHUNK_ROWS), :] = gcol[rows, :]

            for_blocks(nblk_ref[g * GROUP_TILES + j], lambda rb, slot: build_block(j, rb, True, slot), finish)
            return 0

        lax.fori_loop(0, MOE_STEP_TILES, tile, 0)

    @pl.when(jnp.logical_and(s >= MOE_TILE_STEPS, s < MOE_TILE_STEPS + N_EXPERTS))
    def _experts():
        e = s - MOE_TILE_STEPS
        ln = llen_ref[g * N_EXPERTS + e]
        base = e * LIST_MAX
        half = FFN_BLK // 2
        rem = ln % FFN_BLK
        use_half = jnp.logical_and(rem > 0, rem <= half)
        nfull = ln // FFN_BLK + jnp.where(rem > half, 1, 0)
        end = nfull * FFN_BLK + jnp.where(use_half, half, 0)

        def zero_tail(k, _):
            r0 = pl.multiple_of(base + ln + k * CHUNK_ROWS, CHUNK_ROWS)
            lists_ref[pl.ds(r0, CHUNK_ROWS), :] = jnp.zeros((CHUNK_ROWS, D_MODEL), BF16)
            gl_ref[pl.ds(r0, CHUNK_ROWS), :] = jnp.zeros((CHUNK_ROWS, 1), F32)
            return 0

        lax.fori_loop(0, (end - ln) // CHUNK_ROWS, zero_tail, 0)
        wgb_ref[...] = wg_ref[0, 0].astype(BF16)
        wub_ref[...] = wu_ref[0, 0].astype(BF16)
        wdb_ref[...] = wd_ref[0, 0].astype(BF16)

        def ffn_rows(r0, n):
            xs = lists_ref[pl.ds(r0, n), :]
            hid = _silu(_dot(xs, wgb_ref[...])) * _dot(xs, wub_ref[...])
            y = _dot(hid.astype(BF16), wdb_ref[...]) * gl_ref[pl.ds(r0, n), :]
            lists_ref[pl.ds(r0, n), :] = y.astype(BF16)

        def ffn(c, _):
            ffn_rows(pl.multiple_of(base + c * FFN_BLK, FFN_BLK), FFN_BLK)
            return 0

        lax.fori_loop(0, nfull, ffn, 0)

        @pl.when(use_half)
        def _():
            ffn_rows(pl.multiple_of(base + nfull * FFN_BLK, half), half)

    @pl.when(s >= MOE_TILE_STEPS + N_EXPERTS)
    def _scatter():
        def tile(t, _):
            j = (s - MOE_TILE_STEPS - N_EXPERTS) * SCATTER_STEP_TILES + t
            rows = pl.ds(pl.multiple_of(t * TILE, TILE), TILE)
            acc_ref[...] = jnp.zeros_like(acc_ref)

            def finish(tbase, slot):
                y = jnp.concatenate([lists_ref[pl.ds(chunk_rows(tbase, q, ZERO_ROW), CHUNK_ROWS), :]
                                     for q in range(chunks)], axis=0)
                acc_ref[...] += _dot_t(s_ref[slot], y)

            for_blocks(nblk_ref[g * GROUP_TILES + j], lambda rb, slot: build_block(j, rb, False, slot), finish)
            x = xres_ref[rows, :] + mod_ref[0, 0, 5:6, :] * acc_ref[...]
            if final:
                y = x * lax.rsqrt(jnp.mean(x * x, axis=-1, keepdims=True) + EPS) * fn_ref[...]

                @pl.when(g == 0)
                def _():
                    out_refs[0][rows, :] = y

                @pl.when(g > 0)
                def _():
                    out_refs[1][rows, :] = y
            else:
                out_refs[0][rows, :] = x
            return 0

        lax.fori_loop(0, SCATTER_STEP_TILES, tile, 0)


def _moe(layer, final, tables, h2, pos_t, gate_t, wg, wu, wd, xres, mod, fn):
    gt = GROUP_TILES
    ts = MOE_TILE_STEPS
    sc = SCATTER_STEPS
    rows_a = MOE_STEP_TILES * TILE
    rows = SCATTER_STEP_TILES * TILE
    step_c = lambda s: jnp.clip(s - ts - N_EXPERTS, 0, sc - 1)
    tile_a = lambda g, s, *_: (g * ts + jnp.minimum(s, ts - 1), 0)
    tile_c = lambda g, s, *_: (g * sc + step_c(s), 0)
    expert = lambda g, s, *_: (layer, jnp.clip(s - ts, 0, N_EXPERTS - 1), 0, 0)
    group = lambda g, s, *_: (g, 0, 0, 0)
    if final:
        out_specs = [
            pl.BlockSpec((rows, D_MODEL), lambda g, s, *_: (jnp.where(g == 0, step_c(s), sc - 1), 0)),
            pl.BlockSpec((rows, D_MODEL), lambda g, s, *_: (jnp.where(g == 0, 0, (g - 1) * sc + step_c(s)), 0)),
        ]
        out_shape = [jax.ShapeDtypeStruct((NT_CTX * TILE, D_MODEL), F32),
                     jax.ShapeDtypeStruct((NT_LAT * TILE, D_MODEL), F32)]
    else:
        out_specs = pl.BlockSpec((rows, D_MODEL), tile_c)
        out_shape = jax.ShapeDtypeStruct((N_TOK, D_MODEL), F32)
    grid_spec = pltpu.PrefetchScalarGridSpec(
        num_scalar_prefetch=4,
        grid=(N_GROUPS, ts + N_EXPERTS + sc),
        in_specs=[
            pl.BlockSpec((rows_a, D_MODEL), tile_a),
            pl.BlockSpec((1, gt, N_EXPERTS, TILE), group),
            pl.BlockSpec((1, gt, N_EXPERTS, TILE), group),
            pl.BlockSpec((1, 1, D_MODEL, EXPERT_FF), expert),
            pl.BlockSpec((1, 1, D_MODEL, EXPERT_FF), expert),
            pl.BlockSpec((1, 1, EXPERT_FF, D_MODEL), expert),
            pl.BlockSpec((rows, D_MODEL), tile_c),
            pl.BlockSpec((1, 1, 6, D_MODEL), lambda g, s, *_: (layer, g, 0, 0)),
            pl.BlockSpec((1, D_MODEL), lambda g, s, *_: (0, 0)),
        ],
        out_specs=out_specs,
        scratch_shapes=[
            pltpu.VMEM((LIST_ROWS, D_MODEL), BF16),
            pltpu.VMEM((LIST_ROWS, 1), F32),
            pltpu.VMEM((BLOCK_SLOTS, STACK_BLK, TILE), BF16),
            pltpu.VMEM((BLOCK_SLOTS, STACK_BLK, TILE), F32),
            pltpu.VMEM((TILE, D_MODEL), F32),
            pltpu.VMEM((D_MODEL, EXPERT_FF), BF16),
            pltpu.VMEM((D_MODEL, EXPERT_FF), BF16),
            pltpu.VMEM((EXPERT_FF, D_MODEL), BF16),
        ],
    )
    return pl.pallas_call(
        functools.partial(_moe_kernel, final=final),
        grid_spec=grid_spec,
        out_shape=out_shape,
        compiler_params=_cparams(("arbitrary", "arbitrary")),
        name="moe",
    )(*tables, h2, pos_t, gate_t, wg, wu, wd, xres, mod, fn)


def _rope_tables(rot_dim):
    axis_dim = rot_dim // 2
    tok = jnp.arange(DEC_SEQ)
    rows = (tok // GRID_W).astype(F32)
    cols = (tok % GRID_W).astype(F32)
    inv_freq = ROPE_THETA ** (-jnp.arange(0, axis_dim, 2, dtype=F32) / axis_dim)
    ar = rows[:, None] * inv_freq
    ac = cols[:, None] * inv_freq
    cos = jnp.concatenate([jnp.cos(ar), jnp.cos(ar), jnp.cos(ac), jnp.cos(ac)], axis=1)
    sin = jnp.concatenate([-jnp.sin(ar), jnp.sin(ar), -jnp.sin(ac), jnp.sin(ac)], axis=1)
    rep = LANES // rot_dim
    cos = jnp.tile(cos, (1, rep)).reshape(NT_SEQ_IO, IO_TILE, LANES)
    sin = jnp.tile(sin, (1, rep)).reshape(NT_SEQ_IO, IO_TILE, LANES)
    cos = jnp.concatenate([jnp.ones((1, IO_TILE, LANES), F32), cos], axis=0)
    sin = jnp.concatenate([jnp.zeros((1, IO_TILE, LANES), F32), sin], axis=0)
    return cos, sin


def _excl_cumsum(a, axis):
    return jnp.cumsum(a, axis=axis) - a


def _route_and_moe(layer, final, h2, logits, wg, wu, wd, xres, mod, fn):
    pos_t, gate_t = _route(logits)
    sel = pos_t >= 0
    cnt = jnp.sum(sel, axis=-1, dtype=jnp.int32)
    npad = (cnt + CHUNK_ROWS - 1) // CHUNK_ROWS * CHUNK_ROWS
    seg_off = _excl_cumsum(npad, 2)
    list_off = _excl_cumsum(npad, 1)
    rank0 = _excl_cumsum(cnt, 1)
    stack_pos = jnp.where(sel, pos_t - rank0[..., None].astype(F32) + seg_off[..., None].astype(F32), -1.0)
    nblk = (jnp.sum(npad, axis=2) + STACK_BLK - 1) // STACK_BLK
    llen = jnp.sum(npad, axis=1)
    r = (jnp.arange(MAX_STACK_BLKS * STACK_BLK // CHUNK_ROWS, dtype=jnp.int32) * CHUNK_ROWS)[None, None, None, :]
    in_seg = jnp.logical_and(r >= seg_off[..., None], r < (seg_off + npad)[..., None])
    eidx = jnp.arange(N_EXPERTS, dtype=jnp.int32)[None, None, :, None]
    dst = jnp.sum(jnp.where(in_seg, eidx * LIST_MAX + list_off[..., None] + r - seg_off[..., None], 0), axis=2)
    used = jnp.any(in_seg, axis=2)
    ctab = jnp.where(used, dst, -1).astype(jnp.int32)
    etab = jnp.sum(jnp.where(in_seg, eidx, 0), axis=2).astype(jnp.int32)
    tables = (nblk.reshape(-1).astype(jnp.int32), ctab.reshape(-1), etab.reshape(-1),
              llen.reshape(-1).astype(jnp.int32))
    return _moe(layer, final, tables, h2, stack_pos, gate_t, wg, wu, wd, xres, mod, fn)


def _split_router(w):
    hi = w.astype(BF16)
    lo = (w - hi.astype(F32)).astype(BF16)
    pad = jnp.zeros((D_MODEL, LANES - 2 * N_EXPERTS), BF16)
    return jnp.concatenate([hi, lo, pad], axis=1)


def kernel(x_prompt, x_sample, c, cache_attn_k, cache_attn_v, cache_mla_ckv, cache_mla_krope, c_ctx, w_mod, b_mod,
           norm1, norm2, ev_w_in, ev_q_norm, ev_k_norm, ev_w_s, ev_b_s, ev_w_out, od_w_in, od_q_a_norm,
           od_kv_a_norm, od_w_uq, od_w_ukv, od_conv_w, od_w_out, moe_router, moe_w_gate, moe_w_up, moe_w_down,
           final_norm):
    nctx = NT_CTX * TILE
    xc = x_prompt.reshape(nctx, D_MODEL)
    xl = x_sample.reshape(NT_LAT * TILE, D_MODEL)
    cvec = jnp.concatenate([c_ctx[None], c, jnp.zeros((3, D_MODEL), F32)], axis=0)
    mod = _modulation(cvec, w_mod, b_mod).reshape(2, 8, 6, D_MODEL)

    cos64, sin64 = _rope_tables(HEAD_DIM)
    seg = jnp.arange(512) // HEAD_DIM
    bd = (seg[:, None] == seg[None, :]).astype(BF16)
    bs_tab = jnp.repeat(ev_b_s[0].T, 64, axis=1)
    gated, qt, k, kb, v, vt = _even_in(
        xc, xl, mod, norm1[0:1], ev_w_in[0].astype(BF16), jnp.tile(ev_q_norm[0], 8)[None], jnp.tile(ev_k_norm[0], 2)[None],
        cos64, sin64, jnp.concatenate([ev_w_s[0, 0::2], ev_w_s[0, 1::2]], axis=2).astype(BF16), bs_tab, bd)
    new_k = k[:nctx].reshape(BATCH, SEQ, B_KV_HEADS, HEAD_DIM).transpose(0, 2, 1, 3)[:, None]
    new_v = v[:nctx].reshape(BATCH, SEQ, B_KV_HEADS, HEAD_DIM).transpose(0, 2, 1, 3)[:, None]
    ck = cache_attn_k[:, 0].transpose(0, 2, 1, 3).reshape(DEC_BATCH, PAST_LEN, LANES).astype(BF16)
    cv = cache_attn_v[:, 0].transpose(0, 1, 3, 2).reshape(DEC_BATCH, LANES, PAST_LEN).astype(BF16)
    attn_t = _gqa_attention(qt, kb, vt, ck, cv)
    x1, h2, logits = _even_out(0, gated, attn_t, ev_w_out[0].astype(BF16), xc, xl, mod, norm2[0:1],
                               _split_router(moe_router[0]))
    x2 = _route_and_moe(0, False, h2, logits, moe_w_gate, moe_w_up, moe_w_down, x1, mod, final_norm[None])

    cos32, sin32 = _rope_tables(C_ROPE)
    w_in1 = od_w_in[0]
    w_in1 = jnp.concatenate([w_in1[:, :C_Q_LORA + C_KV_LORA + C_ROPE], jnp.zeros((D_MODEL, LANES - C_ROPE), F32),
                             w_in1[:, C_Q_LORA + C_KV_LORA + C_ROPE:]], axis=1).astype(BF16)
    wuq = od_w_uq[0].reshape(C_Q_LORA, C_HEADS, C_NOPE + C_ROPE)
    wuq = jnp.concatenate([wuq[:, :, :C_NOPE].reshape(C_Q_LORA, -1), wuq[:, :, C_NOPE:].reshape(C_Q_LORA, -1)],
                          axis=1).astype(BF16)
    wukv = od_w_ukv[0].reshape(C_KV_LORA, C_HEADS, C_NOPE + C_V)
    wukv = jnp.concatenate([wukv[:, :, :C_NOPE].reshape(C_KV_LORA, -1), wukv[:, :, C_NOPE:].reshape(C_KV_LORA, -1)],
                           axis=1).astype(BF16)
    qnt, qrt, ckv, kr, krb, gb, z, kn, vt1 = _odd_in(x2, mod, norm1[1:2], w_in1, od_q_a_norm[0][None],
                                                     od_kv_a_norm[0][None], wuq, wukv, cos32, sin32)
    new_ckv = ckv[:nctx].reshape(BATCH, 1, SEQ, C_KV_LORA)
    new_kr = kr[:nctx, :C_ROPE].reshape(BATCH, 1, SEQ, C_ROPE)
    kn_cache, vt_cache = _kvup(cache_mla_ckv[:, 0].reshape(DEC_BATCH * PAST_LEN, C_KV_LORA), wukv)
    ckr = jnp.pad(cache_mla_krope[:, 0], ((0, 0), (0, 0), (0, LANES - C_ROPE))).astype(BF16)
    attn_t = _mla_attention(qnt, qrt, kn, krb, vt1, kn_cache, ckr, vt_cache)
    cw = jnp.concatenate([od_conv_w[0], jnp.zeros((5, D_WIDTH), F32)], axis=0)
    x3, h2, logits = _odd_out(1, attn_t, z, gb, cw, od_w_out[0].astype(BF16), x2, mod, norm2[1:2],
                              _split_router(moe_router[1]))
    y_c, y_l = _route_and_moe(1, True, h2, logits, moe_w_gate, moe_w_up, moe_w_down, x3, mod, final_norm[None])
    y_prompt = y_c.reshape(BATCH, SEQ, D_MODEL)
    y_sample = y_l.reshape(DEC_BATCH, DEC_SEQ, D_MODEL)
    return y_prompt, y_sample, new_k, new_v, new_ckv, new_kr
```

```python
import functools
import math

import jax
import jax.numpy as jnp
from jax import lax
from jax.experimental import pallas as pl
from jax.experimental.pallas import tpu as pltpu

F32 = jnp.float32
BF16 = jnp.bfloat16

D_MODEL = 1024
BATCH = 16
SEQ = 256
DEC_BATCH = 4
DEC_SEQ = 4096
PAST_LEN = 512
GRID_W = 64
ROPE_THETA = 10000.0
EPS = 1e-6
HEAD_DIM = 64
CHUNK = 128
A_GROUPS = 8
A_WIDTH = 512
B_HEADS = 8
B_KV_HEADS = 2
B_WIDTH = 512
C_HEADS = 8
C_NOPE = 64
C_ROPE = 32
C_V = 64
C_Q_LORA = 256
C_KV_LORA = 128
D_WIDTH = 512
N_EXPERTS = 16
EXPERT_FF = 512
EC_CAPACITY = 2

LOG2E = math.log2(math.e)
LANES = 128
BF16_ROWS = 16
ATT_TK = 256
MLA_PAIRS = 2
GQA_STREAMS = 2
TILE = 256
NT_CTX = BATCH * SEQ // TILE
NT_SEQ = DEC_SEQ // TILE
NT_LAT = DEC_BATCH * NT_SEQ
NT = NT_CTX + NT_LAT
N_TOK = NT * TILE
IO_TILE = 1024
NT_CTX_IO = BATCH * SEQ // IO_TILE
NT_SEQ_IO = DEC_SEQ // IO_TILE
NT_IO = N_TOK // IO_TILE
GROUP = 4096
N_GROUPS = N_TOK // GROUP
GROUP_TILES = GROUP // TILE
GROUP_CAP = EC_CAPACITY * GROUP // N_EXPERTS
CHUNK_ROWS = BF16_ROWS
STACK_BLK = 256
MAX_STACK_BLKS = -(-N_EXPERTS * TILE // STACK_BLK)
MOE_STEP_TILES = 4
MOE_TILE_STEPS = GROUP_TILES // MOE_STEP_TILES
SCATTER_STEP_TILES = 1
SCATTER_STEPS = GROUP_TILES // SCATTER_STEP_TILES
BLOCK_SLOTS = 3
FFN_BLK = 256
LIST_MAX = -(-(GROUP_CAP + GROUP_TILES * (CHUNK_ROWS - 1)) // FFN_BLK) * FFN_BLK
DUMP_ROW = N_EXPERTS * LIST_MAX
ZERO_ROW = DUMP_ROW + CHUNK_ROWS
LIST_ROWS = ZERO_ROW + CHUNK_ROWS
VMEM_LIMIT = 60 * 1024 * 1024


def _cparams(sem):
    return pltpu.CompilerParams(dimension_semantics=sem, vmem_limit_bytes=VMEM_LIMIT)


def _dot(a, b):
    return jnp.dot(a, b, preferred_element_type=F32)


def _silu(x):
    return x / (1.0 + jnp.exp(-x))


def _mod_row(i):
    return jnp.where(i < NT_CTX_IO, 0, 1 + (i - NT_CTX_IO) // NT_SEQ_IO)


def _tab_blk(i):
    return jnp.where(i < NT_CTX_IO, 0, 1 + (i - NT_CTX_IO) % NT_SEQ_IO)


def _mod_kernel(c_ref, w_ref, b_ref, o_ref):
    s = _silu(c_ref[...])
    o_ref[0] = _dot(s.astype(BF16), w_ref[0].astype(BF16)) + b_ref[0]


def _modulation(cvec, w_mod, b_mod):
    depth = w_mod.shape[0]
    nchunk = 6
    return pl.pallas_call(
        _mod_kernel,
        grid=(depth, nchunk),
        in_specs=[
            pl.BlockSpec((8, D_MODEL), lambda l, k: (0, 0)),
            pl.BlockSpec((1, D_MODEL, D_MODEL), lambda l, k: (l, 0, k)),
            pl.BlockSpec((1, 1, D_MODEL), lambda l, k: (l, 0, k)),
        ],
        out_specs=pl.BlockSpec((1, 8, D_MODEL), lambda l, k: (l, 0, k)),
        out_shape=jax.ShapeDtypeStruct((depth, 8, 6 * D_MODEL), F32),
        compiler_params=_cparams(("arbitrary", "arbitrary")),
        name="modulation",
    )(cvec, w_mod, b_mod.reshape(depth, 1, 6 * D_MODEL))


def _norm_mod(x, gain, scale, shift):
    ms = jnp.mean(x * x, axis=-1, keepdims=True)
    return (x * lax.rsqrt(ms + EPS) * gain) * (1.0 + scale) + shift


def _seg_mean_sq(z, bd, width):
    zz = z * z
    hi = zz.astype(BF16)
    lo = (zz - hi.astype(F32)).astype(BF16)
    return (_dot(hi, bd) + _dot(lo, bd)) * (1.0 / width)


def _rope(z, cos, sin_signed, half):
    w = z.shape[1]
    lane = lax.broadcasted_iota(jnp.int32, z.shape, 1)
    first = (lane % (2 * half)) < half
    partner = jnp.where(first, pltpu.roll(z, w - half, 1), pltpu.roll(z, half, 1))
    return z * cos + partner * sin_signed


def _tile_lanes(t, n):
    return jnp.concatenate([t] * n, axis=1) if n > 1 else t


def _even_in_kernel(xc_ref, xl_ref, mod_ref, n1_ref, w_ref, qg_ref, kg_ref, cos_ref, sin_ref, ws_ref, bs_ref, bd_ref,
                    gated_ref, qt_ref, k_ref, kb_ref, v_ref, vt_ref):
    x = _group_tile(pl.program_id(0), xc_ref, xl_ref)
    h = _norm_mod(x, n1_ref[...], mod_ref[0, 0, 1:2, :], mod_ref[0, 0, 0:1, :])
    p = _dot(h.astype(BF16), w_ref[...])
    u = p[:, 0:512]
    va = p[:, 512:1024].astype(BF16)
    q = p[:, 1024:1536]
    k = p[:, 1536:1664]
    v = p[:, 1664:1792]
    bd = bd_ref[...]
    cos = cos_ref[0]
    sin = sin_ref[0]
    qn = q * lax.rsqrt(_seg_mean_sq(q, bd, HEAD_DIM) + EPS) * qg_ref[...]
    kn = k * lax.rsqrt(_seg_mean_sq(k, bd[0:LANES, 0:LANES], HEAD_DIM) + EPS) * kg_ref[...]
    qr = _rope(qn, _tile_lanes(cos, 4), _tile_lanes(sin, 4), HEAD_DIM // 4)
    kr = _rope(kn, cos, sin, HEAD_DIM // 4)
    qt_ref[...] = (qr * (HEAD_DIM ** -0.5 * LOG2E)).T.astype(BF16)
    k_ref[...] = kr
    kb_ref[...] = kr.astype(BF16)
    v_ref[...] = v
    vt_ref[...] = v.T.astype(BF16)
    lane = lax.broadcasted_iota(jnp.int32, (CHUNK, LANES), 1)
    for ch in range(IO_TILE // CHUNK):
        rows = slice(ch * CHUNK, (ch + 1) * CHUNK)
        cols = []
        for pair in range(A_GROUPS // 2):
            vp = va[rows, pair * LANES:(pair + 1) * LANES]
            zero = jnp.zeros_like(vp)
            rhs = jnp.concatenate([jnp.where(lane < LANES // 2, vp, zero), jnp.where(lane < LANES // 2, zero, vp)],
                                  axis=0)
            cols.append(_dot(ws_ref[pair], rhs))
        s = jnp.concatenate(cols, axis=1) + bs_ref[...]
        gated_ref[rows, :] = (u[rows, :] * s).astype(BF16)


def _even_in(xc, xl, mod, n1, w_in, qg, kg, cos, sin, ws, bs_tab, bd):
    full = lambda shape: pl.BlockSpec(shape, lambda i: (0,) * len(shape))
    return pl.pallas_call(
        _even_in_kernel,
        grid=(NT_IO,),
        in_specs=[
            pl.BlockSpec((IO_TILE, D_MODEL), _CTX_ROW),
            pl.BlockSpec((IO_TILE, D_MODEL), _LAT_ROW),
            pl.BlockSpec((1, 1, 6, D_MODEL), lambda i: (0, _mod_row(i), 0, 0)),
            full((1, D_MODEL)),
            full(w_in.shape),
            full((1, 512)),
            full((1, LANES)),
            pl.BlockSpec((1, IO_TILE, LANES), lambda i: (_tab_blk(i), 0, 0)),
            pl.BlockSpec((1, IO_TILE, LANES), lambda i: (_tab_blk(i), 0, 0)),
            full(ws.shape),
            full(bs_tab.shape),
            full(bd.shape),
        ],
        out_specs=[
            pl.BlockSpec((IO_TILE, 512), lambda i: (i, 0)),
            pl.BlockSpec((512, IO_TILE), lambda i: (0, i)),
            pl.BlockSpec((IO_TILE, LANES), lambda i: (i, 0)),
            pl.BlockSpec((IO_TILE, LANES), lambda i: (i, 0)),
            pl.BlockSpec((IO_TILE, LANES), lambda i: (i, 0)),
            pl.BlockSpec((LANES, IO_TILE), lambda i: (0, i)),
        ],
        out_shape=[
            jax.ShapeDtypeStruct((N_TOK, 512), BF16),
            jax.ShapeDtypeStruct((512, N_TOK), BF16),
            jax.ShapeDtypeStruct((N_TOK, LANES), F32),
            jax.ShapeDtypeStruct((N_TOK, LANES), BF16),
            jax.ShapeDtypeStruct((N_TOK, LANES), F32),
            jax.ShapeDtypeStruct((LANES, N_TOK), BF16),
        ],
        compiler_params=_cparams(("parallel",)),
        name="even_in",
    )(xc, xl, mod, n1, w_in, qg, kg, cos, sin, ws, bs_tab, bd)


def _odd_in_kernel(x_ref, mod_ref, n1_ref, w_ref, qa_ref, kva_ref, wuq_ref, wukv_ref, cos_ref, sin_ref,
                   qnt_ref, qrt_ref, ckv_ref, kr_ref, krb_ref, gb_ref, z_ref, kn_ref, vt_ref):
    h = _norm_mod(x_ref[...], n1_ref[...], mod_ref[0, 0, 1:2, :], mod_ref[0, 0, 0:1, :])
    p = _dot(h.astype(BF16), w_ref[...])
    cq = p[:, 0:256]
    ckv = p[:, 256:384]
    kr = p[:, 384:512]
    gb_ref[...] = p[:, 512:1024]
    z_ref[...] = p[:, 1024:1536] * p[:, 1536:2048]
    cqn = cq * lax.rsqrt(jnp.mean(cq * cq, axis=-1, keepdims=True) + EPS) * qa_ref[...]
    q = _dot(cqn.astype(BF16), wuq_ref[...])
    scale = (C_NOPE + C_ROPE) ** -0.5 * LOG2E
    cos = cos_ref[0]
    sin = sin_ref[0]
    qnt_ref[...] = (q[:, 0:512] * scale).T.astype(BF16)
    qr = _rope(q[:, 512:768], _tile_lanes(cos, 2), _tile_lanes(sin, 2), C_ROPE // 4)
    qrt_ref[...] = (qr * scale).T.astype(BF16)
    ckvn = ckv * lax.rsqrt(jnp.mean(ckv * ckv, axis=-1, keepdims=True) + EPS) * kva_ref[...]
    ckv_ref[...] = ckvn
    kv = _dot(ckvn.astype(BF16), wukv_ref[...])
    kn_ref[...] = kv[:, 0:512].astype(BF16)
    vt_ref[...] = kv[:, 512:1024].T.astype(BF16)
    krr = _rope(kr, cos, sin, C_ROPE // 4)
    kr_ref[...] = krr
    krb_ref[...] = krr.astype(BF16)


def _odd_in(x, mod, n1, w_in, qa, kva, wuq, wukv, cos, sin):
    full = lambda shape: pl.BlockSpec(shape, lambda i: (0,) * len(shape))
    tile = lambda w: pl.BlockSpec((IO_TILE, w), lambda i: (i, 0))
    return pl.pallas_call(
        _odd_in_kernel,
        grid=(NT_IO,),
        in_specs=[
            tile(D_MODEL),
            pl.BlockSpec((1, 1, 6, D_MODEL), lambda i: (1, _mod_row(i), 0, 0)),
            full((1, D_MODEL)),
            full(w_in.shape),
            full((1, C_Q_LORA)),
            full((1, C_KV_LORA)),
            full(wuq.shape),
            full(wukv.shape),
            pl.BlockSpec((1, IO_TILE, LANES), lambda i: (_tab_blk(i), 0, 0)),
            pl.BlockSpec((1, IO_TILE, LANES), lambda i: (_tab_blk(i), 0, 0)),
        ],
        out_specs=[pl.BlockSpec((512, IO_TILE), lambda i: (0, i)),
                   pl.BlockSpec((256, IO_TILE), lambda i: (0, i)),
                   tile(LANES), tile(LANES), tile(LANES), tile(512), tile(512),
                   tile(512), pl.BlockSpec((512, IO_TILE), lambda i: (0, i))],
        out_shape=[
            jax.ShapeDtypeStruct((512, N_TOK), BF16),
            jax.ShapeDtypeStruct((256, N_TOK), BF16),
            jax.ShapeDtypeStruct((N_TOK, LANES), F32),
            jax.ShapeDtypeStruct((N_TOK, LANES), F32),
            jax.ShapeDtypeStruct((N_TOK, LANES), BF16),
            jax.ShapeDtypeStruct((N_TOK, 512), F32),
            jax.ShapeDtypeStruct((N_TOK, 512), F32),
            jax.ShapeDtypeStruct((N_TOK, 512), BF16),
            jax.ShapeDtypeStruct((512, N_TOK), BF16),
        ],
        compiler_params=_cparams(("parallel",)),
        name="odd_in",
    )(x, mod, n1, w_in, qa, kva, wuq, wukv, cos, sin)


def _kvup_kernel(c_ref, w_ref, kn_ref, vt_ref):
    p = _dot(c_ref[...].astype(BF16), w_ref[...])
    kn_ref[...] = p[:, 0:512].astype(BF16)
    vt_ref[...] = p[:, 512:1024].T.astype(BF16)


def _kvup(ckv_all, w_ukv):
    n = ckv_all.shape[0]
    return pl.pallas_call(
        _kvup_kernel,
        grid=(n // TILE,),
        in_specs=[pl.BlockSpec((TILE, C_KV_LORA), lambda i: (i, 0)),
                  pl.BlockSpec(w_ukv.shape, lambda i: (0, 0))],
        out_specs=[pl.BlockSpec((TILE, 512), lambda i: (i, 0)), pl.BlockSpec((512, TILE), lambda i: (0, i))],
        out_shape=[jax.ShapeDtypeStruct((n, 512), BF16), jax.ShapeDtypeStruct((512, n), BF16)],
        compiler_params=_cparams(("parallel",)),
        name="mla_kv_up",
    )(ckv_all, w_ukv)


def _softmax_pv(qc, key_chunk, vt_chunks, nk):
    return _softmax_pv_streams([(qc, key_chunk, vt_chunks)], nk)[0]


def _softmax_pv_streams(streams, nk):
    ones = jnp.ones((BF16_ROWS, ATT_TK), BF16)
    ms = [jnp.full((1, qc.shape[1]), -jnp.inf, F32) for qc, _, _ in streams]
    accs = [jnp.zeros((HEAD_DIM + BF16_ROWS, qc.shape[1]), F32) for qc, _, _ in streams]
    nxt = [_dot(key_chunk(0), qc) for qc, key_chunk, _ in streams]
    for c in range(nk):
        for i, (qc, key_chunk, vt_chunks) in enumerate(streams):
            s = nxt[i]
            if c + 1 < nk:
                nxt[i] = _dot(key_chunk(c + 1), qc)
            mn = jnp.maximum(ms[i], jnp.max(s, axis=0, keepdims=True))
            p = jnp.exp2(s - mn).astype(BF16)
            vts = vt_chunks(c)
            if len(vts) == 1:
                pv = _dot(jnp.concatenate([vts[0], ones], axis=0), p)
            else:
                tq = qc.shape[1] // len(vts)
                pv = jnp.concatenate([_dot(jnp.concatenate([vt, ones], axis=0), p[:, k * tq:(k + 1) * tq])
                                      for k, vt in enumerate(vts)], axis=1)
            accs[i] = jnp.exp2(ms[i] - mn) * accs[i] + pv
            ms[i] = mn
    return [acc[0:HEAD_DIM, :] / acc[HEAD_DIM:HEAD_DIM + 1, :] for acc in accs]


def _chunk_locator(segments):
    counts = [seg[0].shape[0] // ATT_TK for seg in segments]

    def locate(c):
        for seg, n in zip(segments, counts):
            if c < n:
                return seg, slice(c * ATT_TK, (c + 1) * ATT_TK)
            c -= n
        raise IndexError(c)

    return locate, sum(counts)


def _gqa_kernel(q_ref, *refs):
    o_ref = refs[-1]
    locate, nk = _chunk_locator([refs[i:i + 2] for i in range(0, len(refs) - 1, 2)])
    kvh = pl.program_id(1)
    tq = q_ref.shape[1]
    n = B_HEADS // B_KV_HEADS
    cols = []
    for hh in range(n):
        q = q_ref[hh * HEAD_DIM:(hh + 1) * HEAD_DIM, :]
        z = jnp.zeros_like(q)
        cols.append(jnp.where(kvh == 0, jnp.concatenate([q, z], axis=0), jnp.concatenate([z, q], axis=0)))
    def key_chunk(c):
        (k_ref, _), rows = locate(c)
        return k_ref[rows, :]

    def vt_chunks(c):
        (_, vt_ref), cols_c = locate(c)
        return [vt_ref[:, cols_c]]

    per = n // GQA_STREAMS
    outs = _softmax_pv_streams(
        [(jnp.concatenate(cols[i * per:(i + 1) * per], axis=1), key_chunk, vt_chunks) for i in range(GQA_STREAMS)], nk)
    for hh in range(n):
        o = outs[hh // per][:, (hh % per) * tq:(hh % per + 1) * tq]
        o_ref[hh * HEAD_DIM:(hh + 1) * HEAD_DIM, :] = o.astype(BF16)


def _mla_kernel(qn_ref, qr_ref, *refs):
    o_ref = refs[-1]
    locate, nk = _chunk_locator([refs[i:i + 3] for i in range(0, len(refs) - 1, 3)])
    tq = qn_ref.shape[1]
    z = jnp.zeros((C_NOPE, tq), BF16)
    zr = jnp.zeros((LANES - C_ROPE, tq), BF16)
    streams = []
    for pair in range(MLA_PAIRS):
        cols = []
        for hh in range(2):
            h = 2 * pair + hh
            qn = qn_ref[h * C_NOPE:(h + 1) * C_NOPE, :]
            qr = qr_ref[h * C_ROPE:(h + 1) * C_ROPE, :]
            cols.append(jnp.concatenate(([qn, z] if hh == 0 else [z, qn]) + [qr, zr], axis=0))

        def key_chunk(c, pair=pair):
            (kn_ref, kr_ref, _), rows = locate(c)
            return jnp.concatenate([kn_ref[rows, pair * LANES:(pair + 1) * LANES], kr_ref[rows, :]], axis=1)

        def vt_chunks(c, pair=pair):
            (_, _, vt_ref), cols_c = locate(c)
            return [vt_ref[(2 * pair + hh) * C_V:(2 * pair + hh + 1) * C_V, cols_c] for hh in range(2)]

        streams.append((jnp.concatenate(cols, axis=1), key_chunk, vt_chunks))
    outs = _softmax_pv_streams(streams, nk)
    for pair in range(MLA_PAIRS):
        for hh in range(2):
            h = 2 * pair + hh
            o_ref[h * C_V:(h + 1) * C_V, :] = outs[pair][:, hh * tq:(hh + 1) * tq].astype(BF16)


def _attn_call(body, grid, in_specs, out_spec, args, n_tiles):
    return pl.pallas_call(
        body,
        grid=grid,
        in_specs=in_specs,
        out_specs=out_spec,
        out_shape=jax.ShapeDtypeStruct((512, n_tiles * TILE), BF16),
        compiler_params=_cparams(("parallel",) * len(grid)),
        name="attention",
    )(*args)


def _gqa_attention(qt, kb, vt, k_cache, vt_cache):
    assert NT_CTX * TILE == DEC_SEQ
    rows = (B_HEADS // B_KV_HEADS) * HEAD_DIM
    o_ctx = _attn_call(
        _gqa_kernel, (BATCH, B_KV_HEADS),
        [pl.BlockSpec((rows, TILE), lambda b, h: (h, b)),
         pl.BlockSpec((None, TILE, LANES), lambda b, h: (b, 0, 0)),
         pl.BlockSpec((HEAD_DIM, TILE), lambda b, h: (h, b))],
        pl.BlockSpec((rows, TILE), lambda b, h: (h, b)),
        [qt, kb.reshape(NT, TILE, LANES), vt], NT_CTX)
    o_lat = _attn_call(
        _gqa_kernel, (DEC_BATCH, B_KV_HEADS, NT_SEQ),
        [pl.BlockSpec((rows, TILE), lambda b, h, j: (h, NT_CTX + b * NT_SEQ + j)),
         pl.BlockSpec((None, PAST_LEN, LANES), lambda b, h, j: (b, 0, 0)),
         pl.BlockSpec((None, HEAD_DIM, PAST_LEN), lambda b, h, j: (b, h, 0)),
         pl.BlockSpec((DEC_SEQ, LANES), lambda b, h, j: (1 + b, 0)),
         pl.BlockSpec((HEAD_DIM, DEC_SEQ), lambda b, h, j: (h, 1 + b))],
        pl.BlockSpec((rows, TILE), lambda b, h, j: (h, b * NT_SEQ + j)),
        [qt, k_cache, vt_cache, kb, vt], NT_LAT)
    return o_ctx, o_lat


def _mla_attention(qnt, qrt, kn, krb, vt, kn_cache, kr_cache, vt_cache):
    nh = 2 * MLA_PAIRS
    steps = C_HEADS // nh
    o_ctx = _attn_call(
        _mla_kernel, (BATCH, steps),
        [pl.BlockSpec((nh * C_NOPE, TILE), lambda b, p: (p, b)),
         pl.BlockSpec((nh * C_ROPE, TILE), lambda b, p: (p, b)),
         pl.BlockSpec((TILE, MLA_PAIRS * LANES), lambda b, p: (b, p)),
         pl.BlockSpec((TILE, LANES), lambda b, p: (b, 0)),
         pl.BlockSpec((nh * C_V, TILE), lambda b, p: (p, b))],
        pl.BlockSpec((nh * C_V, TILE), lambda b, p: (p, b)),
        [qnt, qrt, kn, krb, vt], NT_CTX)
    o_lat = _attn_call(
        _mla_kernel, (DEC_BATCH, steps, NT_SEQ),
        [pl.BlockSpec((nh * C_NOPE, TILE), lambda b, p, j: (p, NT_CTX + b * NT_SEQ + j)),
         pl.BlockSpec((nh * C_ROPE, TILE), lambda b, p, j: (p, NT_CTX + b * NT_SEQ + j)),
         pl.BlockSpec((PAST_LEN, MLA_PAIRS * LANES), lambda b, p, j: (b, p)),
         pl.BlockSpec((None, PAST_LEN, LANES), lambda b, p, j: (b, 0, 0)),
         pl.BlockSpec((nh * C_V, PAST_LEN), lambda b, p, j: (p, b)),
         pl.BlockSpec((DEC_SEQ, MLA_PAIRS * LANES), lambda b, p, j: (1 + b, p)),
         pl.BlockSpec((DEC_SEQ, LANES), lambda b, p, j: (1 + b, 0)),
         pl.BlockSpec((nh * C_V, DEC_SEQ), lambda b, p, j: (p, 1 + b))],
        pl.BlockSpec((nh * C_V, TILE), lambda b, p, j: (p, b * NT_SEQ + j)),
        [qnt, qrt, kn_cache, kr_cache, vt_cache, kn, krb, vt], NT_LAT)
    return o_ctx, o_lat


def _group_tile(i, ctx_ref, lat_ref):
    return jnp.where(i < NT_CTX_IO, ctx_ref[...], lat_ref[...])


_CTX_COL = lambda i: (0, jnp.minimum(i, NT_CTX_IO - 1))
_LAT_COL = lambda i: (0, jnp.maximum(i - NT_CTX_IO, 0))
_CTX_ROW = lambda i: (jnp.minimum(i, NT_CTX_IO - 1), 0)
_LAT_ROW = lambda i: (jnp.maximum(i - NT_CTX_IO, 0), 0)


def _finish_out(o, x_in, mod_ref, n2_ref, wr_ref, xo_ref, h2_ref, lg_ref):
    x = x_in + mod_ref[0, 0, 2:3, :] * o
    xo_ref[...] = x
    h2 = _norm_mod(x, n2_ref[...], mod_ref[0, 0, 4:5, :], mod_ref[0, 0, 3:4, :])
    hi = h2.astype(BF16)
    lo = (h2 - hi.astype(F32)).astype(BF16)
    h2_ref[...] = hi
    parts = (_dot(hi, wr_ref[...]) + _dot(lo, wr_ref[...])).T
    lg_ref[...] = parts[0:N_EXPERTS, :] + parts[N_EXPERTS:2 * N_EXPERTS, :]


def _dot_t(at, b):
    return lax.dot_general(at, b, (((0,), (0,)), ((), ())), preferred_element_type=F32)


def _even_out_kernel(a_ref, btc_ref, btl_ref, w_ref, xc_ref, xl_ref, mod_ref, n2_ref, wr_ref,
                     xo_ref, h2_ref, lg_ref):
    i = pl.program_id(0)
    bt = _group_tile(i, btc_ref, btl_ref)
    o = _dot(a_ref[...], w_ref[0:512, :]) + _dot_t(bt, w_ref[512:1024, :])
    _finish_out(o, _group_tile(i, xc_ref, xl_ref), mod_ref, n2_ref, wr_ref, xo_ref, h2_ref, lg_ref)


def _odd_out_kernel(atc_ref, atl_ref, z_ref, zp_ref, zn_ref, gb_ref, cw_ref, w_ref, x_ref, mod_ref, n2_ref,
                    wr_ref, xo_ref, h2_ref, lg_ref):
    i = pl.program_id(0)
    at = _group_tile(i, atc_ref, atl_ref)
    ctx = i < NT_CTX_IO
    j = (i - NT_CTX_IO) % NT_SEQ_IO
    first = jnp.logical_or(ctx, j == 0)
    last = jnp.logical_or(ctx, j == NT_SEQ_IO - 1)
    z = z_ref[...]
    row = lax.broadcasted_iota(jnp.int32, z.shape, 0)
    halo_p = jnp.where(first, 0.0, zp_ref[7:8, :])
    halo_n = jnp.where(last, 0.0, zn_ref[0:1, :])
    zprev = jnp.where(row == 0, halo_p, pltpu.roll(z, 1, 0))
    znext = jnp.where(row == IO_TILE - 1, halo_n, pltpu.roll(z, IO_TILE - 1, 0))
    zprev = jnp.where(jnp.logical_and(ctx, row % SEQ == 0), 0.0, zprev)
    znext = jnp.where(jnp.logical_and(ctx, row % SEQ == SEQ - 1), 0.0, znext)
    y = zprev * cw_ref[0:1, :] + z * cw_ref[1:2, :] + znext * cw_ref[2:3, :]
    d = (gb_ref[...] * y).astype(BF16)
    o = _dot_t(at, w_ref[0:512, :]) + _dot(d, w_ref[512:1024, :])
    _finish_out(o, x_ref[...], mod_ref, n2_ref, wr_ref, xo_ref, h2_ref, lg_ref)


_OUT_SHAPES = [
    jax.ShapeDtypeStruct((N_TOK, D_MODEL), F32),
    jax.ShapeDtypeStruct((N_TOK, D_MODEL), BF16),
    jax.ShapeDtypeStruct((N_EXPERTS, N_TOK), F32),
]
_LOGIT_SPEC = pl.BlockSpec((N_EXPERTS, IO_TILE), lambda i: (0, i))


def _even_out(layer, a, b, w_out, xc, xl, mod, n2, wr):
    full = lambda shape: pl.BlockSpec(shape, lambda i: (0,) * len(shape))
    tile = lambda w: pl.BlockSpec((IO_TILE, w), lambda i: (i, 0))
    return pl.pallas_call(
        _even_out_kernel,
        grid=(NT_IO,),
        in_specs=[tile(512), pl.BlockSpec((512, IO_TILE), _CTX_COL), pl.BlockSpec((512, IO_TILE), _LAT_COL),
                  full(w_out.shape), pl.BlockSpec((IO_TILE, D_MODEL), _CTX_ROW),
                  pl.BlockSpec((IO_TILE, D_MODEL), _LAT_ROW),
                  pl.BlockSpec((1, 1, 6, D_MODEL), lambda i: (layer, _mod_row(i), 0, 0)),
                  full((1, D_MODEL)), full(wr.shape)],
        out_specs=[tile(D_MODEL), tile(D_MODEL), _LOGIT_SPEC],
        out_shape=_OUT_SHAPES,
        compiler_params=_cparams(("parallel",)),
        name="even_out",
    )(a, b[0], b[1], w_out, xc, xl, mod, n2, wr)


def _odd_out(layer, a, z, gb, cw, w_out, x, mod, n2, wr):
    full = lambda shape: pl.BlockSpec(shape, lambda i: (0,) * len(shape))
    tile = lambda w: pl.BlockSpec((IO_TILE, w), lambda i: (i, 0))
    rb = IO_TILE // 8
    return pl.pallas_call(
        _odd_out_kernel,
        grid=(NT_IO,),
        in_specs=[pl.BlockSpec((512, IO_TILE), _CTX_COL), pl.BlockSpec((512, IO_TILE), _LAT_COL), tile(512),
                  pl.BlockSpec((8, 512), lambda i: (jnp.maximum(i * rb - 1, 0), 0)),
                  pl.BlockSpec((8, 512), lambda i: (jnp.minimum(i * rb + rb, NT_IO * rb - 1), 0)),
                  tile(512), full(cw.shape), full(w_out.shape), tile(D_MODEL),
                  pl.BlockSpec((1, 1, 6, D_MODEL), lambda i: (layer, _mod_row(i), 0, 0)),
                  full((1, D_MODEL)), full(wr.shape)],
        out_specs=[tile(D_MODEL), tile(D_MODEL), _LOGIT_SPEC],
        out_shape=_OUT_SHAPES,
        compiler_params=_cparams(("parallel",)),
        name="odd_out",
    )(a[0], a[1], z, z, z, gb, cw, w_out, x, mod, n2, wr)


def _select_tokens(aff, n_dom, cap, tri, pos_ref, gate_ref):
    w = aff.shape[1] // n_dom
    doms = [aff[:, d * w:(d + 1) * w] for d in range(n_dom)]
    bits = [jnp.zeros((N_EXPERTS, 1), jnp.int32) for _ in range(n_dom)]
    for bit in range(30, -1, -1):
        for d in range(n_dom):
            cand = bits[d] | (1 << bit)
            cnt = jnp.sum((doms[d] >= lax.bitcast_convert_type(cand, F32)).astype(F32), axis=1, keepdims=True)
            bits[d] = jnp.where(cnt >= cap, cand, bits[d])
    for d in range(n_dom):
        thr = lax.bitcast_convert_type(bits[d], F32)
        gt = doms[d] > thr
        eq = doms[d] == thr
        need = cap - jnp.sum(gt.astype(F32), axis=1, keepdims=True)
        eq_seen = jnp.zeros((N_EXPERTS, 1), F32)
        sel_seen = jnp.zeros((N_EXPERTS, 1), F32) + d * cap
        for blk in range(w // TILE):
            cols = slice(blk * TILE, (blk + 1) * TILE)
            eq_b = eq[:, cols].astype(F32)
            eq_rank = _dot(eq_b.astype(BF16), tri) + eq_seen
            sel = jnp.logical_or(gt[:, cols], jnp.logical_and(eq[:, cols], eq_rank < need))
            sel_f = sel.astype(F32)
            pos = _dot(sel_f.astype(BF16), tri) + sel_seen
            tile = d * (w // TILE) + blk
            pos_ref[0, tile] = jnp.where(sel, pos, -1.0)
            gate_ref[0, tile] = jnp.where(sel, doms[d][:, cols], 0.0)
            eq_seen = eq_seen + jnp.sum(eq_b, axis=1, keepdims=True)
            sel_seen = sel_seen + jnp.sum(sel_f, axis=1, keepdims=True)


def _route_kernel(lg_ref, pos_ref, gate_ref):
    lg = lg_ref[...]
    ex = jnp.exp(lg - jnp.max(lg, axis=0, keepdims=True))
    aff = ex / jnp.sum(ex, axis=0, keepdims=True)
    r = lax.broadcasted_iota(jnp.int32, (TILE, TILE), 0)
    c = lax.broadcasted_iota(jnp.int32, (TILE, TILE), 1)
    tri = (r < c).astype(BF16)

    @pl.when(pl.program_id(0) == 0)
    def _():
        _select_tokens(aff, GROUP // SEQ, EC_CAPACITY * SEQ // N_EXPERTS, tri, pos_ref, gate_ref)

    @pl.when(pl.program_id(0) > 0)
    def _():
        _select_tokens(aff, GROUP // DEC_SEQ, EC_CAPACITY * DEC_SEQ // N_EXPERTS, tri, pos_ref, gate_ref)


def _route(logits_t):
    blk = pl.BlockSpec((1, GROUP_TILES, N_EXPERTS, TILE), lambda g: (g, 0, 0, 0))
    shape = jax.ShapeDtypeStruct((N_GROUPS, GROUP_TILES, N_EXPERTS, TILE), F32)
    return pl.pallas_call(
        _route_kernel,
        grid=(N_GROUPS,),
        in_specs=[pl.BlockSpec((N_EXPERTS, GROUP), lambda g: (0, g))],
        out_specs=[blk, blk],
        out_shape=[shape, shape],
        compiler_params=_cparams(("parallel",)),
        name="route",
    )(logits_t)


def _moe_kernel(nblk_ref, ctab_ref, etab_ref, llen_ref,
                x_ref, pos_ref, gate_ref, wg_ref, wu_ref, wd_ref, xres_ref, mod_ref, fn_ref, *rest, final):
    n_out = 2 if final else 1
    out_refs = rest[:n_out]
    lists_ref, gl_ref, s_ref, gs_ref, acc_ref, wgb_ref, wub_ref, wdb_ref = rest[n_out:]
    g = pl.program_id(0)
    s = pl.program_id(1)
    chunks = STACK_BLK // CHUNK_ROWS
    rows16 = lax.broadcasted_iota(jnp.int32, (CHUNK_ROWS, TILE), 0)

    @pl.when(jnp.logical_and(g == 0, s == 0))
    def _():
        lists_ref[ZERO_ROW:ZERO_ROW + CHUNK_ROWS, :] = jnp.zeros((CHUNK_ROWS, D_MODEL), BF16)

    def chunk_rows(tbase, q, unused_row):
        d = ctab_ref[tbase + q]
        return pl.multiple_of(jnp.where(d >= 0, d, unused_row), CHUNK_ROWS)

    def build_block(j, rb, with_gate, slot):
        tbase = ((g * GROUP_TILES + j) * MAX_STACK_BLKS + rb) * chunks
        for q in range(chunks):
            d = ctab_ref[tbase + q]
            e = etab_ref[tbase + q]
            posrow = pos_ref[0, j, pl.ds(e, 1), :]
            rowid = (rows16 + (rb * STACK_BLK + q * CHUNK_ROWS)).astype(F32)
            hit = jnp.logical_and(posrow == rowid, d >= 0)
            s_ref[slot, q * CHUNK_ROWS:(q + 1) * CHUNK_ROWS, :] = hit.astype(BF16)
            if with_gate:
                gaterow = gate_ref[0, j, pl.ds(e, 1), :]
                gs_ref[slot, q * CHUNK_ROWS:(q + 1) * CHUNK_ROWS, :] = jnp.where(hit, gaterow, 0.0)
        return tbase

    def for_blocks(nb, build, finish):
        def run(first, n):
            tbases = [build(first + k, k) for k in range(n)]
            for k in range(n):
                finish(tbases[k], k)

        def group(i, _):
            run(BLOCK_SLOTS * i, BLOCK_SLOTS)
            return 0

        lax.fori_loop(0, nb // BLOCK_SLOTS, group, 0)
        for left in range(1, BLOCK_SLOTS):
            @pl.when(nb % BLOCK_SLOTS == left)
            def _(left=left):
                run(nb - left, left)

    @pl.when(s < MOE_TILE_STEPS)
    def _gather():
        def tile(t, _):
            j = s * MOE_STEP_TILES + t
            tok = pl.multiple_of(t * TILE, TILE)

            def finish(tbase, slot):
                picked = _dot(s_ref[slot], x_ref[pl.ds(tok, TILE), :]).astype(BF16)
                gcol = jnp.sum(gs_ref[slot], axis=1, keepdims=True)
                for q in range(chunks):
                    d = chunk_rows(tbase, q, DUMP_ROW)
                    rows = slice(q * CHUNK_ROWS, (q + 1) * CHUNK_ROWS)
                    lists_ref[pl.ds(d, CHUNK_ROWS), :] = picked[rows, :]
                    gl_ref[pl.ds(d, CHUNK_ROWS), :] = gcol[rows, :]

            for_blocks(nblk_ref[g * GROUP_TILES + j], lambda rb, slot: build_block(j, rb, True, slot), finish)
            return 0

        lax.fori_loop(0, MOE_STEP_TILES, tile, 0)

    @pl.when(jnp.logical_and(s >= MOE_TILE_STEPS, s < MOE_TILE_STEPS + N_EXPERTS))
    def _experts():
        e = s - MOE_TILE_STEPS
        ln = llen_ref[g * N_EXPERTS + e]
        base = e * LIST_MAX
        half = FFN_BLK // 2
        rem = ln % FFN_BLK
        use_half = jnp.logical_and(rem > 0, rem <= half)
        nfull = ln // FFN_BLK + jnp.where(rem > half, 1, 0)
        end = nfull * FFN_BLK + jnp.where(use_half, half, 0)

        def zero_tail(k, _):
            r0 = pl.multiple_of(base + ln + k * CHUNK_ROWS, CHUNK_ROWS)
            lists_ref[pl.ds(r0, CHUNK_ROWS), :] = jnp.zeros((CHUNK_ROWS, D_MODEL), BF16)
            gl_ref[pl.ds(r0, CHUNK_ROWS), :] = jnp.zeros((CHUNK_ROWS, 1), F32)
            return 0

        lax.fori_loop(0, (end - ln) // CHUNK_ROWS, zero_tail, 0)
        wgb_ref[...] = wg_ref[0, 0].astype(BF16)
        wub_ref[...] = wu_ref[0, 0].astype(BF16)
        wdb_ref[...] = wd_ref[0, 0].astype(BF16)

        def ffn_rows(r0, n):
            xs = lists_ref[pl.ds(r0, n), :]
            hid = _silu(_dot(xs, wgb_ref[...])) * _dot(xs, wub_ref[...])
            y = _dot(hid.astype(BF16), wdb_ref[...]) * gl_ref[pl.ds(r0, n), :]
            lists_ref[pl.ds(r0, n), :] = y.astype(BF16)

        def ffn(c, _):
            ffn_rows(pl.multiple_of(base + c * FFN_BLK, FFN_BLK), FFN_BLK)
            return 0

        lax.fori_loop(0, nfull, ffn, 0)

        @pl.when(use_half)
        def _():
            ffn_rows(pl.multiple_of(base + nfull * FFN_BLK, half), half)

    @pl.when(s >= MOE_TILE_STEPS + N_EXPERTS)
    def _scatter():
        def tile(t, _):
            j = (s - MOE_TILE_STEPS - N_EXPERTS) * SCATTER_STEP_TILES + t
            rows = pl.ds(pl.multiple_of(t * TILE, TILE), TILE)
            acc_ref[...] = jnp.zeros_like(acc_ref)

            def finish(tbase, slot):
                y = jnp.concatenate([lists_ref[pl.ds(chunk_rows(tbase, q, ZERO_ROW), CHUNK_ROWS), :]
                                     for q in range(chunks)], axis=0)
                acc_ref[...] += _dot_t(s_ref[slot], y)

            for_blocks(nblk_ref[g * GROUP_TILES + j], lambda rb, slot: build_block(j, rb, False, slot), finish)
            x = xres_ref[rows, :] + mod_ref[0, 0, 5:6, :] * acc_ref[...]
            if final:
                y = x * lax.rsqrt(jnp.mean(x * x, axis=-1, keepdims=True) + EPS) * fn_ref[...]

                @pl.when(g == 0)
                def _():
                    out_refs[0][rows, :] = y

                @pl.when(g > 0)
                def _():
                    out_refs[1][rows, :] = y
            else:
                out_refs[0][rows, :] = x
            return 0

        lax.fori_loop(0, SCATTER_STEP_TILES, tile, 0)


def _moe(layer, final, tables, h2, pos_t, gate_t, wg, wu, wd, xres, mod, fn):
    gt = GROUP_TILES
    ts = MOE_TILE_STEPS
    sc = SCATTER_STEPS
    rows_a = MOE_STEP_TILES * TILE
    rows = SCATTER_STEP_TILES * TILE
    step_c = lambda s: jnp.clip(s - ts - N_EXPERTS, 0, sc - 1)
    tile_a = lambda g, s, *_: (g * ts + jnp.minimum(s, ts - 1), 0)
    tile_c = lambda g, s, *_: (g * sc + step_c(s), 0)
    expert = lambda g, s, *_: (layer, jnp.clip(s - ts, 0, N_EXPERTS - 1), 0, 0)
    group = lambda g, s, *_: (g, 0, 0, 0)
    if final:
        out_specs = [
            pl.BlockSpec((rows, D_MODEL), lambda g, s, *_: (jnp.where(g == 0, step_c(s), sc - 1), 0)),
            pl.BlockSpec((rows, D_MODEL), lambda g, s, *_: (jnp.where(g == 0, 0, (g - 1) * sc + step_c(s)), 0)),
        ]
        out_shape = [jax.ShapeDtypeStruct((NT_CTX * TILE, D_MODEL), F32),
                     jax.ShapeDtypeStruct((NT_LAT * TILE, D_MODEL), F32)]
    else:
        out_specs = pl.BlockSpec((rows, D_MODEL), tile_c)
        out_shape = jax.ShapeDtypeStruct((N_TOK, D_MODEL), F32)
    grid_spec = pltpu.PrefetchScalarGridSpec(
        num_scalar_prefetch=4,
        grid=(N_GROUPS, ts + N_EXPERTS + sc),
        in_specs=[
            pl.BlockSpec((rows_a, D_MODEL), tile_a),
            pl.BlockSpec((1, gt, N_EXPERTS, TILE), group),
            pl.BlockSpec((1, gt, N_EXPERTS, TILE), group),
            pl.BlockSpec((1, 1, D_MODEL, EXPERT_FF), expert),
            pl.BlockSpec((1, 1, D_MODEL, EXPERT_FF), expert),
            pl.BlockSpec((1, 1, EXPERT_FF, D_MODEL), expert),
            pl.BlockSpec((rows, D_MODEL), tile_c),
            pl.BlockSpec((1, 1, 6, D_MODEL), lambda g, s, *_: (layer, g, 0, 0)),
            pl.BlockSpec((1, D_MODEL), lambda g, s, *_: (0, 0)),
        ],
        out_specs=out_specs,
        scratch_shapes=[
            pltpu.VMEM((LIST_ROWS, D_MODEL), BF16),
            pltpu.VMEM((LIST_ROWS, 1), F32),
            pltpu.VMEM((BLOCK_SLOTS, STACK_BLK, TILE), BF16),
            pltpu.VMEM((BLOCK_SLOTS, STACK_BLK, TILE), F32),
            pltpu.VMEM((TILE, D_MODEL), F32),
            pltpu.VMEM((D_MODEL, EXPERT_FF), BF16),
            pltpu.VMEM((D_MODEL, EXPERT_FF), BF16),
            pltpu.VMEM((EXPERT_FF, D_MODEL), BF16),
        ],
    )
    return pl.pallas_call(
        functools.partial(_moe_kernel, final=final),
        grid_spec=grid_spec,
        out_shape=out_shape,
        compiler_params=_cparams(("arbitrary", "arbitrary")),
        name="moe",
    )(*tables, h2, pos_t, gate_t, wg, wu, wd, xres, mod, fn)


def _rope_tables(rot_dim):
    axis_dim = rot_dim // 2
    tok = jnp.arange(DEC_SEQ)
    rows = (tok // GRID_W).astype(F32)
    cols = (tok % GRID_W).astype(F32)
    inv_freq = ROPE_THETA ** (-jnp.arange(0, axis_dim, 2, dtype=F32) / axis_dim)
    ar = rows[:, None] * inv_freq
    ac = cols[:, None] * inv_freq
    cos = jnp.concatenate([jnp.cos(ar), jnp.cos(ar), jnp.cos(ac), jnp.cos(ac)], axis=1)
    sin = jnp.concatenate([-jnp.sin(ar), jnp.sin(ar), -jnp.sin(ac), jnp.sin(ac)], axis=1)
    rep = LANES // rot_dim
    cos = jnp.tile(cos, (1, rep)).reshape(NT_SEQ_IO, IO_TILE, LANES)
    sin = jnp.tile(sin, (1, rep)).reshape(NT_SEQ_IO, IO_TILE, LANES)
    cos = jnp.concatenate([jnp.ones((1, IO_TILE, LANES), F32), cos], axis=0)
    sin = jnp.concatenate([jnp.zeros((1, IO_TILE, LANES), F32), sin], axis=0)
    return cos, sin


def _excl_cumsum(a, axis):
    return jnp.cumsum(a, axis=axis) - a


def _route_and_moe(layer, final, h2, logits, wg, wu, wd, xres, mod, fn):
    pos_t, gate_t = _route(logits)
    sel = pos_t >= 0
    cnt = jnp.sum(sel, axis=-1, dtype=jnp.int32)
    npad = (cnt + CHUNK_ROWS - 1) // CHUNK_ROWS * CHUNK_ROWS
    seg_off = _excl_cumsum(npad, 2)
    list_off = _excl_cumsum(npad, 1)
    rank0 = _excl_cumsum(cnt, 1)
    stack_pos = jnp.where(sel, pos_t - rank0[..., None].astype(F32) + seg_off[..., None].astype(F32), -1.0)
    nblk = (jnp.sum(npad, axis=2) + STACK_BLK - 1) // STACK_BLK
    llen = jnp.sum(npad, axis=1)
    r = (jnp.arange(MAX_STACK_BLKS * STACK_BLK // CHUNK_ROWS, dtype=jnp.int32) * CHUNK_ROWS)[None, None, None, :]
    in_seg = jnp.logical_and(r >= seg_off[..., None], r < (seg_off + npad)[..., None])
    eidx = jnp.arange(N_EXPERTS, dtype=jnp.int32)[None, None, :, None]
    dst = jnp.sum(jnp.where(in_seg, eidx * LIST_MAX + list_off[..., None] + r - seg_off[..., None], 0), axis=2)
    used = jnp.any(in_seg, axis=2)
    ctab = jnp.where(used, dst, -1).astype(jnp.int32)
    etab = jnp.sum(jnp.where(in_seg, eidx, 0), axis=2).astype(jnp.int32)
    tables = (nblk.reshape(-1).astype(jnp.int32), ctab.reshape(-1), etab.reshape(-1),
              llen.reshape(-1).astype(jnp.int32))
    return _moe(layer, final, tables, h2, stack_pos, gate_t, wg, wu, wd, xres, mod, fn)


def _split_router(w):
    hi = w.astype(BF16)
    lo = (w - hi.astype(F32)).astype(BF16)
    pad = jnp.zeros((D_MODEL, LANES - 2 * N_EXPERTS), BF16)
    return jnp.concatenate([hi, lo, pad], axis=1)


def kernel(x_prompt, x_sample, c, cache_attn_k, cache_attn_v, cache_mla_ckv, cache_mla_krope, c_ctx, w_mod, b_mod,
           norm1, norm2, ev_w_in, ev_q_norm, ev_k_norm, ev_w_s, ev_b_s, ev_w_out, od_w_in, od_q_a_norm,
           od_kv_a_norm, od_w_uq, od_w_ukv, od_conv_w, od_w_out, moe_router, moe_w_gate, moe_w_up, moe_w_down,
           final_norm):
    nctx = NT_CTX * TILE
    xc = x_prompt.reshape(nctx, D_MODEL)
    xl = x_sample.reshape(NT_LAT * TILE, D_MODEL)
    cvec = jnp.concatenate([c_ctx[None], c, jnp.zeros((3, D_MODEL), F32)], axis=0)
    mod = _modulation(cvec, w_mod, b_mod).reshape(2, 8, 6, D_MODEL)

    cos64, sin64 = _rope_tables(HEAD_DIM)
    seg = jnp.arange(512) // HEAD_DIM
    bd = (seg[:, None] == seg[None, :]).astype(BF16)
    bs_tab = jnp.repeat(ev_b_s[0].T, 64, axis=1)
    gated, qt, k, kb, v, vt = _even_in(
        xc, xl, mod, norm1[0:1], ev_w_in[0].astype(BF16), jnp.tile(ev_q_norm[0], 8)[None], jnp.tile(ev_k_norm[0], 2)[None],
        cos64, sin64, jnp.concatenate([ev_w_s[0, 0::2], ev_w_s[0, 1::2]], axis=2).astype(BF16), bs_tab, bd)
    new_k = k[:nctx].reshape(BATCH, SEQ, B_KV_HEADS, HEAD_DIM).transpose(0, 2, 1, 3)[:, None]
    new_v = v[:nctx].reshape(BATCH, SEQ, B_KV_HEADS, HEAD_DIM).transpose(0, 2, 1, 3)[:, None]
    ck = cache_attn_k[:, 0].transpose(0, 2, 1, 3).reshape(DEC_BATCH, PAST_LEN, LANES).astype(BF16)
    cv = cache_attn_v[:, 0].transpose(0, 1, 3, 2).reshape(DEC_BATCH, LANES, PAST_LEN).astype(BF16)
    attn_t = _gqa_attention(qt, kb, vt, ck, cv)
    x1, h2, logits = _even_out(0, gated, attn_t, ev_w_out[0].astype(BF16), xc, xl, mod, norm2[0:1],
                               _split_router(moe_router[0]))
    x2 = _route_and_moe(0, False, h2, logits, moe_w_gate, moe_w_up, moe_w_down, x1, mod, final_norm[None])

    cos32, sin32 = _rope_tables(C_ROPE)
    w_in1 = od_w_in[0]
    w_in1 = jnp.concatenate([w_in1[:, :C_Q_LORA + C_KV_LORA + C_ROPE], jnp.zeros((D_MODEL, LANES - C_ROPE), F32),
                             w_in1[:, C_Q_LORA + C_KV_LORA + C_ROPE:]], axis=1).astype(BF16)
    wuq = od_w_uq[0].reshape(C_Q_LORA, C_HEADS, C_NOPE + C_ROPE)
    wuq = jnp.concatenate([wuq[:, :, :C_NOPE].reshape(C_Q_LORA, -1), wuq[:, :, C_NOPE:].reshape(C_Q_LORA, -1)],
                          axis=1).astype(BF16)
    wukv = od_w_ukv[0].reshape(C_KV_LORA, C_HEADS, C_NOPE + C_V)
    wukv = jnp.concatenate([wukv[:, :, :C_NOPE].reshape(C_KV_LORA, -1), wukv[:, :, C_NOPE:].reshape(C_KV_LORA, -1)],
                           axis=1).astype(BF16)
    qnt, qrt, ckv, kr, krb, gb, z, kn, vt1 = _odd_in(x2, mod, norm1[1:2], w_in1, od_q_a_norm[0][None],
                                                     od_kv_a_norm[0][None], wuq, wukv, cos32, sin32)
    new_ckv = ckv[:nctx].reshape(BATCH, 1, SEQ, C_KV_LORA)
    new_kr = kr[:nctx, :C_ROPE].reshape(BATCH, 1, SEQ, C_ROPE)
    kn_cache, vt_cache = _kvup(cache_mla_ckv[:, 0].reshape(DEC_BATCH * PAST_LEN, C_KV_LORA), wukv)
    ckr = jnp.pad(cache_mla_krope[:, 0], ((0, 0), (0, 0), (0, LANES - C_ROPE))).astype(BF16)
    attn_t = _mla_attention(qnt, qrt, kn, krb, vt1, kn_cache, ckr, vt_cache)
    cw = jnp.concatenate([od_conv_w[0], jnp.zeros((5, D_WIDTH), F32)], axis=0)
    x3, h2, logits = _odd_out(1, attn_t, z, gb, cw, od_w_out[0].astype(BF16), x2, mod, norm2[1:2],
                              _split_router(moe_router[1]))
    y_c, y_l = _route_and_moe(1, True, h2, logits, moe_w_gate, moe_w_up, moe_w_down, x3, mod, final_norm[None])
    y_prompt = y_c.reshape(BATCH, SEQ, D_MODEL)
    y_sample = y_l.reshape(DEC_BATCH, DEC_SEQ, D_MODEL)
    return y_prompt, y_sample, new_k, new_v, new_ckv, new_kr
```

```python
import functools
import math

import jax
import jax.numpy as jnp
from jax import lax
from jax.experimental import pallas as pl
from jax.experimental.pallas import tpu as pltpu

F32 = jnp.float32
BF16 = jnp.bfloat16

D_MODEL = 1024
BATCH = 16
SEQ = 256
DEC_BATCH = 4
DEC_SEQ = 4096
PAST_LEN = 512
GRID_W = 64
ROPE_THETA = 10000.0
EPS = 1e-6
HEAD_DIM = 64
CHUNK = 128
A_GROUPS = 8
A_WIDTH = 512
B_HEADS = 8
B_KV_HEADS = 2
B_WIDTH = 512
C_HEADS = 8
C_NOPE = 64
C_ROPE = 32
C_V = 64
C_Q_LORA = 256
C_KV_LORA = 128
D_WIDTH = 512
N_EXPERTS = 16
EXPERT_FF = 512
EC_CAPACITY = 2

LOG2E = math.log2(math.e)
LANES = 128
BF16_ROWS = 16
ATT_TK = 256
MLA_PAIRS = 2
GQA_STREAMS = 2
TILE = 256
NT_CTX = BATCH * SEQ // TILE
NT_SEQ = DEC_SEQ // TILE
NT_LAT = DEC_BATCH * NT_SEQ
NT = NT_CTX + NT_LAT
N_TOK = NT * TILE
IO_TILE = 1024
NT_CTX_IO = BATCH * SEQ // IO_TILE
NT_SEQ_IO = DEC_SEQ // IO_TILE
NT_IO = N_TOK // IO_TILE
OUT_SUBTILES = 4
GROUP = 4096
N_GROUPS = N_TOK // GROUP
GROUP_TILES = GROUP // TILE
GROUP_CAP = EC_CAPACITY * GROUP // N_EXPERTS
CHUNK_ROWS = BF16_ROWS
STACK_BLK = 256
MAX_STACK_BLKS = -(-N_EXPERTS * TILE // STACK_BLK)
MOE_STEP_TILES = 4
MOE_TILE_STEPS = GROUP_TILES // MOE_STEP_TILES
SCATTER_STEP_TILES = 1
SCATTER_STEPS = GROUP_TILES // SCATTER_STEP_TILES
BLOCK_SLOTS = 3
FFN_BLK = 256
LIST_MAX = -(-(GROUP_CAP + GROUP_TILES * (CHUNK_ROWS - 1)) // FFN_BLK) * FFN_BLK
DUMP_ROW = N_EXPERTS * LIST_MAX
ZERO_ROW = DUMP_ROW + CHUNK_ROWS
LIST_ROWS = ZERO_ROW + CHUNK_ROWS
VMEM_LIMIT = 60 * 1024 * 1024


def _cparams(sem):
    return pltpu.CompilerParams(dimension_semantics=sem, vmem_limit_bytes=VMEM_LIMIT)


def _dot(a, b):
    return jnp.dot(a, b, preferred_element_type=F32)


def _silu(x):
    return x / (1.0 + jnp.exp(-x))


def _mod_row(i):
    return jnp.where(i < NT_CTX_IO, 0, 1 + (i - NT_CTX_IO) // NT_SEQ_IO)


def _tab_blk(i):
    return jnp.where(i < NT_CTX_IO, 0, 1 + (i - NT_CTX_IO) % NT_SEQ_IO)


def _mod_kernel(c_ref, w_ref, b_ref, o_ref):
    s = _silu(c_ref[...])
    o_ref[0] = _dot(s.astype(BF16), w_ref[0].astype(BF16)) + b_ref[0]


def _modulation(cvec, w_mod, b_mod):
    depth = w_mod.shape[0]
    nchunk = 6
    return pl.pallas_call(
        _mod_kernel,
        grid=(depth, nchunk),
        in_specs=[
            pl.BlockSpec((8, D_MODEL), lambda l, k: (0, 0)),
            pl.BlockSpec((1, D_MODEL, D_MODEL), lambda l, k: (l, 0, k)),
            pl.BlockSpec((1, 1, D_MODEL), lambda l, k: (l, 0, k)),
        ],
        out_specs=pl.BlockSpec((1, 8, D_MODEL), lambda l, k: (l, 0, k)),
        out_shape=jax.ShapeDtypeStruct((depth, 8, 6 * D_MODEL), F32),
        compiler_params=_cparams(("arbitrary", "arbitrary")),
        name="modulation",
    )(cvec, w_mod, b_mod.reshape(depth, 1, 6 * D_MODEL))


def _norm_mod(x, gain, scale, shift):
    ms = jnp.mean(x * x, axis=-1, keepdims=True)
    return (x * lax.rsqrt(ms + EPS) * gain) * (1.0 + scale) + shift


def _seg_mean_sq(z, bd, width):
    zz = z * z
    hi = zz.astype(BF16)
    lo = (zz - hi.astype(F32)).astype(BF16)
    return (_dot(hi, bd) + _dot(lo, bd)) * (1.0 / width)


def _rope(z, cos, sin_signed, half):
    w = z.shape[1]
    lane = lax.broadcasted_iota(jnp.int32, z.shape, 1)
    first = (lane % (2 * half)) < half
    partner = jnp.where(first, pltpu.roll(z, w - half, 1), pltpu.roll(z, half, 1))
    return z * cos + partner * sin_signed


def _tile_lanes(t, n):
    return jnp.concatenate([t] * n, axis=1) if n > 1 else t


def _even_in_kernel(xc_ref, xl_ref, mod_ref, n1_ref, w_ref, qg_ref, kg_ref, cos_ref, sin_ref, ws_ref, bs_ref, bd_ref,
                    gated_ref, qt_ref, k_ref, kb_ref, v_ref, vt_ref):
    x = _group_tile(pl.program_id(0), xc_ref, xl_ref)
    h = _norm_mod(x, n1_ref[...], mod_ref[0, 0, 1:2, :], mod_ref[0, 0, 0:1, :])
    p = _dot(h.astype(BF16), w_ref[...])
    u = p[:, 0:512]
    va = p[:, 512:1024].astype(BF16)
    q = p[:, 1024:1536]
    k = p[:, 1536:1664]
    v = p[:, 1664:1792]
    bd = bd_ref[...]
    cos = cos_ref[0]
    sin = sin_ref[0]
    qn = q * lax.rsqrt(_seg_mean_sq(q, bd, HEAD_DIM) + EPS) * qg_ref[...]
    kn = k * lax.rsqrt(_seg_mean_sq(k, bd[0:LANES, 0:LANES], HEAD_DIM) + EPS) * kg_ref[...]
    qr = _rope(qn, _tile_lanes(cos, 4), _tile_lanes(sin, 4), HEAD_DIM // 4)
    kr = _rope(kn, cos, sin, HEAD_DIM // 4)
    qt_ref[...] = (qr * (HEAD_DIM ** -0.5 * LOG2E)).T.astype(BF16)
    k_ref[...] = kr
    kb_ref[...] = kr.astype(BF16)
    v_ref[...] = v
    vt_ref[...] = v.T.astype(BF16)
    lane = lax.broadcasted_iota(jnp.int32, (CHUNK, LANES), 1)
    for ch in range(IO_TILE // CHUNK):
        rows = slice(ch * CHUNK, (ch + 1) * CHUNK)
        cols = []
        for pair in range(A_GROUPS // 2):
            vp = va[rows, pair * LANES:(pair + 1) * LANES]
            zero = jnp.zeros_like(vp)
            rhs = jnp.concatenate([jnp.where(lane < LANES // 2, vp, zero), jnp.where(lane < LANES // 2, zero, vp)],
                                  axis=0)
            cols.append(_dot(ws_ref[pair], rhs))
        s = jnp.concatenate(cols, axis=1) + bs_ref[...]
        gated_ref[rows, :] = (u[rows, :] * s).astype(BF16)


def _even_in(xc, xl, mod, n1, w_in, qg, kg, cos, sin, ws, bs_tab, bd):
    full = lambda shape: pl.BlockSpec(shape, lambda i: (0,) * len(shape))
    return pl.pallas_call(
        _even_in_kernel,
        grid=(NT_IO,),
        in_specs=[
            pl.BlockSpec((IO_TILE, D_MODEL), _CTX_ROW),
            pl.BlockSpec((IO_TILE, D_MODEL), _LAT_ROW),
            pl.BlockSpec((1, 1, 6, D_MODEL), lambda i: (0, _mod_row(i), 0, 0)),
            full((1, D_MODEL)),
            full(w_in.shape),
            full((1, 512)),
            full((1, LANES)),
            pl.BlockSpec((1, IO_TILE, LANES), lambda i: (_tab_blk(i), 0, 0)),
            pl.BlockSpec((1, IO_TILE, LANES), lambda i: (_tab_blk(i), 0, 0)),
            full(ws.shape),
            full(bs_tab.shape),
            full(bd.shape),
        ],
        out_specs=[
            pl.BlockSpec((IO_TILE, 512), lambda i: (i, 0)),
            pl.BlockSpec((512, IO_TILE), lambda i: (0, i)),
            pl.BlockSpec((IO_TILE, LANES), lambda i: (i, 0)),
            pl.BlockSpec((IO_TILE, LANES), lambda i: (i, 0)),
            pl.BlockSpec((IO_TILE, LANES), lambda i: (i, 0)),
            pl.BlockSpec((LANES, IO_TILE), lambda i: (0, i)),
        ],
        out_shape=[
            jax.ShapeDtypeStruct((N_TOK, 512), BF16),
            jax.ShapeDtypeStruct((512, N_TOK), BF16),
            jax.ShapeDtypeStruct((N_TOK, LANES), F32),
            jax.ShapeDtypeStruct((N_TOK, LANES), BF16),
            jax.ShapeDtypeStruct((N_TOK, LANES), F32),
            jax.ShapeDtypeStruct((LANES, N_TOK), BF16),
        ],
        compiler_params=_cparams(("parallel",)),
        name="even_in",
    )(xc, xl, mod, n1, w_in, qg, kg, cos, sin, ws, bs_tab, bd)


def _odd_in_kernel(x_ref, mod_ref, n1_ref, w_ref, qa_ref, kva_ref, wuq_ref, wukv_ref, cos_ref, sin_ref,
                   qnt_ref, qrt_ref, ckv_ref, kr_ref, krb_ref, gb_ref, z_ref, kn_ref, vt_ref):
    h = _norm_mod(x_ref[...], n1_ref[...], mod_ref[0, 0, 1:2, :], mod_ref[0, 0, 0:1, :])
    p = _dot(h.astype(BF16), w_ref[...])
    cq = p[:, 0:256]
    ckv = p[:, 256:384]
    kr = p[:, 384:512]
    gb_ref[...] = p[:, 512:1024]
    z_ref[...] = p[:, 1024:1536] * p[:, 1536:2048]
    cqn = cq * lax.rsqrt(jnp.mean(cq * cq, axis=-1, keepdims=True) + EPS) * qa_ref[...]
    q = _dot(cqn.astype(BF16), wuq_ref[...])
    scale = (C_NOPE + C_ROPE) ** -0.5 * LOG2E
    cos = cos_ref[0]
    sin = sin_ref[0]
    qnt_ref[...] = (q[:, 0:512] * scale).T.astype(BF16)
    qr = _rope(q[:, 512:768], _tile_lanes(cos, 2), _tile_lanes(sin, 2), C_ROPE // 4)
    qrt_ref[...] = (qr * scale).T.astype(BF16)
    ckvn = ckv * lax.rsqrt(jnp.mean(ckv * ckv, axis=-1, keepdims=True) + EPS) * kva_ref[...]
    ckv_ref[...] = ckvn
    kv = _dot(ckvn.astype(BF16), wukv_ref[...])
    kn_ref[...] = kv[:, 0:512].astype(BF16)
    vt_ref[...] = kv[:, 512:1024].T.astype(BF16)
    krr = _rope(kr, cos, sin, C_ROPE // 4)
    kr_ref[...] = krr
    krb_ref[...] = krr.astype(BF16)


def _odd_in(x, mod, n1, w_in, qa, kva, wuq, wukv, cos, sin):
    full = lambda shape: pl.BlockSpec(shape, lambda i: (0,) * len(shape))
    tile = lambda w: pl.BlockSpec((IO_TILE, w), lambda i: (i, 0))
    return pl.pallas_call(
        _odd_in_kernel,
        grid=(NT_IO,),
        in_specs=[
            tile(D_MODEL),
            pl.BlockSpec((1, 1, 6, D_MODEL), lambda i: (1, _mod_row(i), 0, 0)),
            full((1, D_MODEL)),
            full(w_in.shape),
            full((1, C_Q_LORA)),
            full((1, C_KV_LORA)),
            full(wuq.shape),
            full(wukv.shape),
            pl.BlockSpec((1, IO_TILE, LANES), lambda i: (_tab_blk(i), 0, 0)),
            pl.BlockSpec((1, IO_TILE, LANES), lambda i: (_tab_blk(i), 0, 0)),
        ],
        out_specs=[pl.BlockSpec((512, IO_TILE), lambda i: (0, i)),
                   pl.BlockSpec((256, IO_TILE), lambda i: (0, i)),
                   tile(LANES), tile(LANES), tile(LANES), tile(512), tile(512),
                   tile(512), pl.BlockSpec((512, IO_TILE), lambda i: (0, i))],
        out_shape=[
            jax.ShapeDtypeStruct((512, N_TOK), BF16),
            jax.ShapeDtypeStruct((256, N_TOK), BF16),
            jax.ShapeDtypeStruct((N_TOK, LANES), F32),
            jax.ShapeDtypeStruct((N_TOK, LANES), F32),
            jax.ShapeDtypeStruct((N_TOK, LANES), BF16),
            jax.ShapeDtypeStruct((N_TOK, 512), F32),
            jax.ShapeDtypeStruct((N_TOK, 512), F32),
            jax.ShapeDtypeStruct((N_TOK, 512), BF16),
            jax.ShapeDtypeStruct((512, N_TOK), BF16),
        ],
        compiler_params=_cparams(("parallel",)),
        name="odd_in",
    )(x, mod, n1, w_in, qa, kva, wuq, wukv, cos, sin)


def _kvup_kernel(c_ref, w_ref, kn_ref, vt_ref):
    p = _dot(c_ref[...].astype(BF16), w_ref[...])
    kn_ref[...] = p[:, 0:512].astype(BF16)
    vt_ref[...] = p[:, 512:1024].T.astype(BF16)


def _kvup(ckv_all, w_ukv):
    n = ckv_all.shape[0]
    return pl.pallas_call(
        _kvup_kernel,
        grid=(n // TILE,),
        in_specs=[pl.BlockSpec((TILE, C_KV_LORA), lambda i: (i, 0)),
                  pl.BlockSpec(w_ukv.shape, lambda i: (0, 0))],
        out_specs=[pl.BlockSpec((TILE, 512), lambda i: (i, 0)), pl.BlockSpec((512, TILE), lambda i: (0, i))],
        out_shape=[jax.ShapeDtypeStruct((n, 512), BF16), jax.ShapeDtypeStruct((512, n), BF16)],
        compiler_params=_cparams(("parallel",)),
        name="mla_kv_up",
    )(ckv_all, w_ukv)


def _softmax_pv(qc, key_chunk, vt_chunks, nk):
    return _softmax_pv_streams([(qc, key_chunk, vt_chunks)], nk)[0]


def _softmax_pv_streams(streams, nk):
    ones = jnp.ones((BF16_ROWS, ATT_TK), BF16)
    ms = [jnp.full((1, qc.shape[1]), -jnp.inf, F32) for qc, _, _ in streams]
    accs = [jnp.zeros((HEAD_DIM + BF16_ROWS, qc.shape[1]), F32) for qc, _, _ in streams]
    nxt = [_dot(key_chunk(0), qc) for qc, key_chunk, _ in streams]
    for c in range(nk):
        for i, (qc, key_chunk, vt_chunks) in enumerate(streams):
            s = nxt[i]
            if c + 1 < nk:
                nxt[i] = _dot(key_chunk(c + 1), qc)
            mn = jnp.maximum(ms[i], jnp.max(s, axis=0, keepdims=True))
            p = jnp.exp2(s - mn).astype(BF16)
            vts = vt_chunks(c)
            if len(vts) == 1:
                pv = _dot(jnp.concatenate([vts[0], ones], axis=0), p)
            else:
                tq = qc.shape[1] // len(vts)
                pv = jnp.concatenate([_dot(jnp.concatenate([vt, ones], axis=0), p[:, k * tq:(k + 1) * tq])
                                      for k, vt in enumerate(vts)], axis=1)
            accs[i] = jnp.exp2(ms[i] - mn) * accs[i] + pv
            ms[i] = mn
    return [acc[0:HEAD_DIM, :] / acc[HEAD_DIM:HEAD_DIM + 1, :] for acc in accs]


def _chunk_locator(segments):
    counts = [seg[0].shape[0] // ATT_TK for seg in segments]

    def locate(c):
        for seg, n in zip(segments, counts):
            if c < n:
                return seg, slice(c * ATT_TK, (c + 1) * ATT_TK)
            c -= n
        raise IndexError(c)

    return locate, sum(counts)


def _gqa_kernel(q_ref, *refs):
    o_ref = refs[-1]
    locate, nk = _chunk_locator([refs[i:i + 2] for i in range(0, len(refs) - 1, 2)])
    kvh = pl.program_id(1)
    tq = q_ref.shape[1]
    n = B_HEADS // B_KV_HEADS
    cols = []
    for hh in range(n):
        q = q_ref[hh * HEAD_DIM:(hh + 1) * HEAD_DIM, :]
        z = jnp.zeros_like(q)
        cols.append(jnp.where(kvh == 0, jnp.concatenate([q, z], axis=0), jnp.concatenate([z, q], axis=0)))
    def key_chunk(c):
        (k_ref, _), rows = locate(c)
        return k_ref[rows, :]

    def vt_chunks(c):
        (_, vt_ref), cols_c = locate(c)
        return [vt_ref[:, cols_c]]

    per = n // GQA_STREAMS
    outs = _softmax_pv_streams(
        [(jnp.concatenate(cols[i * per:(i + 1) * per], axis=1), key_chunk, vt_chunks) for i in range(GQA_STREAMS)], nk)
    for hh in range(n):
        o = outs[hh // per][:, (hh % per) * tq:(hh % per + 1) * tq]
        o_ref[hh * HEAD_DIM:(hh + 1) * HEAD_DIM, :] = o.astype(BF16)


def _mla_kernel(qn_ref, qr_ref, *refs):
    o_ref = refs[-1]
    locate, nk = _chunk_locator([refs[i:i + 3] for i in range(0, len(refs) - 1, 3)])
    tq = qn_ref.shape[1]
    z = jnp.zeros((C_NOPE, tq), BF16)
    zr = jnp.zeros((LANES - C_ROPE, tq), BF16)
    streams = []
    for pair in range(MLA_PAIRS):
        cols = []
        for hh in range(2):
            h = 2 * pair + hh
            qn = qn_ref[h * C_NOPE:(h + 1) * C_NOPE, :]
            qr = qr_ref[h * C_ROPE:(h + 1) * C_ROPE, :]
            cols.append(jnp.concatenate(([qn, z] if hh == 0 else [z, qn]) + [qr, zr], axis=0))

        def key_chunk(c, pair=pair):
            (kn_ref, kr_ref, _), rows = locate(c)
            return jnp.concatenate([kn_ref[rows, pair * LANES:(pair + 1) * LANES], kr_ref[rows, :]], axis=1)

        def vt_chunks(c, pair=pair):
            (_, _, vt_ref), cols_c = locate(c)
            return [vt_ref[(2 * pair + hh) * C_V:(2 * pair + hh + 1) * C_V, cols_c] for hh in range(2)]

        streams.append((jnp.concatenate(cols, axis=1), key_chunk, vt_chunks))
    outs = _softmax_pv_streams(streams, nk)
    for pair in range(MLA_PAIRS):
        for hh in range(2):
            h = 2 * pair + hh
            o_ref[h * C_V:(h + 1) * C_V, :] = outs[pair][:, hh * tq:(hh + 1) * tq].astype(BF16)


def _attn_call(body, grid, in_specs, out_spec, args, n_tiles):
    return pl.pallas_call(
        body,
        grid=grid,
        in_specs=in_specs,
        out_specs=out_spec,
        out_shape=jax.ShapeDtypeStruct((512, n_tiles * TILE), BF16),
        compiler_params=_cparams(("parallel",) * len(grid)),
        name="attention",
    )(*args)


def _gqa_attention(qt, kb, vt, k_cache, vt_cache):
    assert NT_CTX * TILE == DEC_SEQ
    rows = (B_HEADS // B_KV_HEADS) * HEAD_DIM
    o_ctx = _attn_call(
        _gqa_kernel, (BATCH, B_KV_HEADS),
        [pl.BlockSpec((rows, TILE), lambda b, h: (h, b)),
         pl.BlockSpec((None, TILE, LANES), lambda b, h: (b, 0, 0)),
         pl.BlockSpec((HEAD_DIM, TILE), lambda b, h: (h, b))],
        pl.BlockSpec((rows, TILE), lambda b, h: (h, b)),
        [qt, kb.reshape(NT, TILE, LANES), vt], NT_CTX)
    o_lat = _attn_call(
        _gqa_kernel, (DEC_BATCH, B_KV_HEADS, NT_SEQ),
        [pl.BlockSpec((rows, TILE), lambda b, h, j: (h, NT_CTX + b * NT_SEQ + j)),
         pl.BlockSpec((None, PAST_LEN, LANES), lambda b, h, j: (b, 0, 0)),
         pl.BlockSpec((None, HEAD_DIM, PAST_LEN), lambda b, h, j: (b, h, 0)),
         pl.BlockSpec((DEC_SEQ, LANES), lambda b, h, j: (1 + b, 0)),
         pl.BlockSpec((HEAD_DIM, DEC_SEQ), lambda b, h, j: (h, 1 + b))],
        pl.BlockSpec((rows, TILE), lambda b, h, j: (h, b * NT_SEQ + j)),
        [qt, k_cache, vt_cache, kb, vt], NT_LAT)
    return o_ctx, o_lat


def _mla_attention(qnt, qrt, kn, krb, vt, kn_cache, kr_cache, vt_cache):
    nh = 2 * MLA_PAIRS
    steps = C_HEADS // nh
    o_ctx = _attn_call(
        _mla_kernel, (BATCH, steps),
        [pl.BlockSpec((nh * C_NOPE, TILE), lambda b, p: (p, b)),
         pl.BlockSpec((nh * C_ROPE, TILE), lambda b, p: (p, b)),
         pl.BlockSpec((TILE, MLA_PAIRS * LANES), lambda b, p: (b, p)),
         pl.BlockSpec((TILE, LANES), lambda b, p: (b, 0)),
         pl.BlockSpec((nh * C_V, TILE), lambda b, p: (p, b))],
        pl.BlockSpec((nh * C_V, TILE), lambda b, p: (p, b)),
        [qnt, qrt, kn, krb, vt], NT_CTX)
    o_lat = _attn_call(
        _mla_kernel, (DEC_BATCH, steps, NT_SEQ),
        [pl.BlockSpec((nh * C_NOPE, TILE), lambda b, p, j: (p, NT_CTX + b * NT_SEQ + j)),
         pl.BlockSpec((nh * C_ROPE, TILE), lambda b, p, j: (p, NT_CTX + b * NT_SEQ + j)),
         pl.BlockSpec((PAST_LEN, MLA_PAIRS * LANES), lambda b, p, j: (b, p)),
         pl.BlockSpec((None, PAST_LEN, LANES), lambda b, p, j: (b, 0, 0)),
         pl.BlockSpec((nh * C_V, PAST_LEN), lambda b, p, j: (p, b)),
         pl.BlockSpec((DEC_SEQ, MLA_PAIRS * LANES), lambda b, p, j: (1 + b, p)),
         pl.BlockSpec((DEC_SEQ, LANES), lambda b, p, j: (1 + b, 0)),
         pl.BlockSpec((nh * C_V, DEC_SEQ), lambda b, p, j: (p, 1 + b))],
        pl.BlockSpec((nh * C_V, TILE), lambda b, p, j: (p, b * NT_SEQ + j)),
        [qnt, qrt, kn_cache, kr_cache, vt_cache, kn, krb, vt], NT_LAT)
    return o_ctx, o_lat


def _group_tile(i, ctx_ref, lat_ref):
    return jnp.where(i < NT_CTX_IO, ctx_ref[...], lat_ref[...])


_CTX_COL = lambda i: (0, jnp.minimum(i, NT_CTX_IO - 1))
_LAT_COL = lambda i: (0, jnp.maximum(i - NT_CTX_IO, 0))
_CTX_ROW = lambda i: (jnp.minimum(i, NT_CTX_IO - 1), 0)
_LAT_ROW = lambda i: (jnp.maximum(i - NT_CTX_IO, 0), 0)


def _finish_out(o, x_in, mod_ref, n2_ref, wr_ref, xo_ref, h2_ref, lg_ref, rows=slice(None)):
    x = x_in + mod_ref[0, 0, 2:3, :] * o
    xo_ref[rows, :] = x
    h2 = _norm_mod(x, n2_ref[...], mod_ref[0, 0, 4:5, :], mod_ref[0, 0, 3:4, :])
    hi = h2.astype(BF16)
    lo = (h2 - hi.astype(F32)).astype(BF16)
    h2_ref[rows, :] = hi
    parts = (_dot(hi, wr_ref[...]) + _dot(lo, wr_ref[...])).T
    lg_ref[:, rows] = parts[0:N_EXPERTS, :] + parts[N_EXPERTS:2 * N_EXPERTS, :]


def _dot_t(at, b):
    return lax.dot_general(at, b, (((0,), (0,)), ((), ())), preferred_element_type=F32)


def _even_out_kernel(a_ref, btc_ref, btl_ref, w_ref, xc_ref, xl_ref, mod_ref, n2_ref, wr_ref,
                     xo_ref, h2_ref, lg_ref):
    ctx = pl.program_id(0) < NT_CTX_IO
    sub = IO_TILE // OUT_SUBTILES
    for h in range(OUT_SUBTILES):
        rows = slice(h * sub, (h + 1) * sub)
        bt = jnp.where(ctx, btc_ref[:, rows], btl_ref[:, rows])
        o = _dot(a_ref[rows, :], w_ref[0:512, :]) + _dot_t(bt, w_ref[512:1024, :])
        x_in = jnp.where(ctx, xc_ref[rows, :], xl_ref[rows, :])
        _finish_out(o, x_in, mod_ref, n2_ref, wr_ref, xo_ref, h2_ref, lg_ref, rows)


def _odd_out_kernel(atc_ref, atl_ref, z_ref, zp_ref, zn_ref, gb_ref, cw_ref, w_ref, x_ref, mod_ref, n2_ref,
                    wr_ref, xo_ref, h2_ref, lg_ref):
    i = pl.program_id(0)
    at = _group_tile(i, atc_ref, atl_ref)
    ctx = i < NT_CTX_IO
    j = (i - NT_CTX_IO) % NT_SEQ_IO
    first = jnp.logical_or(ctx, j == 0)
    last = jnp.logical_or(ctx, j == NT_SEQ_IO - 1)
    z = z_ref[...]
    row = lax.broadcasted_iota(jnp.int32, z.shape, 0)
    halo_p = jnp.where(first, 0.0, zp_ref[7:8, :])
    halo_n = jnp.where(last, 0.0, zn_ref[0:1, :])
    zprev = jnp.where(row == 0, halo_p, pltpu.roll(z, 1, 0))
    znext = jnp.where(row == IO_TILE - 1, halo_n, pltpu.roll(z, IO_TILE - 1, 0))
    zprev = jnp.where(jnp.logical_and(ctx, row % SEQ == 0), 0.0, zprev)
    znext = jnp.where(jnp.logical_and(ctx, row % SEQ == SEQ - 1), 0.0, znext)
    y = zprev * cw_ref[0:1, :] + z * cw_ref[1:2, :] + znext * cw_ref[2:3, :]
    d = (gb_ref[...] * y).astype(BF16)
    sub = IO_TILE // OUT_SUBTILES
    for h in range(OUT_SUBTILES):
        rows = slice(h * sub, (h + 1) * sub)
        o = _dot_t(at[:, rows], w_ref[0:512, :]) + _dot(d[rows, :], w_ref[512:1024, :])
        _finish_out(o, x_ref[rows, :], mod_ref, n2_ref, wr_ref, xo_ref, h2_ref, lg_ref, rows)


_OUT_SHAPES = [
    jax.ShapeDtypeStruct((N_TOK, D_MODEL), F32),
    jax.ShapeDtypeStruct((N_TOK, D_MODEL), BF16),
    jax.ShapeDtypeStruct((N_EXPERTS, N_TOK), F32),
]
_LOGIT_SPEC = pl.BlockSpec((N_EXPERTS, IO_TILE), lambda i: (0, i))


def _even_out(layer, a, b, w_out, xc, xl, mod, n2, wr):
    full = lambda shape: pl.BlockSpec(shape, lambda i: (0,) * len(shape))
    tile = lambda w: pl.BlockSpec((IO_TILE, w), lambda i: (i, 0))
    return pl.pallas_call(
        _even_out_kernel,
        grid=(NT_IO,),
        in_specs=[tile(512), pl.BlockSpec((512, IO_TILE), _CTX_COL), pl.BlockSpec((512, IO_TILE), _LAT_COL),
                  full(w_out.shape), pl.BlockSpec((IO_TILE, D_MODEL), _CTX_ROW),
                  pl.BlockSpec((IO_TILE, D_MODEL), _LAT_ROW),
                  pl.BlockSpec((1, 1, 6, D_MODEL), lambda i: (layer, _mod_row(i), 0, 0)),
                  full((1, D_MODEL)), full(wr.shape)],
        out_specs=[tile(D_MODEL), tile(D_MODEL), _LOGIT_SPEC],
        out_shape=_OUT_SHAPES,
        compiler_params=_cparams(("parallel",)),
        name="even_out",
    )(a, b[0], b[1], w_out, xc, xl, mod, n2, wr)


def _odd_out(layer, a, z, gb, cw, w_out, x, mod, n2, wr):
    full = lambda shape: pl.BlockSpec(shape, lambda i: (0,) * len(shape))
    tile = lambda w: pl.BlockSpec((IO_TILE, w), lambda i: (i, 0))
    rb = IO_TILE // 8
    return pl.pallas_call(
        _odd_out_kernel,
        grid=(NT_IO,),
        in_specs=[pl.BlockSpec((512, IO_TILE), _CTX_COL), pl.BlockSpec((512, IO_TILE), _LAT_COL), tile(512),
                  pl.BlockSpec((8, 512), lambda i: (jnp.maximum(i * rb - 1, 0), 0)),
                  pl.BlockSpec((8, 512), lambda i: (jnp.minimum(i * rb + rb, NT_IO * rb - 1), 0)),
                  tile(512), full(cw.shape), full(w_out.shape), tile(D_MODEL),
                  pl.BlockSpec((1, 1, 6, D_MODEL), lambda i: (layer, _mod_row(i), 0, 0)),
                  full((1, D_MODEL)), full(wr.shape)],
        out_specs=[tile(D_MODEL), tile(D_MODEL), _LOGIT_SPEC],
        out_shape=_OUT_SHAPES,
        compiler_params=_cparams(("parallel",)),
        name="odd_out",
    )(a[0], a[1], z, z, z, gb, cw, w_out, x, mod, n2, wr)


def _select_tokens(aff, n_dom, cap, tri, pos_ref, gate_ref):
    w = aff.shape[1] // n_dom
    doms = [aff[:, d * w:(d + 1) * w] for d in range(n_dom)]
    bits = [jnp.zeros((N_EXPERTS, 1), jnp.int32) for _ in range(n_dom)]
    for bit in range(30, -1, -1):
        for d in range(n_dom):
            cand = bits[d] | (1 << bit)
            cnt = jnp.sum((doms[d] >= lax.bitcast_convert_type(cand, F32)).astype(F32), axis=1, keepdims=True)
            bits[d] = jnp.where(cnt >= cap, cand, bits[d])
    for d in range(n_dom):
        thr = lax.bitcast_convert_type(bits[d], F32)
        gt = doms[d] > thr
        eq = doms[d] == thr
        need = cap - jnp.sum(gt.astype(F32), axis=1, keepdims=True)
        eq_seen = jnp.zeros((N_EXPERTS, 1), F32)
        sel_seen = jnp.zeros((N_EXPERTS, 1), F32) + d * cap
        for blk in range(w // TILE):
            cols = slice(blk * TILE, (blk + 1) * TILE)
            eq_b = eq[:, cols].astype(F32)
            eq_rank = _dot(eq_b.astype(BF16), tri) + eq_seen
            sel = jnp.logical_or(gt[:, cols], jnp.logical_and(eq[:, cols], eq_rank < need))
            sel_f = sel.astype(F32)
            pos = _dot(sel_f.astype(BF16), tri) + sel_seen
            tile = d * (w // TILE) + blk
            pos_ref[0, tile] = jnp.where(sel, pos, -1.0)
            gate_ref[0, tile] = jnp.where(sel, doms[d][:, cols], 0.0)
            eq_seen = eq_seen + jnp.sum(eq_b, axis=1, keepdims=True)
            sel_seen = sel_seen + jnp.sum(sel_f, axis=1, keepdims=True)


def _route_kernel(lg_ref, pos_ref, gate_ref):
    lg = lg_ref[...]
    ex = jnp.exp(lg - jnp.max(lg, axis=0, keepdims=True))
    aff = ex / jnp.sum(ex, axis=0, keepdims=True)
    r = lax.broadcasted_iota(jnp.int32, (TILE, TILE), 0)
    c = lax.broadcasted_iota(jnp.int32, (TILE, TILE), 1)
    tri = (r < c).astype(BF16)

    @pl.when(pl.program_id(0) == 0)
    def _():
        _select_tokens(aff, GROUP // SEQ, EC_CAPACITY * SEQ // N_EXPERTS, tri, pos_ref, gate_ref)

    @pl.when(pl.program_id(0) > 0)
    def _():
        _select_tokens(aff, GROUP // DEC_SEQ, EC_CAPACITY * DEC_SEQ // N_EXPERTS, tri, pos_ref, gate_ref)


def _route(logits_t):
    blk = pl.BlockSpec((1, GROUP_TILES, N_EXPERTS, TILE), lambda g: (g, 0, 0, 0))
    shape = jax.ShapeDtypeStruct((N_GROUPS, GROUP_TILES, N_EXPERTS, TILE), F32)
    return pl.pallas_call(
        _route_kernel,
        grid=(N_GROUPS,),
        in_specs=[pl.BlockSpec((N_EXPERTS, GROUP), lambda g: (0, g))],
        out_specs=[blk, blk],
        out_shape=[shape, shape],
        compiler_params=_cparams(("parallel",)),
        name="route",
    )(logits_t)


def _moe_kernel(nblk_ref, ctab_ref, etab_ref, llen_ref,
                x_ref, pos_ref, gate_ref, wg_ref, wu_ref, wd_ref, xres_ref, mod_ref, fn_ref, *rest, final):
    n_out = 2 if final else 1
    out_refs = rest[:n_out]
    lists_ref, gl_ref, s_ref, gs_ref, acc_ref, wgb_ref, wub_ref, wdb_ref = rest[n_out:]
    g = pl.program_id(0)
    s = pl.program_id(1)
    chunks = STACK_BLK // CHUNK_ROWS
    rows16 = lax.broadcasted_iota(jnp.int32, (CHUNK_ROWS, TILE), 0)

    @pl.when(jnp.logical_and(g == 0, s == 0))
    def _():
        lists_ref[ZERO_ROW:ZERO_ROW + CHUNK_ROWS, :] = jnp.zeros((CHUNK_ROWS, D_MODEL), BF16)

    def chunk_rows(tbase, q, unused_row):
        d = ctab_ref[tbase + q]
        return pl.multiple_of(jnp.where(d >= 0, d, unused_row), CHUNK_ROWS)

    def build_block(j, rb, with_gate, slot):
        tbase = ((g * GROUP_TILES + j) * MAX_STACK_BLKS + rb) * chunks
        for q in range(chunks):
            d = ctab_ref[tbase + q]
            e = etab_ref[tbase + q]
            posrow = pos_ref[0, j, pl.ds(e, 1), :]
            rowid = (rows16 + (rb * STACK_BLK + q * CHUNK_ROWS)).astype(F32)
            hit = jnp.logical_and(posrow == rowid, d >= 0)
            s_ref[slot, q * CHUNK_ROWS:(q + 1) * CHUNK_ROWS, :] = hit.astype(BF16)
            if with_gate:
                gaterow = gate_ref[0, j, pl.ds(e, 1), :]
                gs_ref[slot, q * CHUNK_ROWS:(q + 1) * CHUNK_ROWS, :] = jnp.where(hit, gaterow, 0.0)
        return tbase

    def for_blocks(nb, build, finish):
        def run(first, n):
            tbases = [build(first + k, k) for k in range(n)]
            for k in range(n):
                finish(tbases[k], k)

        def group(i, _):
            run(BLOCK_SLOTS * i, BLOCK_SLOTS)
            return 0

        lax.fori_loop(0, nb // BLOCK_SLOTS, group, 0)
        for left in range(1, BLOCK_SLOTS):
            @pl.when(nb % BLOCK_SLOTS == left)
            def _(left=left):
                run(nb - left, left)

    @pl.when(s < MOE_TILE_STEPS)
    def _gather():
        def tile(t, _):
            j = s * MOE_STEP_TILES + t
            tok = pl.multiple_of(t * TILE, TILE)

            def finish(tbase, slot):
                picked = _dot(s_ref[slot], x_ref[pl.ds(tok, TILE), :]).astype(BF16)
                gcol = jnp.sum(gs_ref[slot], axis=1, keepdims=True)
                for q in range(chunks):
                    d = chunk_rows(tbase, q, DUMP_ROW)
                    rows = slice(q * CHUNK_ROWS, (q + 1) * CHUNK_ROWS)
                    lists_ref[pl.ds(d, CHUNK_ROWS), :] = picked[rows, :]
                    gl_ref[pl.ds(d, CHUNK_ROWS), :] = gcol[rows, :]

            for_blocks(nblk_ref[g * GROUP_TILES + j], lambda rb, slot: build_block(j, rb, True, slot), finish)
            return 0

        lax.fori_loop(0, MOE_STEP_TILES, tile, 0)

    @pl.when(jnp.logical_and(s >= MOE_TILE_STEPS, s < MOE_TILE_STEPS + N_EXPERTS))
    def _experts():
        e = s - MOE_TILE_STEPS
        ln = llen_ref[g * N_EXPERTS + e]
        base = e * LIST_MAX
        half = FFN_BLK // 2
        rem = ln % FFN_BLK
        use_half = jnp.logical_and(rem > 0, rem <= half)
        nfull = ln // FFN_BLK + jnp.where(rem > half, 1, 0)
        end = nfull * FFN_BLK + jnp.where(use_half, half, 0)

        def zero_tail(k, _):
            r0 = pl.multiple_of(base + ln + k * CHUNK_ROWS, CHUNK_ROWS)
            lists_ref[pl.ds(r0, CHUNK_ROWS), :] = jnp.zeros((CHUNK_ROWS, D_MODEL), BF16)
            gl_ref[pl.ds(r0, CHUNK_ROWS), :] = jnp.zeros((CHUNK_ROWS, 1), F32)
            return 0

        lax.fori_loop(0, (end - ln) // CHUNK_ROWS, zero_tail, 0)
        wgb_ref[...] = wg_ref[0, 0].astype(BF16)
        wub_ref[...] = wu_ref[0, 0].astype(BF16)
        wdb_ref[...] = wd_ref[0, 0].astype(BF16)

        def ffn_rows(r0, n):
            xs = lists_ref[pl.ds(r0, n), :]
            hid = _silu(_dot(xs, wgb_ref[...])) * _dot(xs, wub_ref[...])
            y = _dot(hid.astype(BF16), wdb_ref[...]) * gl_ref[pl.ds(r0, n), :]
            lists_ref[pl.ds(r0, n), :] = y.astype(BF16)

        def ffn(c, _):
            ffn_rows(pl.multiple_of(base + c * FFN_BLK, FFN_BLK), FFN_BLK)
            return 0

        lax.fori_loop(0, nfull, ffn, 0)

        @pl.when(use_half)
        def _():
            ffn_rows(pl.multiple_of(base + nfull * FFN_BLK, half), half)

    @pl.when(s >= MOE_TILE_STEPS + N_EXPERTS)
    def _scatter():
        def tile(t, _):
            j = (s - MOE_TILE_STEPS - N_EXPERTS) * SCATTER_STEP_TILES + t
            rows = pl.ds(pl.multiple_of(t * TILE, TILE), TILE)
            acc_ref[...] = jnp.zeros_like(acc_ref)

            def finish(tbase, slot):
                y = jnp.concatenate([lists_ref[pl.ds(chunk_rows(tbase, q, ZERO_ROW), CHUNK_ROWS), :]
                                     for q in range(chunks)], axis=0)
                acc_ref[...] += _dot_t(s_ref[slot], y)

            for_blocks(nblk_ref[g * GROUP_TILES + j], lambda rb, slot: build_block(j, rb, False, slot), finish)
            x = xres_ref[rows, :] + mod_ref[0, 0, 5:6, :] * acc_ref[...]
            if final:
                y = x * lax.rsqrt(jnp.mean(x * x, axis=-1, keepdims=True) + EPS) * fn_ref[...]

                @pl.when(g == 0)
                def _():
                    out_refs[0][rows, :] = y

                @pl.when(g > 0)
                def _():
                    out_refs[1][rows, :] = y
            else:
                out_refs[0][rows, :] = x
            return 0

        lax.fori_loop(0, SCATTER_STEP_TILES, tile, 0)


def _moe(layer, final, tables, h2, pos_t, gate_t, wg, wu, wd, xres, mod, fn):
    gt = GROUP_TILES
    ts = MOE_TILE_STEPS
    sc = SCATTER_STEPS
    rows_a = MOE_STEP_TILES * TILE
    rows = SCATTER_STEP_TILES * TILE
    step_c = lambda s: jnp.clip(s - ts - N_EXPERTS, 0, sc - 1)
    tile_a = lambda g, s, *_: (g * ts + jnp.minimum(s, ts - 1), 0)
    tile_c = lambda g, s, *_: (g * sc + step_c(s), 0)
    expert = lambda g, s, *_: (layer, jnp.clip(s - ts, 0, N_EXPERTS - 1), 0, 0)
    group = lambda g, s, *_: (g, 0, 0, 0)
    if final:
        out_specs = [
            pl.BlockSpec((rows, D_MODEL), lambda g, s, *_: (jnp.where(g == 0, step_c(s), sc - 1), 0)),
            pl.BlockSpec((rows, D_MODEL), lambda g, s, *_: (jnp.where(g == 0, 0, (g - 1) * sc + step_c(s)), 0)),
        ]
        out_shape = [jax.ShapeDtypeStruct((NT_CTX * TILE, D_MODEL), F32),
                     jax.ShapeDtypeStruct((NT_LAT * TILE, D_MODEL), F32)]
    else:
        out_specs = pl.BlockSpec((rows, D_MODEL), tile_c)
        out_shape = jax.ShapeDtypeStruct((N_TOK, D_MODEL), F32)
    grid_spec = pltpu.PrefetchScalarGridSpec(
        num_scalar_prefetch=4,
        grid=(N_GROUPS, ts + N_EXPERTS + sc),
        in_specs=[
            pl.BlockSpec((rows_a, D_MODEL), tile_a),
            pl.BlockSpec((1, gt, N_EXPERTS, TILE), group),
            pl.BlockSpec((1, gt, N_EXPERTS, TILE), group),
            pl.BlockSpec((1, 1, D_MODEL, EXPERT_FF), expert),
            pl.BlockSpec((1, 1, D_MODEL, EXPERT_FF), expert),
            pl.BlockSpec((1, 1, EXPERT_FF, D_MODEL), expert),
            pl.BlockSpec((rows, D_MODEL), tile_c),
            pl.BlockSpec((1, 1, 6, D_MODEL), lambda g, s, *_: (layer, g, 0, 0)),
            pl.BlockSpec((1, D_MODEL), lambda g, s, *_: (0, 0)),
        ],
        out_specs=out_specs,
        scratch_shapes=[
            pltpu.VMEM((LIST_ROWS, D_MODEL), BF16),
            pltpu.VMEM((LIST_ROWS, 1), F32),
            pltpu.VMEM((BLOCK_SLOTS, STACK_BLK, TILE), BF16),
            pltpu.VMEM((BLOCK_SLOTS, STACK_BLK, TILE), F32),
            pltpu.VMEM((TILE, D_MODEL), F32),
            pltpu.VMEM((D_MODEL, EXPERT_FF), BF16),
            pltpu.VMEM((D_MODEL, EXPERT_FF), BF16),
            pltpu.VMEM((EXPERT_FF, D_MODEL), BF16),
        ],
    )
    return pl.pallas_call(
        functools.partial(_moe_kernel, final=final),
        grid_spec=grid_spec,
        out_shape=out_shape,
        compiler_params=_cparams(("arbitrary", "arbitrary")),
        name="moe",
    )(*tables, h2, pos_t, gate_t, wg, wu, wd, xres, mod, fn)


def _rope_tables(rot_dim):
    axis_dim = rot_dim // 2
    tok = jnp.arange(DEC_SEQ)
    rows = (tok // GRID_W).astype(F32)
    cols = (tok % GRID_W).astype(F32)
    inv_freq = ROPE_THETA ** (-jnp.arange(0, axis_dim, 2, dtype=F32) / axis_dim)
    ar = rows[:, None] * inv_freq
    ac = cols[:, None] * inv_freq
    cos = jnp.concatenate([jnp.cos(ar), jnp.cos(ar), jnp.cos(ac), jnp.cos(ac)], axis=1)
    sin = jnp.concatenate([-jnp.sin(ar), jnp.sin(ar), -jnp.sin(ac), jnp.sin(ac)], axis=1)
    rep = LANES // rot_dim
    cos = jnp.tile(cos, (1, rep)).reshape(NT_SEQ_IO, IO_TILE, LANES)
    sin = jnp.tile(sin, (1, rep)).reshape(NT_SEQ_IO, IO_TILE, LANES)
    cos = jnp.concatenate([jnp.ones((1, IO_TILE, LANES), F32), cos], axis=0)
    sin = jnp.concatenate([jnp.zeros((1, IO_TILE, LANES), F32), sin], axis=0)
    return cos, sin


def _excl_cumsum(a, axis):
    return jnp.cumsum(a, axis=axis) - a


def _route_and_moe(layer, final, h2, logits, wg, wu, wd, xres, mod, fn):
    pos_t, gate_t = _route(logits)
    sel = pos_t >= 0
    cnt = jnp.sum(sel, axis=-1, dtype=jnp.int32)
    npad = (cnt + CHUNK_ROWS - 1) // CHUNK_ROWS * CHUNK_ROWS
    seg_off = _excl_cumsum(npad, 2)
    list_off = _excl_cumsum(npad, 1)
    rank0 = _excl_cumsum(cnt, 1)
    stack_pos = jnp.where(sel, pos_t - rank0[..., None].astype(F32) + seg_off[..., None].astype(F32), -1.0)
    nblk = (jnp.sum(npad, axis=2) + STACK_BLK - 1) // STACK_BLK
    llen = jnp.sum(npad, axis=1)
    r = (jnp.arange(MAX_STACK_BLKS * STACK_BLK // CHUNK_ROWS, dtype=jnp.int32) * CHUNK_ROWS)[None, None, None, :]
    in_seg = jnp.logical_and(r >= seg_off[..., None], r < (seg_off + npad)[..., None])
    eidx = jnp.arange(N_EXPERTS, dtype=jnp.int32)[None, None, :, None]
    dst = jnp.sum(jnp.where(in_seg, eidx * LIST_MAX + list_off[..., None] + r - seg_off[..., None], 0), axis=2)
    used = jnp.any(in_seg, axis=2)
    ctab = jnp.where(used, dst, -1).astype(jnp.int32)
    etab = jnp.sum(jnp.where(in_seg, eidx, 0), axis=2).astype(jnp.int32)
    tables = (nblk.reshape(-1).astype(jnp.int32), ctab.reshape(-1), etab.reshape(-1),
              llen.reshape(-1).astype(jnp.int32))
    return _moe(layer, final, tables, h2, stack_pos, gate_t, wg, wu, wd, xres, mod, fn)


def _split_router(w):
    hi = w.astype(BF16)
    lo = (w - hi.astype(F32)).astype(BF16)
    pad = jnp.zeros((D_MODEL, LANES - 2 * N_EXPERTS), BF16)
    return jnp.concatenate([hi, lo, pad], axis=1)


def kernel(x_prompt, x_sample, c, cache_attn_k, cache_attn_v, cache_mla_ckv, cache_mla_krope, c_ctx, w_mod, b_mod,
           norm1, norm2, ev_w_in, ev_q_norm, ev_k_norm, ev_w_s, ev_b_s, ev_w_out, od_w_in, od_q_a_norm,
           od_kv_a_norm, od_w_uq, od_w_ukv, od_conv_w, od_w_out, moe_router, moe_w_gate, moe_w_up, moe_w_down,
           final_norm):
    nctx = NT_CTX * TILE
    xc = x_prompt.reshape(nctx, D_MODEL)
    xl = x_sample.reshape(NT_LAT * TILE, D_MODEL)
    cvec = jnp.concatenate([c_ctx[None], c, jnp.zeros((3, D_MODEL), F32)], axis=0)
    mod = _modulation(cvec, w_mod, b_mod).reshape(2, 8, 6, D_MODEL)

    cos64, sin64 = _rope_tables(HEAD_DIM)
    seg = jnp.arange(512) // HEAD_DIM
    bd = (seg[:, None] == seg[None, :]).astype(BF16)
    bs_tab = jnp.repeat(ev_b_s[0].T, 64, axis=1)
    gated, qt, k, kb, v, vt = _even_in(
        xc, xl, mod, norm1[0:1], ev_w_in[0].astype(BF16), jnp.tile(ev_q_norm[0], 8)[None], jnp.tile(ev_k_norm[0], 2)[None],
        cos64, sin64, jnp.concatenate([ev_w_s[0, 0::2], ev_w_s[0, 1::2]], axis=2).astype(BF16), bs_tab, bd)
    new_k = k[:nctx].reshape(BATCH, SEQ, B_KV_HEADS, HEAD_DIM).transpose(0, 2, 1, 3)[:, None]
    new_v = v[:nctx].reshape(BATCH, SEQ, B_KV_HEADS, HEAD_DIM).transpose(0, 2, 1, 3)[:, None]
    ck = cache_attn_k[:, 0].transpose(0, 2, 1, 3).reshape(DEC_BATCH, PAST_LEN, LANES).astype(BF16)
    cv = cache_attn_v[:, 0].transpose(0, 1, 3, 2).reshape(DEC_BATCH, LANES, PAST_LEN).astype(BF16)
    attn_t = _gqa_attention(qt, kb, vt, ck, cv)
    x1, h2, logits = _even_out(0, gated, attn_t, ev_w_out[0].astype(BF16), xc, xl, mod, norm2[0:1],
                               _split_router(moe_router[0]))
    x2 = _route_and_moe(0, False, h2, logits, moe_w_gate, moe_w_up, moe_w_down, x1, mod, final_norm[None])

    cos32, sin32 = _rope_tables(C_ROPE)
    w_in1 = od_w_in[0]
    w_in1 = jnp.concatenate([w_in1[:, :C_Q_LORA + C_KV_LORA + C_ROPE], jnp.zeros((D_MODEL, LANES - C_ROPE), F32),
                             w_in1[:, C_Q_LORA + C_KV_LORA + C_ROPE:]], axis=1).astype(BF16)
    wuq = od_w_uq[0].reshape(C_Q_LORA, C_HEADS, C_NOPE + C_ROPE)
    wuq = jnp.concatenate([wuq[:, :, :C_NOPE].reshape(C_Q_LORA, -1), wuq[:, :, C_NOPE:].reshape(C_Q_LORA, -1)],
                          axis=1).astype(BF16)
    wukv = od_w_ukv[0].reshape(C_KV_LORA, C_HEADS, C_NOPE + C_V)
    wukv = jnp.concatenate([wukv[:, :, :C_NOPE].reshape(C_KV_LORA, -1), wukv[:, :, C_NOPE:].reshape(C_KV_LORA, -1)],
                           axis=1).astype(BF16)
    qnt, qrt, ckv, kr, krb, gb, z, kn, vt1 = _odd_in(x2, mod, norm1[1:2], w_in1, od_q_a_norm[0][None],
                                                     od_kv_a_norm[0][None], wuq, wukv, cos32, sin32)
    new_ckv = ckv[:nctx].reshape(BATCH, 1, SEQ, C_KV_LORA)
    new_kr = kr[:nctx, :C_ROPE].reshape(BATCH, 1, SEQ, C_ROPE)
    kn_cache, vt_cache = _kvup(cache_mla_ckv[:, 0].reshape(DEC_BATCH * PAST_LEN, C_KV_LORA), wukv)
    ckr = jnp.pad(cache_mla_krope[:, 0], ((0, 0), (0, 0), (0, LANES - C_ROPE))).astype(BF16)
    attn_t = _mla_attention(qnt, qrt, kn, krb, vt1, kn_cache, ckr, vt_cache)
    cw = jnp.concatenate([od_conv_w[0], jnp.zeros((5, D_WIDTH), F32)], axis=0)
    x3, h2, logits = _odd_out(1, attn_t, z, gb, cw, od_w_out[0].astype(BF16), x2, mod, norm2[1:2],
                              _split_router(moe_router[1]))
    y_c, y_l = _route_and_moe(1, True, h2, logits, moe_w_gate, moe_w_up, moe_w_down, x3, mod, final_norm[None])
    y_prompt = y_c.reshape(BATCH, SEQ, D_MODEL)
    y_sample = y_l.reshape(DEC_BATCH, DEC_SEQ, D_MODEL)
    return y_prompt, y_sample, new_k, new_v, new_ckv, new_kr
```

```python
import functools
import math

import jax
import jax.numpy as jnp
from jax import lax
from jax.experimental import pallas as pl
from jax.experimental.pallas import tpu as pltpu

F32 = jnp.float32
BF16 = jnp.bfloat16

D_MODEL = 1024
BATCH = 16
SEQ = 256
DEC_BATCH = 4
DEC_SEQ = 4096
PAST_LEN = 512
GRID_W = 64
ROPE_THETA = 10000.0
EPS = 1e-6
HEAD_DIM = 64
CHUNK = 128
A_GROUPS = 8
A_WIDTH = 512
B_HEADS = 8
B_KV_HEADS = 2
B_WIDTH = 512
C_HEADS = 8
C_NOPE = 64
C_ROPE = 32
C_V = 64
C_Q_LORA = 256
C_KV_LORA = 128
D_WIDTH = 512
N_EXPERTS = 16
EXPERT_FF = 512
EC_CAPACITY = 2

LOG2E = math.log2(math.e)
LANES = 128
BF16_ROWS = 16
ATT_TK = 256
MLA_PAIRS = 2
GQA_STREAMS = 2
TILE = 256
NT_CTX = BATCH * SEQ // TILE
NT_SEQ = DEC_SEQ // TILE
NT_LAT = DEC_BATCH * NT_SEQ
NT = NT_CTX + NT_LAT
N_TOK = NT * TILE
IO_TILE = 1024
NT_CTX_IO = BATCH * SEQ // IO_TILE
NT_SEQ_IO = DEC_SEQ // IO_TILE
NT_IO = N_TOK // IO_TILE
OUT_SUBTILES = 4
GROUP = 4096
N_GROUPS = N_TOK // GROUP
GROUP_TILES = GROUP // TILE
GROUP_CAP = EC_CAPACITY * GROUP // N_EXPERTS
CHUNK_ROWS = BF16_ROWS
STACK_BLK = 256
MAX_STACK_BLKS = -(-N_EXPERTS * TILE // STACK_BLK)
MOE_STEP_TILES = 4
MOE_TILE_STEPS = GROUP_TILES // MOE_STEP_TILES
SCATTER_STEP_TILES = 1
SCATTER_STEPS = GROUP_TILES // SCATTER_STEP_TILES
BLOCK_SLOTS = 3
FFN_BLK = 256
LIST_MAX = -(-(GROUP_CAP + GROUP_TILES * (CHUNK_ROWS - 1)) // FFN_BLK) * FFN_BLK
DUMP_ROW = N_EXPERTS * LIST_MAX
ZERO_ROW = DUMP_ROW + CHUNK_ROWS
LIST_ROWS = ZERO_ROW + CHUNK_ROWS
VMEM_LIMIT = 60 * 1024 * 1024


def _cparams(sem):
    return pltpu.CompilerParams(dimension_semantics=sem, vmem_limit_bytes=VMEM_LIMIT)


def _dot(a, b):
    return jnp.dot(a, b, preferred_element_type=F32)


def _silu(x):
    return x / (1.0 + jnp.exp(-x))


def _mod_row(i):
    return jnp.where(i < NT_CTX_IO, 0, 1 + (i - NT_CTX_IO) // NT_SEQ_IO)


def _tab_blk(i):
    return jnp.where(i < NT_CTX_IO, 0, 1 + (i - NT_CTX_IO) % NT_SEQ_IO)


def _mod_kernel(c_ref, w_ref, b_ref, o_ref):
    s = _silu(c_ref[...])
    o_ref[0] = _dot(s.astype(BF16), w_ref[0].astype(BF16)) + b_ref[0]


def _modulation(cvec, w_mod, b_mod):
    depth = w_mod.shape[0]
    nchunk = 6
    return pl.pallas_call(
        _mod_kernel,
        grid=(depth, nchunk),
        in_specs=[
            pl.BlockSpec((8, D_MODEL), lambda l, k: (0, 0)),
            pl.BlockSpec((1, D_MODEL, D_MODEL), lambda l, k: (l, 0, k)),
            pl.BlockSpec((1, 1, D_MODEL), lambda l, k: (l, 0, k)),
        ],
        out_specs=pl.BlockSpec((1, 8, D_MODEL), lambda l, k: (l, 0, k)),
        out_shape=jax.ShapeDtypeStruct((depth, 8, 6 * D_MODEL), F32),
        compiler_params=_cparams(("arbitrary", "arbitrary")),
        name="modulation",
    )(cvec, w_mod, b_mod.reshape(depth, 1, 6 * D_MODEL))


def _norm_mod(x, gain, scale, shift):
    ms = jnp.mean(x * x, axis=-1, keepdims=True)
    return (x * lax.rsqrt(ms + EPS) * gain) * (1.0 + scale) + shift


def _seg_mean_sq(z, bd, width):
    zz = z * z
    hi = zz.astype(BF16)
    lo = (zz - hi.astype(F32)).astype(BF16)
    return (_dot(hi, bd) + _dot(lo, bd)) * (1.0 / width)


def _rope(z, cos, sin_signed, half):
    w = z.shape[1]
    lane = lax.broadcasted_iota(jnp.int32, z.shape, 1)
    first = (lane % (2 * half)) < half
    partner = jnp.where(first, pltpu.roll(z, w - half, 1), pltpu.roll(z, half, 1))
    return z * cos + partner * sin_signed


def _tile_lanes(t, n):
    return jnp.concatenate([t] * n, axis=1) if n > 1 else t


def _even_in_kernel(xc_ref, xl_ref, mod_ref, n1_ref, w_ref, qg_ref, kg_ref, cos_ref, sin_ref, ws_ref, bs_ref, bd_ref,
                    gated_ref, qt_ref, k_ref, kb_ref, v_ref, vt_ref):
    x = _group_tile(pl.program_id(0), xc_ref, xl_ref)
    h = _norm_mod(x, n1_ref[...], mod_ref[0, 0, 1:2, :], mod_ref[0, 0, 0:1, :])
    p = _dot(h.astype(BF16), w_ref[...])
    u = p[:, 0:512]
    va = p[:, 512:1024].astype(BF16)
    q = p[:, 1024:1536]
    k = p[:, 1536:1664]
    v = p[:, 1664:1792]
    bd = bd_ref[...]
    cos = cos_ref[0]
    sin = sin_ref[0]
    qn = q * lax.rsqrt(_seg_mean_sq(q, bd, HEAD_DIM) + EPS) * qg_ref[...]
    kn = k * lax.rsqrt(_seg_mean_sq(k, bd[0:LANES, 0:LANES], HEAD_DIM) + EPS) * kg_ref[...]
    qr = _rope(qn, _tile_lanes(cos, 4), _tile_lanes(sin, 4), HEAD_DIM // 4)
    kr = _rope(kn, cos, sin, HEAD_DIM // 4)
    qt_ref[...] = (qr * (HEAD_DIM ** -0.5 * LOG2E)).T.astype(BF16)
    k_ref[...] = kr
    kb_ref[...] = kr.astype(BF16)
    v_ref[...] = v
    vt_ref[...] = v.T.astype(BF16)
    lane = lax.broadcasted_iota(jnp.int32, (CHUNK, LANES), 1)
    for ch in range(IO_TILE // CHUNK):
        rows = slice(ch * CHUNK, (ch + 1) * CHUNK)
        cols = []
        for pair in range(A_GROUPS // 2):
            vp = va[rows, pair * LANES:(pair + 1) * LANES]
            zero = jnp.zeros_like(vp)
            rhs = jnp.concatenate([jnp.where(lane < LANES // 2, vp, zero), jnp.where(lane < LANES // 2, zero, vp)],
                                  axis=0)
            cols.append(_dot(ws_ref[pair], rhs))
        s = jnp.concatenate(cols, axis=1) + bs_ref[...]
        gated_ref[rows, :] = (u[rows, :] * s).astype(BF16)


def _even_in(xc, xl, mod, n1, w_in, qg, kg, cos, sin, ws, bs_tab, bd):
    full = lambda shape: pl.BlockSpec(shape, lambda i: (0,) * len(shape))
    return pl.pallas_call(
        _even_in_kernel,
        grid=(NT_IO,),
        in_specs=[
            pl.BlockSpec((IO_TILE, D_MODEL), _CTX_ROW),
            pl.BlockSpec((IO_TILE, D_MODEL), _LAT_ROW),
            pl.BlockSpec((1, 1, 6, D_MODEL), lambda i: (0, _mod_row(i), 0, 0)),
            full((1, D_MODEL)),
            full(w_in.shape),
            full((1, 512)),
            full((1, LANES)),
            pl.BlockSpec((1, IO_TILE, LANES), lambda i: (_tab_blk(i), 0, 0)),
            pl.BlockSpec((1, IO_TILE, LANES), lambda i: (_tab_blk(i), 0, 0)),
            full(ws.shape),
            full(bs_tab.shape),
            full(bd.shape),
        ],
        out_specs=[
            pl.BlockSpec((IO_TILE, 512), lambda i: (i, 0)),
            pl.BlockSpec((512, IO_TILE), lambda i: (0, i)),
            pl.BlockSpec((IO_TILE, LANES), lambda i: (i, 0)),
            pl.BlockSpec((IO_TILE, LANES), lambda i: (i, 0)),
            pl.BlockSpec((IO_TILE, LANES), lambda i: (i, 0)),
            pl.BlockSpec((LANES, IO_TILE), lambda i: (0, i)),
        ],
        out_shape=[
            jax.ShapeDtypeStruct((N_TOK, 512), BF16),
            jax.ShapeDtypeStruct((512, N_TOK), BF16),
            jax.ShapeDtypeStruct((N_TOK, LANES), F32),
            jax.ShapeDtypeStruct((N_TOK, LANES), BF16),
            jax.ShapeDtypeStruct((N_TOK, LANES), F32),
            jax.ShapeDtypeStruct((LANES, N_TOK), BF16),
        ],
        compiler_params=_cparams(("parallel",)),
        name="even_in",
    )(xc, xl, mod, n1, w_in, qg, kg, cos, sin, ws, bs_tab, bd)


def _odd_in_kernel(x_ref, mod_ref, n1_ref, w_ref, qa_ref, kva_ref, wuq_ref, wukv_ref, cos_ref, sin_ref,
                   qnt_ref, qrt_ref, ckv_ref, kr_ref, krb_ref, gb_ref, z_ref, kn_ref, vt_ref):
    sub = IO_TILE // OUT_SUBTILES
    scale = (C_NOPE + C_ROPE) ** -0.5 * LOG2E
    for hh in range(OUT_SUBTILES):
        rows = slice(hh * sub, (hh + 1) * sub)
        h = _norm_mod(x_ref[rows, :], n1_ref[...], mod_ref[0, 0, 1:2, :], mod_ref[0, 0, 0:1, :])
        p = _dot(h.astype(BF16), w_ref[...])
        cq = p[:, 0:256]
        ckv = p[:, 256:384]
        kr = p[:, 384:512]
        gb_ref[rows, :] = p[:, 512:1024]
        z_ref[rows, :] = p[:, 1024:1536] * p[:, 1536:2048]
        cqn = cq * lax.rsqrt(jnp.mean(cq * cq, axis=-1, keepdims=True) + EPS) * qa_ref[...]
        q = _dot(cqn.astype(BF16), wuq_ref[...])
        cos = cos_ref[0, rows, :]
        sin = sin_ref[0, rows, :]
        qnt_ref[:, rows] = (q[:, 0:512] * scale).T.astype(BF16)
        qr = _rope(q[:, 512:768], _tile_lanes(cos, 2), _tile_lanes(sin, 2), C_ROPE // 4)
        qrt_ref[:, rows] = (qr * scale).T.astype(BF16)
        ckvn = ckv * lax.rsqrt(jnp.mean(ckv * ckv, axis=-1, keepdims=True) + EPS) * kva_ref[...]
        ckv_ref[rows, :] = ckvn
        kv = _dot(ckvn.astype(BF16), wukv_ref[...])
        kn_ref[rows, :] = kv[:, 0:512].astype(BF16)
        vt_ref[:, rows] = kv[:, 512:1024].T.astype(BF16)
        krr = _rope(kr, cos, sin, C_ROPE // 4)
        kr_ref[rows, :] = krr
        krb_ref[rows, :] = krr.astype(BF16)


def _odd_in(x, mod, n1, w_in, qa, kva, wuq, wukv, cos, sin):
    full = lambda shape: pl.BlockSpec(shape, lambda i: (0,) * len(shape))
    tile = lambda w: pl.BlockSpec((IO_TILE, w), lambda i: (i, 0))
    return pl.pallas_call(
        _odd_in_kernel,
        grid=(NT_IO,),
        in_specs=[
            tile(D_MODEL),
            pl.BlockSpec((1, 1, 6, D_MODEL), lambda i: (1, _mod_row(i), 0, 0)),
            full((1, D_MODEL)),
            full(w_in.shape),
            full((1, C_Q_LORA)),
            full((1, C_KV_LORA)),
            full(wuq.shape),
            full(wukv.shape),
            pl.BlockSpec((1, IO_TILE, LANES), lambda i: (_tab_blk(i), 0, 0)),
            pl.BlockSpec((1, IO_TILE, LANES), lambda i: (_tab_blk(i), 0, 0)),
        ],
        out_specs=[pl.BlockSpec((512, IO_TILE), lambda i: (0, i)),
                   pl.BlockSpec((256, IO_TILE), lambda i: (0, i)),
                   tile(LANES), tile(LANES), tile(LANES), tile(512), tile(512),
                   tile(512), pl.BlockSpec((512, IO_TILE), lambda i: (0, i))],
        out_shape=[
            jax.ShapeDtypeStruct((512, N_TOK), BF16),
            jax.ShapeDtypeStruct((256, N_TOK), BF16),
            jax.ShapeDtypeStruct((N_TOK, LANES), F32),
            jax.ShapeDtypeStruct((N_TOK, LANES), F32),
            jax.ShapeDtypeStruct((N_TOK, LANES), BF16),
            jax.ShapeDtypeStruct((N_TOK, 512), F32),
            jax.ShapeDtypeStruct((N_TOK, 512), F32),
            jax.ShapeDtypeStruct((N_TOK, 512), BF16),
            jax.ShapeDtypeStruct((512, N_TOK), BF16),
        ],
        compiler_params=_cparams(("parallel",)),
        name="odd_in",
    )(x, mod, n1, w_in, qa, kva, wuq, wukv, cos, sin)


def _kvup_kernel(c_ref, w_ref, kn_ref, vt_ref):
    p = _dot(c_ref[...].astype(BF16), w_ref[...])
    kn_ref[...] = p[:, 0:512].astype(BF16)
    vt_ref[...] = p[:, 512:1024].T.astype(BF16)


def _kvup(ckv_all, w_ukv):
    n = ckv_all.shape[0]
    return pl.pallas_call(
        _kvup_kernel,
        grid=(n // TILE,),
        in_specs=[pl.BlockSpec((TILE, C_KV_LORA), lambda i: (i, 0)),
                  pl.BlockSpec(w_ukv.shape, lambda i: (0, 0))],
        out_specs=[pl.BlockSpec((TILE, 512), lambda i: (i, 0)), pl.BlockSpec((512, TILE), lambda i: (0, i))],
        out_shape=[jax.ShapeDtypeStruct((n, 512), BF16), jax.ShapeDtypeStruct((512, n), BF16)],
        compiler_params=_cparams(("parallel",)),
        name="mla_kv_up",
    )(ckv_all, w_ukv)


def _softmax_pv(qc, key_chunk, vt_chunks, nk):
    return _softmax_pv_streams([(qc, key_chunk, vt_chunks)], nk)[0]


def _softmax_pv_streams(streams, nk):
    ones = jnp.ones((BF16_ROWS, ATT_TK), BF16)
    ms = [jnp.full((1, qc.shape[1]), -jnp.inf, F32) for qc, _, _ in streams]
    accs = [jnp.zeros((HEAD_DIM + BF16_ROWS, qc.shape[1]), F32) for qc, _, _ in streams]
    nxt = [_dot(key_chunk(0), qc) for qc, key_chunk, _ in streams]
    for c in range(nk):
        for i, (qc, key_chunk, vt_chunks) in enumerate(streams):
            s = nxt[i]
            if c + 1 < nk:
                nxt[i] = _dot(key_chunk(c + 1), qc)
            mn = jnp.maximum(ms[i], jnp.max(s, axis=0, keepdims=True))
            p = jnp.exp2(s - mn).astype(BF16)
            vts = vt_chunks(c)
            if len(vts) == 1:
                pv = _dot(jnp.concatenate([vts[0], ones], axis=0), p)
            else:
                tq = qc.shape[1] // len(vts)
                pv = jnp.concatenate([_dot(jnp.concatenate([vt, ones], axis=0), p[:, k * tq:(k + 1) * tq])
                                      for k, vt in enumerate(vts)], axis=1)
            accs[i] = jnp.exp2(ms[i] - mn) * accs[i] + pv
            ms[i] = mn
    return [acc[0:HEAD_DIM, :] / acc[HEAD_DIM:HEAD_DIM + 1, :] for acc in accs]


def _chunk_locator(segments):
    counts = [seg[0].shape[0] // ATT_TK for seg in segments]

    def locate(c):
        for seg, n in zip(segments, counts):
            if c < n:
                return seg, slice(c * ATT_TK, (c + 1) * ATT_TK)
            c -= n
        raise IndexError(c)

    return locate, sum(counts)


def _gqa_kernel(q_ref, *refs):
    o_ref = refs[-1]
    locate, nk = _chunk_locator([refs[i:i + 2] for i in range(0, len(refs) - 1, 2)])
    kvh = pl.program_id(1)
    tq = q_ref.shape[1]
    n = B_HEADS // B_KV_HEADS
    cols = []
    for hh in range(n):
        q = q_ref[hh * HEAD_DIM:(hh + 1) * HEAD_DIM, :]
        z = jnp.zeros_like(q)
        cols.append(jnp.where(kvh == 0, jnp.concatenate([q, z], axis=0), jnp.concatenate([z, q], axis=0)))
    def key_chunk(c):
        (k_ref, _), rows = locate(c)
        return k_ref[rows, :]

    def vt_chunks(c):
        (_, vt_ref), cols_c = locate(c)
        return [vt_ref[:, cols_c]]

    per = n // GQA_STREAMS
    outs = _softmax_pv_streams(
        [(jnp.concatenate(cols[i * per:(i + 1) * per], axis=1), key_chunk, vt_chunks) for i in range(GQA_STREAMS)], nk)
    for hh in range(n):
        o = outs[hh // per][:, (hh % per) * tq:(hh % per + 1) * tq]
        o_ref[hh * HEAD_DIM:(hh + 1) * HEAD_DIM, :] = o.astype(BF16)


def _mla_kernel(qn_ref, qr_ref, *refs):
    o_ref = refs[-1]
    locate, nk = _chunk_locator([refs[i:i + 3] for i in range(0, len(refs) - 1, 3)])
    tq = qn_ref.shape[1]
    z = jnp.zeros((C_NOPE, tq), BF16)
    zr = jnp.zeros((LANES - C_ROPE, tq), BF16)
    streams = []
    for pair in range(MLA_PAIRS):
        cols = []
        for hh in range(2):
            h = 2 * pair + hh
            qn = qn_ref[h * C_NOPE:(h + 1) * C_NOPE, :]
            qr = qr_ref[h * C_ROPE:(h + 1) * C_ROPE, :]
            cols.append(jnp.concatenate(([qn, z] if hh == 0 else [z, qn]) + [qr, zr], axis=0))

        def key_chunk(c, pair=pair):
            (kn_ref, kr_ref, _), rows = locate(c)
            return jnp.concatenate([kn_ref[rows, pair * LANES:(pair + 1) * LANES], kr_ref[rows, :]], axis=1)

        def vt_chunks(c, pair=pair):
            (_, _, vt_ref), cols_c = locate(c)
            return [vt_ref[(2 * pair + hh) * C_V:(2 * pair + hh + 1) * C_V, cols_c] for hh in range(2)]

        streams.append((jnp.concatenate(cols, axis=1), key_chunk, vt_chunks))
    outs = _softmax_pv_streams(streams, nk)
    for pair in range(MLA_PAIRS):
        for hh in range(2):
            h = 2 * pair + hh
            o_ref[h * C_V:(h + 1) * C_V, :] = outs[pair][:, hh * tq:(hh + 1) * tq].astype(BF16)


def _attn_call(body, grid, in_specs, out_spec, args, n_tiles):
    return pl.pallas_call(
        body,
        grid=grid,
        in_specs=in_specs,
        out_specs=out_spec,
        out_shape=jax.ShapeDtypeStruct((512, n_tiles * TILE), BF16),
        compiler_params=_cparams(("parallel",) * len(grid)),
        name="attention",
    )(*args)


def _gqa_attention(qt, kb, vt, k_cache, vt_cache):
    assert NT_CTX * TILE == DEC_SEQ
    rows = (B_HEADS // B_KV_HEADS) * HEAD_DIM
    o_ctx = _attn_call(
        _gqa_kernel, (BATCH, B_KV_HEADS),
        [pl.BlockSpec((rows, TILE), lambda b, h: (h, b)),
         pl.BlockSpec((None, TILE, LANES), lambda b, h: (b, 0, 0)),
         pl.BlockSpec((HEAD_DIM, TILE), lambda b, h: (h, b))],
        pl.BlockSpec((rows, TILE), lambda b, h: (h, b)),
        [qt, kb.reshape(NT, TILE, LANES), vt], NT_CTX)
    o_lat = _attn_call(
        _gqa_kernel, (DEC_BATCH, B_KV_HEADS, NT_SEQ),
        [pl.BlockSpec((rows, TILE), lambda b, h, j: (h, NT_CTX + b * NT_SEQ + j)),
         pl.BlockSpec((None, PAST_LEN, LANES), lambda b, h, j: (b, 0, 0)),
         pl.BlockSpec((None, HEAD_DIM, PAST_LEN), lambda b, h, j: (b, h, 0)),
         pl.BlockSpec((DEC_SEQ, LANES), lambda b, h, j: (1 + b, 0)),
         pl.BlockSpec((HEAD_DIM, DEC_SEQ), lambda b, h, j: (h, 1 + b))],
        pl.BlockSpec((rows, TILE), lambda b, h, j: (h, b * NT_SEQ + j)),
        [qt, k_cache, vt_cache, kb, vt], NT_LAT)
    return o_ctx, o_lat


def _mla_attention(qnt, qrt, kn, krb, vt, kn_cache, kr_cache, vt_cache):
    nh = 2 * MLA_PAIRS
    steps = C_HEADS // nh
    o_ctx = _attn_call(
        _mla_kernel, (BATCH, steps),
        [pl.BlockSpec((nh * C_NOPE, TILE), lambda b, p: (p, b)),
         pl.BlockSpec((nh * C_ROPE, TILE), lambda b, p: (p, b)),
         pl.BlockSpec((TILE, MLA_PAIRS * LANES), lambda b, p: (b, p)),
         pl.BlockSpec((TILE, LANES), lambda b, p: (b, 0)),
         pl.BlockSpec((nh * C_V, TILE), lambda b, p: (p, b))],
        pl.BlockSpec((nh * C_V, TILE), lambda b, p: (p, b)),
        [qnt, qrt, kn, krb, vt], NT_CTX)
    o_lat = _attn_call(
        _mla_kernel, (DEC_BATCH, steps, NT_SEQ),
        [pl.BlockSpec((nh * C_NOPE, TILE), lambda b, p, j: (p, NT_CTX + b * NT_SEQ + j)),
         pl.BlockSpec((nh * C_ROPE, TILE), lambda b, p, j: (p, NT_CTX + b * NT_SEQ + j)),
         pl.BlockSpec((PAST_LEN, MLA_PAIRS * LANES), lambda b, p, j: (b, p)),
         pl.BlockSpec((None, PAST_LEN, LANES), lambda b, p, j: (b, 0, 0)),
         pl.BlockSpec((nh * C_V, PAST_LEN), lambda b, p, j: (p, b)),
         pl.BlockSpec((DEC_SEQ, MLA_PAIRS * LANES), lambda b, p, j: (1 + b, p)),
         pl.BlockSpec((DEC_SEQ, LANES), lambda b, p, j: (1 + b, 0)),
         pl.BlockSpec((nh * C_V, DEC_SEQ), lambda b, p, j: (p, 1 + b))],
        pl.BlockSpec((nh * C_V, TILE), lambda b, p, j: (p, b * NT_SEQ + j)),
        [qnt, qrt, kn_cache, kr_cache, vt_cache, kn, krb, vt], NT_LAT)
    return o_ctx, o_lat


def _group_tile(i, ctx_ref, lat_ref):
    return jnp.where(i < NT_CTX_IO, ctx_ref[...], lat_ref[...])


_CTX_COL = lambda i: (0, jnp.minimum(i, NT_CTX_IO - 1))
_LAT_COL = lambda i: (0, jnp.maximum(i - NT_CTX_IO, 0))
_CTX_ROW = lambda i: (jnp.minimum(i, NT_CTX_IO - 1), 0)
_LAT_ROW = lambda i: (jnp.maximum(i - NT_CTX_IO, 0), 0)


def _finish_out(o, x_in, mod_ref, n2_ref, wr_ref, xo_ref, h2_ref, lg_ref, rows=slice(None)):
    x = x_in + mod_ref[0, 0, 2:3, :] * o
    xo_ref[rows, :] = x
    h2 = _norm_mod(x, n2_ref[...], mod_ref[0, 0, 4:5, :], mod_ref[0, 0, 3:4, :])
    hi = h2.astype(BF16)
    lo = (h2 - hi.astype(F32)).astype(BF16)
    h2_ref[rows, :] = hi
    parts = (_dot(hi, wr_ref[...]) + _dot(lo, wr_ref[...])).T
    lg_ref[:, rows] = parts[0:N_EXPERTS, :] + parts[N_EXPERTS:2 * N_EXPERTS, :]


def _dot_t(at, b):
    return lax.dot_general(at, b, (((0,), (0,)), ((), ())), preferred_element_type=F32)


def _even_out_kernel(a_ref, btc_ref, btl_ref, w_ref, xc_ref, xl_ref, mod_ref, n2_ref, wr_ref,
                     xo_ref, h2_ref, lg_ref):
    ctx = pl.program_id(0) < NT_CTX_IO
    sub = IO_TILE // OUT_SUBTILES
    for h in range(OUT_SUBTILES):
        rows = slice(h * sub, (h + 1) * sub)
        bt = jnp.where(ctx, btc_ref[:, rows], btl_ref[:, rows])
        o = _dot(a_ref[rows, :], w_ref[0:512, :]) + _dot_t(bt, w_ref[512:1024, :])
        x_in = jnp.where(ctx, xc_ref[rows, :], xl_ref[rows, :])
        _finish_out(o, x_in, mod_ref, n2_ref, wr_ref, xo_ref, h2_ref, lg_ref, rows)


def _odd_out_kernel(atc_ref, atl_ref, z_ref, zp_ref, zn_ref, gb_ref, cw_ref, w_ref, x_ref, mod_ref, n2_ref,
                    wr_ref, xo_ref, h2_ref, lg_ref):
    i = pl.program_id(0)
    at = _group_tile(i, atc_ref, atl_ref)
    ctx = i < NT_CTX_IO
    j = (i - NT_CTX_IO) % NT_SEQ_IO
    first = jnp.logical_or(ctx, j == 0)
    last = jnp.logical_or(ctx, j == NT_SEQ_IO - 1)
    z = z_ref[...]
    row = lax.broadcasted_iota(jnp.int32, z.shape, 0)
    halo_p = jnp.where(first, 0.0, zp_ref[7:8, :])
    halo_n = jnp.where(last, 0.0, zn_ref[0:1, :])
    zprev = jnp.where(row == 0, halo_p, pltpu.roll(z, 1, 0))
    znext = jnp.where(row == IO_TILE - 1, halo_n, pltpu.roll(z, IO_TILE - 1, 0))
    zprev = jnp.where(jnp.logical_and(ctx, row % SEQ == 0), 0.0, zprev)
    znext = jnp.where(jnp.logical_and(ctx, row % SEQ == SEQ - 1), 0.0, znext)
    y = zprev * cw_ref[0:1, :] + z * cw_ref[1:2, :] + znext * cw_ref[2:3, :]
    d = (gb_ref[...] * y).astype(BF16)
    sub = IO_TILE // OUT_SUBTILES
    for h in range(OUT_SUBTILES):
        rows = slice(h * sub, (h + 1) * sub)
        o = _dot_t(at[:, rows], w_ref[0:512, :]) + _dot(d[rows, :], w_ref[512:1024, :])
        _finish_out(o, x_ref[rows, :], mod_ref, n2_ref, wr_ref, xo_ref, h2_ref, lg_ref, rows)


_OUT_SHAPES = [
    jax.ShapeDtypeStruct((N_TOK, D_MODEL), F32),
    jax.ShapeDtypeStruct((N_TOK, D_MODEL), BF16),
    jax.ShapeDtypeStruct((N_EXPERTS, N_TOK), F32),
]
_LOGIT_SPEC = pl.BlockSpec((N_EXPERTS, IO_TILE), lambda i: (0, i))


def _even_out(layer, a, b, w_out, xc, xl, mod, n2, wr):
    full = lambda shape: pl.BlockSpec(shape, lambda i: (0,) * len(shape))
    tile = lambda w: pl.BlockSpec((IO_TILE, w), lambda i: (i, 0))
    return pl.pallas_call(
        _even_out_kernel,
        grid=(NT_IO,),
        in_specs=[tile(512), pl.BlockSpec((512, IO_TILE), _CTX_COL), pl.BlockSpec((512, IO_TILE), _LAT_COL),
                  full(w_out.shape), pl.BlockSpec((IO_TILE, D_MODEL), _CTX_ROW),
                  pl.BlockSpec((IO_TILE, D_MODEL), _LAT_ROW),
                  pl.BlockSpec((1, 1, 6, D_MODEL), lambda i: (layer, _mod_row(i), 0, 0)),
                  full((1, D_MODEL)), full(wr.shape)],
        out_specs=[tile(D_MODEL), tile(D_MODEL), _LOGIT_SPEC],
        out_shape=_OUT_SHAPES,
        compiler_params=_cparams(("parallel",)),
        name="even_out",
    )(a, b[0], b[1], w_out, xc, xl, mod, n2, wr)


def _odd_out(layer, a, z, gb, cw, w_out, x, mod, n2, wr):
    full = lambda shape: pl.BlockSpec(shape, lambda i: (0,) * len(shape))
    tile = lambda w: pl.BlockSpec((IO_TILE, w), lambda i: (i, 0))
    rb = IO_TILE // 8
    return pl.pallas_call(
        _odd_out_kernel,
        grid=(NT_IO,),
        in_specs=[pl.BlockSpec((512, IO_TILE), _CTX_COL), pl.BlockSpec((512, IO_TILE), _LAT_COL), tile(512),
                  pl.BlockSpec((8, 512), lambda i: (jnp.maximum(i * rb - 1, 0), 0)),
                  pl.BlockSpec((8, 512), lambda i: (jnp.minimum(i * rb + rb, NT_IO * rb - 1), 0)),
                  tile(512), full(cw.shape), full(w_out.shape), tile(D_MODEL),
                  pl.BlockSpec((1, 1, 6, D_MODEL), lambda i: (layer, _mod_row(i), 0, 0)),
                  full((1, D_MODEL)), full(wr.shape)],
        out_specs=[tile(D_MODEL), tile(D_MODEL), _LOGIT_SPEC],
        out_shape=_OUT_SHAPES,
        compiler_params=_cparams(("parallel",)),
        name="odd_out",
    )(a[0], a[1], z, z, z, gb, cw, w_out, x, mod, n2, wr)


def _select_tokens(aff, n_dom, cap, tri, pos_ref, gate_ref):
    w = aff.shape[1] // n_dom
    doms = [aff[:, d * w:(d + 1) * w] for d in range(n_dom)]
    bits = [jnp.zeros((N_EXPERTS, 1), jnp.int32) for _ in range(n_dom)]
    for bit in range(30, -1, -1):
        for d in range(n_dom):
            cand = bits[d] | (1 << bit)
            cnt = jnp.sum((doms[d] >= lax.bitcast_convert_type(cand, F32)).astype(F32), axis=1, keepdims=True)
            bits[d] = jnp.where(cnt >= cap, cand, bits[d])
    for d in range(n_dom):
        thr = lax.bitcast_convert_type(bits[d], F32)
        gt = doms[d] > thr
        eq = doms[d] == thr
        need = cap - jnp.sum(gt.astype(F32), axis=1, keepdims=True)
        eq_seen = jnp.zeros((N_EXPERTS, 1), F32)
        sel_seen = jnp.zeros((N_EXPERTS, 1), F32) + d * cap
        for blk in range(w // TILE):
            cols = slice(blk * TILE, (blk + 1) * TILE)
            eq_b = eq[:, cols].astype(F32)
            eq_rank = _dot(eq_b.astype(BF16), tri) + eq_seen
            sel = jnp.logical_or(gt[:, cols], jnp.logical_and(eq[:, cols], eq_rank < need))
            sel_f = sel.astype(F32)
            pos = _dot(sel_f.astype(BF16), tri) + sel_seen
            tile = d * (w // TILE) + blk
            pos_ref[0, tile] = jnp.where(sel, pos, -1.0)
            gate_ref[0, tile] = jnp.where(sel, doms[d][:, cols], 0.0)
            eq_seen = eq_seen + jnp.sum(eq_b, axis=1, keepdims=True)
            sel_seen = sel_seen + jnp.sum(sel_f, axis=1, keepdims=True)


def _route_kernel(lg_ref, pos_ref, gate_ref):
    lg = lg_ref[...]
    ex = jnp.exp(lg - jnp.max(lg, axis=0, keepdims=True))
    aff = ex / jnp.sum(ex, axis=0, keepdims=True)
    r = lax.broadcasted_iota(jnp.int32, (TILE, TILE), 0)
    c = lax.broadcasted_iota(jnp.int32, (TILE, TILE), 1)
    tri = (r < c).astype(BF16)

    @pl.when(pl.program_id(0) == 0)
    def _():
        _select_tokens(aff, GROUP // SEQ, EC_CAPACITY * SEQ // N_EXPERTS, tri, pos_ref, gate_ref)

    @pl.when(pl.program_id(0) > 0)
    def _():
        _select_tokens(aff, GROUP // DEC_SEQ, EC_CAPACITY * DEC_SEQ // N_EXPERTS, tri, pos_ref, gate_ref)


def _route(logits_t):
    blk = pl.BlockSpec((1, GROUP_TILES, N_EXPERTS, TILE), lambda g: (g, 0, 0, 0))
    shape = jax.ShapeDtypeStruct((N_GROUPS, GROUP_TILES, N_EXPERTS, TILE), F32)
    return pl.pallas_call(
        _route_kernel,
        grid=(N_GROUPS,),
        in_specs=[pl.BlockSpec((N_EXPERTS, GROUP), lambda g: (0, g))],
        out_specs=[blk, blk],
        out_shape=[shape, shape],
        compiler_params=_cparams(("parallel",)),
        name="route",
    )(logits_t)


def _moe_kernel(nblk_ref, ctab_ref, etab_ref, llen_ref,
                x_ref, pos_ref, gate_ref, wg_ref, wu_ref, wd_ref, xres_ref, mod_ref, fn_ref, *rest, final):
    n_out = 2 if final else 1
    out_refs = rest[:n_out]
    lists_ref, gl_ref, s_ref, gs_ref, acc_ref, wgb_ref, wub_ref, wdb_ref = rest[n_out:]
    g = pl.program_id(0)
    s = pl.program_id(1)
    chunks = STACK_BLK // CHUNK_ROWS
    rows16 = lax.broadcasted_iota(jnp.int32, (CHUNK_ROWS, TILE), 0)

    @pl.when(jnp.logical_and(g == 0, s == 0))
    def _():
        lists_ref[ZERO_ROW:ZERO_ROW + CHUNK_ROWS, :] = jnp.zeros((CHUNK_ROWS, D_MODEL), BF16)

    def chunk_rows(tbase, q, unused_row):
        d = ctab_ref[tbase + q]
        return pl.multiple_of(jnp.where(d >= 0, d, unused_row), CHUNK_ROWS)

    def build_block(j, rb, with_gate, slot):
        tbase = ((g * GROUP_TILES + j) * MAX_STACK_BLKS + rb) * chunks
        for q in range(chunks):
            d = ctab_ref[tbase + q]
            e = etab_ref[tbase + q]
            posrow = pos_ref[0, j, pl.ds(e, 1), :]
            rowid = (rows16 + (rb * STACK_BLK + q * CHUNK_ROWS)).astype(F32)
            hit = jnp.logical_and(posrow == rowid, d >= 0)
            s_ref[slot, q * CHUNK_ROWS:(q + 1) * CHUNK_ROWS, :] = hit.astype(BF16)
            if with_gate:
                gaterow = gate_ref[0, j, pl.ds(e, 1), :]
                gs_ref[slot, q * CHUNK_ROWS:(q + 1) * CHUNK_ROWS, :] = jnp.where(hit, gaterow, 0.0)
        return tbase

    def for_blocks(nb, build, finish):
        def run(first, n):
            tbases = [build(first + k, k) for k in range(n)]
            for k in range(n):
                finish(tbases[k], k)

        def group(i, _):
            run(BLOCK_SLOTS * i, BLOCK_SLOTS)
            return 0

        lax.fori_loop(0, nb // BLOCK_SLOTS, group, 0)
        for left in range(1, BLOCK_SLOTS):
            @pl.when(nb % BLOCK_SLOTS == left)
            def _(left=left):
                run(nb - left, left)

    @pl.when(s < MOE_TILE_STEPS)
    def _gather():
        def tile(t, _):
            j = s * MOE_STEP_TILES + t
            tok = pl.multiple_of(t * TILE, TILE)

            def finish(tbase, slot):
                picked = _dot(s_ref[slot], x_ref[pl.ds(tok, TILE), :]).astype(BF16)
                gcol = jnp.sum(gs_ref[slot], axis=1, keepdims=True)
                for q in range(chunks):
                    d = chunk_rows(tbase, q, DUMP_ROW)
                    rows = slice(q * CHUNK_ROWS, (q + 1) * CHUNK_ROWS)
                    lists_ref[pl.ds(d, CHUNK_ROWS), :] = picked[rows, :]
                    gl_ref[pl.ds(d, CHUNK_ROWS), :] = gcol[rows, :]

            for_blocks(nblk_ref[g * GROUP_TILES + j], lambda rb, slot: build_block(j, rb, True, slot), finish)
            return 0

        lax.fori_loop(0, MOE_STEP_TILES, tile, 0)

    @pl.when(jnp.logical_and(s >= MOE_TILE_STEPS, s < MOE_TILE_STEPS + N_EXPERTS))
    def _experts():
        e = s - MOE_TILE_STEPS
        ln = llen_ref[g * N_EXPERTS + e]
        base = e * LIST_MAX
        half = FFN_BLK // 2
        rem = ln % FFN_BLK
        use_half = jnp.logical_and(rem > 0, rem <= half)
        nfull = ln // FFN_BLK + jnp.where(rem > half, 1, 0)
        end = nfull * FFN_BLK + jnp.where(use_half, half, 0)

        def zero_tail(k, _):
            r0 = pl.multiple_of(base + ln + k * CHUNK_ROWS, CHUNK_ROWS)
            lists_ref[pl.ds(r0, CHUNK_ROWS), :] = jnp.zeros((CHUNK_ROWS, D_MODEL), BF16)
            gl_ref[pl.ds(r0, CHUNK_ROWS), :] = jnp.zeros((CHUNK_ROWS, 1), F32)
            return 0

        lax.fori_loop(0, (end - ln) // CHUNK_ROWS, zero_tail, 0)
        wgb_ref[...] = wg_ref[0, 0].astype(BF16)
        wub_ref[...] = wu_ref[0, 0].astype(BF16)
        wdb_ref[...] = wd_ref[0, 0].astype(BF16)

        def ffn_rows(r0, n):
            xs = lists_ref[pl.ds(r0, n), :]
            hid = _silu(_dot(xs, wgb_ref[...])) * _dot(xs, wub_ref[...])
            y = _dot(hid.astype(BF16), wdb_ref[...]) * gl_ref[pl.ds(r0, n), :]
            lists_ref[pl.ds(r0, n), :] = y.astype(BF16)

        def ffn(c, _):
            ffn_rows(pl.multiple_of(base + c * FFN_BLK, FFN_BLK), FFN_BLK)
            return 0

        lax.fori_loop(0, nfull, ffn, 0)

        @pl.when(use_half)
        def _():
            ffn_rows(pl.multiple_of(base + nfull * FFN_BLK, half), half)

    @pl.when(s >= MOE_TILE_STEPS + N_EXPERTS)
    def _scatter():
        def tile(t, _):
            j = (s - MOE_TILE_STEPS - N_EXPERTS) * SCATTER_STEP_TILES + t
            rows = pl.ds(pl.multiple_of(t * TILE, TILE), TILE)
            acc_ref[...] = jnp.zeros_like(acc_ref)

            def finish(tbase, slot):
                y = jnp.concatenate([lists_ref[pl.ds(chunk_rows(tbase, q, ZERO_ROW), CHUNK_ROWS), :]
                                     for q in range(chunks)], axis=0)
                acc_ref[...] += _dot_t(s_ref[slot], y)

            for_blocks(nblk_ref[g * GROUP_TILES + j], lambda rb, slot: build_block(j, rb, False, slot), finish)
            x = xres_ref[rows, :] + mod_ref[0, 0, 5:6, :] * acc_ref[...]
            if final:
                y = x * lax.rsqrt(jnp.mean(x * x, axis=-1, keepdims=True) + EPS) * fn_ref[...]

                @pl.when(g == 0)
                def _():
                    out_refs[0][rows, :] = y

                @pl.when(g > 0)
                def _():
                    out_refs[1][rows, :] = y
            else:
                out_refs[0][rows, :] = x
            return 0

        lax.fori_loop(0, SCATTER_STEP_TILES, tile, 0)


def _moe(layer, final, tables, h2, pos_t, gate_t, wg, wu, wd, xres, mod, fn):
    gt = GROUP_TILES
    ts = MOE_TILE_STEPS
    sc = SCATTER_STEPS
    rows_a = MOE_STEP_TILES * TILE
    rows = SCATTER_STEP_TILES * TILE
    step_c = lambda s: jnp.clip(s - ts - N_EXPERTS, 0, sc - 1)
    tile_a = lambda g, s, *_: (g * ts + jnp.minimum(s, ts - 1), 0)
    tile_c = lambda g, s, *_: (g * sc + step_c(s), 0)
    expert = lambda g, s, *_: (layer, jnp.clip(s - ts, 0, N_EXPERTS - 1), 0, 0)
    group = lambda g, s, *_: (g, 0, 0, 0)
    if final:
        out_specs = [
            pl.BlockSpec((rows, D_MODEL), lambda g, s, *_: (jnp.where(g == 0, step_c(s), sc - 1), 0)),
            pl.BlockSpec((rows, D_MODEL), lambda g, s, *_: (jnp.where(g == 0, 0, (g - 1) * sc + step_c(s)), 0)),
        ]
        out_shape = [jax.ShapeDtypeStruct((NT_CTX * TILE, D_MODEL), F32),
                     jax.ShapeDtypeStruct((NT_LAT * TILE, D_MODEL), F32)]
    else:
        out_specs = pl.BlockSpec((rows, D_MODEL), tile_c)
        out_shape = jax.ShapeDtypeStruct((N_TOK, D_MODEL), F32)
    grid_spec = pltpu.PrefetchScalarGridSpec(
        num_scalar_prefetch=4,
        grid=(N_GROUPS, ts + N_EXPERTS + sc),
        in_specs=[
            pl.BlockSpec((rows_a, D_MODEL), tile_a),
            pl.BlockSpec((1, gt, N_EXPERTS, TILE), group),
            pl.BlockSpec((1, gt, N_EXPERTS, TILE), group),
            pl.BlockSpec((1, 1, D_MODEL, EXPERT_FF), expert),
            pl.BlockSpec((1, 1, D_MODEL, EXPERT_FF), expert),
            pl.BlockSpec((1, 1, EXPERT_FF, D_MODEL), expert),
            pl.BlockSpec((rows, D_MODEL), tile_c),
            pl.BlockSpec((1, 1, 6, D_MODEL), lambda g, s, *_: (layer, g, 0, 0)),
            pl.BlockSpec((1, D_MODEL), lambda g, s, *_: (0, 0)),
        ],
        out_specs=out_specs,
        scratch_shapes=[
            pltpu.VMEM((LIST_ROWS, D_MODEL), BF16),
            pltpu.VMEM((LIST_ROWS, 1), F32),
            pltpu.VMEM((BLOCK_SLOTS, STACK_BLK, TILE), BF16),
            pltpu.VMEM((BLOCK_SLOTS, STACK_BLK, TILE), F32),
            pltpu.VMEM((TILE, D_MODEL), F32),
            pltpu.VMEM((D_MODEL, EXPERT_FF), BF16),
            pltpu.VMEM((D_MODEL, EXPERT_FF), BF16),
            pltpu.VMEM((EXPERT_FF, D_MODEL), BF16),
        ],
    )
    return pl.pallas_call(
        functools.partial(_moe_kernel, final=final),
        grid_spec=grid_spec,
        out_shape=out_shape,
        compiler_params=_cparams(("arbitrary", "arbitrary")),
        name="moe",
    )(*tables, h2, pos_t, gate_t, wg, wu, wd, xres, mod, fn)


def _rope_tables(rot_dim):
    axis_dim = rot_dim // 2
    tok = jnp.arange(DEC_SEQ)
    rows = (tok // GRID_W).astype(F32)
    cols = (tok % GRID_W).astype(F32)
    inv_freq = ROPE_THETA ** (-jnp.arange(0, axis_dim, 2, dtype=F32) / axis_dim)
    ar = rows[:, None] * inv_freq
    ac = cols[:, None] * inv_freq
    cos = jnp.concatenate([jnp.cos(ar), jnp.cos(ar), jnp.cos(ac), jnp.cos(ac)], axis=1)
    sin = jnp.concatenate([-jnp.sin(ar), jnp.sin(ar), -jnp.sin(ac), jnp.sin(ac)], axis=1)
    rep = LANES // rot_dim
    cos = jnp.tile(cos, (1, rep)).reshape(NT_SEQ_IO, IO_TILE, LANES)
    sin = jnp.tile(sin, (1, rep)).reshape(NT_SEQ_IO, IO_TILE, LANES)
    cos = jnp.concatenate([jnp.ones((1, IO_TILE, LANES), F32), cos], axis=0)
    sin = jnp.concatenate([jnp.zeros((1, IO_TILE, LANES), F32), sin], axis=0)
    return cos, sin


def _excl_cumsum(a, axis):
    return jnp.cumsum(a, axis=axis) - a


def _route_and_moe(layer, final, h2, logits, wg, wu, wd, xres, mod, fn):
    pos_t, gate_t = _route(logits)
    sel = pos_t >= 0
    cnt = jnp.sum(sel, axis=-1, dtype=jnp.int32)
    npad = (cnt + CHUNK_ROWS - 1) // CHUNK_ROWS * CHUNK_ROWS
    seg_off = _excl_cumsum(npad, 2)
    list_off = _excl_cumsum(npad, 1)
    rank0 = _excl_cumsum(cnt, 1)
    stack_pos = jnp.where(sel, pos_t - rank0[..., None].astype(F32) + seg_off[..., None].astype(F32), -1.0)
    nblk = (jnp.sum(npad, axis=2) + STACK_BLK - 1) // STACK_BLK
    llen = jnp.sum(npad, axis=1)
    r = (jnp.arange(MAX_STACK_BLKS * STACK_BLK // CHUNK_ROWS, dtype=jnp.int32) * CHUNK_ROWS)[None, None, None, :]
    in_seg = jnp.logical_and(r >= seg_off[..., None], r < (seg_off + npad)[..., None])
    eidx = jnp.arange(N_EXPERTS, dtype=jnp.int32)[None, None, :, None]
    dst = jnp.sum(jnp.where(in_seg, eidx * LIST_MAX + list_off[..., None] + r - seg_off[..., None], 0), axis=2)
    used = jnp.any(in_seg, axis=2)
    ctab = jnp.where(used, dst, -1).astype(jnp.int32)
    etab = jnp.sum(jnp.where(in_seg, eidx, 0), axis=2).astype(jnp.int32)
    tables = (nblk.reshape(-1).astype(jnp.int32), ctab.reshape(-1), etab.reshape(-1),
              llen.reshape(-1).astype(jnp.int32))
    return _moe(layer, final, tables, h2, stack_pos, gate_t, wg, wu, wd, xres, mod, fn)


def _split_router(w):
    hi = w.astype(BF16)
    lo = (w - hi.astype(F32)).astype(BF16)
    pad = jnp.zeros((D_MODEL, LANES - 2 * N_EXPERTS), BF16)
    return jnp.concatenate([hi, lo, pad], axis=1)


def kernel(x_prompt, x_sample, c, cache_attn_k, cache_attn_v, cache_mla_ckv, cache_mla_krope, c_ctx, w_mod, b_mod,
           norm1, norm2, ev_w_in, ev_q_norm, ev_k_norm, ev_w_s, ev_b_s, ev_w_out, od_w_in, od_q_a_norm,
           od_kv_a_norm, od_w_uq, od_w_ukv, od_conv_w, od_w_out, moe_router, moe_w_gate, moe_w_up, moe_w_down,
           final_norm):
    nctx = NT_CTX * TILE
    xc = x_prompt.reshape(nctx, D_MODEL)
    xl = x_sample.reshape(NT_LAT * TILE, D_MODEL)
    cvec = jnp.concatenate([c_ctx[None], c, jnp.zeros((3, D_MODEL), F32)], axis=0)
    mod = _modulation(cvec, w_mod, b_mod).reshape(2, 8, 6, D_MODEL)

    cos64, sin64 = _rope_tables(HEAD_DIM)
    seg = jnp.arange(512) // HEAD_DIM
    bd = (seg[:, None] == seg[None, :]).astype(BF16)
    bs_tab = jnp.repeat(ev_b_s[0].T, 64, axis=1)
    gated, qt, k, kb, v, vt = _even_in(
        xc, xl, mod, norm1[0:1], ev_w_in[0].astype(BF16), jnp.tile(ev_q_norm[0], 8)[None], jnp.tile(ev_k_norm[0], 2)[None],
        cos64, sin64, jnp.concatenate([ev_w_s[0, 0::2], ev_w_s[0, 1::2]], axis=2).astype(BF16), bs_tab, bd)
    new_k = k[:nctx].reshape(BATCH, SEQ, B_KV_HEADS, HEAD_DIM).transpose(0, 2, 1, 3)[:, None]
    new_v = v[:nctx].reshape(BATCH, SEQ, B_KV_HEADS, HEAD_DIM).transpose(0, 2, 1, 3)[:, None]
    ck = cache_attn_k[:, 0].transpose(0, 2, 1, 3).reshape(DEC_BATCH, PAST_LEN, LANES).astype(BF16)
    cv = cache_attn_v[:, 0].transpose(0, 1, 3, 2).reshape(DEC_BATCH, LANES, PAST_LEN).astype(BF16)
    attn_t = _gqa_attention(qt, kb, vt, ck, cv)
    x1, h2, logits = _even_out(0, gated, attn_t, ev_w_out[0].astype(BF16), xc, xl, mod, norm2[0:1],
                               _split_router(moe_router[0]))
    x2 = _route_and_moe(0, False, h2, logits, moe_w_gate, moe_w_up, moe_w_down, x1, mod, final_norm[None])

    cos32, sin32 = _rope_tables(C_ROPE)
    w_in1 = od_w_in[0]
    w_in1 = jnp.concatenate([w_in1[:, :C_Q_LORA + C_KV_LORA + C_ROPE], jnp.zeros((D_MODEL, LANES - C_ROPE), F32),
                             w_in1[:, C_Q_LORA + C_KV_LORA + C_ROPE:]], axis=1).astype(BF16)
    wuq = od_w_uq[0].reshape(C_Q_LORA, C_HEADS, C_NOPE + C_ROPE)
    wuq = jnp.concatenate([wuq[:, :, :C_NOPE].reshape(C_Q_LORA, -1), wuq[:, :, C_NOPE:].reshape(C_Q_LORA, -1)],
                          axis=1).astype(BF16)
    wukv = od_w_ukv[0].reshape(C_KV_LORA, C_HEADS, C_NOPE + C_V)
    wukv = jnp.concatenate([wukv[:, :, :C_NOPE].reshape(C_KV_LORA, -1), wukv[:, :, C_NOPE:].reshape(C_KV_LORA, -1)],
                           axis=1).astype(BF16)
    qnt, qrt, ckv, kr, krb, gb, z, kn, vt1 = _odd_in(x2, mod, norm1[1:2], w_in1, od_q_a_norm[0][None],
                                                     od_kv_a_norm[0][None], wuq, wukv, cos32, sin32)
    new_ckv = ckv[:nctx].reshape(BATCH, 1, SEQ, C_KV_LORA)
    new_kr = kr[:nctx, :C_ROPE].reshape(BATCH, 1, SEQ, C_ROPE)
    kn_cache, vt_cache = _kvup(cache_mla_ckv[:, 0].reshape(DEC_BATCH * PAST_LEN, C_KV_LORA), wukv)
    ckr = jnp.pad(cache_mla_krope[:, 0], ((0, 0), (0, 0), (0, LANES - C_ROPE))).astype(BF16)
    attn_t = _mla_attention(qnt, qrt, kn, krb, vt1, kn_cache, ckr, vt_cache)
    cw = jnp.concatenate([od_conv_w[0], jnp.zeros((5, D_WIDTH), F32)], axis=0)
    x3, h2, logits = _odd_out(1, attn_t, z, gb, cw, od_w_out[0].astype(BF16), x2, mod, norm2[1:2],
                              _split_router(moe_router[1]))
    y_c, y_l = _route_and_moe(1, True, h2, logits, moe_w_gate, moe_w_up, moe_w_down, x3, mod, final_norm[None])
    y_prompt = y_c.reshape(BATCH, SEQ, D_MODEL)
    y_sample = y_l.reshape(DEC_BATCH, DEC_SEQ, D_MODEL)
    return y_prompt, y_sample, new_k, new_v, new_ckv, new_kr
```
